```python
import jax, jax.numpy as jnp
from jax import lax
import numpy as np

D_MODEL = 1024
BATCH = 8
SEQ = 8192
DEPTH = 4

GRID_W = 64
CTX_LEN = 256
HEAD_DIM = 64
NA_HEADS = 8
NA_WIN_H = 8
NA_WIN_W = 16
GQA_Q_HEADS = 8
GQA_KV_HEADS = 2
GQA_REP = GQA_Q_HEADS // GQA_KV_HEADS
NA_WIDTH = NA_HEADS * HEAD_DIM
GQA_Q_WIDTH = GQA_Q_HEADS * HEAD_DIM
GQA_KV_WIDTH = GQA_KV_HEADS * HEAD_DIM
IN_SIZES = (NA_WIDTH, NA_WIDTH, NA_WIDTH, GQA_Q_WIDTH, GQA_KV_WIDTH, GQA_KV_WIDTH, D_MODEL, D_MODEL)
IN_COLS = sum(IN_SIZES)
IN_SPLITS = tuple(int(v) for v in np.cumsum(IN_SIZES)[:-1])
D_FF = -(-(8 * D_MODEL) // (3 * 256)) * 256
Q_BLOCK = 128
ROPE_THETA = 10000.0
EPS = 1e-6
SCALE = HEAD_DIM ** -0.5

kernel_name = "hybrid_natten_gqa_prefix_dit"


def rms_norm(x, g):
    xf = x.astype(jnp.float32)
    y = xf * lax.rsqrt(jnp.mean(xf * xf, axis=-1, keepdims=True) + EPS)
    return (y * g.astype(jnp.float32)).astype(x.dtype)


def modulate(h, shift, scale):
    return h * (1 + scale) + shift


def heads(t, n_heads):
    b, n, _ = t.shape
    return t.reshape(b, n, n_heads, HEAD_DIM).transpose(0, 2, 1, 3)


def merge_heads(t):
    b, h, n, d = t.shape
    return t.transpose(0, 2, 1, 3).reshape(b, n, h * d)


def axial_rope_tables(n_tokens):
    t = jnp.arange(n_tokens)
    row = (t // GRID_W).astype(jnp.float32)
    col = (t % GRID_W).astype(jnp.float32)
    half = HEAD_DIM // 2
    inv = ROPE_THETA ** (-jnp.arange(0, half, 2, dtype=jnp.float32) / half)
    ang = jnp.concatenate([row[:, None] * inv, col[:, None] * inv], axis=-1)
    return jnp.cos(ang), jnp.sin(ang)


def apply_rope(x, cos, sin):
    xf = x.astype(jnp.float32).reshape(x.shape[:-1] + (HEAD_DIM // 2, 2))
    x0, x1 = xf[..., 0], xf[..., 1]
    out = jnp.stack([x0 * cos - x1 * sin, x0 * sin + x1 * cos], axis=-1)
    return out.reshape(x.shape).astype(x.dtype)


def na_window_indices(rows):
    kh = min(NA_WIN_H, rows)
    r = jnp.arange(rows)
    col = jnp.arange(GRID_W)
    r_start = jnp.clip(r - kh // 2, 0, rows - kh)
    row_idx = r_start[:, None] + jnp.arange(kh)[None, :]
    c_start = jnp.clip(col - NA_WIN_W // 2, 0, GRID_W - NA_WIN_W)
    in_win = (col[None, :] >= c_start[:, None]) & (col[None, :] < c_start[:, None] + NA_WIN_W)
    dr_idx = row_idx - r[:, None] + (NA_WIN_H - 1)
    dc = col[None, :] - col[:, None]
    dc_idx = jnp.clip(dc, -(NA_WIN_W - 1), NA_WIN_W - 1) + (NA_WIN_W - 1)
    return row_idx, in_win, dr_idx, dc_idx


def na_bias(rpb, in_win, dr_idx, dc_idx):
    b = rpb.astype(jnp.float32)[:, dr_idx[:, None, :, None], dc_idx[None, :, None, :]]
    return jnp.where(in_win[None, None, :, None, :], b, -jnp.inf)


def na_latent(q, k, v, k_ctx, v_ctx, bias, row_idx):
    b, h, s, d = q.shape
    rows = s // GRID_W
    kh = row_idx.shape[1]
    qg = q.reshape(b, h, rows, GRID_W, d)
    kg = k.reshape(b, h, rows, GRID_W, d)[:, :, row_idx]
    vg = v.reshape(b, h, rows, GRID_W, d)[:, :, row_idx]
    s_win = jnp.einsum('bhrwd,bhrjud->bhrwju', qg, kg).astype(jnp.float32) * SCALE + bias[None]
    s_win = s_win.reshape(b, h, rows, GRID_W, kh * GRID_W)
    s_ctx = jnp.einsum('bhrwd,bhld->bhrwl', qg, k_ctx).astype(jnp.float32) * SCALE
    p = jax.nn.softmax(jnp.concatenate([s_win, s_ctx], axis=-1), axis=-1).astype(v.dtype)
    p_win = p[..., :kh * GRID_W].reshape(b, h, rows, GRID_W, kh, GRID_W)
    p_ctx = p[..., kh * GRID_W:]
    o = jnp.einsum('bhrwju,bhrjud->bhrwd', p_win, vg) + jnp.einsum('bhrwl,bhld->bhrwd', p_ctx, v_ctx)
    return o.reshape(b, h, s, d)


def softmax_attn(q, k, v):
    s = jnp.einsum('bgrqd,bgkd->bgrqk', q, k).astype(jnp.float32) * SCALE
    p = jax.nn.softmax(s, axis=-1).astype(v.dtype)
    return jnp.einsum('bgrqk,bgkd->bgrqd', p, v)


def gqa_latent(q, k_all, v_all):
    b, g, r, s, d = q.shape
    nb = s // Q_BLOCK
    qb = q.reshape(b, g, r, nb, Q_BLOCK, d).transpose(3, 0, 1, 2, 4, 5)
    o = lax.map(lambda qblk: softmax_attn(qblk, k_all, v_all), qb)
    return o.transpose(1, 2, 3, 0, 4, 5).reshape(b, g, r, s, d)


def branch_merge(ya, yb, ga, gb, w_pa, w_pb, w_o):
    merged = jax.nn.sigmoid(ga) * (ya @ w_pa) + jax.nn.sigmoid(gb) * (yb @ w_pb)
    return merged @ w_o


def swiglu(h, w_in, w_out):
    a, u = jnp.split(h @ w_in, 2, axis=-1)
    return (jax.nn.silu(a) * u) @ w_out


def _fwd_setup_inputs(seed: int = 0) -> dict:
    key = jax.random.key(seed)
    ks = jax.random.split(key, 18)
    f32 = jnp.float32
    nrm = lambda k, shp, sc: jax.random.normal(k, shp, f32) * sc
    return {
        "x": nrm(ks[0], (BATCH, SEQ, D_MODEL), 1.0),
        "c": nrm(ks[1], (BATCH, D_MODEL), 1.0),
        "ctx": nrm(ks[2], (BATCH, CTX_LEN, D_MODEL), 1.0),
        "c_ctx": nrm(ks[3], (D_MODEL,), 1.0),
        "w_mod": nrm(ks[4], (DEPTH, D_MODEL, 6 * D_MODEL), D_MODEL ** -0.5),
        "b_mod": nrm(ks[5], (DEPTH, 6 * D_MODEL), 0.01),
        "norm1": 1.0 + nrm(ks[6], (DEPTH, D_MODEL), 0.02),
        "w_in": nrm(ks[7], (DEPTH, D_MODEL, IN_COLS), D_MODEL ** -0.5),
        "na_rpb": nrm(ks[8], (DEPTH, NA_HEADS, 2 * NA_WIN_H - 1, 2 * NA_WIN_W - 1), 0.1),
        "q_gain": 1.0 + nrm(ks[9], (DEPTH, HEAD_DIM), 0.02),
        "k_gain": 1.0 + nrm(ks[10], (DEPTH, HEAD_DIM), 0.02),
        "w_pa": nrm(ks[11], (DEPTH, NA_WIDTH, D_MODEL), NA_WIDTH ** -0.5),
        "w_pb": nrm(ks[12], (DEPTH, GQA_Q_WIDTH, D_MODEL), GQA_Q_WIDTH ** -0.5),
        "w_o": nrm(ks[13], (DEPTH, D_MODEL, D_MODEL), D_MODEL ** -0.5),
        "norm2": 1.0 + nrm(ks[14], (DEPTH, D_MODEL), 0.02),
        "w_ffn_in": nrm(ks[15], (DEPTH, D_MODEL, 2 * D_FF), D_MODEL ** -0.5),
        "w_ffn_out": nrm(ks[16], (DEPTH, D_FF, D_MODEL), D_FF ** -0.5),
        "final_norm": 1.0 + nrm(ks[17], (D_MODEL,), 0.02),
    }


def _fwd_reference(x, c, ctx, c_ctx, w_mod, b_mod, norm1, w_in, na_rpb, q_gain, k_gain, w_pa, w_pb, w_o,
              norm2, w_ffn_in, w_ffn_out, final_norm):
    b, s, _ = x.shape
    n_ctx = ctx.shape[1]
    rows = s // GRID_W
    cos, sin = axial_rope_tables(s)
    row_idx, in_win, dr_idx, dc_idx = na_window_indices(rows)
    silu_c = jax.nn.silu(c)
    silu_cc = jax.nn.silu(c_ctx)

    for l in range(DEPTH):
        last = l == DEPTH - 1
        mod = silu_c @ w_mod[l] + b_mod[l]
        mod_c = silu_cc @ w_mod[l] + b_mod[l]
        sh1, sc1, g1, sh2, sc2, g2 = jnp.split(mod[:, None, :], 6, axis=-1)
        csh1, csc1, cg1, csh2, csc2, cg2 = jnp.split(mod_c, 6)

        h = modulate(rms_norm(x, norm1[l]), sh1, sc1)
        hc = modulate(rms_norm(ctx, norm1[l]), csh1, csc1)
        na_q, na_k, na_v, gq, gk, gv, ga, gb = jnp.split(h @ w_in[l], IN_SPLITS, axis=-1)
        na_qc, na_kc, na_vc, gqc, gkc, gvc, gac, gbc = jnp.split(hc @ w_in[l], IN_SPLITS, axis=-1)

        qa, ka, va = heads(na_q, NA_HEADS), heads(na_k, NA_HEADS), heads(na_v, NA_HEADS)
        qa_c, ka_c, va_c = heads(na_qc, NA_HEADS), heads(na_kc, NA_HEADS), heads(na_vc, NA_HEADS)
        bias = na_bias(na_rpb[l], in_win, dr_idx, dc_idx)
        ya = merge_heads(na_latent(qa, ka, va, ka_c, va_c, bias, row_idx))

        qb = apply_rope(rms_norm(heads(gq, GQA_Q_HEADS), q_gain[l]), cos, sin)
        kb = apply_rope(rms_norm(heads(gk, GQA_KV_HEADS), k_gain[l]), cos, sin)
        vb = heads(gv, GQA_KV_HEADS)
        qb_c = rms_norm(heads(gqc, GQA_Q_HEADS), q_gain[l])
        kb_c = rms_norm(heads(gkc, GQA_KV_HEADS), k_gain[l])
        vb_c = heads(gvc, GQA_KV_HEADS)
        k_all = jnp.concatenate([kb, kb_c], axis=2)
        v_all = jnp.concatenate([vb, vb_c], axis=2)
        ob = gqa_latent(qb.reshape(b, GQA_KV_HEADS, GQA_REP, s, HEAD_DIM), k_all, v_all)
        yb = merge_heads(ob.reshape(b, GQA_Q_HEADS, s, HEAD_DIM))

        x = x + g1 * branch_merge(ya, yb, ga, gb, w_pa[l], w_pb[l], w_o[l])

        if not last:
            ya_c = merge_heads(softmax_attn(qa_c[:, :, None], ka_c, va_c)[:, :, 0])
            ob_c = softmax_attn(qb_c.reshape(b, GQA_KV_HEADS, GQA_REP, n_ctx, HEAD_DIM), kb_c, vb_c)
            yb_c = merge_heads(ob_c.reshape(b, GQA_Q_HEADS, n_ctx, HEAD_DIM))
            ctx = ctx + cg1 * branch_merge(ya_c, yb_c, gac, gbc, w_pa[l], w_pb[l], w_o[l])

        h2 = modulate(rms_norm(x, norm2[l]), sh2, sc2)
        x = x + g2 * swiglu(h2, w_ffn_in[l], w_ffn_out[l])
        if not last:
            hc2 = modulate(rms_norm(ctx, norm2[l]), csh2, csc2)
            ctx = ctx + cg2 * swiglu(hc2, w_ffn_in[l], w_ffn_out[l])

    return rms_norm(x, final_norm)


import jax as _jax
import jax.numpy as _jnp

TWIN_FORMAT = 'train_step'
FWD_PARAMS = ['x', 'c', 'ctx', 'c_ctx', 'w_mod', 'b_mod', 'norm1', 'w_in', 'na_rpb', 'q_gain', 'k_gain', 'w_pa', 'w_pb', 'w_o', 'norm2', 'w_ffn_in', 'w_ffn_out', 'final_norm']
TWIN_WEIGHTS = ['c_ctx', 'w_mod', 'b_mod', 'norm1', 'w_in', 'na_rpb', 'q_gain', 'k_gain', 'w_pa', 'w_pb', 'w_o', 'norm2', 'w_ffn_in', 'w_ffn_out', 'final_norm']
TWIN_DIFF_INPUT = 'x'
TWIN_INPUTS = ['x', 'c', 'ctx', 'c_ctx', 'w_mod', 'b_mod', 'norm1', 'w_in', 'na_rpb', 'q_gain', 'k_gain', 'w_pa', 'w_pb', 'w_o', 'norm2', 'w_ffn_in', 'w_ffn_out', 'final_norm', 'loss_target', 'm_c_ctx', 'm_w_mod', 'm_b_mod', 'm_norm1', 'm_w_in', 'm_na_rpb', 'm_q_gain', 'm_k_gain', 'm_w_pa', 'm_w_pb', 'm_w_o', 'm_norm2', 'm_w_ffn_in', 'm_w_ffn_out', 'm_final_norm', 'v_c_ctx', 'v_w_mod', 'v_b_mod', 'v_norm1', 'v_w_in', 'v_na_rpb', 'v_q_gain', 'v_k_gain', 'v_w_pa', 'v_w_pb', 'v_w_o', 'v_norm2', 'v_w_ffn_in', 'v_w_ffn_out', 'v_final_norm']
TWIN_OUTPUTS = ['loss', 'grad_x', 'grad_c_ctx', 'grad_w_mod', 'grad_b_mod', 'grad_norm1', 'grad_w_in', 'grad_na_rpb', 'grad_q_gain', 'grad_k_gain', 'grad_w_pa', 'grad_w_pb', 'grad_w_o', 'grad_norm2', 'grad_w_ffn_in', 'grad_w_ffn_out', 'grad_final_norm', 'delta_c_ctx', 'delta_w_mod', 'delta_b_mod', 'delta_norm1', 'delta_w_in', 'delta_na_rpb', 'delta_q_gain', 'delta_k_gain', 'delta_w_pa', 'delta_w_pb', 'delta_w_o', 'delta_norm2', 'delta_w_ffn_in', 'delta_w_ffn_out', 'delta_final_norm', 'new_m_c_ctx', 'new_m_w_mod', 'new_m_b_mod', 'new_m_norm1', 'new_m_w_in', 'new_m_na_rpb', 'new_m_q_gain', 'new_m_k_gain', 'new_m_w_pa', 'new_m_w_pb', 'new_m_w_o', 'new_m_norm2', 'new_m_w_ffn_in', 'new_m_w_ffn_out', 'new_m_final_norm', 'new_v_c_ctx', 'new_v_w_mod', 'new_v_b_mod', 'new_v_norm1', 'new_v_w_in', 'new_v_na_rpb', 'new_v_q_gain', 'new_v_k_gain', 'new_v_w_pa', 'new_v_w_pb', 'new_v_w_o', 'new_v_norm2', 'new_v_w_ffn_in', 'new_v_w_ffn_out', 'new_v_final_norm']
TWIN_LEAF_KINDS = {'loss': 'loss', 'grad_x': 'grad_x', 'grad_c_ctx': 'grad_w', 'grad_w_mod': 'grad_w', 'grad_b_mod': 'grad_w', 'grad_norm1': 'grad_w', 'grad_w_in': 'grad_w', 'grad_na_rpb': 'grad_w', 'grad_q_gain': 'grad_w', 'grad_k_gain': 'grad_w', 'grad_w_pa': 'grad_w', 'grad_w_pb': 'grad_w', 'grad_w_o': 'grad_w', 'grad_norm2': 'grad_w', 'grad_w_ffn_in': 'grad_w', 'grad_w_ffn_out': 'grad_w', 'grad_final_norm': 'grad_w', 'delta_c_ctx': 'delta_w', 'delta_w_mod': 'delta_w', 'delta_b_mod': 'delta_w', 'delta_norm1': 'delta_w', 'delta_w_in': 'delta_w', 'delta_na_rpb': 'delta_w', 'delta_q_gain': 'delta_w', 'delta_k_gain': 'delta_w', 'delta_w_pa': 'delta_w', 'delta_w_pb': 'delta_w', 'delta_w_o': 'delta_w', 'delta_norm2': 'delta_w', 'delta_w_ffn_in': 'delta_w', 'delta_w_ffn_out': 'delta_w', 'delta_final_norm': 'delta_w', 'new_m_c_ctx': 'new_m', 'new_m_w_mod': 'new_m', 'new_m_b_mod': 'new_m', 'new_m_norm1': 'new_m', 'new_m_w_in': 'new_m', 'new_m_na_rpb': 'new_m', 'new_m_q_gain': 'new_m', 'new_m_k_gain': 'new_m', 'new_m_w_pa': 'new_m', 'new_m_w_pb': 'new_m', 'new_m_w_o': 'new_m', 'new_m_norm2': 'new_m', 'new_m_w_ffn_in': 'new_m', 'new_m_w_ffn_out': 'new_m', 'new_m_final_norm': 'new_m', 'new_v_c_ctx': 'new_v', 'new_v_w_mod': 'new_v', 'new_v_b_mod': 'new_v', 'new_v_norm1': 'new_v', 'new_v_w_in': 'new_v', 'new_v_na_rpb': 'new_v', 'new_v_q_gain': 'new_v', 'new_v_k_gain': 'new_v', 'new_v_w_pa': 'new_v', 'new_v_w_pb': 'new_v', 'new_v_w_o': 'new_v', 'new_v_norm2': 'new_v', 'new_v_w_ffn_in': 'new_v', 'new_v_w_ffn_out': 'new_v', 'new_v_final_norm': 'new_v'}


def _forward(args):
    return _fwd_reference(*[args[k] for k in FWD_PARAMS])


def _output_shape():
    def fwd():
        inp = _fwd_setup_inputs(0)
        return _fwd_reference(*[inp[k] for k in FWD_PARAMS])
    out = _jax.eval_shape(fwd)
    return out.shape, out.dtype

N_MICROBATCH = 1
ADAM_LR = 0.001
ADAM_B1 = 0.9
ADAM_B2 = 0.999
ADAM_EPS = 1e-08
ADAM_WD = 0.01
ADAM_STEP = 10
PER_EXAMPLE_BATCH_AXIS = {'x': 0, 'c': 0, 'ctx': 0, 'loss_target': 0}
SHARED_INPUTS = []
_WEIGHT_DTYPES = {'c_ctx': _jnp.float32, 'w_mod': _jnp.float32, 'b_mod': _jnp.float32, 'norm1': _jnp.float32, 'w_in': _jnp.float32, 'na_rpb': _jnp.float32, 'q_gain': _jnp.float32, 'k_gain': _jnp.float32, 'w_pa': _jnp.float32, 'w_pb': _jnp.float32, 'w_o': _jnp.float32, 'norm2': _jnp.float32, 'w_ffn_in': _jnp.float32, 'w_ffn_out': _jnp.float32, 'final_norm': _jnp.float32}
MOMENT_SCALE = {'c_ctx': 1.181959e-01, 'w_mod': 8.704273e-02, 'b_mod': 1.522124e-01, 'norm1': 5.416129e-02, 'w_in': 4.246752e-02, 'na_rpb': 4.560813e-03, 'q_gain': 4.281469e-02, 'k_gain': 4.449012e-02, 'w_pa': 4.753237e-02, 'w_pb': 6.189824e-02, 'w_o': 7.313719e-02, 'norm2': 1.583859e-01, 'w_ffn_in': 7.561179e-02, 'w_ffn_out': 1.240205e-01, 'final_norm': 6.487519e+01}


def _to_microbatches(a, axis):
    t = _jnp.moveaxis(a, axis, 0)
    t = t.reshape((N_MICROBATCH, t.shape[0] // N_MICROBATCH) + t.shape[1:])
    return _jnp.moveaxis(t, 1, axis + 1)


def setup_inputs(seed: int = 0) -> dict:
    inp = _fwd_setup_inputs(seed)
    key = _jax.random.fold_in(_jax.random.key(seed), 7919)
    shape, _ = _output_shape()
    out = dict(inp)
    out["loss_target"] = _jax.random.normal(_jax.random.fold_in(key, 0), shape, _jnp.float32)
    for i, name in enumerate(TWIN_WEIGHTS):
        w = inp[name].astype(_jnp.float32)
        if MOMENT_SCALE is None:
            s = _jnp.sqrt(_jnp.mean(_jnp.square(w)) + 1e-30)
        else:
            s = MOMENT_SCALE[name]
        km, kv = _jax.random.split(_jax.random.fold_in(key, i + 1))
        out[name] = w
        out["m_" + name] = s * _jax.random.normal(km, w.shape, _jnp.float32)
        out["v_" + name] = (s * s) * _jax.random.uniform(kv, w.shape, _jnp.float32, 0.5, 1.5)
    if N_MICROBATCH > 1:
        for name, axis in PER_EXAMPLE_BATCH_AXIS.items():
            out[name] = _to_microbatches(out[name], axis)
    return {'x': out['x'], 'c': out['c'], 'ctx': out['ctx'], 'c_ctx': out['c_ctx'], 'w_mod': out['w_mod'], 'b_mod': out['b_mod'], 'norm1': out['norm1'], 'w_in': out['w_in'], 'na_rpb': out['na_rpb'], 'q_gain': out['q_gain'], 'k_gain': out['k_gain'], 'w_pa': out['w_pa'], 'w_pb': out['w_pb'], 'w_o': out['w_o'], 'norm2': out['norm2'], 'w_ffn_in': out['w_ffn_in'], 'w_ffn_out': out['w_ffn_out'], 'final_norm': out['final_norm'], 'loss_target': out['loss_target'], 'm_c_ctx': out['m_c_ctx'], 'm_w_mod': out['m_w_mod'], 'm_b_mod': out['m_b_mod'], 'm_norm1': out['m_norm1'], 'm_w_in': out['m_w_in'], 'm_na_rpb': out['m_na_rpb'], 'm_q_gain': out['m_q_gain'], 'm_k_gain': out['m_k_gain'], 'm_w_pa': out['m_w_pa'], 'm_w_pb': out['m_w_pb'], 'm_w_o': out['m_w_o'], 'm_norm2': out['m_norm2'], 'm_w_ffn_in': out['m_w_ffn_in'], 'm_w_ffn_out': out['m_w_ffn_out'], 'm_final_norm': out['m_final_norm'], 'v_c_ctx': out['v_c_ctx'], 'v_w_mod': out['v_w_mod'], 'v_b_mod': out['v_b_mod'], 'v_norm1': out['v_norm1'], 'v_w_in': out['v_w_in'], 'v_na_rpb': out['v_na_rpb'], 'v_q_gain': out['v_q_gain'], 'v_k_gain': out['v_k_gain'], 'v_w_pa': out['v_w_pa'], 'v_w_pb': out['v_w_pb'], 'v_w_o': out['v_w_o'], 'v_norm2': out['v_norm2'], 'v_w_ffn_in': out['v_w_ffn_in'], 'v_w_ffn_out': out['v_w_ffn_out'], 'v_final_norm': out['v_final_norm']}


def _loss(weights, diff, rest, loss_target):
    with _jax.named_scope("forward"):
        args = {**rest, TWIN_DIFF_INPUT: diff, **{k: w.astype(_WEIGHT_DTYPES[k]) for k, w in weights.items()}}
        y = _forward(args)
    with _jax.named_scope("loss_head"):
        err = _jnp.square(y.astype(_jnp.float32) - loss_target)
        return 0.5 * _jnp.sum(_jnp.mean(err, axis=-1)) if err.ndim else 0.5 * err


def _adamw(w, g, m, v):
    m = ADAM_B1 * m + (1.0 - ADAM_B1) * g
    v = ADAM_B2 * v + (1.0 - ADAM_B2) * _jnp.square(g)
    m_hat = m / (1.0 - ADAM_B1 ** ADAM_STEP)
    v_hat = v / (1.0 - ADAM_B2 ** ADAM_STEP)
    delta = -ADAM_LR * (m_hat / (_jnp.sqrt(v_hat) + ADAM_EPS) + ADAM_WD * w)
    return delta, m, v


def reference(x, c, ctx, c_ctx, w_mod, b_mod, norm1, w_in, na_rpb, q_gain, k_gain, w_pa, w_pb, w_o, norm2, w_ffn_in, w_ffn_out, final_norm, loss_target, m_c_ctx, m_w_mod, m_b_mod, m_norm1, m_w_in, m_na_rpb, m_q_gain, m_k_gain, m_w_pa, m_w_pb, m_w_o, m_norm2, m_w_ffn_in, m_w_ffn_out, m_final_norm, v_c_ctx, v_w_mod, v_b_mod, v_norm1, v_w_in, v_na_rpb, v_q_gain, v_k_gain, v_w_pa, v_w_pb, v_w_o, v_norm2, v_w_ffn_in, v_w_ffn_out, v_final_norm):
    given = dict(x=x, c=c, ctx=ctx, c_ctx=c_ctx, w_mod=w_mod, b_mod=b_mod, norm1=norm1, w_in=w_in, na_rpb=na_rpb, q_gain=q_gain, k_gain=k_gain, w_pa=w_pa, w_pb=w_pb, w_o=w_o, norm2=norm2, w_ffn_in=w_ffn_in, w_ffn_out=w_ffn_out, final_norm=final_norm, loss_target=loss_target, m_c_ctx=m_c_ctx, m_w_mod=m_w_mod, m_b_mod=m_b_mod, m_norm1=m_norm1, m_w_in=m_w_in, m_na_rpb=m_na_rpb, m_q_gain=m_q_gain, m_k_gain=m_k_gain, m_w_pa=m_w_pa, m_w_pb=m_w_pb, m_w_o=m_w_o, m_norm2=m_norm2, m_w_ffn_in=m_w_ffn_in, m_w_ffn_out=m_w_ffn_out, m_final_norm=m_final_norm, v_c_ctx=v_c_ctx, v_w_mod=v_w_mod, v_b_mod=v_b_mod, v_norm1=v_norm1, v_w_in=v_w_in, v_na_rpb=v_na_rpb, v_q_gain=v_q_gain, v_k_gain=v_k_gain, v_w_pa=v_w_pa, v_w_pb=v_w_pb, v_w_o=v_w_o, v_norm2=v_norm2, v_w_ffn_in=v_w_ffn_in, v_w_ffn_out=v_w_ffn_out, v_final_norm=v_final_norm)
    weights = {n: given[n] for n in TWIN_WEIGHTS}
    shared = {n: given[n] for n in SHARED_INPUTS}
    per_example = {n: given[n] for n in ['x', 'c', 'ctx']}
    grad_fn = _jax.value_and_grad(_loss, argnums=(0, 1))

    def one_microbatch(ex, loss_target):
        ex = dict(ex)
        diff = ex.pop(TWIN_DIFF_INPUT)
        return grad_fn(weights, diff, {**shared, **ex}, loss_target)

    if N_MICROBATCH == 1:
        loss, (grad_w, grad_x) = one_microbatch(per_example, given["loss_target"])
    else:
        def body(carry, xs):
            loss_sum, grad_sum = carry
            l_k, (gw_k, gx_k) = one_microbatch(xs[0], xs[1])
            with _jax.named_scope("update"):
                return (loss_sum + l_k, _jax.tree.map(_jnp.add, grad_sum, gw_k)), gx_k

        init = (_jnp.zeros((), _jnp.float32), _jax.tree.map(_jnp.zeros_like, weights))
        (loss, grad_w), grad_x = _jax.lax.scan(body, init, (per_example, given["loss_target"]))
    with _jax.named_scope("update"):
        delta_w, new_m, new_v = {}, {}, {}
        for n in TWIN_WEIGHTS:
            delta_w[n], new_m[n], new_v[n] = _adamw(weights[n], grad_w[n], given["m_" + n], given["v_" + n])
    return (loss, grad_x, *[grad_w[n] for n in TWIN_WEIGHTS], *[delta_w[n] for n in TWIN_WEIGHTS],
            *[new_m[n] for n in TWIN_WEIGHTS], *[new_v[n] for n in TWIN_WEIGHTS])
```

```python
import functools

import numpy as np
import jax
import jax.numpy as jnp
from jax import lax
from jax.experimental import pallas as pl
from jax.experimental.pallas import tpu as pltpu

F32 = jnp.float32
BF16 = jnp.bfloat16
HIGHEST = lax.Precision.HIGHEST

D_MODEL = 1024
DEPTH = 4
GRID_W = 64
HEAD_DIM = 64
NA_HEADS = 8
NA_WIN_H = 8
NA_WIN_W = 16
GQA_Q_HEADS = 8
GQA_KV_HEADS = 2
GQA_REP = GQA_Q_HEADS // GQA_KV_HEADS
NA_WIDTH = NA_HEADS * HEAD_DIM
GQA_Q_WIDTH = GQA_Q_HEADS * HEAD_DIM
GQA_KV_WIDTH = GQA_KV_HEADS * HEAD_DIM
IN_SIZES = (NA_WIDTH, NA_WIDTH, NA_WIDTH, GQA_Q_WIDTH, GQA_KV_WIDTH, GQA_KV_WIDTH, D_MODEL, D_MODEL)
IN_SPLITS = tuple(int(v) for v in np.cumsum(IN_SIZES)[:-1])
ROPE_THETA = 10000.0
EPS = 1e-6
SCALE = HEAD_DIM ** -0.5
ADAM_LR = 0.001
ADAM_B1 = 0.9
ADAM_B2 = 0.999
ADAM_EPS = 1e-08
ADAM_WD = 0.01
ADAM_STEP = 10

N_DEV = 8
MESH = pl.DeviceIdType.MESH
NEG_BIG = -1e30

VMEM_BIG_LIMIT = 52 * 1024 * 1024
NA_QROWS = 4
NA_QB = NA_QROWS * GRID_W
NA_KROWS = 12
NA_KW = NA_KROWS * GRID_W
MM_TILE_ELEMS = 512 * 2176
MOD_ROWS = 256

BIG = ("w_mod", "w_in", "w_pa", "w_pb", "w_o", "w_ffn_in", "w_ffn_out")
ROW_SHARDED = ("w_o", "w_ffn_out")
SMALL = ("c_ctx", "b_mod", "norm1", "na_rpb", "q_gain", "k_gain", "norm2", "final_norm")
WEIGHTS = ("c_ctx", "w_mod", "b_mod", "norm1", "w_in", "na_rpb", "q_gain", "k_gain", "w_pa", "w_pb", "w_o",
           "norm2", "w_ffn_in", "w_ffn_out", "final_norm")


def _pick(n, cands):
    for c in cands:
        if n % c == 0:
            return c
    return n


def _dot_nt(a, b):
    return lax.dot_general(a, b, (((1,), (1,)), ((), ())), preferred_element_type=F32)


def _dot_tn(a, b):
    return lax.dot_general(a, b, (((0,), (0,)), ((), ())), preferred_element_type=F32)


def _dot(a, b):
    return jnp.dot(a, b, preferred_element_type=F32)


def _mm_call(a, b, name):
    m, k = a.shape
    _, n = b.shape
    tn = _pick(n, (2176, 1408, 1024, 512, 256, 128))
    tm = _pick(m, tuple(t for t in (768, 512, 256, 128) if t * tn <= MM_TILE_ELEMS))
    tk = _pick(k, (1024, 768, 512, 256, 128))
    nk = k // tk

    def body(a_ref, b_ref, o_ref, acc_ref):
        kk = pl.program_id(2)
        part = _dot(a_ref[...].astype(BF16), b_ref[...].astype(BF16))

        @pl.when(kk == 0)
        def _():
            acc_ref[...] = part

        @pl.when(kk > 0)
        def _():
            acc_ref[...] += part

        @pl.when(kk == nk - 1)
        def _():
            o_ref[...] = acc_ref[...]

    def body1(a_ref, b_ref, o_ref):
        o_ref[...] = _dot(a_ref[...].astype(BF16), b_ref[...].astype(BF16))

    footprint = 2 * (tm * tk * a.dtype.itemsize + tk * tn * b.dtype.itemsize + tm * tn * 4) + 2 * tm * tn * 4
    limit = int(footprint + (8 << 20))
    if nk == 1:
        return pl.pallas_call(
            body1, grid=(n // tn, m // tm),
            in_specs=[pl.BlockSpec((tm, tk), lambda j, i: (i, 0)), pl.BlockSpec((tk, tn), lambda j, i: (0, j))],
            out_specs=pl.BlockSpec((tm, tn), lambda j, i: (i, j)),
            out_shape=jax.ShapeDtypeStruct((m, n), F32),
            compiler_params=pltpu.CompilerParams(dimension_semantics=("parallel", "parallel"), vmem_limit_bytes=limit),
            name=name,
        )(a, b)
    return pl.pallas_call(
        body, grid=(m // tm, n // tn, nk),
        in_specs=[pl.BlockSpec((tm, tk), lambda i, j, kk: (i, kk)), pl.BlockSpec((tk, tn), lambda i, j, kk: (kk, j))],
        out_specs=pl.BlockSpec((tm, tn), lambda i, j, kk: (i, j)),
        out_shape=jax.ShapeDtypeStruct((m, n), F32),
        scratch_shapes=[pltpu.VMEM((tm, tn), F32)],
        compiler_params=pltpu.CompilerParams(
            dimension_semantics=("parallel", "parallel", "arbitrary"), vmem_limit_bytes=limit),
        name=name,
    )(a, b)


@jax.custom_vjp
def mm(x, w):
    return _mm_call(x.astype(BF16), w.astype(BF16), "mm_fwd")


def _mm_fwd(x, w):
    xb, wb = x.astype(BF16), w.astype(BF16)
    return _mm_call(xb, wb, "mm_fwd"), (xb, wb)


def _mm_bwd(res, dy):
    xb, wb = res
    dyb = dy.astype(BF16)
    dx = _mm_call(dyb, wb.T, "mm_dx")
    dw = _mm_call(xb.T, dyb, "mm_dw")
    return dx, dw


mm.defvjp(_mm_fwd, _mm_bwd)


def _seg_rows(ref, is_ctx):
    return jnp.where(is_ctx, ref[1:2, :], ref[0:1, :])


def _one_hot_row(second):
    return (lax.broadcasted_iota(jnp.int32, (2, 1), 0) == second.astype(jnp.int32)).astype(F32)


def _norm_mod_fwd_call(x, g, shift, scale, n_lat):
    t, d = x.shape
    tt = _pick(t, (256, 128))
    lat_tiles = n_lat // tt

    def body(x_ref, g_ref, sh_ref, sc_ref, y_ref):
        is_ctx = pl.program_id(0) >= lat_tiles
        xv = x_ref[...]
        r = lax.rsqrt(jnp.mean(xv * xv, axis=-1, keepdims=True) + EPS)
        yn = xv * r * g_ref[...]
        y_ref[...] = yn * (1.0 + _seg_rows(sc_ref, is_ctx)) + _seg_rows(sh_ref, is_ctx)

    row = pl.BlockSpec((tt, d), lambda i: (i, 0))
    full = lambda rws: pl.BlockSpec((rws, d), lambda i: (0, 0))
    return pl.pallas_call(
        body, grid=(t // tt,), in_specs=[row, full(1), full(2), full(2)], out_specs=row,
        out_shape=jax.ShapeDtypeStruct((t, d), F32),
        compiler_params=pltpu.CompilerParams(dimension_semantics=("parallel",)), name="norm_mod_fwd",
    )(x, g, shift, scale)


def _norm_mod_bwd_call(x, g, scale, dy, n_lat):
    t, d = x.shape
    tt = _pick(t, (256, 128))
    lat_tiles = n_lat // tt

    def body(x_ref, g_ref, sc_ref, dy_ref, dx_ref, dg_ref, dsh_ref, dsc_ref):
        i = pl.program_id(0)
        is_ctx = i >= lat_tiles

        @pl.when(i == 0)
        def _():
            dg_ref[...] = jnp.zeros_like(dg_ref)
            dsh_ref[...] = jnp.zeros_like(dsh_ref)
            dsc_ref[...] = jnp.zeros_like(dsc_ref)

        xv, dyv, gv = x_ref[...], dy_ref[...], g_ref[...]
        r = lax.rsqrt(jnp.mean(xv * xv, axis=-1, keepdims=True) + EPS)
        nrm = xv * r
        yn = nrm * gv
        seg = _one_hot_row(is_ctx)
        dsh_ref[...] += seg * jnp.sum(dyv, axis=0, keepdims=True)
        dsc_ref[...] += seg * jnp.sum(dyv * yn, axis=0, keepdims=True)
        dyn = dyv * (1.0 + _seg_rows(sc_ref, is_ctx))
        dg_ref[...] += jnp.sum(dyn * nrm, axis=0, keepdims=True)
        dn = dyn * gv
        dx_ref[...] = r * (dn - nrm * jnp.mean(dn * nrm, axis=-1, keepdims=True))

    row = pl.BlockSpec((tt, d), lambda i: (i, 0))
    full = lambda rws: pl.BlockSpec((rws, d), lambda i: (0, 0))
    return pl.pallas_call(
        body, grid=(t // tt,), in_specs=[row, full(1), full(2), row], out_specs=[row, full(1), full(2), full(2)],
        out_shape=[jax.ShapeDtypeStruct((t, d), F32), jax.ShapeDtypeStruct((1, d), F32),
                   jax.ShapeDtypeStruct((2, d), F32), jax.ShapeDtypeStruct((2, d), F32)],
        compiler_params=pltpu.CompilerParams(dimension_semantics=("arbitrary",)), name="norm_mod_bwd",
    )(x, g, scale, dy)


@functools.partial(jax.custom_vjp, nondiff_argnums=(4,))
def norm_mod(x, g, shift, scale, n_lat):
    return _norm_mod_fwd_call(x, g, shift, scale, n_lat)


def _norm_mod_fwd(x, g, shift, scale, n_lat):
    return _norm_mod_fwd_call(x, g, shift, scale, n_lat), (x, g, scale)


def _norm_mod_bwd(n_lat, res, dy):
    x, g, scale = res
    dx, dg, dsh, dsc = _norm_mod_bwd_call(x, g, scale, dy, n_lat)
    return dx, dg, dsh, dsc


norm_mod.defvjp(_norm_mod_fwd, _norm_mod_bwd)


def _rot_matrix():
    p = np.zeros((HEAD_DIM, HEAD_DIM), np.float32)
    for i in range(HEAD_DIM // 2):
        p[2 * i + 1, 2 * i] = -1.0
        p[2 * i, 2 * i + 1] = 1.0
    return jnp.asarray(p)


def _hn_rope_fwd_call(x, gains, cos2, sin2, rot, n_q):
    nh, t, dh = x.shape
    tt = _pick(t, (2816, 1024, 768, 512, 256, 128))

    def body(x_ref, g_ref, cos_ref, sin_ref, rot_ref, y_ref):
        gv = jnp.where(pl.program_id(0) >= n_q, g_ref[1:2, :], g_ref[0:1, :])
        xv = x_ref[...]
        r = lax.rsqrt(jnp.mean(xv * xv, axis=-1, keepdims=True) + EPS)
        y = xv * r * gv
        yr = jnp.dot(y, rot_ref[...], precision=HIGHEST, preferred_element_type=F32)
        y_ref[...] = y * cos_ref[...] + yr * sin_ref[...]

    blk = pl.BlockSpec((None, tt, dh), lambda h, i: (h, i, 0))
    tab = pl.BlockSpec((tt, dh), lambda h, i: (i, 0))
    return pl.pallas_call(
        body, grid=(nh, t // tt),
        in_specs=[blk, pl.BlockSpec((2, dh), lambda h, i: (0, 0)), tab, tab, pl.BlockSpec((dh, dh), lambda h, i: (0, 0))],
        out_specs=blk, out_shape=jax.ShapeDtypeStruct((nh, t, dh), F32),
        compiler_params=pltpu.CompilerParams(dimension_semantics=("parallel", "parallel")), name="hn_rope_fwd",
    )(x, gains, cos2, sin2, rot)


def _hn_rope_bwd_call(x, gains, cos2, sin2, rot, dy, n_q):
    nh, t, dh = x.shape
    tt = _pick(t, (2816, 1024, 768, 512, 256, 128))

    def body(x_ref, g_ref, cos_ref, sin_ref, rot_ref, dy_ref, dx_ref, dg_ref):
        h, i = pl.program_id(0), pl.program_id(1)
        is_k = h >= n_q

        @pl.when((h == 0) & (i == 0))
        def _():
            dg_ref[...] = jnp.zeros_like(dg_ref)

        gv = jnp.where(is_k, g_ref[1:2, :], g_ref[0:1, :])
        xv, dyv = x_ref[...], dy_ref[...]
        r = lax.rsqrt(jnp.mean(xv * xv, axis=-1, keepdims=True) + EPS)
        nrm = xv * r
        dyn = dyv * cos_ref[...] - jnp.dot(dyv * sin_ref[...], rot_ref[...], precision=HIGHEST, preferred_element_type=F32)
        seg = _one_hot_row(is_k)
        dg_ref[...] += seg * jnp.sum(dyn * nrm, axis=0, keepdims=True)
        dn = dyn * gv
        dx_ref[...] = r * (dn - nrm * jnp.mean(dn * nrm, axis=-1, keepdims=True))

    blk = pl.BlockSpec((None, tt, dh), lambda h, i: (h, i, 0))
    tab = pl.BlockSpec((tt, dh), lambda h, i: (i, 0))
    g_spec = pl.BlockSpec((2, dh), lambda h, i: (0, 0))
    return pl.pallas_call(
        body, grid=(nh, t // tt),
        in_specs=[blk, g_spec, tab, tab, pl.BlockSpec((dh, dh), lambda h, i: (0, 0)), blk],
        out_specs=[blk, g_spec],
        out_shape=[jax.ShapeDtypeStruct((nh, t, dh), F32), jax.ShapeDtypeStruct((2, dh), F32)],
        compiler_params=pltpu.CompilerParams(dimension_semantics=("arbitrary", "arbitrary")), name="hn_rope_bwd",
    )(x, gains, cos2, sin2, rot, dy)


@functools.partial(jax.custom_vjp, nondiff_argnums=(5,))
def hn_rope(x, gains, cos2, sin2, rot, n_q):
    return _hn_rope_fwd_call(x, gains, cos2, sin2, rot, n_q)


def _hn_rope_fwd(x, gains, cos2, sin2, rot, n_q):
    return _hn_rope_fwd_call(x, gains, cos2, sin2, rot, n_q), (x, gains, cos2, sin2, rot)


def _hn_rope_bwd(n_q, res, dy):
    x, gains, cos2, sin2, rot = res
    dx, dg = _hn_rope_bwd_call(x, gains, cos2, sin2, rot, dy, n_q)
    return dx, dg, jnp.zeros_like(cos2), jnp.zeros_like(sin2), jnp.zeros_like(rot)


hn_rope.defvjp(_hn_rope_fwd, _hn_rope_bwd)


ATT_TK = 256


def _att_tq(rn):
    return _pick(rn, (512, 256, 128))


def _flash_fwd_call(q, k, v):
    g, rn, dh = q.shape
    nk = k.shape[1]
    tq, tk = _att_tq(rn), ATT_TK
    nkb = nk // tk

    def body(q_ref, k_ref, v_ref, o_ref, lse_ref):
        qs = q_ref[...] * SCALE

        def step(j, carry):
            m, l, acc = carry
            off = pl.multiple_of(j * tk, tk)
            kj = k_ref[pl.ds(off, tk), :]
            vj = v_ref[pl.ds(off, tk), :]
            s = _dot_nt(qs, kj)
            m_new = jnp.maximum(m, jnp.max(s, axis=1, keepdims=True))
            p = jnp.exp(s - m_new)
            alpha = jnp.exp(m - m_new)
            l = alpha * l + jnp.sum(p, axis=1, keepdims=True)
            acc = alpha * acc + _dot(p.astype(BF16), vj)
            return m_new, l, acc

        m, l, acc = lax.fori_loop(
            0, nkb, step, (jnp.full((tq, 1), NEG_BIG, F32), jnp.zeros((tq, 1), F32), jnp.zeros((tq, dh), F32)))
        o_ref[...] = acc / l
        lse_ref[...] = m + jnp.log(l)

    return pl.pallas_call(
        body, grid=(g, rn // tq),
        in_specs=[pl.BlockSpec((None, tq, dh), lambda a, i: (a, i, 0)),
                  pl.BlockSpec((None, nk, dh), lambda a, i: (a, 0, 0)),
                  pl.BlockSpec((None, nk, dh), lambda a, i: (a, 0, 0))],
        out_specs=[pl.BlockSpec((None, tq, dh), lambda a, i: (a, i, 0)), pl.BlockSpec((None, tq, 1), lambda a, i: (a, i, 0))],
        out_shape=[jax.ShapeDtypeStruct((g, rn, dh), F32), jax.ShapeDtypeStruct((g, rn, 1), F32)],
        compiler_params=pltpu.CompilerParams(dimension_semantics=("parallel", "parallel"), vmem_limit_bytes=VMEM_BIG_LIMIT),
        name="flash_fwd",
    )(q, k, v)


def _flash_bwd_call(q, q_t, k, k_t, v, do, do_t, o_t, lse_t):
    g, rn, dh = q.shape
    nk = k.shape[1]
    tq, tk = _att_tq(rn), ATT_TK
    nkb = nk // tk

    def body(q_ref, qt_ref, k_ref, kt_ref, v_ref, do_ref, dot_ref, ot_ref, lse_ref, dqt_ref, dk_ref, dv_ref):
        @pl.when(pl.program_id(1) == 0)
        def _():
            dk_ref[...] = jnp.zeros_like(dk_ref)
            dv_ref[...] = jnp.zeros_like(dv_ref)

        qs = q_ref[...] * SCALE
        qst = qt_ref[...] * SCALE
        dov, dotv = do_ref[...], dot_ref[...]
        delta = jnp.sum(dotv.astype(F32) * ot_ref[...], axis=0, keepdims=True)
        lse = lse_ref[...]

        def step(j, dqt):
            off = pl.multiple_of(j * tk, tk)
            kj = k_ref[pl.ds(off, tk), :]
            vj = v_ref[pl.ds(off, tk), :]
            p_t = jnp.exp(_dot(kj, qst) - lse)
            ds_t = (p_t * (_dot(vj, dotv) - delta)).astype(BF16)
            dv_ref[pl.ds(off, tk), :] += _dot(p_t.astype(BF16), dov)
            dk_ref[pl.ds(off, tk), :] += _dot(ds_t, qs)
            return dqt + _dot(kt_ref[j], ds_t)

        dqt_ref[...] = lax.fori_loop(0, nkb, step, jnp.zeros((dh, tq), F32)) * SCALE

    row = pl.BlockSpec((None, tq, dh), lambda a, i: (a, i, 0))
    col = pl.BlockSpec((None, dh, tq), lambda a, i: (a, 0, i))
    kv = pl.BlockSpec((None, nk, dh), lambda a, i: (a, 0, 0))
    return pl.pallas_call(
        body, grid=(g, rn // tq),
        in_specs=[row, col, kv, pl.BlockSpec((None, nkb, dh, tk), lambda a, i: (a, 0, 0, 0)), kv, row, col, col,
                  pl.BlockSpec((None, 1, tq), lambda a, i: (a, 0, i))],
        out_specs=[col, kv, kv],
        out_shape=[jax.ShapeDtypeStruct((g, dh, rn), F32), jax.ShapeDtypeStruct((g, nk, dh), F32),
                   jax.ShapeDtypeStruct((g, nk, dh), F32)],
        compiler_params=pltpu.CompilerParams(dimension_semantics=("arbitrary", "arbitrary"), vmem_limit_bytes=VMEM_BIG_LIMIT),
        name="flash_bwd",
    )(q, q_t, k, k_t, v, do, do_t, o_t, lse_t)


@jax.custom_vjp
def attn(q, k, v):
    return _flash_fwd_call(q.astype(BF16), k.astype(BF16), v.astype(BF16))[0]


def _attn_fwd(q, k, v):
    qb, kb, vb = q.astype(BF16), k.astype(BF16), v.astype(BF16)
    o, lse = _flash_fwd_call(qb, kb, vb)
    return o, (qb, kb, vb, o, lse)


def _attn_bwd(res, do):
    qb, kb, vb, o, lse = res
    g, rn, dh = qb.shape
    nk = kb.shape[1]
    dob = do.astype(BF16)
    k_t = kb.reshape(g, nk // ATT_TK, ATT_TK, dh).transpose(0, 1, 3, 2)
    dq_t, dk, dv = _flash_bwd_call(qb, qb.transpose(0, 2, 1), kb, k_t, vb, dob, dob.transpose(0, 2, 1),
                                   o.transpose(0, 2, 1), lse.reshape(g, 1, rn))
    return dq_t.transpose(0, 2, 1), dk, dv


attn.defvjp(_attn_fwd, _attn_bwd)


def _na_key_row(i, rows):
    return jnp.clip(NA_QROWS * i - NA_WIN_H // 2, 0, rows - NA_KROWS)


def _na_type(i, nb):
    return jnp.where(i == 0, 0, jnp.where(i == nb - 1, 2, 1))


def _na_fwd_call(q, k, v, kc, vc, slab):
    h, s, dh = q.shape
    n_ctx = kc.shape[1]
    rows, nb = s // GRID_W, s // NA_QB

    def body(q_ref, k_ref, v_ref, kc_ref, vc_ref, slab_ref, o_ref, lse_ref):
        off = pl.multiple_of(_na_key_row(pl.program_id(1), rows) * GRID_W, NA_QB)
        qs = q_ref[...] * SCALE
        kw = k_ref[pl.ds(off, NA_KW), :]
        vw = v_ref[pl.ds(off, NA_KW), :]
        sw = _dot_nt(qs, kw) + slab_ref[...]
        sc = _dot_nt(qs, kc_ref[...])
        m = jnp.maximum(jnp.max(sw, axis=1, keepdims=True), jnp.max(sc, axis=1, keepdims=True))
        pw = jnp.exp(sw - m)
        pc = jnp.exp(sc - m)
        l = jnp.sum(pw, axis=1, keepdims=True) + jnp.sum(pc, axis=1, keepdims=True)
        o_ref[...] = (_dot(pw.astype(BF16), vw) + _dot(pc.astype(BF16), vc_ref[...])) / l
        lse_ref[...] = m + jnp.log(l)

    qblk = pl.BlockSpec((None, NA_QB, dh), lambda a, i: (a, i, 0))
    kv = pl.BlockSpec((None, s, dh), lambda a, i: (a, 0, 0))
    ckv = pl.BlockSpec((None, n_ctx, dh), lambda a, i: (a, 0, 0))
    return pl.pallas_call(
        body, grid=(h, nb),
        in_specs=[qblk, kv, kv, ckv, ckv, pl.BlockSpec((None, None, NA_QB, NA_KW), lambda a, i: (_na_type(i, nb), a, 0, 0))],
        out_specs=[qblk, pl.BlockSpec((None, NA_QB, 1), lambda a, i: (a, i, 0))],
        out_shape=[jax.ShapeDtypeStruct((h, s, dh), F32), jax.ShapeDtypeStruct((h, s, 1), F32)],
        compiler_params=pltpu.CompilerParams(dimension_semantics=("parallel", "parallel"), vmem_limit_bytes=VMEM_BIG_LIMIT),
        name="na_fwd",
    )(q, k, v, kc, vc, slab)


def _na_bwd_call(q, k, v, kc, vc, slab, do, o, lse):
    h, s, dh = q.shape
    n_ctx = kc.shape[1]
    rows, nb = s // GRID_W, s // NA_QB

    def body(q_ref, k_ref, v_ref, kc_ref, vc_ref, slab_ref, do_ref, o_ref, lse_ref,
             dq_ref, dk_ref, dv_ref, dkc_ref, dvc_ref, dslab_ref):
        i = pl.program_id(1)

        @pl.when(i == 0)
        def _():
            dk_ref[...] = jnp.zeros_like(dk_ref)
            dv_ref[...] = jnp.zeros_like(dv_ref)
            dkc_ref[...] = jnp.zeros_like(dkc_ref)
            dvc_ref[...] = jnp.zeros_like(dvc_ref)

        @pl.when((i <= 1) | (i == nb - 1))
        def _():
            dslab_ref[...] = jnp.zeros_like(dslab_ref)

        off = pl.multiple_of(_na_key_row(i, rows) * GRID_W, NA_QB)
        qs = q_ref[...] * SCALE
        kw = k_ref[pl.ds(off, NA_KW), :]
        vw = v_ref[pl.ds(off, NA_KW), :]
        kcv, vcv = kc_ref[...], vc_ref[...]
        dov = do_ref[...]
        lse = lse_ref[...]
        delta = jnp.sum(dov.astype(F32) * o_ref[...], axis=1, keepdims=True)
        pw = jnp.exp(_dot_nt(qs, kw) + slab_ref[...] - lse)
        pc = jnp.exp(_dot_nt(qs, kcv) - lse)
        dsw = pw * (_dot_nt(dov, vw) - delta)
        dsc = pc * (_dot_nt(dov, vcv) - delta)
        dslab_ref[...] += dsw
        dsw, dsc = dsw.astype(BF16), dsc.astype(BF16)
        dq_ref[...] = (_dot(dsw, kw) + _dot(dsc, kcv)) * SCALE
        dk_ref[pl.ds(off, NA_KW), :] += _dot_tn(dsw, qs)
        dv_ref[pl.ds(off, NA_KW), :] += _dot_tn(pw.astype(BF16), dov)
        dkc_ref[...] += _dot_tn(dsc, qs)
        dvc_ref[...] += _dot_tn(pc.astype(BF16), dov)

    qblk = pl.BlockSpec((None, NA_QB, dh), lambda a, i: (a, i, 0))
    kv = pl.BlockSpec((None, s, dh), lambda a, i: (a, 0, 0))
    ckv = pl.BlockSpec((None, n_ctx, dh), lambda a, i: (a, 0, 0))
    slab_spec = pl.BlockSpec((None, None, NA_QB, NA_KW), lambda a, i: (_na_type(i, nb), a, 0, 0))
    return pl.pallas_call(
        body, grid=(h, nb),
        in_specs=[qblk, kv, kv, ckv, ckv, slab_spec, qblk, qblk, pl.BlockSpec((None, NA_QB, 1), lambda a, i: (a, i, 0))],
        out_specs=[qblk, kv, kv, ckv, ckv, slab_spec],
        out_shape=[jax.ShapeDtypeStruct((h, s, dh), F32), jax.ShapeDtypeStruct((h, s, dh), F32),
                   jax.ShapeDtypeStruct((h, s, dh), F32), jax.ShapeDtypeStruct((h, n_ctx, dh), F32),
                   jax.ShapeDtypeStruct((h, n_ctx, dh), F32), jax.ShapeDtypeStruct(slab.shape, F32)],
        compiler_params=pltpu.CompilerParams(dimension_semantics=("arbitrary", "arbitrary"), vmem_limit_bytes=VMEM_BIG_LIMIT),
        name="na_bwd",
    )(q, k, v, kc, vc, slab, do, o, lse)


@jax.custom_vjp
def na_attn(q, k, v, kc, vc, slab):
    return _na_fwd_call(q.astype(BF16), k.astype(BF16), v.astype(BF16), kc.astype(BF16), vc.astype(BF16), slab)[0]


def _na_attn_fwd(q, k, v, kc, vc, slab):
    qb, kb, vb, kcb, vcb = (t.astype(BF16) for t in (q, k, v, kc, vc))
    o, lse = _na_fwd_call(qb, kb, vb, kcb, vcb, slab)
    return o, (qb, kb, vb, kcb, vcb, slab, o, lse)


def _na_attn_bwd(res, do):
    qb, kb, vb, kcb, vcb, slab, o, lse = res
    return tuple(_na_bwd_call(qb, kb, vb, kcb, vcb, slab, do.astype(BF16), o, lse))


na_attn.defvjp(_na_attn_fwd, _na_attn_bwd)


def _na_tables(rows):
    sel = np.zeros((3, NA_QROWS, NA_KROWS, 2 * NA_WIN_H - 1), np.float32)
    row_ok = np.zeros((3, NA_QROWS, NA_KROWS), bool)
    for t, (r0, ks) in enumerate([(0, 0), (NA_QROWS, 0), (rows - NA_QROWS, rows - NA_KROWS)]):
        for a in range(NA_QROWS):
            r = r0 + a
            rs = min(max(r - NA_WIN_H // 2, 0), rows - NA_WIN_H)
            for b in range(NA_KROWS):
                kr = ks + b
                if rs <= kr < rs + NA_WIN_H:
                    sel[t, a, b, kr - r + NA_WIN_H - 1] = 1.0
                    row_ok[t, a, b] = True
    col = np.arange(GRID_W)
    c_start = np.clip(col - NA_WIN_W // 2, 0, GRID_W - NA_WIN_W)
    in_win = (col[None, :] >= c_start[:, None]) & (col[None, :] < c_start[:, None] + NA_WIN_W)
    dc_idx = np.clip(col[None, :] - col[:, None], -(NA_WIN_W - 1), NA_WIN_W - 1) + NA_WIN_W - 1
    onehot = (in_win[:, :, None] & (dc_idx[:, :, None] == np.arange(2 * NA_WIN_W - 1)[None, None, :])).astype(np.float32)
    ok = row_ok[:, :, None, :, None] & in_win[None, None, :, None, :]
    neg = np.where(ok, 0.0, NEG_BIG).astype(np.float32).reshape(3, 1, NA_QB, NA_KW)
    return jnp.asarray(sel), jnp.asarray(onehot), jnp.asarray(neg)


def _na_slab(rpb, tables):
    sel, onehot, neg = tables
    a = jnp.einsum("tabd,hdc->thabc", sel, rpb, precision=HIGHEST)
    val = jnp.einsum("thabc,wuc->thawbu", a, onehot, precision=HIGHEST)
    return val.reshape(3, NA_HEADS, NA_QB, NA_KW) + neg


def _loss_call(y, target):
    s, d = y.shape
    tt = _pick(s, (512, 256, 128))

    def body(y_ref, t_ref, loss_ref, diff_ref):
        @pl.when(pl.program_id(0) == 0)
        def _():
            loss_ref[...] = jnp.zeros_like(loss_ref)

        e = y_ref[...] - t_ref[...]
        diff_ref[...] = e * (1.0 / d)
        loss_ref[...] += 0.5 * jnp.sum(jnp.mean(e * e, axis=-1, keepdims=True), axis=0, keepdims=True)

    row = pl.BlockSpec((tt, d), lambda i: (i, 0))
    return pl.pallas_call(
        body, grid=(s // tt,), in_specs=[row, row], out_specs=[pl.BlockSpec((1, 1), lambda i: (0, 0)), row],
        out_shape=[jax.ShapeDtypeStruct((1, 1), F32), jax.ShapeDtypeStruct((s, d), F32)],
        compiler_params=pltpu.CompilerParams(dimension_semantics=("arbitrary",)), name="loss_head",
    )(y, target)


@jax.custom_vjp
def loss_head(y, target):
    return _loss_call(y, target)[0][0, 0]


def _loss_head_fwd(y, target):
    loss, diff = _loss_call(y, target)
    return loss[0, 0], diff


def _loss_head_bwd(diff, g):
    return diff * g, jnp.zeros_like(diff)


loss_head.defvjp(_loss_head_fwd, _loss_head_bwd)


def _exchange(x, all_gather, name):
    blk = x.shape if all_gather else x.shape[1:]

    def body(x_ref, out_ref, send_sems, recv_sems, local_sem):
        mx, my, mc = lax.axis_index("x"), lax.axis_index("y"), lax.axis_index("c")
        me = 4 * mx + 2 * my + mc

        def src(p):
            return x_ref if all_gather else x_ref.at[p]

        mine = pltpu.make_async_copy(src(me), out_ref.at[me], local_sem)
        mine.start()
        copies = []
        for rel in range(1, N_DEV):
            px, py, pc = mx ^ (rel >> 2), my ^ ((rel >> 1) & 1), mc ^ (rel & 1)
            cp = pltpu.make_async_remote_copy(
                src_ref=src(4 * px + 2 * py + pc), dst_ref=out_ref.at[me],
                send_sem=send_sems.at[rel - 1], recv_sem=recv_sems.at[rel - 1],
                device_id=(px, py, pc), device_id_type=MESH)
            cp.start()
            copies.append(cp)
        for cp in copies:
            cp.wait_send()
        for cp in copies:
            cp.wait_recv()
        mine.wait()

    return pl.pallas_call(
        body,
        in_specs=[pl.BlockSpec(memory_space=pl.ANY)], out_specs=pl.BlockSpec(memory_space=pl.ANY),
        out_shape=jax.ShapeDtypeStruct((N_DEV,) + tuple(blk), x.dtype),
        scratch_shapes=[pltpu.SemaphoreType.DMA((N_DEV - 1,)), pltpu.SemaphoreType.DMA((N_DEV - 1,)),
                        pltpu.SemaphoreType.DMA(())],
        compiler_params=pltpu.CompilerParams(has_side_effects=True), name=name,
    )(x)


def _adam_call(parts, w, m, v, name):
    _, r, c = parts.shape
    tr = _pick(r, (256, 128))

    def body(p_ref, w_ref, m_ref, v_ref, g_ref, d_ref, nm_ref, nv_ref):
        g = p_ref[0]
        for s in range(1, N_DEV):
            g = g + p_ref[s]
        mn = ADAM_B1 * m_ref[...] + (1.0 - ADAM_B1) * g
        vn = ADAM_B2 * v_ref[...] + (1.0 - ADAM_B2) * (g * g)
        m_hat = mn / (1.0 - ADAM_B1 ** ADAM_STEP)
        v_hat = vn / (1.0 - ADAM_B2 ** ADAM_STEP)
        g_ref[...] = g
        d_ref[...] = -ADAM_LR * (m_hat / (jnp.sqrt(v_hat) + ADAM_EPS) + ADAM_WD * w_ref[...])
        nm_ref[...] = mn
        nv_ref[...] = vn

    row = pl.BlockSpec((tr, c), lambda i: (i, 0))
    out = jax.ShapeDtypeStruct((r, c), F32)
    return pl.pallas_call(
        body, grid=(r // tr,), in_specs=[pl.BlockSpec((N_DEV, tr, c), lambda i: (0, i, 0)), row, row, row],
        out_specs=[row, row, row, row], out_shape=[out, out, out, out],
        compiler_params=pltpu.CompilerParams(dimension_semantics=("parallel",), vmem_limit_bytes=VMEM_BIG_LIMIT),
        name=name,
    )(parts, w, m, v)


def _rows_of(a):
    return a.reshape(-1, D_MODEL)


def _shard_slabs(name, full):
    dep, k, n = full.shape
    if name in ROW_SHARDED:
        t = full.reshape(dep, N_DEV, k // N_DEV, n).transpose(1, 0, 2, 3)
    else:
        t = full.reshape(dep, k, N_DEV, n // N_DEV).transpose(2, 0, 1, 3)
    return t.reshape(N_DEV, -1, D_MODEL)


def _unshard(name, slabs, shard_shape):
    dep, k, n = shard_shape
    t = slabs.reshape((N_DEV, dep, k, n))
    if name in ROW_SHARDED:
        return t.transpose(1, 0, 2, 3).reshape(dep, N_DEV * k, n)
    return t.transpose(1, 2, 0, 3).reshape(dep, k, N_DEV * n)


def _pack_small(vals):
    flat = jnp.concatenate([vals[n].reshape(-1) for n in SMALL])
    rows = -(-flat.shape[0] // D_MODEL)
    rows = -(-rows // 8) * 8
    return jnp.pad(flat, (0, rows * D_MODEL - flat.shape[0])).reshape(rows, D_MODEL)


def _unpack_small(packed, like):
    flat, out, off = packed.reshape(-1), {}, 0
    for n in SMALL:
        size = like[n].size
        out[n] = flat[off:off + size].reshape(like[n].shape)
        off += size
    return out


def _rope_tables(s, n_ctx):
    t = jnp.arange(s)
    row = (t // GRID_W).astype(F32)
    col = (t % GRID_W).astype(F32)
    half = HEAD_DIM // 2
    inv = ROPE_THETA ** (-jnp.arange(0, half, 2, dtype=F32) / half)
    ang = jnp.concatenate([row[:, None] * inv, col[:, None] * inv], axis=-1)
    cos2 = jnp.repeat(jnp.cos(ang), 2, axis=-1)
    sin2 = jnp.repeat(jnp.sin(ang), 2, axis=-1)
    cos2 = jnp.concatenate([cos2, jnp.ones((n_ctx, HEAD_DIM), F32)], axis=0)
    sin2 = jnp.concatenate([sin2, jnp.zeros((n_ctx, HEAD_DIM), F32)], axis=0)
    return cos2, sin2


def _to_heads(t, n_heads):
    return t.reshape(t.shape[0], n_heads, HEAD_DIM).transpose(1, 0, 2)


def _from_heads(t):
    return t.transpose(1, 0, 2).reshape(t.shape[1], t.shape[0] * HEAD_DIM)


def _local_loss(p, x, c, ctx, target, consts):
    s, n_ctx = x.shape[0], ctx.shape[0]
    cos2, sin2, rot, na_tables = consts
    is_ctx = (jnp.arange(s + n_ctx) >= s)[:, None]
    seg = lambda rows2: jnp.where(is_ctx, rows2[1:2], rows2[0:1])
    xa = jnp.concatenate([x, ctx], axis=0)
    cond = jnp.concatenate([jax.nn.silu(c), jax.nn.silu(p["c_ctx"])[None, :],
                            jnp.zeros((MOD_ROWS - 2, D_MODEL), F32)], axis=0)

    for l in range(DEPTH):
        mod = (mm(cond, p["w_mod"][l]) + p["b_mod"][l])[:2]
        sh1, sc1, g1, sh2, sc2, g2 = jnp.split(mod, 6, axis=-1)

        h = norm_mod(xa, p["norm1"][l][None, :], sh1, sc1, s)
        na_q, na_k, na_v, gq, gk, gv, ga, gb = jnp.split(mm(h, p["w_in"][l]), IN_SPLITS, axis=-1)

        qa, ka, va = _to_heads(na_q, NA_HEADS), _to_heads(na_k, NA_HEADS), _to_heads(na_v, NA_HEADS)
        slab = _na_slab(p["na_rpb"][l], na_tables)
        ya_lat = na_attn(qa[:, :s], ka[:, :s], va[:, :s], ka[:, s:], va[:, s:], slab)
        ya_ctx = attn(qa[:, s:], ka[:, s:], va[:, s:])
        ya = _from_heads(jnp.concatenate([ya_lat, ya_ctx], axis=1))

        qk = _to_heads(jnp.concatenate([gq, gk], axis=-1), GQA_Q_HEADS + GQA_KV_HEADS)
        gains = jnp.stack([p["q_gain"][l], p["k_gain"][l]])
        qk = hn_rope(qk, gains, cos2, sin2, rot, GQA_Q_HEADS)
        qb, kb, vb = qk[:GQA_Q_HEADS], qk[GQA_Q_HEADS:], _to_heads(gv, GQA_KV_HEADS)
        ob_lat = attn(qb[:, :s].reshape(GQA_KV_HEADS, GQA_REP * s, HEAD_DIM), kb, vb)
        ob_ctx = attn(qb[:, s:].reshape(GQA_KV_HEADS, GQA_REP * n_ctx, HEAD_DIM), kb[:, s:], vb[:, s:])
        yb = _from_heads(jnp.concatenate([ob_lat.reshape(GQA_Q_HEADS, s, HEAD_DIM),
                                          ob_ctx.reshape(GQA_Q_HEADS, n_ctx, HEAD_DIM)], axis=1))

        merged = jax.nn.sigmoid(ga) * mm(ya, p["w_pa"][l]) + jax.nn.sigmoid(gb) * mm(yb, p["w_pb"][l])
        xa = xa + seg(g1) * mm(merged, p["w_o"][l])

        h2 = norm_mod(xa, p["norm2"][l][None, :], sh2, sc2, s)
        a, u = jnp.split(mm(h2, p["w_ffn_in"][l]), 2, axis=-1)
        xa = xa + seg(g2) * mm(jax.nn.silu(a) * u, p["w_ffn_out"][l])

    zeros2 = jnp.zeros((2, D_MODEL), F32)
    y = norm_mod(xa, p["final_norm"][None, :], zeros2, zeros2, s)[:s]
    return loss_head(y, target)


def kernel(x, c, ctx, c_ctx, w_mod, b_mod, norm1, w_in, na_rpb, q_gain, k_gain, w_pa, w_pb, w_o, norm2, w_ffn_in, w_ffn_out, final_norm, loss_target, m_c_ctx, m_w_mod, m_b_mod, m_norm1, m_w_in, m_na_rpb, m_q_gain, m_k_gain, m_w_pa, m_w_pb, m_w_o, m_norm2, m_w_ffn_in, m_w_ffn_out, m_final_norm, v_c_ctx, v_w_mod, v_b_mod, v_norm1, v_w_in, v_na_rpb, v_q_gain, v_k_gain, v_w_pa, v_w_pb, v_w_o, v_norm2, v_w_ffn_in, v_w_ffn_out, v_final_norm):
    w = dict(c_ctx=c_ctx, w_mod=w_mod, b_mod=b_mod, norm1=norm1, w_in=w_in, na_rpb=na_rpb, q_gain=q_gain, k_gain=k_gain,
             w_pa=w_pa, w_pb=w_pb, w_o=w_o, norm2=norm2, w_ffn_in=w_ffn_in, w_ffn_out=w_ffn_out, final_norm=final_norm)
    mom = dict(c_ctx=m_c_ctx, w_mod=m_w_mod, b_mod=m_b_mod, norm1=m_norm1, w_in=m_w_in, na_rpb=m_na_rpb, q_gain=m_q_gain,
               k_gain=m_k_gain, w_pa=m_w_pa, w_pb=m_w_pb, w_o=m_w_o, norm2=m_norm2, w_ffn_in=m_w_ffn_in,
               w_ffn_out=m_w_ffn_out, final_norm=m_final_norm)
    var = dict(c_ctx=v_c_ctx, w_mod=v_w_mod, b_mod=v_b_mod, norm1=v_norm1, w_in=v_w_in, na_rpb=v_na_rpb, q_gain=v_q_gain,
               k_gain=v_k_gain, w_pa=v_w_pa, w_pb=v_w_pb, w_o=v_w_o, norm2=v_norm2, w_ffn_in=v_w_ffn_in,
               w_ffn_out=v_w_ffn_out, final_norm=v_final_norm)
    s, n_ctx = x.shape[1], ctx.shape[1]

    pack_big = lambda d: jnp.concatenate([_rows_of(d[n]) for n in BIG], axis=0)
    gathered = _exchange(pack_big(w).astype(BF16), True, "gather_weights")
    params, off = {n: w[n] for n in SMALL}, 0
    for n in BIG:
        r = w[n].size // D_MODEL
        params[n] = _unshard(n, gathered[:, off:off + r], w[n].shape).astype(F32)
        off += r

    consts = (*_rope_tables(s, n_ctx), _rot_matrix(), _na_tables(s // GRID_W))
    loss, (gp, gx) = jax.value_and_grad(_local_loss, argnums=(0, 1))(params, x[0], c, ctx[0], loss_target[0], consts)
    loss = lax.psum(loss, ("x", "y", "c"))

    parts = _exchange(jnp.concatenate([_shard_slabs(n, gp[n]) for n in BIG], axis=1), False, "scatter_grads")
    big = _adam_call(parts, pack_big(w), pack_big(mom), pack_big(var), "adam_big")
    parts_s = _exchange(_pack_small(gp), True, "gather_small_grads")
    small = _adam_call(parts_s, _pack_small(w), _pack_small(mom), _pack_small(var), "adam_small")

    outs = [{}, {}, {}, {}]
    for k in range(4):
        outs[k].update(_unpack_small(small[k], w))
        off = 0
        for n in BIG:
            r = w[n].size // D_MODEL
            outs[k][n] = big[k][off:off + r].reshape(w[n].shape)
            off += r
    return (loss, gx[None], *[o[n] for o in outs for n in WEIGHTS])
```

```python
import functools

import numpy as np
import jax
import jax.numpy as jnp
from jax import lax
from jax.experimental import pallas as pl
from jax.experimental.pallas import tpu as pltpu

F32 = jnp.float32
BF16 = jnp.bfloat16
HIGHEST = lax.Precision.HIGHEST

D_MODEL = 1024
DEPTH = 4
GRID_W = 64
HEAD_DIM = 64
NA_HEADS = 8
NA_WIN_H = 8
NA_WIN_W = 16
GQA_Q_HEADS = 8
GQA_KV_HEADS = 2
GQA_REP = GQA_Q_HEADS // GQA_KV_HEADS
NA_WIDTH = NA_HEADS * HEAD_DIM
GQA_Q_WIDTH = GQA_Q_HEADS * HEAD_DIM
GQA_KV_WIDTH = GQA_KV_HEADS * HEAD_DIM
IN_SIZES = (NA_WIDTH, NA_WIDTH, NA_WIDTH, GQA_Q_WIDTH, GQA_KV_WIDTH, GQA_KV_WIDTH, D_MODEL, D_MODEL)
IN_SPLITS = tuple(int(v) for v in np.cumsum(IN_SIZES)[:-1])
ROPE_THETA = 10000.0
EPS = 1e-6
SCALE = HEAD_DIM ** -0.5
ADAM_LR = 0.001
ADAM_B1 = 0.9
ADAM_B2 = 0.999
ADAM_EPS = 1e-08
ADAM_WD = 0.01
ADAM_STEP = 10

N_DEV = 8
MESH = pl.DeviceIdType.MESH
NEG_BIG = -1e30

VMEM_BIG_LIMIT = 52 * 1024 * 1024
NA_QROWS = 4
NA_QB = NA_QROWS * GRID_W
NA_KROWS = 12
NA_KW = NA_KROWS * GRID_W
MM_TILE_ELEMS = 512 * 2176
MM_OPERAND_ELEMS = 1408 * 2176
MOD_ROWS = 256

BIG = ("w_mod", "w_in", "w_pa", "w_pb", "w_o", "w_ffn_in", "w_ffn_out")
ROW_SHARDED = ("w_o", "w_ffn_out")
SMALL = ("c_ctx", "b_mod", "norm1", "na_rpb", "q_gain", "k_gain", "norm2", "final_norm")
WEIGHTS = ("c_ctx", "w_mod", "b_mod", "norm1", "w_in", "na_rpb", "q_gain", "k_gain", "w_pa", "w_pb", "w_o",
           "norm2", "w_ffn_in", "w_ffn_out", "final_norm")


def _pick(n, cands):
    for c in cands:
        if n % c == 0:
            return c
    return n


def _dot_nt(a, b):
    return lax.dot_general(a, b, (((1,), (1,)), ((), ())), preferred_element_type=F32)


def _dot_tn(a, b):
    return lax.dot_general(a, b, (((0,), (0,)), ((), ())), preferred_element_type=F32)


def _dot(a, b):
    return jnp.dot(a, b, preferred_element_type=F32)


def _mm_call(a, b, name, mode="nn"):
    (m, k) = a.shape if mode != "tn" else a.shape[::-1]
    n = b.shape[1] if mode != "nt" else b.shape[0]
    tn = _pick(n, (2176, 1408, 1024, 512, 256, 128))
    tm = _pick(m, tuple(t for t in (768, 512, 256, 128) if t * tn <= MM_TILE_ELEMS))
    tk = _pick(k, tuple(t for t in (2816, 2176, 1408, 1024, 768, 512, 256, 128) if t * max(tm, tn) <= MM_OPERAND_ELEMS))
    nk = k // tk
    dot = {"nn": _dot, "tn": _dot_tn, "nt": _dot_nt}[mode]
    a_blk = (tk, tm) if mode == "tn" else (tm, tk)
    b_blk = (tn, tk) if mode == "nt" else (tk, tn)
    a_idx = (lambda i, kk: (kk, i)) if mode == "tn" else (lambda i, kk: (i, kk))
    b_idx = (lambda j, kk: (j, kk)) if mode == "nt" else (lambda j, kk: (kk, j))

    def body(a_ref, b_ref, o_ref, acc_ref):
        kk = pl.program_id(2)
        part = dot(a_ref[...], b_ref[...])

        @pl.when(kk == 0)
        def _():
            acc_ref[...] = part

        @pl.when(kk > 0)
        def _():
            acc_ref[...] += part

        @pl.when(kk == nk - 1)
        def _():
            o_ref[...] = acc_ref[...]

    def body1(a_ref, b_ref, o_ref):
        o_ref[...] = dot(a_ref[...], b_ref[...])

    footprint = 2 * (tm * tk * 2 + tk * tn * 2 + tm * tn * 4) + 2 * tm * tn * 4
    limit = int(footprint + (8 << 20))
    if nk == 1:
        return pl.pallas_call(
            body1, grid=(n // tn, m // tm),
            in_specs=[pl.BlockSpec(a_blk, lambda j, i: a_idx(i, 0)), pl.BlockSpec(b_blk, lambda j, i: b_idx(j, 0))],
            out_specs=pl.BlockSpec((tm, tn), lambda j, i: (i, j)),
            out_shape=jax.ShapeDtypeStruct((m, n), F32),
            compiler_params=pltpu.CompilerParams(dimension_semantics=("parallel", "parallel"), vmem_limit_bytes=limit),
            name=name,
        )(a, b)
    return pl.pallas_call(
        body, grid=(m // tm, n // tn, nk),
        in_specs=[pl.BlockSpec(a_blk, lambda i, j, kk: a_idx(i, kk)), pl.BlockSpec(b_blk, lambda i, j, kk: b_idx(j, kk))],
        out_specs=pl.BlockSpec((tm, tn), lambda i, j, kk: (i, j)),
        out_shape=jax.ShapeDtypeStruct((m, n), F32),
        scratch_shapes=[pltpu.VMEM((tm, tn), F32)],
        compiler_params=pltpu.CompilerParams(
            dimension_semantics=("parallel", "parallel", "arbitrary"), vmem_limit_bytes=limit),
        name=name,
    )(a, b)


@jax.custom_vjp
def mm(x, w):
    return _mm_call(x.astype(BF16), w.astype(BF16), "mm_fwd")


def _mm_fwd(x, w):
    xb, wb = x.astype(BF16), w.astype(BF16)
    return _mm_call(xb, wb, "mm_fwd"), (xb, wb)


def _mm_bwd(res, dy):
    xb, wb = res
    dyb = dy.astype(BF16)
    return _mm_call(dyb, wb, "mm_dx", "nt"), _mm_call(xb, dyb, "mm_dw", "tn")


mm.defvjp(_mm_fwd, _mm_bwd)


def _seg_rows(ref, is_ctx):
    return jnp.where(is_ctx, ref[1:2, :], ref[0:1, :])


def _one_hot_row(second):
    return (lax.broadcasted_iota(jnp.int32, (2, 1), 0) == second.astype(jnp.int32)).astype(F32)


def _norm_mod_fwd_call(x, g, shift, scale, n_lat):
    t, d = x.shape
    tt = _pick(t, (256, 128))
    lat_tiles = n_lat // tt

    def body(x_ref, g_ref, sh_ref, sc_ref, y_ref):
        is_ctx = pl.program_id(0) >= lat_tiles
        xv = x_ref[...]
        r = lax.rsqrt(jnp.mean(xv * xv, axis=-1, keepdims=True) + EPS)
        yn = xv * r * g_ref[...]
        y_ref[...] = yn * (1.0 + _seg_rows(sc_ref, is_ctx)) + _seg_rows(sh_ref, is_ctx)

    row = pl.BlockSpec((tt, d), lambda i: (i, 0))
    full = lambda rws: pl.BlockSpec((rws, d), lambda i: (0, 0))
    return pl.pallas_call(
        body, grid=(t // tt,), in_specs=[row, full(1), full(2), full(2)], out_specs=row,
        out_shape=jax.ShapeDtypeStruct((t, d), F32),
        compiler_params=pltpu.CompilerParams(dimension_semantics=("parallel",)), name="norm_mod_fwd",
    )(x, g, shift, scale)


def _norm_mod_bwd_call(x, g, scale, dy, n_lat):
    t, d = x.shape
    tt = _pick(t, (256, 128))
    lat_tiles = n_lat // tt

    def body(x_ref, g_ref, sc_ref, dy_ref, dx_ref, dg_ref, dsh_ref, dsc_ref):
        i = pl.program_id(0)
        is_ctx = i >= lat_tiles

        @pl.when(i == 0)
        def _():
            dg_ref[...] = jnp.zeros_like(dg_ref)
            dsh_ref[...] = jnp.zeros_like(dsh_ref)
            dsc_ref[...] = jnp.zeros_like(dsc_ref)

        xv, dyv, gv = x_ref[...], dy_ref[...], g_ref[...]
        r = lax.rsqrt(jnp.mean(xv * xv, axis=-1, keepdims=True) + EPS)
        nrm = xv * r
        yn = nrm * gv
        seg = _one_hot_row(is_ctx)
        dsh_ref[...] += seg * jnp.sum(dyv, axis=0, keepdims=True)
        dsc_ref[...] += seg * jnp.sum(dyv * yn, axis=0, keepdims=True)
        dyn = dyv * (1.0 + _seg_rows(sc_ref, is_ctx))
        dg_ref[...] += jnp.sum(dyn * nrm, axis=0, keepdims=True)
        dn = dyn * gv
        dx_ref[...] = r * (dn - nrm * jnp.mean(dn * nrm, axis=-1, keepdims=True))

    row = pl.BlockSpec((tt, d), lambda i: (i, 0))
    full = lambda rws: pl.BlockSpec((rws, d), lambda i: (0, 0))
    return pl.pallas_call(
        body, grid=(t // tt,), in_specs=[row, full(1), full(2), row], out_specs=[row, full(1), full(2), full(2)],
        out_shape=[jax.ShapeDtypeStruct((t, d), F32), jax.ShapeDtypeStruct((1, d), F32),
                   jax.ShapeDtypeStruct((2, d), F32), jax.ShapeDtypeStruct((2, d), F32)],
        compiler_params=pltpu.CompilerParams(dimension_semantics=("arbitrary",)), name="norm_mod_bwd",
    )(x, g, scale, dy)


@functools.partial(jax.custom_vjp, nondiff_argnums=(4,))
def norm_mod(x, g, shift, scale, n_lat):
    return _norm_mod_fwd_call(x, g, shift, scale, n_lat)


def _norm_mod_fwd(x, g, shift, scale, n_lat):
    return _norm_mod_fwd_call(x, g, shift, scale, n_lat), (x, g, scale)


def _norm_mod_bwd(n_lat, res, dy):
    x, g, scale = res
    dx, dg, dsh, dsc = _norm_mod_bwd_call(x, g, scale, dy, n_lat)
    return dx, dg, dsh, dsc


norm_mod.defvjp(_norm_mod_fwd, _norm_mod_bwd)


def _rot_matrix():
    p = np.zeros((HEAD_DIM, HEAD_DIM), np.float32)
    for i in range(HEAD_DIM // 2):
        p[2 * i + 1, 2 * i] = -1.0
        p[2 * i, 2 * i + 1] = 1.0
    return jnp.asarray(p)


def _hn_rope_fwd_call(x, gains, cos2, sin2, rot, n_q):
    nh, t, dh = x.shape
    tt = _pick(t, (2816, 1024, 768, 512, 256, 128))

    def body(x_ref, g_ref, cos_ref, sin_ref, rot_ref, y_ref):
        gv = jnp.where(pl.program_id(0) >= n_q, g_ref[1:2, :], g_ref[0:1, :])
        xv = x_ref[...]
        r = lax.rsqrt(jnp.mean(xv * xv, axis=-1, keepdims=True) + EPS)
        y = xv * r * gv
        yr = jnp.dot(y, rot_ref[...], precision=HIGHEST, preferred_element_type=F32)
        y_ref[...] = y * cos_ref[...] + yr * sin_ref[...]

    blk = pl.BlockSpec((None, tt, dh), lambda h, i: (h, i, 0))
    tab = pl.BlockSpec((tt, dh), lambda h, i: (i, 0))
    return pl.pallas_call(
        body, grid=(nh, t // tt),
        in_specs=[blk, pl.BlockSpec((2, dh), lambda h, i: (0, 0)), tab, tab, pl.BlockSpec((dh, dh), lambda h, i: (0, 0))],
        out_specs=blk, out_shape=jax.ShapeDtypeStruct((nh, t, dh), F32),
        compiler_params=pltpu.CompilerParams(dimension_semantics=("parallel", "parallel")), name="hn_rope_fwd",
    )(x, gains, cos2, sin2, rot)


def _hn_rope_bwd_call(x, gains, cos2, sin2, rot, dy, n_q):
    nh, t, dh = x.shape
    tt = _pick(t, (2816, 1024, 768, 512, 256, 128))

    def body(x_ref, g_ref, cos_ref, sin_ref, rot_ref, dy_ref, dx_ref, dg_ref):
        h, i = pl.program_id(0), pl.program_id(1)
        is_k = h >= n_q

        @pl.when((h == 0) & (i == 0))
        def _():
            dg_ref[...] = jnp.zeros_like(dg_ref)

        gv = jnp.where(is_k, g_ref[1:2, :], g_ref[0:1, :])
        xv, dyv = x_ref[...], dy_ref[...]
        r = lax.rsqrt(jnp.mean(xv * xv, axis=-1, keepdims=True) + EPS)
        nrm = xv * r
        dyn = dyv * cos_ref[...] - jnp.dot(dyv * sin_ref[...], rot_ref[...], precision=HIGHEST, preferred_element_type=F32)
        seg = _one_hot_row(is_k)
        dg_ref[...] += seg * jnp.sum(dyn * nrm, axis=0, keepdims=True)
        dn = dyn * gv
        dx_ref[...] = r * (dn - nrm * jnp.mean(dn * nrm, axis=-1, keepdims=True))

    blk = pl.BlockSpec((None, tt, dh), lambda h, i: (h, i, 0))
    tab = pl.BlockSpec((tt, dh), lambda h, i: (i, 0))
    g_spec = pl.BlockSpec((2, dh), lambda h, i: (0, 0))
    return pl.pallas_call(
        body, grid=(nh, t // tt),
        in_specs=[blk, g_spec, tab, tab, pl.BlockSpec((dh, dh), lambda h, i: (0, 0)), blk],
        out_specs=[blk, g_spec],
        out_shape=[jax.ShapeDtypeStruct((nh, t, dh), F32), jax.ShapeDtypeStruct((2, dh), F32)],
        compiler_params=pltpu.CompilerParams(dimension_semantics=("arbitrary", "arbitrary")), name="hn_rope_bwd",
    )(x, gains, cos2, sin2, rot, dy)


@functools.partial(jax.custom_vjp, nondiff_argnums=(5,))
def hn_rope(x, gains, cos2, sin2, rot, n_q):
    return _hn_rope_fwd_call(x, gains, cos2, sin2, rot, n_q)


def _hn_rope_fwd(x, gains, cos2, sin2, rot, n_q):
    return _hn_rope_fwd_call(x, gains, cos2, sin2, rot, n_q), (x, gains, cos2, sin2, rot)


def _hn_rope_bwd(n_q, res, dy):
    x, gains, cos2, sin2, rot = res
    dx, dg = _hn_rope_bwd_call(x, gains, cos2, sin2, rot, dy, n_q)
    return dx, dg, jnp.zeros_like(cos2), jnp.zeros_like(sin2), jnp.zeros_like(rot)


hn_rope.defvjp(_hn_rope_fwd, _hn_rope_bwd)


ATT_TK = 256


def _att_tq(rn):
    return _pick(rn, (512, 256, 128))


def _flash_fwd_call(q_t, k, v_t):
    g, dh, rn = q_t.shape
    nk = k.shape[1]
    tq, tk = _att_tq(rn), ATT_TK
    nkb = nk // tk

    def body(qt_ref, k_ref, vt_ref, ot_ref, lse_ref):
        qst = qt_ref[...] * SCALE

        def step(j, carry):
            m, l, acc = carry
            off = pl.multiple_of(j * tk, tk)
            s_t = _dot(k_ref[pl.ds(off, tk), :], qst)
            m_new = jnp.maximum(m, jnp.max(s_t, axis=0, keepdims=True))
            p_t = jnp.exp(s_t - m_new)
            alpha = jnp.exp(m - m_new)
            l = alpha * l + jnp.sum(p_t, axis=0, keepdims=True)
            acc = alpha * acc + _dot(vt_ref[j], p_t.astype(BF16))
            return m_new, l, acc

        m, l, acc = lax.fori_loop(
            0, nkb, step, (jnp.full((1, tq), NEG_BIG, F32), jnp.zeros((1, tq), F32), jnp.zeros((dh, tq), F32)))
        ot_ref[...] = acc / l
        lse_ref[...] = m + jnp.log(l)

    col = pl.BlockSpec((None, dh, tq), lambda a, i: (a, 0, i))
    vec = pl.BlockSpec((None, 1, tq), lambda a, i: (a, 0, i))
    return pl.pallas_call(
        body, grid=(g, rn // tq),
        in_specs=[col, pl.BlockSpec((None, nk, dh), lambda a, i: (a, 0, 0)),
                  pl.BlockSpec((None, nkb, dh, tk), lambda a, i: (a, 0, 0, 0))],
        out_specs=[col, vec],
        out_shape=[jax.ShapeDtypeStruct((g, dh, rn), F32), jax.ShapeDtypeStruct((g, 1, rn), F32)],
        compiler_params=pltpu.CompilerParams(dimension_semantics=("parallel", "parallel"), vmem_limit_bytes=VMEM_BIG_LIMIT),
        name="flash_fwd",
    )(q_t, k, v_t)


def _flash_bwd_call(q, q_t, k, k_t, v, do, do_t, o_t, lse_t):
    g, rn, dh = q.shape
    nk = k.shape[1]
    tq, tk = _att_tq(rn), ATT_TK
    nkb = nk // tk

    def body(q_ref, qt_ref, k_ref, kt_ref, v_ref, do_ref, dot_ref, ot_ref, lse_ref, dqt_ref, dk_ref, dv_ref):
        @pl.when(pl.program_id(1) == 0)
        def _():
            dk_ref[...] = jnp.zeros_like(dk_ref)
            dv_ref[...] = jnp.zeros_like(dv_ref)

        qs = q_ref[...] * SCALE
        qst = qt_ref[...] * SCALE
        dov, dotv = do_ref[...], dot_ref[...]
        delta = jnp.sum(dotv.astype(F32) * ot_ref[...], axis=0, keepdims=True)
        lse = lse_ref[...]

        def step(j, dqt):
            off = pl.multiple_of(j * tk, tk)
            kj = k_ref[pl.ds(off, tk), :]
            vj = v_ref[pl.ds(off, tk), :]
            p_t = jnp.exp(_dot(kj, qst) - lse)
            ds_t = (p_t * (_dot(vj, dotv) - delta)).astype(BF16)
            dv_ref[pl.ds(off, tk), :] += _dot(p_t.astype(BF16), dov)
            dk_ref[pl.ds(off, tk), :] += _dot(ds_t, qs)
            return dqt + _dot(kt_ref[j], ds_t)

        dqt_ref[...] = lax.fori_loop(0, nkb, step, jnp.zeros((dh, tq), F32)) * SCALE

    row = pl.BlockSpec((None, tq, dh), lambda a, i: (a, i, 0))
    col = pl.BlockSpec((None, dh, tq), lambda a, i: (a, 0, i))
    kv = pl.BlockSpec((None, nk, dh), lambda a, i: (a, 0, 0))
    return pl.pallas_call(
        body, grid=(g, rn // tq),
        in_specs=[row, col, kv, pl.BlockSpec((None, nkb, dh, tk), lambda a, i: (a, 0, 0, 0)), kv, row, col, col,
                  pl.BlockSpec((None, 1, tq), lambda a, i: (a, 0, i))],
        out_specs=[col, kv, kv],
        out_shape=[jax.ShapeDtypeStruct((g, dh, rn), F32), jax.ShapeDtypeStruct((g, nk, dh), F32),
                   jax.ShapeDtypeStruct((g, nk, dh), F32)],
        compiler_params=pltpu.CompilerParams(dimension_semantics=("arbitrary", "arbitrary"), vmem_limit_bytes=VMEM_BIG_LIMIT),
        name="flash_bwd",
    )(q, q_t, k, k_t, v, do, do_t, o_t, lse_t)


def _key_blocks_t(t):
    g, nk, dh = t.shape
    return t.reshape(g, nk // ATT_TK, ATT_TK, dh).transpose(0, 1, 3, 2)


@jax.custom_vjp
def attn(q, k, v):
    return _attn_fwd(q, k, v)[0]


def _attn_fwd(q, k, v):
    qb, kb, vb = q.astype(BF16), k.astype(BF16), v.astype(BF16)
    q_t = qb.transpose(0, 2, 1)
    o_t, lse_t = _flash_fwd_call(q_t, kb, _key_blocks_t(vb))
    return o_t.transpose(0, 2, 1), (qb, q_t, kb, vb, o_t, lse_t)


def _attn_bwd(res, do):
    qb, q_t, kb, vb, o_t, lse_t = res
    dob = do.astype(BF16)
    dq_t, dk, dv = _flash_bwd_call(qb, q_t, kb, _key_blocks_t(kb), vb, dob, dob.transpose(0, 2, 1), o_t, lse_t)
    return dq_t.transpose(0, 2, 1), dk, dv


attn.defvjp(_attn_fwd, _attn_bwd)


def _na_key_row(i, rows):
    return jnp.clip(NA_QROWS * i - NA_WIN_H // 2, 0, rows - NA_KROWS)


def _na_type(i, nb):
    return jnp.where(i == 0, 0, jnp.where(i == nb - 1, 2, 1))


def _na_fwd_call(q, k, v, kc, vc, slab):
    h, s, dh = q.shape
    n_ctx = kc.shape[1]
    rows, nb = s // GRID_W, s // NA_QB

    def body(q_ref, k_ref, v_ref, kc_ref, vc_ref, slab_ref, o_ref, lse_ref):
        off = pl.multiple_of(_na_key_row(pl.program_id(1), rows) * GRID_W, NA_QB)
        qs = q_ref[...] * SCALE
        kw = k_ref[pl.ds(off, NA_KW), :]
        vw = v_ref[pl.ds(off, NA_KW), :]
        sw = _dot_nt(qs, kw) + slab_ref[...]
        sc = _dot_nt(qs, kc_ref[...])
        m = jnp.maximum(jnp.max(sw, axis=1, keepdims=True), jnp.max(sc, axis=1, keepdims=True))
        pw = jnp.exp(sw - m)
        pc = jnp.exp(sc - m)
        l = jnp.sum(pw, axis=1, keepdims=True) + jnp.sum(pc, axis=1, keepdims=True)
        o_ref[...] = (_dot(pw.astype(BF16), vw) + _dot(pc.astype(BF16), vc_ref[...])) / l
        lse_ref[...] = m + jnp.log(l)

    qblk = pl.BlockSpec((None, NA_QB, dh), lambda a, i: (a, i, 0))
    kv = pl.BlockSpec((None, s, dh), lambda a, i: (a, 0, 0))
    ckv = pl.BlockSpec((None, n_ctx, dh), lambda a, i: (a, 0, 0))
    return pl.pallas_call(
        body, grid=(h, nb),
        in_specs=[qblk, kv, kv, ckv, ckv, pl.BlockSpec((None, None, NA_QB, NA_KW), lambda a, i: (_na_type(i, nb), a, 0, 0))],
        out_specs=[qblk, pl.BlockSpec((None, NA_QB, 1), lambda a, i: (a, i, 0))],
        out_shape=[jax.ShapeDtypeStruct((h, s, dh), F32), jax.ShapeDtypeStruct((h, s, 1), F32)],
        compiler_params=pltpu.CompilerParams(dimension_semantics=("parallel", "parallel"), vmem_limit_bytes=VMEM_BIG_LIMIT),
        name="na_fwd",
    )(q, k, v, kc, vc, slab)


def _na_bwd_call(q, k, v, kc, vc, slab, do, o, lse):
    h, s, dh = q.shape
    n_ctx = kc.shape[1]
    rows, nb = s // GRID_W, s // NA_QB

    def body(q_ref, k_ref, v_ref, kc_ref, vc_ref, slab_ref, do_ref, o_ref, lse_ref,
             dq_ref, dk_ref, dv_ref, dkc_ref, dvc_ref, dslab_ref):
        i = pl.program_id(1)

        @pl.when(i == 0)
        def _():
            dk_ref[...] = jnp.zeros_like(dk_ref)
            dv_ref[...] = jnp.zeros_like(dv_ref)
            dkc_ref[...] = jnp.zeros_like(dkc_ref)
            dvc_ref[...] = jnp.zeros_like(dvc_ref)

        @pl.when((i <= 1) | (i == nb - 1))
        def _():
            dslab_ref[...] = jnp.zeros_like(dslab_ref)

        off = pl.multiple_of(_na_key_row(i, rows) * GRID_W, NA_QB)
        qs = q_ref[...] * SCALE
        kw = k_ref[pl.ds(off, NA_KW), :]
        vw = v_ref[pl.ds(off, NA_KW), :]
        kcv, vcv = kc_ref[...], vc_ref[...]
        dov = do_ref[...]
        lse = lse_ref[...]
        delta = jnp.sum(dov.astype(F32) * o_ref[...], axis=1, keepdims=True)
        pw = jnp.exp(_dot_nt(qs, kw) + slab_ref[...] - lse)
        pc = jnp.exp(_dot_nt(qs, kcv) - lse)
        dsw = pw * (_dot_nt(dov, vw) - delta)
        dsc = pc * (_dot_nt(dov, vcv) - delta)
        dslab_ref[...] += dsw
        dsw, dsc = dsw.astype(BF16), dsc.astype(BF16)
        dq_ref[...] = (_dot(dsw, kw) + _dot(dsc, kcv)) * SCALE
        dk_ref[pl.ds(off, NA_KW), :] += _dot_tn(dsw, qs)
        dv_ref[pl.ds(off, NA_KW), :] += _dot_tn(pw.astype(BF16), dov)
        dkc_ref[...] += _dot_tn(dsc, qs)
        dvc_ref[...] += _dot_tn(pc.astype(BF16), dov)

    qblk = pl.BlockSpec((None, NA_QB, dh), lambda a, i: (a, i, 0))
    kv = pl.BlockSpec((None, s, dh), lambda a, i: (a, 0, 0))
    ckv = pl.BlockSpec((None, n_ctx, dh), lambda a, i: (a, 0, 0))
    slab_spec = pl.BlockSpec((None, None, NA_QB, NA_KW), lambda a, i: (_na_type(i, nb), a, 0, 0))
    return pl.pallas_call(
        body, grid=(h, nb),
        in_specs=[qblk, kv, kv, ckv, ckv, slab_spec, qblk, qblk, pl.BlockSpec((None, NA_QB, 1), lambda a, i: (a, i, 0))],
        out_specs=[qblk, kv, kv, ckv, ckv, slab_spec],
        out_shape=[jax.ShapeDtypeStruct((h, s, dh), F32), jax.ShapeDtypeStruct((h, s, dh), F32),
                   jax.ShapeDtypeStruct((h, s, dh), F32), jax.ShapeDtypeStruct((h, n_ctx, dh), F32),
                   jax.ShapeDtypeStruct((h, n_ctx, dh), F32), jax.ShapeDtypeStruct(slab.shape, F32)],
        compiler_params=pltpu.CompilerParams(dimension_semantics=("arbitrary", "arbitrary"), vmem_limit_bytes=VMEM_BIG_LIMIT),
        name="na_bwd",
    )(q, k, v, kc, vc, slab, do, o, lse)


@jax.custom_vjp
def na_attn(q, k, v, kc, vc, slab):
    return _na_fwd_call(q.astype(BF16), k.astype(BF16), v.astype(BF16), kc.astype(BF16), vc.astype(BF16), slab)[0]


def _na_attn_fwd(q, k, v, kc, vc, slab):
    qb, kb, vb, kcb, vcb = (t.astype(BF16) for t in (q, k, v, kc, vc))
    o, lse = _na_fwd_call(qb, kb, vb, kcb, vcb, slab)
    return o, (qb, kb, vb, kcb, vcb, slab, o, lse)


def _na_attn_bwd(res, do):
    qb, kb, vb, kcb, vcb, slab, o, lse = res
    return tuple(_na_bwd_call(qb, kb, vb, kcb, vcb, slab, do.astype(BF16), o, lse))


na_attn.defvjp(_na_attn_fwd, _na_attn_bwd)


def _na_tables(rows):
    sel = np.zeros((3, NA_QROWS, NA_KROWS, 2 * NA_WIN_H - 1), np.float32)
    row_ok = np.zeros((3, NA_QROWS, NA_KROWS), bool)
    for t, (r0, ks) in enumerate([(0, 0), (NA_QROWS, 0), (rows - NA_QROWS, rows - NA_KROWS)]):
        for a in range(NA_QROWS):
            r = r0 + a
            rs = min(max(r - NA_WIN_H // 2, 0), rows - NA_WIN_H)
            for b in range(NA_KROWS):
                kr = ks + b
                if rs <= kr < rs + NA_WIN_H:
                    sel[t, a, b, kr - r + NA_WIN_H - 1] = 1.0
                    row_ok[t, a, b] = True
    col = np.arange(GRID_W)
    c_start = np.clip(col - NA_WIN_W // 2, 0, GRID_W - NA_WIN_W)
    in_win = (col[None, :] >= c_start[:, None]) & (col[None, :] < c_start[:, None] + NA_WIN_W)
    dc_idx = np.clip(col[None, :] - col[:, None], -(NA_WIN_W - 1), NA_WIN_W - 1) + NA_WIN_W - 1
    onehot = (in_win[:, :, None] & (dc_idx[:, :, None] == np.arange(2 * NA_WIN_W - 1)[None, None, :])).astype(np.float32)
    ok = row_ok[:, :, None, :, None] & in_win[None, None, :, None, :]
    neg = np.where(ok, 0.0, NEG_BIG).astype(np.float32).reshape(3, 1, NA_QB, NA_KW)
    return jnp.asarray(sel), jnp.asarray(onehot), jnp.asarray(neg)


def _na_slab(rpb, tables):
    sel, onehot, neg = tables
    a = jnp.einsum("tabd,hdc->thabc", sel, rpb, precision=HIGHEST)
    val = jnp.einsum("thabc,wuc->thawbu", a, onehot, precision=HIGHEST)
    return val.reshape(3, NA_HEADS, NA_QB, NA_KW) + neg


def _loss_call(y, target):
    s, d = y.shape
    tt = _pick(s, (512, 256, 128))

    def body(y_ref, t_ref, loss_ref, diff_ref):
        @pl.when(pl.program_id(0) == 0)
        def _():
            loss_ref[...] = jnp.zeros_like(loss_ref)

        e = y_ref[...] - t_ref[...]
        diff_ref[...] = e * (1.0 / d)
        loss_ref[...] += 0.5 * jnp.sum(jnp.mean(e * e, axis=-1, keepdims=True), axis=0, keepdims=True)

    row = pl.BlockSpec((tt, d), lambda i: (i, 0))
    return pl.pallas_call(
        body, grid=(s // tt,), in_specs=[row, row], out_specs=[pl.BlockSpec((1, 1), lambda i: (0, 0)), row],
        out_shape=[jax.ShapeDtypeStruct((1, 1), F32), jax.ShapeDtypeStruct((s, d), F32)],
        compiler_params=pltpu.CompilerParams(dimension_semantics=("arbitrary",)), name="loss_head",
    )(y, target)


@jax.custom_vjp
def loss_head(y, target):
    return _loss_call(y, target)[0][0, 0]


def _loss_head_fwd(y, target):
    loss, diff = _loss_call(y, target)
    return loss[0, 0], diff


def _loss_head_bwd(diff, g):
    return diff * g, jnp.zeros_like(diff)


loss_head.defvjp(_loss_head_fwd, _loss_head_bwd)


def _exchange(x, all_gather, name):
    blk = x.shape if all_gather else x.shape[1:]

    def body(x_ref, out_ref, send_sems, recv_sems, local_sem):
        mx, my, mc = lax.axis_index("x"), lax.axis_index("y"), lax.axis_index("c")
        me = 4 * mx + 2 * my + mc

        def src(p):
            return x_ref if all_gather else x_ref.at[p]

        mine = pltpu.make_async_copy(src(me), out_ref.at[me], local_sem)
        mine.start()
        copies = []
        for rel in range(1, N_DEV):
            px, py, pc = mx ^ (rel >> 2), my ^ ((rel >> 1) & 1), mc ^ (rel & 1)
            cp = pltpu.make_async_remote_copy(
                src_ref=src(4 * px + 2 * py + pc), dst_ref=out_ref.at[me],
                send_sem=send_sems.at[rel - 1], recv_sem=recv_sems.at[rel - 1],
                device_id=(px, py, pc), device_id_type=MESH)
            cp.start()
            copies.append(cp)
        for cp in copies:
            cp.wait_send()
        for cp in copies:
            cp.wait_recv()
        mine.wait()

    return pl.pallas_call(
        body,
        in_specs=[pl.BlockSpec(memory_space=pl.ANY)], out_specs=pl.BlockSpec(memory_space=pl.ANY),
        out_shape=jax.ShapeDtypeStruct((N_DEV,) + tuple(blk), x.dtype),
        scratch_shapes=[pltpu.SemaphoreType.DMA((N_DEV - 1,)), pltpu.SemaphoreType.DMA((N_DEV - 1,)),
                        pltpu.SemaphoreType.DMA(())],
        compiler_params=pltpu.CompilerParams(has_side_effects=True), name=name,
    )(x)


def _adam_call(parts, w, m, v, name):
    _, r, c = parts.shape
    tr = _pick(r, (256, 128))

    def body(p_ref, w_ref, m_ref, v_ref, g_ref, d_ref, nm_ref, nv_ref):
        g = p_ref[0].astype(F32)
        for s in range(1, N_DEV):
            g = g + p_ref[s].astype(F32)
        mn = ADAM_B1 * m_ref[...] + (1.0 - ADAM_B1) * g
        vn = ADAM_B2 * v_ref[...] + (1.0 - ADAM_B2) * (g * g)
        m_hat = mn / (1.0 - ADAM_B1 ** ADAM_STEP)
        v_hat = vn / (1.0 - ADAM_B2 ** ADAM_STEP)
        g_ref[...] = g
        d_ref[...] = -ADAM_LR * (m_hat / (jnp.sqrt(v_hat) + ADAM_EPS) + ADAM_WD * w_ref[...])
        nm_ref[...] = mn
        nv_ref[...] = vn

    row = pl.BlockSpec((tr, c), lambda i: (i, 0))
    out = jax.ShapeDtypeStruct((r, c), F32)
    return pl.pallas_call(
        body, grid=(r // tr,), in_specs=[pl.BlockSpec((N_DEV, tr, c), lambda i: (0, i, 0)), row, row, row],
        out_specs=[row, row, row, row], out_shape=[out, out, out, out],
        compiler_params=pltpu.CompilerParams(dimension_semantics=("parallel",), vmem_limit_bytes=VMEM_BIG_LIMIT),
        name=name,
    )(parts, w, m, v)


def _rows_of(a):
    return a.reshape(-1, D_MODEL)


def _shard_slabs(name, full):
    dep, k, n = full.shape
    if name in ROW_SHARDED:
        t = full.reshape(dep, N_DEV, k // N_DEV, n).transpose(1, 0, 2, 3)
    else:
        t = full.reshape(dep, k, N_DEV, n // N_DEV).transpose(2, 0, 1, 3)
    return t.reshape(N_DEV, -1, D_MODEL)


def _unshard(name, slabs, shard_shape):
    dep, k, n = shard_shape
    t = slabs.reshape((N_DEV, dep, k, n))
    if name in ROW_SHARDED:
        return t.transpose(1, 0, 2, 3).reshape(dep, N_DEV * k, n)
    return t.transpose(1, 2, 0, 3).reshape(dep, k, N_DEV * n)


def _pack_small(vals):
    flat = jnp.concatenate([vals[n].reshape(-1) for n in SMALL])
    rows = -(-flat.shape[0] // D_MODEL)
    rows = -(-rows // 8) * 8
    return jnp.pad(flat, (0, rows * D_MODEL - flat.shape[0])).reshape(rows, D_MODEL)


def _unpack_small(packed, like):
    flat, out, off = packed.reshape(-1), {}, 0
    for n in SMALL:
        size = like[n].size
        out[n] = flat[off:off + size].reshape(like[n].shape)
        off += size
    return out


def _rope_tables(s, n_ctx):
    t = jnp.arange(s)
    row = (t // GRID_W).astype(F32)
    col = (t % GRID_W).astype(F32)
    half = HEAD_DIM // 2
    inv = ROPE_THETA ** (-jnp.arange(0, half, 2, dtype=F32) / half)
    ang = jnp.concatenate([row[:, None] * inv, col[:, None] * inv], axis=-1)
    cos2 = jnp.repeat(jnp.cos(ang), 2, axis=-1)
    sin2 = jnp.repeat(jnp.sin(ang), 2, axis=-1)
    cos2 = jnp.concatenate([cos2, jnp.ones((n_ctx, HEAD_DIM), F32)], axis=0)
    sin2 = jnp.concatenate([sin2, jnp.zeros((n_ctx, HEAD_DIM), F32)], axis=0)
    return cos2, sin2


def _to_heads(t, n_heads):
    return t.reshape(t.shape[0], n_heads, HEAD_DIM).transpose(1, 0, 2)


def _from_heads(t):
    return t.transpose(1, 0, 2).reshape(t.shape[1], t.shape[0] * HEAD_DIM)


def _local_loss(p, x, c, ctx, target, consts):
    s, n_ctx = x.shape[0], ctx.shape[0]
    cos2, sin2, rot, na_tables = consts
    is_ctx = (jnp.arange(s + n_ctx) >= s)[:, None]
    seg = lambda rows2: jnp.where(is_ctx, rows2[1:2], rows2[0:1])
    xa = jnp.concatenate([x, ctx], axis=0)
    cond = jnp.concatenate([jax.nn.silu(c), jax.nn.silu(p["c_ctx"])[None, :],
                            jnp.zeros((MOD_ROWS - 2, D_MODEL), F32)], axis=0)

    for l in range(DEPTH):
        mod = (mm(cond, p["w_mod"][l]) + p["b_mod"][l])[:2]
        sh1, sc1, g1, sh2, sc2, g2 = jnp.split(mod, 6, axis=-1)

        h = norm_mod(xa, p["norm1"][l][None, :], sh1, sc1, s)
        na_q, na_k, na_v, gq, gk, gv, ga, gb = jnp.split(mm(h, p["w_in"][l]), IN_SPLITS, axis=-1)

        qa, ka, va = _to_heads(na_q, NA_HEADS), _to_heads(na_k, NA_HEADS), _to_heads(na_v, NA_HEADS)
        slab = _na_slab(p["na_rpb"][l], na_tables)
        ya_lat = na_attn(qa[:, :s], ka[:, :s], va[:, :s], ka[:, s:], va[:, s:], slab)
        ya_ctx = attn(qa[:, s:], ka[:, s:], va[:, s:])
        ya = _from_heads(jnp.concatenate([ya_lat, ya_ctx], axis=1))

        qk = _to_heads(jnp.concatenate([gq, gk], axis=-1), GQA_Q_HEADS + GQA_KV_HEADS)
        gains = jnp.stack([p["q_gain"][l], p["k_gain"][l]])
        qk = hn_rope(qk, gains, cos2, sin2, rot, GQA_Q_HEADS)
        qb, kb, vb = qk[:GQA_Q_HEADS], qk[GQA_Q_HEADS:], _to_heads(gv, GQA_KV_HEADS)
        ob_lat = attn(qb[:, :s].reshape(GQA_KV_HEADS, GQA_REP * s, HEAD_DIM), kb, vb)
        ob_ctx = attn(qb[:, s:].reshape(GQA_KV_HEADS, GQA_REP * n_ctx, HEAD_DIM), kb[:, s:], vb[:, s:])
        yb = _from_heads(jnp.concatenate([ob_lat.reshape(GQA_Q_HEADS, s, HEAD_DIM),
                                          ob_ctx.reshape(GQA_Q_HEADS, n_ctx, HEAD_DIM)], axis=1))

        merged = jax.nn.sigmoid(ga) * mm(ya, p["w_pa"][l]) + jax.nn.sigmoid(gb) * mm(yb, p["w_pb"][l])
        xa = xa + seg(g1) * mm(merged, p["w_o"][l])

        h2 = norm_mod(xa, p["norm2"][l][None, :], sh2, sc2, s)
        a, u = jnp.split(mm(h2, p["w_ffn_in"][l]), 2, axis=-1)
        xa = xa + seg(g2) * mm(jax.nn.silu(a) * u, p["w_ffn_out"][l])

    zeros2 = jnp.zeros((2, D_MODEL), F32)
    y = norm_mod(xa, p["final_norm"][None, :], zeros2, zeros2, s)[:s]
    return loss_head(y, target)


def kernel(x, c, ctx, c_ctx, w_mod, b_mod, norm1, w_in, na_rpb, q_gain, k_gain, w_pa, w_pb, w_o, norm2, w_ffn_in, w_ffn_out, final_norm, loss_target, m_c_ctx, m_w_mod, m_b_mod, m_norm1, m_w_in, m_na_rpb, m_q_gain, m_k_gain, m_w_pa, m_w_pb, m_w_o, m_norm2, m_w_ffn_in, m_w_ffn_out, m_final_norm, v_c_ctx, v_w_mod, v_b_mod, v_norm1, v_w_in, v_na_rpb, v_q_gain, v_k_gain, v_w_pa, v_w_pb, v_w_o, v_norm2, v_w_ffn_in, v_w_ffn_out, v_final_norm):
    w = dict(c_ctx=c_ctx, w_mod=w_mod, b_mod=b_mod, norm1=norm1, w_in=w_in, na_rpb=na_rpb, q_gain=q_gain, k_gain=k_gain,
             w_pa=w_pa, w_pb=w_pb, w_o=w_o, norm2=norm2, w_ffn_in=w_ffn_in, w_ffn_out=w_ffn_out, final_norm=final_norm)
    mom = dict(c_ctx=m_c_ctx, w_mod=m_w_mod, b_mod=m_b_mod, norm1=m_norm1, w_in=m_w_in, na_rpb=m_na_rpb, q_gain=m_q_gain,
               k_gain=m_k_gain, w_pa=m_w_pa, w_pb=m_w_pb, w_o=m_w_o, norm2=m_norm2, w_ffn_in=m_w_ffn_in,
               w_ffn_out=m_w_ffn_out, final_norm=m_final_norm)
    var = dict(c_ctx=v_c_ctx, w_mod=v_w_mod, b_mod=v_b_mod, norm1=v_norm1, w_in=v_w_in, na_rpb=v_na_rpb, q_gain=v_q_gain,
               k_gain=v_k_gain, w_pa=v_w_pa, w_pb=v_w_pb, w_o=v_w_o, norm2=v_norm2, w_ffn_in=v_w_ffn_in,
               w_ffn_out=v_w_ffn_out, final_norm=v_final_norm)
    s, n_ctx = x.shape[1], ctx.shape[1]

    pack_big = lambda d: jnp.concatenate([_rows_of(d[n]) for n in BIG], axis=0)
    gathered = _exchange(pack_big(w).astype(BF16), True, "gather_weights")
    params, off = {n: w[n] for n in SMALL}, 0
    for n in BIG:
        r = w[n].size // D_MODEL
        full = _unshard(n, gathered[:, off:off + r], w[n].shape)
        params[n] = [full[l].astype(F32) for l in range(DEPTH)]
        off += r

    consts = (*_rope_tables(s, n_ctx), _rot_matrix(), _na_tables(s // GRID_W))
    loss, (gp, gx) = jax.value_and_grad(_local_loss, argnums=(0, 1))(params, x[0], c, ctx[0], loss_target[0], consts)
    loss = lax.psum(loss, ("x", "y", "c"))

    parts = _exchange(jnp.concatenate([_shard_slabs(n, jnp.stack(gp[n]).astype(BF16)) for n in BIG], axis=1), False,
                      "scatter_grads")
    big = _adam_call(parts, pack_big(w), pack_big(mom), pack_big(var), "adam_big")
    parts_s = _exchange(_pack_small(gp), True, "gather_small_grads")
    small = _adam_call(parts_s, _pack_small(w), _pack_small(mom), _pack_small(var), "adam_small")

    outs = [{}, {}, {}, {}]
    for k in range(4):
        outs[k].update(_unpack_small(small[k], w))
        off = 0
        for n in BIG:
            r = w[n].size // D_MODEL
            outs[k][n] = big[k][off:off + r].reshape(w[n].shape)
            off += r
    return (loss, gx[None], *[o[n] for o in outs for n in WEIGHTS])
```

```python
import functools

import numpy as np
import jax
import jax.numpy as jnp
from jax import lax
from jax.experimental import pallas as pl
from jax.experimental.pallas import tpu as pltpu

F32 = jnp.float32
BF16 = jnp.bfloat16
HIGHEST = lax.Precision.HIGHEST

D_MODEL = 1024
DEPTH = 4
GRID_W = 64
HEAD_DIM = 64
NA_HEADS = 8
NA_WIN_H = 8
NA_WIN_W = 16
GQA_Q_HEADS = 8
GQA_KV_HEADS = 2
GQA_REP = GQA_Q_HEADS // GQA_KV_HEADS
NA_WIDTH = NA_HEADS * HEAD_DIM
GQA_Q_WIDTH = GQA_Q_HEADS * HEAD_DIM
GQA_KV_WIDTH = GQA_KV_HEADS * HEAD_DIM
IN_SIZES = (NA_WIDTH, NA_WIDTH, NA_WIDTH, GQA_Q_WIDTH, GQA_KV_WIDTH, GQA_KV_WIDTH, D_MODEL, D_MODEL)
IN_SPLITS = tuple(int(v) for v in np.cumsum(IN_SIZES)[:-1])
ROPE_THETA = 10000.0
EPS = 1e-6
SCALE = HEAD_DIM ** -0.5
ADAM_LR = 0.001
ADAM_B1 = 0.9
ADAM_B2 = 0.999
ADAM_EPS = 1e-08
ADAM_WD = 0.01
ADAM_STEP = 10

N_DEV = 8
MESH = pl.DeviceIdType.MESH
NEG_BIG = -1e30

VMEM_BIG_LIMIT = 52 * 1024 * 1024
NA_QROWS = 4
NA_QB = NA_QROWS * GRID_W
NA_KROWS = 12
NA_KW = NA_KROWS * GRID_W
MM_TILE_ELEMS = 512 * 2176
MM_OPERAND_ELEMS = 1408 * 2176
MOD_ROWS = 256

BIG = ("w_mod", "w_in", "w_pa", "w_pb", "w_o", "w_ffn_in", "w_ffn_out")
ROW_SHARDED = ("w_o", "w_ffn_out")
SMALL = ("c_ctx", "b_mod", "norm1", "na_rpb", "q_gain", "k_gain", "norm2", "final_norm")
WEIGHTS = ("c_ctx", "w_mod", "b_mod", "norm1", "w_in", "na_rpb", "q_gain", "k_gain", "w_pa", "w_pb", "w_o",
           "norm2", "w_ffn_in", "w_ffn_out", "final_norm")


def _pick(n, cands):
    for c in cands:
        if n % c == 0:
            return c
    return n


def _dot_nt(a, b):
    return lax.dot_general(a, b, (((1,), (1,)), ((), ())), preferred_element_type=F32)


def _dot_tn(a, b):
    return lax.dot_general(a, b, (((0,), (0,)), ((), ())), preferred_element_type=F32)


def _dot(a, b):
    return jnp.dot(a, b, preferred_element_type=F32)


def _mm_call(a, b, name, mode="nn"):
    (m, k) = a.shape if mode != "tn" else a.shape[::-1]
    n = b.shape[1] if mode != "nt" else b.shape[0]
    tn = _pick(n, (2176, 1408, 1024, 512, 256, 128))
    tm = _pick(m, tuple(t for t in (768, 512, 256, 128) if t * tn <= MM_TILE_ELEMS))
    tk = _pick(k, tuple(t for t in (2816, 2176, 1408, 1024, 768, 512, 256, 128) if t * max(tm, tn) <= MM_OPERAND_ELEMS))
    nk = k // tk
    dot = {"nn": _dot, "tn": _dot_tn, "nt": _dot_nt}[mode]
    a_blk = (tk, tm) if mode == "tn" else (tm, tk)
    b_blk = (tn, tk) if mode == "nt" else (tk, tn)
    a_idx = (lambda i, kk: (kk, i)) if mode == "tn" else (lambda i, kk: (i, kk))
    b_idx = (lambda j, kk: (j, kk)) if mode == "nt" else (lambda j, kk: (kk, j))

    def body(a_ref, b_ref, o_ref, acc_ref):
        kk = pl.program_id(2)
        part = dot(a_ref[...], b_ref[...])

        @pl.when(kk == 0)
        def _():
            acc_ref[...] = part

        @pl.when(kk > 0)
        def _():
            acc_ref[...] += part

        @pl.when(kk == nk - 1)
        def _():
            o_ref[...] = acc_ref[...]

    def body1(a_ref, b_ref, o_ref):
        o_ref[...] = dot(a_ref[...], b_ref[...])

    footprint = 2 * (tm * tk * 2 + tk * tn * 2 + tm * tn * 4) + 2 * tm * tn * 4
    limit = int(footprint + (8 << 20))
    if nk == 1:
        return pl.pallas_call(
            body1, grid=(n // tn, m // tm),
            in_specs=[pl.BlockSpec(a_blk, lambda j, i: a_idx(i, 0)), pl.BlockSpec(b_blk, lambda j, i: b_idx(j, 0))],
            out_specs=pl.BlockSpec((tm, tn), lambda j, i: (i, j)),
            out_shape=jax.ShapeDtypeStruct((m, n), F32),
            compiler_params=pltpu.CompilerParams(dimension_semantics=("parallel", "parallel"), vmem_limit_bytes=limit),
            name=name,
        )(a, b)
    return pl.pallas_call(
        body, grid=(m // tm, n // tn, nk),
        in_specs=[pl.BlockSpec(a_blk, lambda i, j, kk: a_idx(i, kk)), pl.BlockSpec(b_blk, lambda i, j, kk: b_idx(j, kk))],
        out_specs=pl.BlockSpec((tm, tn), lambda i, j, kk: (i, j)),
        out_shape=jax.ShapeDtypeStruct((m, n), F32),
        scratch_shapes=[pltpu.VMEM((tm, tn), F32)],
        compiler_params=pltpu.CompilerParams(
            dimension_semantics=("parallel", "parallel", "arbitrary"), vmem_limit_bytes=limit),
        name=name,
    )(a, b)


@jax.custom_vjp
def mm(x, w):
    return _mm_call(x.astype(BF16), w.astype(BF16), "mm_fwd")


def _mm_fwd(x, w):
    xb, wb = x.astype(BF16), w.astype(BF16)
    return _mm_call(xb, wb, "mm_fwd"), (xb, wb)


def _mm_bwd(res, dy):
    xb, wb = res
    dyb = dy.astype(BF16)
    return _mm_call(dyb, wb, "mm_dx", "nt"), _mm_call(xb, dyb, "mm_dw", "tn")


mm.defvjp(_mm_fwd, _mm_bwd)


def _seg_rows(ref, is_ctx):
    return jnp.where(is_ctx, ref[1:2, :], ref[0:1, :])


def _one_hot_row(second):
    return (lax.broadcasted_iota(jnp.int32, (2, 1), 0) == second.astype(jnp.int32)).astype(F32)


def _norm_mod_fwd_call(x, g, shift, scale, n_lat):
    t, d = x.shape
    tt = _pick(t, (256, 128))
    lat_tiles = n_lat // tt

    def body(x_ref, g_ref, sh_ref, sc_ref, y_ref):
        is_ctx = pl.program_id(0) >= lat_tiles
        xv = x_ref[...]
        r = lax.rsqrt(jnp.mean(xv * xv, axis=-1, keepdims=True) + EPS)
        yn = xv * r * g_ref[...]
        y_ref[...] = yn * (1.0 + _seg_rows(sc_ref, is_ctx)) + _seg_rows(sh_ref, is_ctx)

    row = pl.BlockSpec((tt, d), lambda i: (i, 0))
    full = lambda rws: pl.BlockSpec((rws, d), lambda i: (0, 0))
    return pl.pallas_call(
        body, grid=(t // tt,), in_specs=[row, full(1), full(2), full(2)], out_specs=row,
        out_shape=jax.ShapeDtypeStruct((t, d), F32),
        compiler_params=pltpu.CompilerParams(dimension_semantics=("parallel",)), name="norm_mod_fwd",
    )(x, g, shift, scale)


def _norm_mod_bwd_call(x, g, scale, dy, n_lat):
    t, d = x.shape
    tt = _pick(t, (256, 128))
    lat_tiles = n_lat // tt

    def body(x_ref, g_ref, sc_ref, dy_ref, dx_ref, dg_ref, dsh_ref, dsc_ref):
        i = pl.program_id(0)
        is_ctx = i >= lat_tiles

        @pl.when(i == 0)
        def _():
            dg_ref[...] = jnp.zeros_like(dg_ref)
            dsh_ref[...] = jnp.zeros_like(dsh_ref)
            dsc_ref[...] = jnp.zeros_like(dsc_ref)

        xv, dyv, gv = x_ref[...], dy_ref[...], g_ref[...]
        r = lax.rsqrt(jnp.mean(xv * xv, axis=-1, keepdims=True) + EPS)
        nrm = xv * r
        yn = nrm * gv
        seg = _one_hot_row(is_ctx)
        dsh_ref[...] += seg * jnp.sum(dyv, axis=0, keepdims=True)
        dsc_ref[...] += seg * jnp.sum(dyv * yn, axis=0, keepdims=True)
        dyn = dyv * (1.0 + _seg_rows(sc_ref, is_ctx))
        dg_ref[...] += jnp.sum(dyn * nrm, axis=0, keepdims=True)
        dn = dyn * gv
        dx_ref[...] = r * (dn - nrm * jnp.mean(dn * nrm, axis=-1, keepdims=True))

    row = pl.BlockSpec((tt, d), lambda i: (i, 0))
    full = lambda rws: pl.BlockSpec((rws, d), lambda i: (0, 0))
    return pl.pallas_call(
        body, grid=(t // tt,), in_specs=[row, full(1), full(2), row], out_specs=[row, full(1), full(2), full(2)],
        out_shape=[jax.ShapeDtypeStruct((t, d), F32), jax.ShapeDtypeStruct((1, d), F32),
                   jax.ShapeDtypeStruct((2, d), F32), jax.ShapeDtypeStruct((2, d), F32)],
        compiler_params=pltpu.CompilerParams(dimension_semantics=("arbitrary",)), name="norm_mod_bwd",
    )(x, g, scale, dy)


@functools.partial(jax.custom_vjp, nondiff_argnums=(4,))
def norm_mod(x, g, shift, scale, n_lat):
    return _norm_mod_fwd_call(x, g, shift, scale, n_lat)


def _norm_mod_fwd(x, g, shift, scale, n_lat):
    return _norm_mod_fwd_call(x, g, shift, scale, n_lat), (x, g, scale)


def _norm_mod_bwd(n_lat, res, dy):
    x, g, scale = res
    dx, dg, dsh, dsc = _norm_mod_bwd_call(x, g, scale, dy, n_lat)
    return dx, dg, dsh, dsc


norm_mod.defvjp(_norm_mod_fwd, _norm_mod_bwd)


def _rot_matrix():
    p = np.zeros((HEAD_DIM, HEAD_DIM), np.float32)
    for i in range(HEAD_DIM // 2):
        p[2 * i + 1, 2 * i] = -1.0
        p[2 * i, 2 * i + 1] = 1.0
    return jnp.asarray(p)


def _hn_rope_fwd_call(x, gains, cos2, sin2, rot, n_q):
    nh, t, dh = x.shape
    tt = _pick(t, (2816, 1024, 768, 512, 256, 128))

    def body(x_ref, g_ref, cos_ref, sin_ref, rot_ref, y_ref):
        gv = jnp.where(pl.program_id(0) >= n_q, g_ref[1:2, :], g_ref[0:1, :])
        xv = x_ref[...]
        r = lax.rsqrt(jnp.mean(xv * xv, axis=-1, keepdims=True) + EPS)
        y = xv * r * gv
        yr = jnp.dot(y, rot_ref[...], precision=HIGHEST, preferred_element_type=F32)
        y_ref[...] = y * cos_ref[...] + yr * sin_ref[...]

    blk = pl.BlockSpec((None, tt, dh), lambda h, i: (h, i, 0))
    tab = pl.BlockSpec((tt, dh), lambda h, i: (i, 0))
    return pl.pallas_call(
        body, grid=(nh, t // tt),
        in_specs=[blk, pl.BlockSpec((2, dh), lambda h, i: (0, 0)), tab, tab, pl.BlockSpec((dh, dh), lambda h, i: (0, 0))],
        out_specs=blk, out_shape=jax.ShapeDtypeStruct((nh, t, dh), F32),
        compiler_params=pltpu.CompilerParams(dimension_semantics=("parallel", "parallel")), name="hn_rope_fwd",
    )(x, gains, cos2, sin2, rot)


def _hn_rope_bwd_call(x, gains, cos2, sin2, rot, dy, n_q):
    nh, t, dh = x.shape
    tt = _pick(t, (2816, 1024, 768, 512, 256, 128))

    def body(x_ref, g_ref, cos_ref, sin_ref, rot_ref, dy_ref, dx_ref, dg_ref):
        h, i = pl.program_id(0), pl.program_id(1)
        is_k = h >= n_q

        @pl.when((h == 0) & (i == 0))
        def _():
            dg_ref[...] = jnp.zeros_like(dg_ref)

        gv = jnp.where(is_k, g_ref[1:2, :], g_ref[0:1, :])
        xv, dyv = x_ref[...], dy_ref[...]
        r = lax.rsqrt(jnp.mean(xv * xv, axis=-1, keepdims=True) + EPS)
        nrm = xv * r
        dyn = dyv * cos_ref[...] - jnp.dot(dyv * sin_ref[...], rot_ref[...], precision=HIGHEST, preferred_element_type=F32)
        seg = _one_hot_row(is_k)
        dg_ref[...] += seg * jnp.sum(dyn * nrm, axis=0, keepdims=True)
        dn = dyn * gv
        dx_ref[...] = r * (dn - nrm * jnp.mean(dn * nrm, axis=-1, keepdims=True))

    blk = pl.BlockSpec((None, tt, dh), lambda h, i: (h, i, 0))
    tab = pl.BlockSpec((tt, dh), lambda h, i: (i, 0))
    g_spec = pl.BlockSpec((2, dh), lambda h, i: (0, 0))
    return pl.pallas_call(
        body, grid=(nh, t // tt),
        in_specs=[blk, g_spec, tab, tab, pl.BlockSpec((dh, dh), lambda h, i: (0, 0)), blk],
        out_specs=[blk, g_spec],
        out_shape=[jax.ShapeDtypeStruct((nh, t, dh), F32), jax.ShapeDtypeStruct((2, dh), F32)],
        compiler_params=pltpu.CompilerParams(dimension_semantics=("arbitrary", "arbitrary")), name="hn_rope_bwd",
    )(x, gains, cos2, sin2, rot, dy)


@functools.partial(jax.custom_vjp, nondiff_argnums=(5,))
def hn_rope(x, gains, cos2, sin2, rot, n_q):
    return _hn_rope_fwd_call(x, gains, cos2, sin2, rot, n_q)


def _hn_rope_fwd(x, gains, cos2, sin2, rot, n_q):
    return _hn_rope_fwd_call(x, gains, cos2, sin2, rot, n_q), (x, gains, cos2, sin2, rot)


def _hn_rope_bwd(n_q, res, dy):
    x, gains, cos2, sin2, rot = res
    dx, dg = _hn_rope_bwd_call(x, gains, cos2, sin2, rot, dy, n_q)
    return dx, dg, jnp.zeros_like(cos2), jnp.zeros_like(sin2), jnp.zeros_like(rot)


hn_rope.defvjp(_hn_rope_fwd, _hn_rope_bwd)


ATT_TK = 256


def _att_tq(rn):
    return _pick(rn, (512, 256, 128))


def _att_unroll(nkb):
    return _pick(nkb, (3, 2))


def _flash_fwd_call(q_t, k, v_t):
    g, dh, rn = q_t.shape
    nk = k.shape[1]
    tq, tk = _att_tq(rn), ATT_TK
    nkb = nk // tk
    unroll = _att_unroll(nkb)

    def body(qt_ref, k_ref, vt_ref, ot_ref, lse_ref):
        qst = qt_ref[...] * SCALE

        def trip(t, carry):
            m, l, acc = carry
            blocks = [t * unroll + u for u in range(unroll)]
            s_t = [_dot(k_ref[pl.ds(pl.multiple_of(j * tk, tk), tk), :], qst) for j in blocks]
            m_new = functools.reduce(jnp.maximum, [jnp.max(s, axis=0, keepdims=True) for s in s_t], m)
            p_t = [jnp.exp(s - m_new) for s in s_t]
            alpha = jnp.exp(m - m_new)
            l = alpha * l + sum(jnp.sum(p, axis=0, keepdims=True) for p in p_t)
            acc = alpha * acc + sum(_dot(vt_ref[j], p.astype(BF16)) for j, p in zip(blocks, p_t))
            return m_new, l, acc

        m, l, acc = lax.fori_loop(
            0, nkb // unroll, trip, (jnp.full((1, tq), NEG_BIG, F32), jnp.zeros((1, tq), F32), jnp.zeros((dh, tq), F32)))
        ot_ref[...] = acc / l
        lse_ref[...] = m + jnp.log(l)

    col = pl.BlockSpec((None, dh, tq), lambda a, i: (a, 0, i))
    vec = pl.BlockSpec((None, 1, tq), lambda a, i: (a, 0, i))
    return pl.pallas_call(
        body, grid=(g, rn // tq),
        in_specs=[col, pl.BlockSpec((None, nk, dh), lambda a, i: (a, 0, 0)),
                  pl.BlockSpec((None, nkb, dh, tk), lambda a, i: (a, 0, 0, 0))],
        out_specs=[col, vec],
        out_shape=[jax.ShapeDtypeStruct((g, dh, rn), F32), jax.ShapeDtypeStruct((g, 1, rn), F32)],
        compiler_params=pltpu.CompilerParams(dimension_semantics=("parallel", "parallel"), vmem_limit_bytes=VMEM_BIG_LIMIT),
        name="flash_fwd",
    )(q_t, k, v_t)


def _flash_bwd_call(q, q_t, k, k_t, v, do, do_t, o_t, lse_t):
    g, rn, dh = q.shape
    nk = k.shape[1]
    tq, tk = _att_tq(rn), ATT_TK
    nkb = nk // tk
    unroll = _att_unroll(nkb)

    def body(q_ref, qt_ref, k_ref, kt_ref, v_ref, do_ref, dot_ref, ot_ref, lse_ref, dqt_ref, dk_ref, dv_ref):
        @pl.when(pl.program_id(1) == 0)
        def _():
            dk_ref[...] = jnp.zeros_like(dk_ref)
            dv_ref[...] = jnp.zeros_like(dv_ref)

        qs = q_ref[...] * SCALE
        qst = qt_ref[...] * SCALE
        dov, dotv = do_ref[...], dot_ref[...]
        delta = jnp.sum(dotv.astype(F32) * ot_ref[...], axis=0, keepdims=True)
        lse = lse_ref[...]

        def step(j, dqt):
            off = pl.multiple_of(j * tk, tk)
            kj = k_ref[pl.ds(off, tk), :]
            vj = v_ref[pl.ds(off, tk), :]
            p_t = jnp.exp(_dot(kj, qst) - lse)
            ds_t = (p_t * (_dot(vj, dotv) - delta)).astype(BF16)
            dv_ref[pl.ds(off, tk), :] += _dot(p_t.astype(BF16), dov)
            dk_ref[pl.ds(off, tk), :] += _dot(ds_t, qs)
            return dqt + _dot(kt_ref[j], ds_t)

        def trip(t, dqt):
            for u in range(unroll):
                dqt = step(t * unroll + u, dqt)
            return dqt

        dqt_ref[...] = lax.fori_loop(0, nkb // unroll, trip, jnp.zeros((dh, tq), F32)) * SCALE

    row = pl.BlockSpec((None, tq, dh), lambda a, i: (a, i, 0))
    col = pl.BlockSpec((None, dh, tq), lambda a, i: (a, 0, i))
    kv = pl.BlockSpec((None, nk, dh), lambda a, i: (a, 0, 0))
    return pl.pallas_call(
        body, grid=(g, rn // tq),
        in_specs=[row, col, kv, pl.BlockSpec((None, nkb, dh, tk), lambda a, i: (a, 0, 0, 0)), kv, row, col, col,
                  pl.BlockSpec((None, 1, tq), lambda a, i: (a, 0, i))],
        out_specs=[col, kv, kv],
        out_shape=[jax.ShapeDtypeStruct((g, dh, rn), F32), jax.ShapeDtypeStruct((g, nk, dh), F32),
                   jax.ShapeDtypeStruct((g, nk, dh), F32)],
        compiler_params=pltpu.CompilerParams(dimension_semantics=("arbitrary", "arbitrary"), vmem_limit_bytes=VMEM_BIG_LIMIT),
        name="flash_bwd",
    )(q, q_t, k, k_t, v, do, do_t, o_t, lse_t)


def _key_blocks_t(t):
    g, nk, dh = t.shape
    return t.reshape(g, nk // ATT_TK, ATT_TK, dh).transpose(0, 1, 3, 2)


@jax.custom_vjp
def attn(q, k, v):
    return _attn_fwd(q, k, v)[0]


def _attn_fwd(q, k, v):
    qb, kb, vb = q.astype(BF16), k.astype(BF16), v.astype(BF16)
    q_t = qb.transpose(0, 2, 1)
    o_t, lse_t = _flash_fwd_call(q_t, kb, _key_blocks_t(vb))
    return o_t.transpose(0, 2, 1), (qb, q_t, kb, vb, o_t, lse_t)


def _attn_bwd(res, do):
    qb, q_t, kb, vb, o_t, lse_t = res
    dob = do.astype(BF16)
    dq_t, dk, dv = _flash_bwd_call(qb, q_t, kb, _key_blocks_t(kb), vb, dob, dob.transpose(0, 2, 1), o_t, lse_t)
    return dq_t.transpose(0, 2, 1), dk, dv


attn.defvjp(_attn_fwd, _attn_bwd)


def _na_key_row(i, rows):
    return jnp.clip(NA_QROWS * i - NA_WIN_H // 2, 0, rows - NA_KROWS)


def _na_type(i, nb):
    return jnp.where(i == 0, 0, jnp.where(i == nb - 1, 2, 1))


def _na_fwd_call(q, k, v, kc, vc, slab):
    h, s, dh = q.shape
    n_ctx = kc.shape[1]
    rows, nb = s // GRID_W, s // NA_QB

    def body(q_ref, k_ref, v_ref, kc_ref, vc_ref, slab_ref, o_ref, lse_ref):
        off = pl.multiple_of(_na_key_row(pl.program_id(1), rows) * GRID_W, NA_QB)
        qs = q_ref[...] * SCALE
        kw = k_ref[pl.ds(off, NA_KW), :]
        vw = v_ref[pl.ds(off, NA_KW), :]
        sw = _dot_nt(qs, kw) + slab_ref[...]
        sc = _dot_nt(qs, kc_ref[...])
        m = jnp.maximum(jnp.max(sw, axis=1, keepdims=True), jnp.max(sc, axis=1, keepdims=True))
        pw = jnp.exp(sw - m)
        pc = jnp.exp(sc - m)
        l = jnp.sum(pw, axis=1, keepdims=True) + jnp.sum(pc, axis=1, keepdims=True)
        o_ref[...] = (_dot(pw.astype(BF16), vw) + _dot(pc.astype(BF16), vc_ref[...])) / l
        lse_ref[...] = m + jnp.log(l)

    qblk = pl.BlockSpec((None, NA_QB, dh), lambda a, i: (a, i, 0))
    kv = pl.BlockSpec((None, s, dh), lambda a, i: (a, 0, 0))
    ckv = pl.BlockSpec((None, n_ctx, dh), lambda a, i: (a, 0, 0))
    return pl.pallas_call(
        body, grid=(h, nb),
        in_specs=[qblk, kv, kv, ckv, ckv, pl.BlockSpec((None, None, NA_QB, NA_KW), lambda a, i: (_na_type(i, nb), a, 0, 0))],
        out_specs=[qblk, pl.BlockSpec((None, NA_QB, 1), lambda a, i: (a, i, 0))],
        out_shape=[jax.ShapeDtypeStruct((h, s, dh), F32), jax.ShapeDtypeStruct((h, s, 1), F32)],
        compiler_params=pltpu.CompilerParams(dimension_semantics=("parallel", "parallel"), vmem_limit_bytes=VMEM_BIG_LIMIT),
        name="na_fwd",
    )(q, k, v, kc, vc, slab)


def _na_bwd_call(q, k, v, kc, vc, slab, do, o, lse):
    h, s, dh = q.shape
    n_ctx = kc.shape[1]
    rows, nb = s // GRID_W, s // NA_QB

    def body(q_ref, k_ref, v_ref, kc_ref, vc_ref, slab_ref, do_ref, o_ref, lse_ref,
             dq_ref, dk_ref, dv_ref, dkc_ref, dvc_ref, dslab_ref):
        i = pl.program_id(1)

        @pl.when(i == 0)
        def _():
            dk_ref[...] = jnp.zeros_like(dk_ref)
            dv_ref[...] = jnp.zeros_like(dv_ref)
            dkc_ref[...] = jnp.zeros_like(dkc_ref)
            dvc_ref[...] = jnp.zeros_like(dvc_ref)

        @pl.when((i <= 1) | (i == nb - 1))
        def _():
            dslab_ref[...] = jnp.zeros_like(dslab_ref)

        off = pl.multiple_of(_na_key_row(i, rows) * GRID_W, NA_QB)
        qs = q_ref[...] * SCALE
        kw = k_ref[pl.ds(off, NA_KW), :]
        vw = v_ref[pl.ds(off, NA_KW), :]
        kcv, vcv = kc_ref[...], vc_ref[...]
        dov = do_ref[...]
        lse = lse_ref[...]
        delta = jnp.sum(dov.astype(F32) * o_ref[...], axis=1, keepdims=True)
        pw = jnp.exp(_dot_nt(qs, kw) + slab_ref[...] - lse)
        pc = jnp.exp(_dot_nt(qs, kcv) - lse)
        dsw = pw * (_dot_nt(dov, vw) - delta)
        dsc = pc * (_dot_nt(dov, vcv) - delta)
        dslab_ref[...] += dsw
        dsw, dsc = dsw.astype(BF16), dsc.astype(BF16)
        dq_ref[...] = (_dot(dsw, kw) + _dot(dsc, kcv)) * SCALE
        dk_ref[pl.ds(off, NA_KW), :] += _dot_tn(dsw, qs)
        dv_ref[pl.ds(off, NA_KW), :] += _dot_tn(pw.astype(BF16), dov)
        dkc_ref[...] += _dot_tn(dsc, qs)
        dvc_ref[...] += _dot_tn(pc.astype(BF16), dov)

    qblk = pl.BlockSpec((None, NA_QB, dh), lambda a, i: (a, i, 0))
    kv = pl.BlockSpec((None, s, dh), lambda a, i: (a, 0, 0))
    ckv = pl.BlockSpec((None, n_ctx, dh), lambda a, i: (a, 0, 0))
    slab_spec = pl.BlockSpec((None, None, NA_QB, NA_KW), lambda a, i: (_na_type(i, nb), a, 0, 0))
    return pl.pallas_call(
        body, grid=(h, nb),
        in_specs=[qblk, kv, kv, ckv, ckv, slab_spec, qblk, qblk, pl.BlockSpec((None, NA_QB, 1), lambda a, i: (a, i, 0))],
        out_specs=[qblk, kv, kv, ckv, ckv, slab_spec],
        out_shape=[jax.ShapeDtypeStruct((h, s, dh), F32), jax.ShapeDtypeStruct((h, s, dh), F32),
                   jax.ShapeDtypeStruct((h, s, dh), F32), jax.ShapeDtypeStruct((h, n_ctx, dh), F32),
                   jax.ShapeDtypeStruct((h, n_ctx, dh), F32), jax.ShapeDtypeStruct(slab.shape, F32)],
        compiler_params=pltpu.CompilerParams(dimension_semantics=("arbitrary", "arbitrary"), vmem_limit_bytes=VMEM_BIG_LIMIT),
        name="na_bwd",
    )(q, k, v, kc, vc, slab, do, o, lse)


@jax.custom_vjp
def na_attn(q, k, v, kc, vc, slab):
    return _na_fwd_call(q.astype(BF16), k.astype(BF16), v.astype(BF16), kc.astype(BF16), vc.astype(BF16), slab)[0]


def _na_attn_fwd(q, k, v, kc, vc, slab):
    qb, kb, vb, kcb, vcb = (t.astype(BF16) for t in (q, k, v, kc, vc))
    o, lse = _na_fwd_call(qb, kb, vb, kcb, vcb, slab)
    return o, (qb, kb, vb, kcb, vcb, slab, o, lse)


def _na_attn_bwd(res, do):
    qb, kb, vb, kcb, vcb, slab, o, lse = res
    return tuple(_na_bwd_call(qb, kb, vb, kcb, vcb, slab, do.astype(BF16), o, lse))


na_attn.defvjp(_na_attn_fwd, _na_attn_bwd)


def _na_tables(rows):
    sel = np.zeros((3, NA_QROWS, NA_KROWS, 2 * NA_WIN_H - 1), np.float32)
    row_ok = np.zeros((3, NA_QROWS, NA_KROWS), bool)
    for t, (r0, ks) in enumerate([(0, 0), (NA_QROWS, 0), (rows - NA_QROWS, rows - NA_KROWS)]):
        for a in range(NA_QROWS):
            r = r0 + a
            rs = min(max(r - NA_WIN_H // 2, 0), rows - NA_WIN_H)
            for b in range(NA_KROWS):
                kr = ks + b
                if rs <= kr < rs + NA_WIN_H:
                    sel[t, a, b, kr - r + NA_WIN_H - 1] = 1.0
                    row_ok[t, a, b] = True
    col = np.arange(GRID_W)
    c_start = np.clip(col - NA_WIN_W // 2, 0, GRID_W - NA_WIN_W)
    in_win = (col[None, :] >= c_start[:, None]) & (col[None, :] < c_start[:, None] + NA_WIN_W)
    dc_idx = np.clip(col[None, :] - col[:, None], -(NA_WIN_W - 1), NA_WIN_W - 1) + NA_WIN_W - 1
    onehot = (in_win[:, :, None] & (dc_idx[:, :, None] == np.arange(2 * NA_WIN_W - 1)[None, None, :])).astype(np.float32)
    ok = row_ok[:, :, None, :, None] & in_win[None, None, :, None, :]
    neg = np.where(ok, 0.0, NEG_BIG).astype(np.float32).reshape(3, 1, NA_QB, NA_KW)
    return jnp.asarray(sel), jnp.asarray(onehot), jnp.asarray(neg)


def _na_slab(rpb, tables):
    sel, onehot, neg = tables
    a = jnp.einsum("tabd,hdc->thabc", sel, rpb, precision=HIGHEST)
    val = jnp.einsum("thabc,wuc->thawbu", a, onehot, precision=HIGHEST)
    return val.reshape(3, NA_HEADS, NA_QB, NA_KW) + neg


def _loss_call(y, target):
    s, d = y.shape
    tt = _pick(s, (512, 256, 128))

    def body(y_ref, t_ref, loss_ref, diff_ref):
        @pl.when(pl.program_id(0) == 0)
        def _():
            loss_ref[...] = jnp.zeros_like(loss_ref)

        e = y_ref[...] - t_ref[...]
        diff_ref[...] = e * (1.0 / d)
        loss_ref[...] += 0.5 * jnp.sum(jnp.mean(e * e, axis=-1, keepdims=True), axis=0, keepdims=True)

    row = pl.BlockSpec((tt, d), lambda i: (i, 0))
    return pl.pallas_call(
        body, grid=(s // tt,), in_specs=[row, row], out_specs=[pl.BlockSpec((1, 1), lambda i: (0, 0)), row],
        out_shape=[jax.ShapeDtypeStruct((1, 1), F32), jax.ShapeDtypeStruct((s, d), F32)],
        compiler_params=pltpu.CompilerParams(dimension_semantics=("arbitrary",)), name="loss_head",
    )(y, target)


@jax.custom_vjp
def loss_head(y, target):
    return _loss_call(y, target)[0][0, 0]


def _loss_head_fwd(y, target):
    loss, diff = _loss_call(y, target)
    return loss[0, 0], diff


def _loss_head_bwd(diff, g):
    return diff * g, jnp.zeros_like(diff)


loss_head.defvjp(_loss_head_fwd, _loss_head_bwd)


def _exchange(x, all_gather, name):
    blk = x.shape if all_gather else x.shape[1:]

    def body(x_ref, out_ref, send_sems, recv_sems, local_sem):
        mx, my, mc = lax.axis_index("x"), lax.axis_index("y"), lax.axis_index("c")
        me = 4 * mx + 2 * my + mc

        def src(p):
            return x_ref if all_gather else x_ref.at[p]

        mine = pltpu.make_async_copy(src(me), out_ref.at[me], local_sem)
        mine.start()
        copies = []
        for rel in range(1, N_DEV):
            px, py, pc = mx ^ (rel >> 2), my ^ ((rel >> 1) & 1), mc ^ (rel & 1)
            cp = pltpu.make_async_remote_copy(
                src_ref=src(4 * px + 2 * py + pc), dst_ref=out_ref.at[me],
                send_sem=send_sems.at[rel - 1], recv_sem=recv_sems.at[rel - 1],
                device_id=(px, py, pc), device_id_type=MESH)
            cp.start()
            copies.append(cp)
        for cp in copies:
            cp.wait_send()
        for cp in copies:
            cp.wait_recv()
        mine.wait()

    return pl.pallas_call(
        body,
        in_specs=[pl.BlockSpec(memory_space=pl.ANY)], out_specs=pl.BlockSpec(memory_space=pl.ANY),
        out_shape=jax.ShapeDtypeStruct((N_DEV,) + tuple(blk), x.dtype),
        scratch_shapes=[pltpu.SemaphoreType.DMA((N_DEV - 1,)), pltpu.SemaphoreType.DMA((N_DEV - 1,)),
                        pltpu.SemaphoreType.DMA(())],
        compiler_params=pltpu.CompilerParams(has_side_effects=True), name=name,
    )(x)


def _adam_call(parts, w, m, v, name):
    _, r, c = parts.shape
    tr = _pick(r, (256, 128))

    def body(p_ref, w_ref, m_ref, v_ref, g_ref, d_ref, nm_ref, nv_ref):
        g = p_ref[0].astype(F32)
        for s in range(1, N_DEV):
            g = g + p_ref[s].astype(F32)
        mn = ADAM_B1 * m_ref[...] + (1.0 - ADAM_B1) * g
        vn = ADAM_B2 * v_ref[...] + (1.0 - ADAM_B2) * (g * g)
        m_hat = mn / (1.0 - ADAM_B1 ** ADAM_STEP)
        v_hat = vn / (1.0 - ADAM_B2 ** ADAM_STEP)
        g_ref[...] = g
        d_ref[...] = -ADAM_LR * (m_hat / (jnp.sqrt(v_hat) + ADAM_EPS) + ADAM_WD * w_ref[...])
        nm_ref[...] = mn
        nv_ref[...] = vn

    row = pl.BlockSpec((tr, c), lambda i: (i, 0))
    out = jax.ShapeDtypeStruct((r, c), F32)
    return pl.pallas_call(
        body, grid=(r // tr,), in_specs=[pl.BlockSpec((N_DEV, tr, c), lambda i: (0, i, 0)), row, row, row],
        out_specs=[row, row, row, row], out_shape=[out, out, out, out],
        compiler_params=pltpu.CompilerParams(dimension_semantics=("parallel",), vmem_limit_bytes=VMEM_BIG_LIMIT),
        name=name,
    )(parts, w, m, v)


def _rows_of(a):
    return a.reshape(-1, D_MODEL)


def _shard_slabs(name, full):
    dep, k, n = full.shape
    if name in ROW_SHARDED:
        t = full.reshape(dep, N_DEV, k // N_DEV, n).transpose(1, 0, 2, 3)
    else:
        t = full.reshape(dep, k, N_DEV, n // N_DEV).transpose(2, 0, 1, 3)
    return t.reshape(N_DEV, -1, D_MODEL)


def _unshard(name, slabs, shard_shape):
    dep, k, n = shard_shape
    t = slabs.reshape((N_DEV, dep, k, n))
    if name in ROW_SHARDED:
        return t.transpose(1, 0, 2, 3).reshape(dep, N_DEV * k, n)
    return t.transpose(1, 2, 0, 3).reshape(dep, k, N_DEV * n)


def _pack_small(vals):
    flat = jnp.concatenate([vals[n].reshape(-1) for n in SMALL])
    rows = -(-flat.shape[0] // D_MODEL)
    rows = -(-rows // 8) * 8
    return jnp.pad(flat, (0, rows * D_MODEL - flat.shape[0])).reshape(rows, D_MODEL)


def _unpack_small(packed, like):
    flat, out, off = packed.reshape(-1), {}, 0
    for n in SMALL:
        size = like[n].size
        out[n] = flat[off:off + size].reshape(like[n].shape)
        off += size
    return out


def _rope_tables(s, n_ctx):
    t = jnp.arange(s)
    row = (t // GRID_W).astype(F32)
    col = (t % GRID_W).astype(F32)
    half = HEAD_DIM // 2
    inv = ROPE_THETA ** (-jnp.arange(0, half, 2, dtype=F32) / half)
    ang = jnp.concatenate([row[:, None] * inv, col[:, None] * inv], axis=-1)
    cos2 = jnp.repeat(jnp.cos(ang), 2, axis=-1)
    sin2 = jnp.repeat(jnp.sin(ang), 2, axis=-1)
    cos2 = jnp.concatenate([cos2, jnp.ones((n_ctx, HEAD_DIM), F32)], axis=0)
    sin2 = jnp.concatenate([sin2, jnp.zeros((n_ctx, HEAD_DIM), F32)], axis=0)
    return cos2, sin2


def _to_heads(t, n_heads):
    return t.reshape(t.shape[0], n_heads, HEAD_DIM).transpose(1, 0, 2)


def _from_heads(t):
    return t.transpose(1, 0, 2).reshape(t.shape[1], t.shape[0] * HEAD_DIM)


def _local_loss(p, x, c, ctx, target, consts):
    s, n_ctx = x.shape[0], ctx.shape[0]
    cos2, sin2, rot, na_tables = consts
    is_ctx = (jnp.arange(s + n_ctx) >= s)[:, None]
    seg = lambda rows2: jnp.where(is_ctx, rows2[1:2], rows2[0:1])
    xa = jnp.concatenate([x, ctx], axis=0)
    cond = jnp.concatenate([jax.nn.silu(c), jax.nn.silu(p["c_ctx"])[None, :],
                            jnp.zeros((MOD_ROWS - 2, D_MODEL), F32)], axis=0)

    for l in range(DEPTH):
        mod = (mm(cond, p["w_mod"][l]) + p["b_mod"][l])[:2]
        sh1, sc1, g1, sh2, sc2, g2 = jnp.split(mod, 6, axis=-1)

        h = norm_mod(xa, p["norm1"][l][None, :], sh1, sc1, s)
        na_q, na_k, na_v, gq, gk, gv, ga, gb = jnp.split(mm(h, p["w_in"][l]), IN_SPLITS, axis=-1)

        qa, ka, va = _to_heads(na_q, NA_HEADS), _to_heads(na_k, NA_HEADS), _to_heads(na_v, NA_HEADS)
        slab = _na_slab(p["na_rpb"][l], na_tables)
        ya_lat = na_attn(qa[:, :s], ka[:, :s], va[:, :s], ka[:, s:], va[:, s:], slab)
        ya_ctx = attn(qa[:, s:], ka[:, s:], va[:, s:])
        ya = _from_heads(jnp.concatenate([ya_lat, ya_ctx], axis=1))

        qk = _to_heads(jnp.concatenate([gq, gk], axis=-1), GQA_Q_HEADS + GQA_KV_HEADS)
        gains = jnp.stack([p["q_gain"][l], p["k_gain"][l]])
        qk = hn_rope(qk, gains, cos2, sin2, rot, GQA_Q_HEADS)
        qb, kb, vb = qk[:GQA_Q_HEADS], qk[GQA_Q_HEADS:], _to_heads(gv, GQA_KV_HEADS)
        ob_lat = attn(qb[:, :s].reshape(GQA_KV_HEADS, GQA_REP * s, HEAD_DIM), kb, vb)
        ob_ctx = attn(qb[:, s:].reshape(GQA_KV_HEADS, GQA_REP * n_ctx, HEAD_DIM), kb[:, s:], vb[:, s:])
        yb = _from_heads(jnp.concatenate([ob_lat.reshape(GQA_Q_HEADS, s, HEAD_DIM),
                                          ob_ctx.reshape(GQA_Q_HEADS, n_ctx, HEAD_DIM)], axis=1))

        merged = jax.nn.sigmoid(ga) * mm(ya, p["w_pa"][l]) + jax.nn.sigmoid(gb) * mm(yb, p["w_pb"][l])
        xa = xa + seg(g1) * mm(merged, p["w_o"][l])

        h2 = norm_mod(xa, p["norm2"][l][None, :], sh2, sc2, s)
        a, u = jnp.split(mm(h2, p["w_ffn_in"][l]), 2, axis=-1)
        xa = xa + seg(g2) * mm(jax.nn.silu(a) * u, p["w_ffn_out"][l])

    zeros2 = jnp.zeros((2, D_MODEL), F32)
    y = norm_mod(xa, p["final_norm"][None, :], zeros2, zeros2, s)[:s]
    return loss_head(y, target)


def kernel(x, c, ctx, c_ctx, w_mod, b_mod, norm1, w_in, na_rpb, q_gain, k_gain, w_pa, w_pb, w_o, norm2, w_ffn_in, w_ffn_out, final_norm, loss_target, m_c_ctx, m_w_mod, m_b_mod, m_norm1, m_w_in, m_na_rpb, m_q_gain, m_k_gain, m_w_pa, m_w_pb, m_w_o, m_norm2, m_w_ffn_in, m_w_ffn_out, m_final_norm, v_c_ctx, v_w_mod, v_b_mod, v_norm1, v_w_in, v_na_rpb, v_q_gain, v_k_gain, v_w_pa, v_w_pb, v_w_o, v_norm2, v_w_ffn_in, v_w_ffn_out, v_final_norm):
    w = dict(c_ctx=c_ctx, w_mod=w_mod, b_mod=b_mod, norm1=norm1, w_in=w_in, na_rpb=na_rpb, q_gain=q_gain, k_gain=k_gain,
             w_pa=w_pa, w_pb=w_pb, w_o=w_o, norm2=norm2, w_ffn_in=w_ffn_in, w_ffn_out=w_ffn_out, final_norm=final_norm)
    mom = dict(c_ctx=m_c_ctx, w_mod=m_w_mod, b_mod=m_b_mod, norm1=m_norm1, w_in=m_w_in, na_rpb=m_na_rpb, q_gain=m_q_gain,
               k_gain=m_k_gain, w_pa=m_w_pa, w_pb=m_w_pb, w_o=m_w_o, norm2=m_norm2, w_ffn_in=m_w_ffn_in,
               w_ffn_out=m_w_ffn_out, final_norm=m_final_norm)
    var = dict(c_ctx=v_c_ctx, w_mod=v_w_mod, b_mod=v_b_mod, norm1=v_norm1, w_in=v_w_in, na_rpb=v_na_rpb, q_gain=v_q_gain,
               k_gain=v_k_gain, w_pa=v_w_pa, w_pb=v_w_pb, w_o=v_w_o, norm2=v_norm2, w_ffn_in=v_w_ffn_in,
               w_ffn_out=v_w_ffn_out, final_norm=v_final_norm)
    s, n_ctx = x.shape[1], ctx.shape[1]

    pack_big = lambda d: jnp.concatenate([_rows_of(d[n]) for n in BIG], axis=0)
    gathered = _exchange(pack_big(w).astype(BF16), True, "gather_weights")
    params, off = {n: w[n] for n in SMALL}, 0
    for n in BIG:
        r = w[n].size // D_MODEL
        full = _unshard(n, gathered[:, off:off + r], w[n].shape)
        params[n] = [full[l].astype(F32) for l in range(DEPTH)]
        off += r

    consts = (*_rope_tables(s, n_ctx), _rot_matrix(), _na_tables(s // GRID_W))
    loss, (gp, gx) = jax.value_and_grad(_local_loss, argnums=(0, 1))(params, x[0], c, ctx[0], loss_target[0], consts)
    loss = lax.psum(loss, ("x", "y", "c"))

    parts = _exchange(jnp.concatenate([_shard_slabs(n, jnp.stack(gp[n]).astype(BF16)) for n in BIG], axis=1), False,
                      "scatter_grads")
    big = _adam_call(parts, pack_big(w), pack_big(mom), pack_big(var), "adam_big")
    parts_s = _exchange(_pack_small(gp), True, "gather_small_grads")
    small = _adam_call(parts_s, _pack_small(w), _pack_small(mom), _pack_small(var), "adam_small")

    outs = [{}, {}, {}, {}]
    for k in range(4):
        outs[k].update(_unpack_small(small[k], w))
        off = 0
        for n in BIG:
            r = w[n].size // D_MODEL
            outs[k][n] = big[k][off:off + r].reshape(w[n].shape)
            off += r
    return (loss, gx[None], *[o[n] for o in outs for n in WEIGHTS])
```

```python
import functools

import numpy as np
import jax
import jax.numpy as jnp
from jax import lax
from jax.experimental import pallas as pl
from jax.experimental.pallas import tpu as pltpu

F32 = jnp.float32
BF16 = jnp.bfloat16
HIGHEST = lax.Precision.HIGHEST

D_MODEL = 1024
DEPTH = 4
GRID_W = 64
HEAD_DIM = 64
NA_HEADS = 8
NA_WIN_H = 8
NA_WIN_W = 16
GQA_Q_HEADS = 8
GQA_KV_HEADS = 2
GQA_REP = GQA_Q_HEADS // GQA_KV_HEADS
NA_WIDTH = NA_HEADS * HEAD_DIM
GQA_Q_WIDTH = GQA_Q_HEADS * HEAD_DIM
GQA_KV_WIDTH = GQA_KV_HEADS * HEAD_DIM
IN_SIZES = (NA_WIDTH, NA_WIDTH, NA_WIDTH, GQA_Q_WIDTH, GQA_KV_WIDTH, GQA_KV_WIDTH, D_MODEL, D_MODEL)
IN_SPLITS = tuple(int(v) for v in np.cumsum(IN_SIZES)[:-1])
ROPE_THETA = 10000.0
EPS = 1e-6
SCALE = HEAD_DIM ** -0.5
ADAM_LR = 0.001
ADAM_B1 = 0.9
ADAM_B2 = 0.999
ADAM_EPS = 1e-08
ADAM_WD = 0.01
ADAM_STEP = 10

N_DEV = 8
MESH = pl.DeviceIdType.MESH
NEG_BIG = -1e30

VMEM_BIG_LIMIT = 52 * 1024 * 1024
NA_QROWS = 4
NA_QB = NA_QROWS * GRID_W
NA_KROWS = 12
NA_KW = NA_KROWS * GRID_W
MM_TILE_ELEMS = 512 * 2176
MM_OPERAND_ELEMS = 1408 * 2176
MOD_ROWS = 256

BIG = ("w_mod", "w_in", "w_pa", "w_pb", "w_o", "w_ffn_in", "w_ffn_out")
ROW_SHARDED = ("w_o", "w_ffn_out")
SMALL = ("c_ctx", "b_mod", "norm1", "na_rpb", "q_gain", "k_gain", "norm2", "final_norm")
WEIGHTS = ("c_ctx", "w_mod", "b_mod", "norm1", "w_in", "na_rpb", "q_gain", "k_gain", "w_pa", "w_pb", "w_o",
           "norm2", "w_ffn_in", "w_ffn_out", "final_norm")


def _pick(n, cands):
    for c in cands:
        if n % c == 0:
            return c
    return n


def _dot_nt(a, b):
    return lax.dot_general(a, b, (((1,), (1,)), ((), ())), preferred_element_type=F32)


def _dot_tn(a, b):
    return lax.dot_general(a, b, (((0,), (0,)), ((), ())), preferred_element_type=F32)


def _dot(a, b):
    return jnp.dot(a, b, preferred_element_type=F32)


def _mm_call(a, b, name, mode="nn", out_dtype=F32):
    (m, k) = a.shape if mode != "tn" else a.shape[::-1]
    n = b.shape[1] if mode != "nt" else b.shape[0]
    tn = _pick(n, (2176, 1408, 1024, 512, 256, 128))
    tm = _pick(m, tuple(t for t in (768, 512, 256, 128) if t * tn <= MM_TILE_ELEMS))
    tk = _pick(k, tuple(t for t in (2816, 2176, 1408, 1024, 768, 512, 256, 128) if t * max(tm, tn) <= MM_OPERAND_ELEMS))
    nk = k // tk
    dot = {"nn": _dot, "tn": _dot_tn, "nt": _dot_nt}[mode]
    a_blk = (tk, tm) if mode == "tn" else (tm, tk)
    b_blk = (tn, tk) if mode == "nt" else (tk, tn)
    a_idx = (lambda i, kk: (kk, i)) if mode == "tn" else (lambda i, kk: (i, kk))
    b_idx = (lambda j, kk: (j, kk)) if mode == "nt" else (lambda j, kk: (kk, j))

    def body(a_ref, b_ref, o_ref, acc_ref):
        kk = pl.program_id(2)
        part = dot(a_ref[...], b_ref[...])

        @pl.when(kk == 0)
        def _():
            acc_ref[...] = part

        @pl.when(kk > 0)
        def _():
            acc_ref[...] += part

        @pl.when(kk == nk - 1)
        def _():
            o_ref[...] = acc_ref[...].astype(o_ref.dtype)

    def body1(a_ref, b_ref, o_ref):
        o_ref[...] = dot(a_ref[...], b_ref[...]).astype(o_ref.dtype)

    footprint = 2 * (tm * tk * 2 + tk * tn * 2 + tm * tn * 4) + 2 * tm * tn * 4
    limit = int(footprint + (8 << 20))
    if nk == 1:
        return pl.pallas_call(
            body1, grid=(n // tn, m // tm),
            in_specs=[pl.BlockSpec(a_blk, lambda j, i: a_idx(i, 0)), pl.BlockSpec(b_blk, lambda j, i: b_idx(j, 0))],
            out_specs=pl.BlockSpec((tm, tn), lambda j, i: (i, j)),
            out_shape=jax.ShapeDtypeStruct((m, n), out_dtype),
            compiler_params=pltpu.CompilerParams(dimension_semantics=("parallel", "parallel"), vmem_limit_bytes=limit),
            name=name,
        )(a, b)
    return pl.pallas_call(
        body, grid=(m // tm, n // tn, nk),
        in_specs=[pl.BlockSpec(a_blk, lambda i, j, kk: a_idx(i, kk)), pl.BlockSpec(b_blk, lambda i, j, kk: b_idx(j, kk))],
        out_specs=pl.BlockSpec((tm, tn), lambda i, j, kk: (i, j)),
        out_shape=jax.ShapeDtypeStruct((m, n), out_dtype),
        scratch_shapes=[pltpu.VMEM((tm, tn), F32)],
        compiler_params=pltpu.CompilerParams(
            dimension_semantics=("parallel", "parallel", "arbitrary"), vmem_limit_bytes=limit),
        name=name,
    )(a, b)


@jax.custom_vjp
def mm(x, w):
    return _mm_call(x.astype(BF16), w, "mm_fwd")


def _mm_fwd(x, w):
    xb = x.astype(BF16)
    return _mm_call(xb, w, "mm_fwd"), (xb, w)


def _mm_bwd(res, dy):
    xb, w = res
    dyb = dy.astype(BF16)
    return _mm_call(dyb, w, "mm_dx", "nt"), _mm_call(xb, dyb, "mm_dw", "tn", BF16)


mm.defvjp(_mm_fwd, _mm_bwd)


def _seg_rows(ref, is_ctx):
    return jnp.where(is_ctx, ref[1:2, :], ref[0:1, :])


def _one_hot_row(second):
    return (lax.broadcasted_iota(jnp.int32, (2, 1), 0) == second.astype(jnp.int32)).astype(F32)


def _norm_mod_fwd_call(x, g, shift, scale, n_lat):
    t, d = x.shape
    tt = _pick(t, (256, 128))
    lat_tiles = n_lat // tt

    def body(x_ref, g_ref, sh_ref, sc_ref, y_ref):
        is_ctx = pl.program_id(0) >= lat_tiles
        xv = x_ref[...]
        r = lax.rsqrt(jnp.mean(xv * xv, axis=-1, keepdims=True) + EPS)
        yn = xv * r * g_ref[...]
        y_ref[...] = yn * (1.0 + _seg_rows(sc_ref, is_ctx)) + _seg_rows(sh_ref, is_ctx)

    row = pl.BlockSpec((tt, d), lambda i: (i, 0))
    full = lambda rws: pl.BlockSpec((rws, d), lambda i: (0, 0))
    return pl.pallas_call(
        body, grid=(t // tt,), in_specs=[row, full(1), full(2), full(2)], out_specs=row,
        out_shape=jax.ShapeDtypeStruct((t, d), F32),
        compiler_params=pltpu.CompilerParams(dimension_semantics=("parallel",)), name="norm_mod_fwd",
    )(x, g, shift, scale)


def _norm_mod_bwd_call(x, g, scale, dy, n_lat):
    t, d = x.shape
    tt = _pick(t, (256, 128))
    lat_tiles = n_lat // tt

    def body(x_ref, g_ref, sc_ref, dy_ref, dx_ref, dg_ref, dsh_ref, dsc_ref):
        i = pl.program_id(0)
        is_ctx = i >= lat_tiles

        @pl.when(i == 0)
        def _():
            dg_ref[...] = jnp.zeros_like(dg_ref)
            dsh_ref[...] = jnp.zeros_like(dsh_ref)
            dsc_ref[...] = jnp.zeros_like(dsc_ref)

        xv, dyv, gv = x_ref[...], dy_ref[...], g_ref[...]
        r = lax.rsqrt(jnp.mean(xv * xv, axis=-1, keepdims=True) + EPS)
        nrm = xv * r
        yn = nrm * gv
        seg = _one_hot_row(is_ctx)
        dsh_ref[...] += seg * jnp.sum(dyv, axis=0, keepdims=True)
        dsc_ref[...] += seg * jnp.sum(dyv * yn, axis=0, keepdims=True)
        dyn = dyv * (1.0 + _seg_rows(sc_ref, is_ctx))
        dg_ref[...] += jnp.sum(dyn * nrm, axis=0, keepdims=True)
        dn = dyn * gv
        dx_ref[...] = r * (dn - nrm * jnp.mean(dn * nrm, axis=-1, keepdims=True))

    row = pl.BlockSpec((tt, d), lambda i: (i, 0))
    full = lambda rws: pl.BlockSpec((rws, d), lambda i: (0, 0))
    return pl.pallas_call(
        body, grid=(t // tt,), in_specs=[row, full(1), full(2), row], out_specs=[row, full(1), full(2), full(2)],
        out_shape=[jax.ShapeDtypeStruct((t, d), F32), jax.ShapeDtypeStruct((1, d), F32),
                   jax.ShapeDtypeStruct((2, d), F32), jax.ShapeDtypeStruct((2, d), F32)],
        compiler_params=pltpu.CompilerParams(dimension_semantics=("arbitrary",)), name="norm_mod_bwd",
    )(x, g, scale, dy)


@functools.partial(jax.custom_vjp, nondiff_argnums=(4,))
def norm_mod(x, g, shift, scale, n_lat):
    return _norm_mod_fwd_call(x, g, shift, scale, n_lat)


def _norm_mod_fwd(x, g, shift, scale, n_lat):
    return _norm_mod_fwd_call(x, g, shift, scale, n_lat), (x, g, scale)


def _norm_mod_bwd(n_lat, res, dy):
    x, g, scale = res
    dx, dg, dsh, dsc = _norm_mod_bwd_call(x, g, scale, dy, n_lat)
    return dx, dg, dsh, dsc


norm_mod.defvjp(_norm_mod_fwd, _norm_mod_bwd)


def _rot_matrix():
    p = np.zeros((HEAD_DIM, HEAD_DIM), np.float32)
    for i in range(HEAD_DIM // 2):
        p[2 * i + 1, 2 * i] = -1.0
        p[2 * i, 2 * i + 1] = 1.0
    return jnp.asarray(p)


def _hn_rope_fwd_call(x, gains, cos2, sin2, rot, n_q):
    nh, t, dh = x.shape
    tt = _pick(t, (2816, 1024, 768, 512, 256, 128))

    def body(x_ref, g_ref, cos_ref, sin_ref, rot_ref, y_ref):
        gv = jnp.where(pl.program_id(0) >= n_q, g_ref[1:2, :], g_ref[0:1, :])
        xv = x_ref[...]
        r = lax.rsqrt(jnp.mean(xv * xv, axis=-1, keepdims=True) + EPS)
        y = xv * r * gv
        yr = jnp.dot(y, rot_ref[...], precision=HIGHEST, preferred_element_type=F32)
        y_ref[...] = y * cos_ref[...] + yr * sin_ref[...]

    blk = pl.BlockSpec((None, tt, dh), lambda h, i: (h, i, 0))
    tab = pl.BlockSpec((tt, dh), lambda h, i: (i, 0))
    return pl.pallas_call(
        body, grid=(nh, t // tt),
        in_specs=[blk, pl.BlockSpec((2, dh), lambda h, i: (0, 0)), tab, tab, pl.BlockSpec((dh, dh), lambda h, i: (0, 0))],
        out_specs=blk, out_shape=jax.ShapeDtypeStruct((nh, t, dh), F32),
        compiler_params=pltpu.CompilerParams(dimension_semantics=("parallel", "parallel")), name="hn_rope_fwd",
    )(x, gains, cos2, sin2, rot)


def _hn_rope_bwd_call(x, gains, cos2, sin2, rot, dy, n_q):
    nh, t, dh = x.shape
    tt = _pick(t, (2816, 1024, 768, 512, 256, 128))

    def body(x_ref, g_ref, cos_ref, sin_ref, rot_ref, dy_ref, dx_ref, dg_ref):
        h, i = pl.program_id(0), pl.program_id(1)
        is_k = h >= n_q

        @pl.when((h == 0) & (i == 0))
        def _():
            dg_ref[...] = jnp.zeros_like(dg_ref)

        gv = jnp.where(is_k, g_ref[1:2, :], g_ref[0:1, :])
        xv, dyv = x_ref[...], dy_ref[...]
        r = lax.rsqrt(jnp.mean(xv * xv, axis=-1, keepdims=True) + EPS)
        nrm = xv * r
        dyn = dyv * cos_ref[...] - jnp.dot(dyv * sin_ref[...], rot_ref[...], precision=HIGHEST, preferred_element_type=F32)
        seg = _one_hot_row(is_k)
        dg_ref[...] += seg * jnp.sum(dyn * nrm, axis=0, keepdims=True)
        dn = dyn * gv
        dx_ref[...] = r * (dn - nrm * jnp.mean(dn * nrm, axis=-1, keepdims=True))

    blk = pl.BlockSpec((None, tt, dh), lambda h, i: (h, i, 0))
    tab = pl.BlockSpec((tt, dh), lambda h, i: (i, 0))
    g_spec = pl.BlockSpec((2, dh), lambda h, i: (0, 0))
    return pl.pallas_call(
        body, grid=(nh, t // tt),
        in_specs=[blk, g_spec, tab, tab, pl.BlockSpec((dh, dh), lambda h, i: (0, 0)), blk],
        out_specs=[blk, g_spec],
        out_shape=[jax.ShapeDtypeStruct((nh, t, dh), F32), jax.ShapeDtypeStruct((2, dh), F32)],
        compiler_params=pltpu.CompilerParams(dimension_semantics=("arbitrary", "arbitrary")), name="hn_rope_bwd",
    )(x, gains, cos2, sin2, rot, dy)


@functools.partial(jax.custom_vjp, nondiff_argnums=(5,))
def hn_rope(x, gains, cos2, sin2, rot, n_q):
    return _hn_rope_fwd_call(x, gains, cos2, sin2, rot, n_q)


def _hn_rope_fwd(x, gains, cos2, sin2, rot, n_q):
    return _hn_rope_fwd_call(x, gains, cos2, sin2, rot, n_q), (x, gains, cos2, sin2, rot)


def _hn_rope_bwd(n_q, res, dy):
    x, gains, cos2, sin2, rot = res
    dx, dg = _hn_rope_bwd_call(x, gains, cos2, sin2, rot, dy, n_q)
    return dx, dg, jnp.zeros_like(cos2), jnp.zeros_like(sin2), jnp.zeros_like(rot)


hn_rope.defvjp(_hn_rope_fwd, _hn_rope_bwd)


EXCHANGE_SCRATCH = [pltpu.SemaphoreType.DMA((N_DEV - 1,)), pltpu.SemaphoreType.DMA((N_DEV - 1,)), pltpu.SemaphoreType.DMA(())]


def _exchange_copies(x_ref, out_ref, send_sems, recv_sems, local_sem, all_gather):
    mx, my, mc = lax.axis_index("x"), lax.axis_index("y"), lax.axis_index("c")
    me = 4 * mx + 2 * my + mc
    src = (lambda p: x_ref) if all_gather else (lambda p: x_ref.at[p])
    local = pltpu.make_async_copy(src(me), out_ref.at[me], local_sem)
    remote = []
    for rel in range(1, N_DEV):
        px, py, pc = mx ^ (rel >> 2), my ^ ((rel >> 1) & 1), mc ^ (rel & 1)
        remote.append(pltpu.make_async_remote_copy(
            src_ref=src(4 * px + 2 * py + pc), dst_ref=out_ref.at[me],
            send_sem=send_sems.at[rel - 1], recv_sem=recv_sems.at[rel - 1],
            device_id=(px, py, pc), device_id_type=MESH))
    return local, remote


def _exchange_start(*refs, all_gather):
    local, remote = _exchange_copies(*refs, all_gather)
    local.start()
    for cp in remote:
        cp.start()


def _exchange_wait(*refs, all_gather):
    local, remote = _exchange_copies(*refs, all_gather)
    for cp in remote:
        cp.wait_send()
    for cp in remote:
        cp.wait_recv()
    local.wait()


def _exchange_shape(x, all_gather):
    return jax.ShapeDtypeStruct((N_DEV,) + tuple(x.shape if all_gather else x.shape[1:]), x.dtype)


def _exchange(x, all_gather, name):
    def body(x_ref, out_ref, *sems):
        _exchange_start(x_ref, out_ref, *sems, all_gather=all_gather)
        _exchange_wait(x_ref, out_ref, *sems, all_gather=all_gather)

    return pl.pallas_call(
        body, in_specs=[pl.BlockSpec(memory_space=pl.ANY)], out_specs=pl.BlockSpec(memory_space=pl.ANY),
        out_shape=_exchange_shape(x, all_gather), scratch_shapes=EXCHANGE_SCRATCH,
        compiler_params=pltpu.CompilerParams(has_side_effects=True), name=name,
    )(x)


ATT_TK = 256


def _att_tq(rn):
    return _pick(rn, (512, 256, 128))


def _att_unroll(nkb):
    return _pick(nkb, (3, 2))


def _flash_fwd_call(q_t, k, v_t, xsend=None):
    g, dh, rn = q_t.shape
    nk = k.shape[1]
    tq, tk = _att_tq(rn), ATT_TK
    nkb = nk // tk
    unroll = _att_unroll(nkb)
    n_i = rn // tq

    def body(*refs):
        if xsend is None:
            qt_ref, k_ref, vt_ref, ot_ref, lse_ref = refs
        else:
            qt_ref, k_ref, vt_ref, x_ref, ot_ref, lse_ref, xout_ref, *sems = refs
            a, i = pl.program_id(0), pl.program_id(1)

            @pl.when((a == 0) & (i == 0))
            def _():
                _exchange_start(x_ref, xout_ref, *sems, all_gather=True)

        qst = qt_ref[...] * SCALE

        def trip(t, carry):
            m, l, acc = carry
            blocks = [t * unroll + u for u in range(unroll)]
            s_t = [_dot(k_ref[pl.ds(pl.multiple_of(j * tk, tk), tk), :], qst) for j in blocks]
            m_new = functools.reduce(jnp.maximum, [jnp.max(s, axis=0, keepdims=True) for s in s_t], m)
            p_t = [jnp.exp(s - m_new) for s in s_t]
            alpha = jnp.exp(m - m_new)
            l = alpha * l + sum(jnp.sum(p, axis=0, keepdims=True) for p in p_t)
            acc = alpha * acc + sum(_dot(vt_ref[j], p.astype(BF16)) for j, p in zip(blocks, p_t))
            return m_new, l, acc

        m, l, acc = lax.fori_loop(
            0, nkb // unroll, trip, (jnp.full((1, tq), NEG_BIG, F32), jnp.zeros((1, tq), F32), jnp.zeros((dh, tq), F32)))
        ot_ref[...] = acc / l
        lse_ref[...] = m + jnp.log(l)

        if xsend is not None:
            @pl.when((a == g - 1) & (i == n_i - 1))
            def _():
                _exchange_wait(x_ref, xout_ref, *sems, all_gather=True)

    col = pl.BlockSpec((None, dh, tq), lambda a, i: (a, 0, i))
    vec = pl.BlockSpec((None, 1, tq), lambda a, i: (a, 0, i))
    hbm = pl.BlockSpec(memory_space=pl.ANY)
    in_specs = [col, pl.BlockSpec((None, nk, dh), lambda a, i: (a, 0, 0)),
                pl.BlockSpec((None, nkb, dh, tk), lambda a, i: (a, 0, 0, 0))]
    out_specs = [col, vec]
    out_shape = [jax.ShapeDtypeStruct((g, dh, rn), F32), jax.ShapeDtypeStruct((g, 1, rn), F32)]
    args = (q_t, k, v_t)
    if xsend is not None:
        in_specs, out_specs, args = in_specs + [hbm], out_specs + [hbm], args + (xsend,)
        out_shape = out_shape + [_exchange_shape(xsend, True)]
    return pl.pallas_call(
        body, grid=(g, n_i), in_specs=in_specs, out_specs=out_specs, out_shape=out_shape,
        scratch_shapes=[] if xsend is None else EXCHANGE_SCRATCH,
        compiler_params=pltpu.CompilerParams(dimension_semantics=("arbitrary", "arbitrary"), vmem_limit_bytes=VMEM_BIG_LIMIT),
        name="flash_fwd" if xsend is None else "flash_fwd_gather",
    )(*args)


def _flash_bwd_call(q, q_t, k, k_t, v, do, do_t, o_t, lse_t, xsend=None):
    g, rn, dh = q.shape
    nk = k.shape[1]
    tq, tk = _att_tq(rn), ATT_TK
    nkb = nk // tk
    unroll = _att_unroll(nkb)
    n_i = rn // tq

    def body(*refs):
        q_ref, qt_ref, k_ref, kt_ref, v_ref, do_ref, dot_ref, ot_ref, lse_ref = refs[:9]
        if xsend is None:
            dqt_ref, dk_ref, dv_ref = refs[9:]
        else:
            x_ref, dqt_ref, dk_ref, dv_ref, xout_ref, *sems = refs[9:]
        a, i = pl.program_id(0), pl.program_id(1)

        if xsend is not None:
            @pl.when((a == 0) & (i == 0))
            def _():
                _exchange_start(x_ref, xout_ref, *sems, all_gather=False)

        @pl.when(i == 0)
        def _():
            dk_ref[...] = jnp.zeros_like(dk_ref)
            dv_ref[...] = jnp.zeros_like(dv_ref)

        qs = q_ref[...] * SCALE
        qst = qt_ref[...] * SCALE
        dov, dotv = do_ref[...], dot_ref[...]
        delta = jnp.sum(dotv.astype(F32) * ot_ref[...], axis=0, keepdims=True)
        lse = lse_ref[...]

        def step(j, dqt):
            off = pl.multiple_of(j * tk, tk)
            kj = k_ref[pl.ds(off, tk), :]
            vj = v_ref[pl.ds(off, tk), :]
            p_t = jnp.exp(_dot(kj, qst) - lse)
            ds_t = (p_t * (_dot(vj, dotv) - delta)).astype(BF16)
            dv_ref[pl.ds(off, tk), :] += _dot(p_t.astype(BF16), dov)
            dk_ref[pl.ds(off, tk), :] += _dot(ds_t, qs)
            return dqt + _dot(kt_ref[j], ds_t)

        def trip(t, dqt):
            for u in range(unroll):
                dqt = step(t * unroll + u, dqt)
            return dqt

        dqt_ref[...] = lax.fori_loop(0, nkb // unroll, trip, jnp.zeros((dh, tq), F32)) * SCALE

        if xsend is not None:
            @pl.when((a == g - 1) & (i == n_i - 1))
            def _():
                _exchange_wait(x_ref, xout_ref, *sems, all_gather=False)

    row = pl.BlockSpec((None, tq, dh), lambda a, i: (a, i, 0))
    col = pl.BlockSpec((None, dh, tq), lambda a, i: (a, 0, i))
    kv = pl.BlockSpec((None, nk, dh), lambda a, i: (a, 0, 0))
    hbm = pl.BlockSpec(memory_space=pl.ANY)
    in_specs = [row, col, kv, pl.BlockSpec((None, nkb, dh, tk), lambda a, i: (a, 0, 0, 0)), kv, row, col, col,
                pl.BlockSpec((None, 1, tq), lambda a, i: (a, 0, i))]
    out_specs = [col, kv, kv]
    out_shape = [jax.ShapeDtypeStruct((g, dh, rn), F32), jax.ShapeDtypeStruct((g, nk, dh), F32),
                 jax.ShapeDtypeStruct((g, nk, dh), F32)]
    args = (q, q_t, k, k_t, v, do, do_t, o_t, lse_t)
    if xsend is not None:
        in_specs, out_specs, args = in_specs + [hbm], out_specs + [hbm], args + (xsend,)
        out_shape = out_shape + [_exchange_shape(xsend, False)]
    return pl.pallas_call(
        body, grid=(g, n_i), in_specs=in_specs, out_specs=out_specs, out_shape=out_shape,
        scratch_shapes=[] if xsend is None else EXCHANGE_SCRATCH,
        compiler_params=pltpu.CompilerParams(dimension_semantics=("arbitrary", "arbitrary"), vmem_limit_bytes=VMEM_BIG_LIMIT),
        name="flash_bwd" if xsend is None else "flash_bwd_scatter",
    )(*args)


def _key_blocks_t(t):
    g, nk, dh = t.shape
    return t.reshape(g, nk // ATT_TK, ATT_TK, dh).transpose(0, 1, 3, 2)


@jax.custom_vjp
def attn(q, k, v):
    return _attn_fwd(q, k, v)[0]


def _attn_fwd(q, k, v):
    qb, kb, vb = q.astype(BF16), k.astype(BF16), v.astype(BF16)
    q_t = qb.transpose(0, 2, 1)
    o_t, lse_t = _flash_fwd_call(q_t, kb, _key_blocks_t(vb))
    return o_t.transpose(0, 2, 1), (qb, q_t, kb, vb, o_t, lse_t)


def _attn_bwd(res, do):
    qb, q_t, kb, vb, o_t, lse_t = res
    dob = do.astype(BF16)
    dq_t, dk, dv = _flash_bwd_call(qb, q_t, kb, _key_blocks_t(kb), vb, dob, dob.transpose(0, 2, 1), o_t, lse_t)
    return dq_t.transpose(0, 2, 1), dk, dv


attn.defvjp(_attn_fwd, _attn_bwd)


def _shard_rows(shards):
    return jnp.concatenate([shards[n].reshape(-1, D_MODEL) for n in BIG], axis=0).astype(BF16)


def _fulls_of(gathered, shards):
    out, off = {}, 0
    for n in BIG:
        k, nn = shards[n].shape
        r = k * nn // D_MODEL
        t = gathered[:, off:off + r].reshape(N_DEV, k, nn)
        out[n] = t.reshape(N_DEV * k, nn) if n in ROW_SHARDED else t.transpose(1, 0, 2).reshape(k, N_DEV * nn)
        off += r
    return out


def _slabs_of(dfulls):
    parts = []
    for n in BIG:
        k, nn = dfulls[n].shape
        t = (dfulls[n].reshape(N_DEV, k // N_DEV, nn) if n in ROW_SHARDED
             else dfulls[n].reshape(k, N_DEV, nn // N_DEV).transpose(1, 0, 2))
        parts.append(t.reshape(N_DEV, -1, D_MODEL))
    return jnp.concatenate(parts, axis=1)


def _dshards_of(parts, dfulls):
    rows, out, off = _sum8_call(parts), {}, 0
    for n in BIG:
        k, nn = dfulls[n].shape
        shape = (k // N_DEV, nn) if n in ROW_SHARDED else (k, nn // N_DEV)
        r = shape[0] * shape[1] // D_MODEL
        out[n] = rows[off:off + r].reshape(shape)
        off += r
    return out


@jax.custom_vjp
def gather_layer(shards):
    return _fulls_of(_exchange(_shard_rows(shards), True, "gather_weights"), shards)


def _gather_layer_fwd(shards):
    return gather_layer(shards), None


def _gather_layer_bwd(_, dfulls):
    return (_dshards_of(_exchange(_slabs_of(dfulls), False, "scatter_grads"), dfulls),)


gather_layer.defvjp(_gather_layer_fwd, _gather_layer_bwd)


@jax.custom_vjp
def attn_gather(q, k, v, shards):
    return _attn_gather_fwd(q, k, v, shards)[0]


def _attn_gather_fwd(q, k, v, shards):
    qb, kb, vb = q.astype(BF16), k.astype(BF16), v.astype(BF16)
    q_t = qb.transpose(0, 2, 1)
    o_t, lse_t, gathered = _flash_fwd_call(q_t, kb, _key_blocks_t(vb), _shard_rows(shards))
    return (o_t.transpose(0, 2, 1), _fulls_of(gathered, shards)), (qb, q_t, kb, vb, o_t, lse_t)


def _attn_gather_bwd(res, cts):
    qb, q_t, kb, vb, o_t, lse_t = res
    do, dfulls = cts
    dob = do.astype(BF16)
    dq_t, dk, dv, parts = _flash_bwd_call(qb, q_t, kb, _key_blocks_t(kb), vb, dob, dob.transpose(0, 2, 1), o_t, lse_t,
                                          _slabs_of(dfulls))
    return dq_t.transpose(0, 2, 1), dk, dv, _dshards_of(parts, dfulls)


attn_gather.defvjp(_attn_gather_fwd, _attn_gather_bwd)


def _na_key_row(i, rows):
    return jnp.clip(NA_QROWS * i - NA_WIN_H // 2, 0, rows - NA_KROWS)


def _na_type(i, nb):
    return jnp.where(i == 0, 0, jnp.where(i == nb - 1, 2, 1))


def _na_fwd_call(q, k, v, kc, vc, slab):
    h, s, dh = q.shape
    n_ctx = kc.shape[1]
    rows, nb = s // GRID_W, s // NA_QB

    def body(q_ref, k_ref, v_ref, kc_ref, vc_ref, slab_ref, o_ref, lse_ref):
        off = pl.multiple_of(_na_key_row(pl.program_id(1), rows) * GRID_W, NA_QB)
        qs = q_ref[...] * SCALE
        kw = k_ref[pl.ds(off, NA_KW), :]
        vw = v_ref[pl.ds(off, NA_KW), :]
        sw = _dot_nt(qs, kw) + slab_ref[...]
        sc = _dot_nt(qs, kc_ref[...])
        m = jnp.maximum(jnp.max(sw, axis=1, keepdims=True), jnp.max(sc, axis=1, keepdims=True))
        pw = jnp.exp(sw - m)
        pc = jnp.exp(sc - m)
        l = jnp.sum(pw, axis=1, keepdims=True) + jnp.sum(pc, axis=1, keepdims=True)
        o_ref[...] = (_dot(pw.astype(BF16), vw) + _dot(pc.astype(BF16), vc_ref[...])) / l
        lse_ref[...] = m + jnp.log(l)

    qblk = pl.BlockSpec((None, NA_QB, dh), lambda a, i: (a, i, 0))
    kv = pl.BlockSpec((None, s, dh), lambda a, i: (a, 0, 0))
    ckv = pl.BlockSpec((None, n_ctx, dh), lambda a, i: (a, 0, 0))
    return pl.pallas_call(
        body, grid=(h, nb),
        in_specs=[qblk, kv, kv, ckv, ckv, pl.BlockSpec((None, None, NA_QB, NA_KW), lambda a, i: (_na_type(i, nb), a, 0, 0))],
        out_specs=[qblk, pl.BlockSpec((None, NA_QB, 1), lambda a, i: (a, i, 0))],
        out_shape=[jax.ShapeDtypeStruct((h, s, dh), F32), jax.ShapeDtypeStruct((h, s, 1), F32)],
        compiler_params=pltpu.CompilerParams(dimension_semantics=("parallel", "parallel"), vmem_limit_bytes=VMEM_BIG_LIMIT),
        name="na_fwd",
    )(q, k, v, kc, vc, slab)


def _na_bwd_call(q, k, v, kc, vc, slab, do, o, lse):
    h, s, dh = q.shape
    n_ctx = kc.shape[1]
    rows, nb = s // GRID_W, s // NA_QB

    def body(q_ref, k_ref, v_ref, kc_ref, vc_ref, slab_ref, do_ref, o_ref, lse_ref,
             dq_ref, dk_ref, dv_ref, dkc_ref, dvc_ref, dslab_ref):
        i = pl.program_id(1)

        @pl.when(i == 0)
        def _():
            dk_ref[...] = jnp.zeros_like(dk_ref)
            dv_ref[...] = jnp.zeros_like(dv_ref)
            dkc_ref[...] = jnp.zeros_like(dkc_ref)
            dvc_ref[...] = jnp.zeros_like(dvc_ref)

        @pl.when((i <= 1) | (i == nb - 1))
        def _():
            dslab_ref[...] = jnp.zeros_like(dslab_ref)

        off = pl.multiple_of(_na_key_row(i, rows) * GRID_W, NA_QB)
        qs = q_ref[...] * SCALE
        kw = k_ref[pl.ds(off, NA_KW), :]
        vw = v_ref[pl.ds(off, NA_KW), :]
        kcv, vcv = kc_ref[...], vc_ref[...]
        dov = do_ref[...]
        lse = lse_ref[...]
        delta = jnp.sum(dov.astype(F32) * o_ref[...], axis=1, keepdims=True)
        pw = jnp.exp(_dot_nt(qs, kw) + slab_ref[...] - lse)
        pc = jnp.exp(_dot_nt(qs, kcv) - lse)
        dsw = pw * (_dot_nt(dov, vw) - delta)
        dsc = pc * (_dot_nt(dov, vcv) - delta)
        dslab_ref[...] += dsw
        dsw, dsc = dsw.astype(BF16), dsc.astype(BF16)
        dq_ref[...] = (_dot(dsw, kw) + _dot(dsc, kcv)) * SCALE
        dk_ref[pl.ds(off, NA_KW), :] += _dot_tn(dsw, qs)
        dv_ref[pl.ds(off, NA_KW), :] += _dot_tn(pw.astype(BF16), dov)
        dkc_ref[...] += _dot_tn(dsc, qs)
        dvc_ref[...] += _dot_tn(pc.astype(BF16), dov)

    qblk = pl.BlockSpec((None, NA_QB, dh), lambda a, i: (a, i, 0))
    kv = pl.BlockSpec((None, s, dh), lambda a, i: (a, 0, 0))
    ckv = pl.BlockSpec((None, n_ctx, dh), lambda a, i: (a, 0, 0))
    slab_spec = pl.BlockSpec((None, None, NA_QB, NA_KW), lambda a, i: (_na_type(i, nb), a, 0, 0))
    return pl.pallas_call(
        body, grid=(h, nb),
        in_specs=[qblk, kv, kv, ckv, ckv, slab_spec, qblk, qblk, pl.BlockSpec((None, NA_QB, 1), lambda a, i: (a, i, 0))],
        out_specs=[qblk, kv, kv, ckv, ckv, slab_spec],
        out_shape=[jax.ShapeDtypeStruct((h, s, dh), F32), jax.ShapeDtypeStruct((h, s, dh), F32),
                   jax.ShapeDtypeStruct((h, s, dh), F32), jax.ShapeDtypeStruct((h, n_ctx, dh), F32),
                   jax.ShapeDtypeStruct((h, n_ctx, dh), F32), jax.ShapeDtypeStruct(slab.shape, F32)],
        compiler_params=pltpu.CompilerParams(dimension_semantics=("arbitrary", "arbitrary"), vmem_limit_bytes=VMEM_BIG_LIMIT),
        name="na_bwd",
    )(q, k, v, kc, vc, slab, do, o, lse)


@jax.custom_vjp
def na_attn(q, k, v, kc, vc, slab):
    return _na_fwd_call(q.astype(BF16), k.astype(BF16), v.astype(BF16), kc.astype(BF16), vc.astype(BF16), slab)[0]


def _na_attn_fwd(q, k, v, kc, vc, slab):
    qb, kb, vb, kcb, vcb = (t.astype(BF16) for t in (q, k, v, kc, vc))
    o, lse = _na_fwd_call(qb, kb, vb, kcb, vcb, slab)
    return o, (qb, kb, vb, kcb, vcb, slab, o, lse)


def _na_attn_bwd(res, do):
    qb, kb, vb, kcb, vcb, slab, o, lse = res
    return tuple(_na_bwd_call(qb, kb, vb, kcb, vcb, slab, do.astype(BF16), o, lse))


na_attn.defvjp(_na_attn_fwd, _na_attn_bwd)


def _na_tables(rows):
    sel = np.zeros((3, NA_QROWS, NA_KROWS, 2 * NA_WIN_H - 1), np.float32)
    row_ok = np.zeros((3, NA_QROWS, NA_KROWS), bool)
    for t, (r0, ks) in enumerate([(0, 0), (NA_QROWS, 0), (rows - NA_QROWS, rows - NA_KROWS)]):
        for a in range(NA_QROWS):
            r = r0 + a
            rs = min(max(r - NA_WIN_H // 2, 0), rows - NA_WIN_H)
            for b in range(NA_KROWS):
                kr = ks + b
                if rs <= kr < rs + NA_WIN_H:
                    sel[t, a, b, kr - r + NA_WIN_H - 1] = 1.0
                    row_ok[t, a, b] = True
    col = np.arange(GRID_W)
    c_start = np.clip(col - NA_WIN_W // 2, 0, GRID_W - NA_WIN_W)
    in_win = (col[None, :] >= c_start[:, None]) & (col[None, :] < c_start[:, None] + NA_WIN_W)
    dc_idx = np.clip(col[None, :] - col[:, None], -(NA_WIN_W - 1), NA_WIN_W - 1) + NA_WIN_W - 1
    onehot = (in_win[:, :, None] & (dc_idx[:, :, None] == np.arange(2 * NA_WIN_W - 1)[None, None, :])).astype(np.float32)
    ok = row_ok[:, :, None, :, None] & in_win[None, None, :, None, :]
    neg = np.where(ok, 0.0, NEG_BIG).astype(np.float32).reshape(3, 1, NA_QB, NA_KW)
    return jnp.asarray(sel), jnp.asarray(onehot), jnp.asarray(neg)


def _na_slab(rpb, tables):
    sel, onehot, neg = tables
    a = jnp.einsum("tabd,hdc->thabc", sel, rpb, precision=HIGHEST)
    val = jnp.einsum("thabc,wuc->thawbu", a, onehot, precision=HIGHEST)
    return val.reshape(3, NA_HEADS, NA_QB, NA_KW) + neg


def _loss_call(y, target):
    s, d = y.shape
    tt = _pick(s, (512, 256, 128))

    def body(y_ref, t_ref, loss_ref, diff_ref):
        @pl.when(pl.program_id(0) == 0)
        def _():
            loss_ref[...] = jnp.zeros_like(loss_ref)

        e = y_ref[...] - t_ref[...]
        diff_ref[...] = e * (1.0 / d)
        loss_ref[...] += 0.5 * jnp.sum(jnp.mean(e * e, axis=-1, keepdims=True), axis=0, keepdims=True)

    row = pl.BlockSpec((tt, d), lambda i: (i, 0))
    return pl.pallas_call(
        body, grid=(s // tt,), in_specs=[row, row], out_specs=[pl.BlockSpec((1, 1), lambda i: (0, 0)), row],
        out_shape=[jax.ShapeDtypeStruct((1, 1), F32), jax.ShapeDtypeStruct((s, d), F32)],
        compiler_params=pltpu.CompilerParams(dimension_semantics=("arbitrary",)), name="loss_head",
    )(y, target)


@jax.custom_vjp
def loss_head(y, target):
    return _loss_call(y, target)[0][0, 0]


def _loss_head_fwd(y, target):
    loss, diff = _loss_call(y, target)
    return loss[0, 0], diff


def _loss_head_bwd(diff, g):
    return diff * g, jnp.zeros_like(diff)


loss_head.defvjp(_loss_head_fwd, _loss_head_bwd)


def _sum8_call(parts):
    _, r, c = parts.shape
    tr = _pick(r, (328, 256, 128))

    def body(p_ref, g_ref):
        g = p_ref[0].astype(F32)
        for s in range(1, N_DEV):
            g = g + p_ref[s].astype(F32)
        g_ref[...] = g

    return pl.pallas_call(
        body, grid=(r // tr,), in_specs=[pl.BlockSpec((N_DEV, tr, c), lambda i: (0, i, 0))],
        out_specs=pl.BlockSpec((tr, c), lambda i: (i, 0)), out_shape=jax.ShapeDtypeStruct((r, c), F32),
        compiler_params=pltpu.CompilerParams(dimension_semantics=("parallel",)), name="sum8",
    )(parts)


def _adam_call(g, w, m, v):
    r, c = g.shape
    tr = _pick(r, (256, 128))

    def body(g_ref, w_ref, m_ref, v_ref, d_ref, nm_ref, nv_ref):
        gv = g_ref[...]
        mn = ADAM_B1 * m_ref[...] + (1.0 - ADAM_B1) * gv
        vn = ADAM_B2 * v_ref[...] + (1.0 - ADAM_B2) * (gv * gv)
        m_hat = mn / (1.0 - ADAM_B1 ** ADAM_STEP)
        v_hat = vn / (1.0 - ADAM_B2 ** ADAM_STEP)
        d_ref[...] = -ADAM_LR * (m_hat / (jnp.sqrt(v_hat) + ADAM_EPS) + ADAM_WD * w_ref[...])
        nm_ref[...] = mn
        nv_ref[...] = vn

    row = pl.BlockSpec((tr, c), lambda i: (i, 0))
    out = jax.ShapeDtypeStruct((r, c), F32)
    return pl.pallas_call(
        body, grid=(r // tr,), in_specs=[row, row, row, row], out_specs=[row, row, row], out_shape=[out, out, out],
        compiler_params=pltpu.CompilerParams(dimension_semantics=("parallel",)), name="adam",
    )(g, w, m, v)


def _pack_small(vals, extra=None):
    flat = [vals[n].reshape(-1) for n in SMALL] + ([] if extra is None else [extra.reshape(-1)])
    flat = jnp.concatenate(flat)
    rows = -(-flat.shape[0] // D_MODEL)
    rows = -(-rows // 8) * 8
    return jnp.pad(flat, (0, rows * D_MODEL - flat.shape[0])).reshape(rows, D_MODEL)


def _unpack_small(packed, like):
    flat, out, off = packed.reshape(-1), {}, 0
    for n in SMALL:
        size = like[n].size
        out[n] = flat[off:off + size].reshape(like[n].shape)
        off += size
    return out, flat[off]


def _rope_tables(s, n_ctx):
    t = jnp.arange(s)
    row = (t // GRID_W).astype(F32)
    col = (t % GRID_W).astype(F32)
    half = HEAD_DIM // 2
    inv = ROPE_THETA ** (-jnp.arange(0, half, 2, dtype=F32) / half)
    ang = jnp.concatenate([row[:, None] * inv, col[:, None] * inv], axis=-1)
    cos2 = jnp.repeat(jnp.cos(ang), 2, axis=-1)
    sin2 = jnp.repeat(jnp.sin(ang), 2, axis=-1)
    cos2 = jnp.concatenate([cos2, jnp.ones((n_ctx, HEAD_DIM), F32)], axis=0)
    sin2 = jnp.concatenate([sin2, jnp.zeros((n_ctx, HEAD_DIM), F32)], axis=0)
    return cos2, sin2


def _to_heads(t, n_heads):
    return t.reshape(t.shape[0], n_heads, HEAD_DIM).transpose(1, 0, 2)


def _from_heads(t):
    return t.transpose(1, 0, 2).reshape(t.shape[1], t.shape[0] * HEAD_DIM)


def _local_loss(shards, p, x, c, ctx, target, consts):
    s, n_ctx = x.shape[0], ctx.shape[0]
    cos2, sin2, rot, na_tables = consts
    is_ctx = (jnp.arange(s + n_ctx) >= s)[:, None]
    seg = lambda rows2: jnp.where(is_ctx, rows2[1:2], rows2[0:1])
    xa = jnp.concatenate([x, ctx], axis=0)
    cond = jnp.concatenate([jax.nn.silu(c), jax.nn.silu(p["c_ctx"])[None, :],
                            jnp.zeros((MOD_ROWS - 2, D_MODEL), F32)], axis=0)
    layer_shards = lambda l: {n: shards[n][l] for n in BIG}
    w = gather_layer(layer_shards(0))

    for l in range(DEPTH):
        mod = (mm(cond, w["w_mod"]) + p["b_mod"][l])[:2]
        sh1, sc1, g1, sh2, sc2, g2 = jnp.split(mod, 6, axis=-1)

        h = norm_mod(xa, p["norm1"][l][None, :], sh1, sc1, s)
        na_q, na_k, na_v, gq, gk, gv, ga, gb = jnp.split(mm(h, w["w_in"]), IN_SPLITS, axis=-1)

        qa, ka, va = _to_heads(na_q, NA_HEADS), _to_heads(na_k, NA_HEADS), _to_heads(na_v, NA_HEADS)
        slab = _na_slab(p["na_rpb"][l], na_tables)
        ya_lat = na_attn(qa[:, :s], ka[:, :s], va[:, :s], ka[:, s:], va[:, s:], slab)
        ya_ctx = attn(qa[:, s:], ka[:, s:], va[:, s:])
        ya = _from_heads(jnp.concatenate([ya_lat, ya_ctx], axis=1))

        qk = _to_heads(jnp.concatenate([gq, gk], axis=-1), GQA_Q_HEADS + GQA_KV_HEADS)
        gains = jnp.stack([p["q_gain"][l], p["k_gain"][l]])
        qk = hn_rope(qk, gains, cos2, sin2, rot, GQA_Q_HEADS)
        qb, kb, vb = qk[:GQA_Q_HEADS], qk[GQA_Q_HEADS:], _to_heads(gv, GQA_KV_HEADS)
        q_lat = qb[:, :s].reshape(GQA_KV_HEADS, GQA_REP * s, HEAD_DIM)
        if l + 1 < DEPTH:
            ob_lat, w_next = attn_gather(q_lat, kb, vb, layer_shards(l + 1))
        else:
            ob_lat, w_next = attn(q_lat, kb, vb), None
        ob_ctx = attn(qb[:, s:].reshape(GQA_KV_HEADS, GQA_REP * n_ctx, HEAD_DIM), kb[:, s:], vb[:, s:])
        yb = _from_heads(jnp.concatenate([ob_lat.reshape(GQA_Q_HEADS, s, HEAD_DIM),
                                          ob_ctx.reshape(GQA_Q_HEADS, n_ctx, HEAD_DIM)], axis=1))

        merged = jax.nn.sigmoid(ga) * mm(ya, w["w_pa"]) + jax.nn.sigmoid(gb) * mm(yb, w["w_pb"])
        xa = xa + seg(g1) * mm(merged, w["w_o"])

        h2 = norm_mod(xa, p["norm2"][l][None, :], sh2, sc2, s)
        a, u = jnp.split(mm(h2, w["w_ffn_in"]), 2, axis=-1)
        xa = xa + seg(g2) * mm(jax.nn.silu(a) * u, w["w_ffn_out"])
        w = w_next

    zeros2 = jnp.zeros((2, D_MODEL), F32)
    y = norm_mod(xa, p["final_norm"][None, :], zeros2, zeros2, s)[:s]
    return loss_head(y, target)


def kernel(x, c, ctx, c_ctx, w_mod, b_mod, norm1, w_in, na_rpb, q_gain, k_gain, w_pa, w_pb, w_o, norm2, w_ffn_in, w_ffn_out, final_norm, loss_target, m_c_ctx, m_w_mod, m_b_mod, m_norm1, m_w_in, m_na_rpb, m_q_gain, m_k_gain, m_w_pa, m_w_pb, m_w_o, m_norm2, m_w_ffn_in, m_w_ffn_out, m_final_norm, v_c_ctx, v_w_mod, v_b_mod, v_norm1, v_w_in, v_na_rpb, v_q_gain, v_k_gain, v_w_pa, v_w_pb, v_w_o, v_norm2, v_w_ffn_in, v_w_ffn_out, v_final_norm):
    w = dict(c_ctx=c_ctx, w_mod=w_mod, b_mod=b_mod, norm1=norm1, w_in=w_in, na_rpb=na_rpb, q_gain=q_gain, k_gain=k_gain,
             w_pa=w_pa, w_pb=w_pb, w_o=w_o, norm2=norm2, w_ffn_in=w_ffn_in, w_ffn_out=w_ffn_out, final_norm=final_norm)
    mom = dict(c_ctx=m_c_ctx, w_mod=m_w_mod, b_mod=m_b_mod, norm1=m_norm1, w_in=m_w_in, na_rpb=m_na_rpb, q_gain=m_q_gain,
               k_gain=m_k_gain, w_pa=m_w_pa, w_pb=m_w_pb, w_o=m_w_o, norm2=m_norm2, w_ffn_in=m_w_ffn_in,
               w_ffn_out=m_w_ffn_out, final_norm=m_final_norm)
    var = dict(c_ctx=v_c_ctx, w_mod=v_w_mod, b_mod=v_b_mod, norm1=v_norm1, w_in=v_w_in, na_rpb=v_na_rpb, q_gain=v_q_gain,
               k_gain=v_k_gain, w_pa=v_w_pa, w_pb=v_w_pb, w_o=v_w_o, norm2=v_norm2, w_ffn_in=v_w_ffn_in,
               w_ffn_out=v_w_ffn_out, final_norm=v_final_norm)
    s, n_ctx = x.shape[1], ctx.shape[1]
    depth = w_mod.shape[0]

    shards = {n: [w[n][l] for l in range(depth)] for n in BIG}
    small = {n: w[n] for n in SMALL}
    consts = (*_rope_tables(s, n_ctx), _rot_matrix(), _na_tables(s // GRID_W))
    loss, (g_shards, g_small, gx) = jax.value_and_grad(_local_loss, argnums=(0, 1, 2))(
        shards, small, x[0], c, ctx[0], loss_target[0], consts)

    parts_s = _exchange(_pack_small(g_small, loss), True, "gather_small_grads")
    g_packed = _sum8_call(parts_s)
    grads, loss = _unpack_small(g_packed, w)
    zero = jnp.zeros((1,), F32)
    upd_s = _adam_call(g_packed, _pack_small(w, zero), _pack_small(mom, zero), _pack_small(var, zero))
    outs = [grads] + [_unpack_small(u, w)[0] for u in upd_s]

    for n in BIG:
        flat = lambda t: t.reshape(-1, t.shape[-1])
        g = jnp.stack(g_shards[n])
        upd = _adam_call(flat(g), flat(w[n]), flat(mom[n]), flat(var[n]))
        for k, t in enumerate([g] + [u.reshape(w[n].shape) for u in upd]):
            outs[k][n] = t
    return (loss, gx[None], *[o[n] for o in outs for n in WEIGHTS])
```

```python
import functools

import numpy as np
import jax
import jax.numpy as jnp
from jax import lax
from jax.experimental import pallas as pl
from jax.experimental.pallas import tpu as pltpu

F32 = jnp.float32
BF16 = jnp.bfloat16
HIGHEST = lax.Precision.HIGHEST

D_MODEL = 1024
DEPTH = 4
GRID_W = 64
HEAD_DIM = 64
NA_HEADS = 8
NA_WIN_H = 8
NA_WIN_W = 16
GQA_Q_HEADS = 8
GQA_KV_HEADS = 2
GQA_REP = GQA_Q_HEADS // GQA_KV_HEADS
NA_WIDTH = NA_HEADS * HEAD_DIM
GQA_Q_WIDTH = GQA_Q_HEADS * HEAD_DIM
GQA_KV_WIDTH = GQA_KV_HEADS * HEAD_DIM
IN_SIZES = (NA_WIDTH, NA_WIDTH, NA_WIDTH, GQA_Q_WIDTH, GQA_KV_WIDTH, GQA_KV_WIDTH, D_MODEL, D_MODEL)
IN_SPLITS = tuple(int(v) for v in np.cumsum(IN_SIZES)[:-1])
ROPE_THETA = 10000.0
EPS = 1e-6
SCALE = HEAD_DIM ** -0.5
ADAM_LR = 0.001
ADAM_B1 = 0.9
ADAM_B2 = 0.999
ADAM_EPS = 1e-08
ADAM_WD = 0.01
ADAM_STEP = 10

N_DEV = 8
MESH = pl.DeviceIdType.MESH
NEG_BIG = -1e30

VMEM_BIG_LIMIT = 52 * 1024 * 1024
NA_QROWS = 4
NA_QB = NA_QROWS * GRID_W
NA_KROWS = 12
NA_KW = NA_KROWS * GRID_W
NA_FWD_HEADS = 4
MM_TILE_ELEMS = 512 * 2176
MM_OPERAND_ELEMS = 1408 * 2176
MOD_ROWS = 256

BIG = ("w_mod", "w_in", "w_pa", "w_pb", "w_o", "w_ffn_in", "w_ffn_out")
ROW_SHARDED = ("w_o", "w_ffn_out")
SMALL = ("c_ctx", "b_mod", "norm1", "na_rpb", "q_gain", "k_gain", "norm2", "final_norm")
WEIGHTS = ("c_ctx", "w_mod", "b_mod", "norm1", "w_in", "na_rpb", "q_gain", "k_gain", "w_pa", "w_pb", "w_o",
           "norm2", "w_ffn_in", "w_ffn_out", "final_norm")


def _pick(n, cands):
    for c in cands:
        if n % c == 0:
            return c
    return n


def _dot_nt(a, b):
    return lax.dot_general(a, b, (((1,), (1,)), ((), ())), preferred_element_type=F32)


def _dot_tn(a, b):
    return lax.dot_general(a, b, (((0,), (0,)), ((), ())), preferred_element_type=F32)


def _dot(a, b):
    return jnp.dot(a, b, preferred_element_type=F32)


def _mm_call(a, b, name, mode="nn", out_dtype=F32):
    (m, k) = a.shape if mode != "tn" else a.shape[::-1]
    n = b.shape[1] if mode != "nt" else b.shape[0]
    tn = _pick(n, (2176, 1408, 1024, 512, 256, 128))
    tm = _pick(m, tuple(t for t in (768, 512, 256, 128) if t * tn <= MM_TILE_ELEMS))
    tk = _pick(k, tuple(t for t in (2816, 2176, 1408, 1024, 768, 512, 256, 128) if t * max(tm, tn) <= MM_OPERAND_ELEMS))
    nk = k // tk
    dot = {"nn": _dot, "tn": _dot_tn, "nt": _dot_nt}[mode]
    a_blk = (tk, tm) if mode == "tn" else (tm, tk)
    b_blk = (tn, tk) if mode == "nt" else (tk, tn)
    a_idx = (lambda i, kk: (kk, i)) if mode == "tn" else (lambda i, kk: (i, kk))
    b_idx = (lambda j, kk: (j, kk)) if mode == "nt" else (lambda j, kk: (kk, j))

    def body(a_ref, b_ref, o_ref, acc_ref):
        kk = pl.program_id(2)
        part = dot(a_ref[...], b_ref[...])

        @pl.when(kk == 0)
        def _():
            acc_ref[...] = part

        @pl.when(kk > 0)
        def _():
            acc_ref[...] += part

        @pl.when(kk == nk - 1)
        def _():
            o_ref[...] = acc_ref[...].astype(o_ref.dtype)

    def body1(a_ref, b_ref, o_ref):
        o_ref[...] = dot(a_ref[...], b_ref[...]).astype(o_ref.dtype)

    footprint = 2 * (tm * tk * 2 + tk * tn * 2 + tm * tn * 4) + 2 * tm * tn * 4
    limit = int(footprint + (8 << 20))
    if nk == 1:
        return pl.pallas_call(
            body1, grid=(n // tn, m // tm),
            in_specs=[pl.BlockSpec(a_blk, lambda j, i: a_idx(i, 0)), pl.BlockSpec(b_blk, lambda j, i: b_idx(j, 0))],
            out_specs=pl.BlockSpec((tm, tn), lambda j, i: (i, j)),
            out_shape=jax.ShapeDtypeStruct((m, n), out_dtype),
            compiler_params=pltpu.CompilerParams(dimension_semantics=("parallel", "parallel"), vmem_limit_bytes=limit),
            name=name,
        )(a, b)
    return pl.pallas_call(
        body, grid=(m // tm, n // tn, nk),
        in_specs=[pl.BlockSpec(a_blk, lambda i, j, kk: a_idx(i, kk)), pl.BlockSpec(b_blk, lambda i, j, kk: b_idx(j, kk))],
        out_specs=pl.BlockSpec((tm, tn), lambda i, j, kk: (i, j)),
        out_shape=jax.ShapeDtypeStruct((m, n), out_dtype),
        scratch_shapes=[pltpu.VMEM((tm, tn), F32)],
        compiler_params=pltpu.CompilerParams(
            dimension_semantics=("parallel", "parallel", "arbitrary"), vmem_limit_bytes=limit),
        name=name,
    )(a, b)


@jax.custom_vjp
def mm(x, w):
    return _mm_call(x, w, "mm_fwd")


def _mm_fwd(x, w):
    return _mm_call(x, w, "mm_fwd"), (x, w)


def _mm_bwd(res, dy):
    x, w = res
    dyb = dy.astype(BF16)
    return _mm_call(dyb, w, "mm_dx", "nt", BF16), _mm_call(x, dyb, "mm_dw", "tn", BF16)


mm.defvjp(_mm_fwd, _mm_bwd)


def _seg_rows(ref, is_ctx):
    return jnp.where(is_ctx, ref[1:2, :], ref[0:1, :])


def _one_hot_row(second):
    return (lax.broadcasted_iota(jnp.int32, (2, 1), 0) == second.astype(jnp.int32)).astype(F32)


def _norm_mod_fwd_call(x, g, shift, scale, n_lat, out_dtype):
    t, d = x.shape
    tt = _pick(t, (256, 128))
    lat_tiles = n_lat // tt

    def body(x_ref, g_ref, sh_ref, sc_ref, y_ref):
        is_ctx = pl.program_id(0) >= lat_tiles
        xv = x_ref[...]
        r = lax.rsqrt(jnp.mean(xv * xv, axis=-1, keepdims=True) + EPS)
        yn = xv * r * g_ref[...]
        y_ref[...] = (yn * (1.0 + _seg_rows(sc_ref, is_ctx)) + _seg_rows(sh_ref, is_ctx)).astype(y_ref.dtype)

    row = pl.BlockSpec((tt, d), lambda i: (i, 0))
    full = lambda rws: pl.BlockSpec((rws, d), lambda i: (0, 0))
    return pl.pallas_call(
        body, grid=(t // tt,), in_specs=[row, full(1), full(2), full(2)], out_specs=row,
        out_shape=jax.ShapeDtypeStruct((t, d), out_dtype),
        compiler_params=pltpu.CompilerParams(dimension_semantics=("parallel",)), name="norm_mod_fwd",
    )(x, g, shift, scale)


def _norm_mod_bwd_call(x, g, scale, dy, n_lat):
    t, d = x.shape
    tt = _pick(t, (256, 128))
    lat_tiles = n_lat // tt

    def body(x_ref, g_ref, sc_ref, dy_ref, dx_ref, dg_ref, dsh_ref, dsc_ref):
        i = pl.program_id(0)
        is_ctx = i >= lat_tiles

        @pl.when(i == 0)
        def _():
            dg_ref[...] = jnp.zeros_like(dg_ref)
            dsh_ref[...] = jnp.zeros_like(dsh_ref)
            dsc_ref[...] = jnp.zeros_like(dsc_ref)

        xv, dyv, gv = x_ref[...], dy_ref[...].astype(F32), g_ref[...]
        r = lax.rsqrt(jnp.mean(xv * xv, axis=-1, keepdims=True) + EPS)
        nrm = xv * r
        yn = nrm * gv
        seg = _one_hot_row(is_ctx)
        dsh_ref[...] += seg * jnp.sum(dyv, axis=0, keepdims=True)
        dsc_ref[...] += seg * jnp.sum(dyv * yn, axis=0, keepdims=True)
        dyn = dyv * (1.0 + _seg_rows(sc_ref, is_ctx))
        dg_ref[...] += jnp.sum(dyn * nrm, axis=0, keepdims=True)
        dn = dyn * gv
        dx_ref[...] = r * (dn - nrm * jnp.mean(dn * nrm, axis=-1, keepdims=True))

    row = pl.BlockSpec((tt, d), lambda i: (i, 0))
    full = lambda rws: pl.BlockSpec((rws, d), lambda i: (0, 0))
    return pl.pallas_call(
        body, grid=(t // tt,), in_specs=[row, full(1), full(2), row], out_specs=[row, full(1), full(2), full(2)],
        out_shape=[jax.ShapeDtypeStruct((t, d), F32), jax.ShapeDtypeStruct((1, d), F32),
                   jax.ShapeDtypeStruct((2, d), F32), jax.ShapeDtypeStruct((2, d), F32)],
        compiler_params=pltpu.CompilerParams(dimension_semantics=("arbitrary",)), name="norm_mod_bwd",
    )(x, g, scale, dy)


@functools.partial(jax.custom_vjp, nondiff_argnums=(4, 5))
def norm_mod(x, g, shift, scale, n_lat, out_dtype):
    return _norm_mod_fwd_call(x, g, shift, scale, n_lat, out_dtype)


def _norm_mod_fwd(x, g, shift, scale, n_lat, out_dtype):
    return _norm_mod_fwd_call(x, g, shift, scale, n_lat, out_dtype), (x, g, scale)


def _norm_mod_bwd(n_lat, out_dtype, res, dy):
    x, g, scale = res
    dx, dg, dsh, dsc = _norm_mod_bwd_call(x, g, scale, dy, n_lat)
    return dx, dg, dsh, dsc


norm_mod.defvjp(_norm_mod_fwd, _norm_mod_bwd)


def _rot_matrix():
    p = np.zeros((HEAD_DIM, HEAD_DIM), np.float32)
    for i in range(HEAD_DIM // 2):
        p[2 * i + 1, 2 * i] = -1.0
        p[2 * i, 2 * i + 1] = 1.0
    return jnp.asarray(p)


def _hn_rope_fwd_call(x, gains, cos2, sin2, rot, n_q):
    nh, t, dh = x.shape
    tt = _pick(t, (2816, 1024, 768, 512, 256, 128))

    def body(x_ref, g_ref, cos_ref, sin_ref, rot_ref, y_ref):
        gv = jnp.where(pl.program_id(0) >= n_q, g_ref[1:2, :], g_ref[0:1, :])
        xv = x_ref[...]
        r = lax.rsqrt(jnp.mean(xv * xv, axis=-1, keepdims=True) + EPS)
        y = xv * r * gv
        yr = jnp.dot(y, rot_ref[...], precision=HIGHEST, preferred_element_type=F32)
        y_ref[...] = y * cos_ref[...] + yr * sin_ref[...]

    blk = pl.BlockSpec((None, tt, dh), lambda h, i: (h, i, 0))
    tab = pl.BlockSpec((tt, dh), lambda h, i: (i, 0))
    return pl.pallas_call(
        body, grid=(nh, t // tt),
        in_specs=[blk, pl.BlockSpec((2, dh), lambda h, i: (0, 0)), tab, tab, pl.BlockSpec((dh, dh), lambda h, i: (0, 0))],
        out_specs=blk, out_shape=jax.ShapeDtypeStruct((nh, t, dh), F32),
        compiler_params=pltpu.CompilerParams(dimension_semantics=("parallel", "parallel")), name="hn_rope_fwd",
    )(x, gains, cos2, sin2, rot)


def _hn_rope_bwd_call(x, gains, cos2, sin2, rot, dy, n_q):
    nh, t, dh = x.shape
    tt = _pick(t, (2816, 1024, 768, 512, 256, 128))

    def body(x_ref, g_ref, cos_ref, sin_ref, rot_ref, dy_ref, dx_ref, dg_ref):
        h, i = pl.program_id(0), pl.program_id(1)
        is_k = h >= n_q

        @pl.when((h == 0) & (i == 0))
        def _():
            dg_ref[...] = jnp.zeros_like(dg_ref)

        gv = jnp.where(is_k, g_ref[1:2, :], g_ref[0:1, :])
        xv, dyv = x_ref[...], dy_ref[...]
        r = lax.rsqrt(jnp.mean(xv * xv, axis=-1, keepdims=True) + EPS)
        nrm = xv * r
        dyn = dyv * cos_ref[...] - jnp.dot(dyv * sin_ref[...], rot_ref[...], precision=HIGHEST, preferred_element_type=F32)
        seg = _one_hot_row(is_k)
        dg_ref[...] += seg * jnp.sum(dyn * nrm, axis=0, keepdims=True)
        dn = dyn * gv
        dx_ref[...] = r * (dn - nrm * jnp.mean(dn * nrm, axis=-1, keepdims=True))

    blk = pl.BlockSpec((None, tt, dh), lambda h, i: (h, i, 0))
    tab = pl.BlockSpec((tt, dh), lambda h, i: (i, 0))
    g_spec = pl.BlockSpec((2, dh), lambda h, i: (0, 0))
    return pl.pallas_call(
        body, grid=(nh, t // tt),
        in_specs=[blk, g_spec, tab, tab, pl.BlockSpec((dh, dh), lambda h, i: (0, 0)), blk],
        out_specs=[blk, g_spec],
        out_shape=[jax.ShapeDtypeStruct((nh, t, dh), F32), jax.ShapeDtypeStruct((2, dh), F32)],
        compiler_params=pltpu.CompilerParams(dimension_semantics=("arbitrary", "arbitrary")), name="hn_rope_bwd",
    )(x, gains, cos2, sin2, rot, dy)


@functools.partial(jax.custom_vjp, nondiff_argnums=(5,))
def hn_rope(x, gains, cos2, sin2, rot, n_q):
    return _hn_rope_fwd_call(x, gains, cos2, sin2, rot, n_q)


def _hn_rope_fwd(x, gains, cos2, sin2, rot, n_q):
    return _hn_rope_fwd_call(x, gains, cos2, sin2, rot, n_q), (x, gains, cos2, sin2, rot)


def _hn_rope_bwd(n_q, res, dy):
    x, gains, cos2, sin2, rot = res
    dx, dg = _hn_rope_bwd_call(x, gains, cos2, sin2, rot, dy, n_q)
    return dx, dg, jnp.zeros_like(cos2), jnp.zeros_like(sin2), jnp.zeros_like(rot)


hn_rope.defvjp(_hn_rope_fwd, _hn_rope_bwd)


EXCHANGE_SCRATCH = [pltpu.SemaphoreType.DMA((N_DEV - 1,)), pltpu.SemaphoreType.DMA((N_DEV - 1,)), pltpu.SemaphoreType.DMA(())]


def _exchange_copies(x_ref, out_ref, send_sems, recv_sems, local_sem, all_gather):
    mx, my, mc = lax.axis_index("x"), lax.axis_index("y"), lax.axis_index("c")
    me = 4 * mx + 2 * my + mc
    src = (lambda p: x_ref) if all_gather else (lambda p: x_ref.at[p])
    local = pltpu.make_async_copy(src(me), out_ref.at[me], local_sem)
    remote = []
    for rel in range(1, N_DEV):
        px, py, pc = mx ^ (rel >> 2), my ^ ((rel >> 1) & 1), mc ^ (rel & 1)
        remote.append(pltpu.make_async_remote_copy(
            src_ref=src(4 * px + 2 * py + pc), dst_ref=out_ref.at[me],
            send_sem=send_sems.at[rel - 1], recv_sem=recv_sems.at[rel - 1],
            device_id=(px, py, pc), device_id_type=MESH))
    return local, remote


def _exchange_start(*refs, all_gather):
    local, remote = _exchange_copies(*refs, all_gather)
    local.start()
    for cp in remote:
        cp.start()


def _exchange_wait(*refs, all_gather):
    local, remote = _exchange_copies(*refs, all_gather)
    for cp in remote:
        cp.wait_send()
    for cp in remote:
        cp.wait_recv()
    local.wait()


def _exchange_shape(x, all_gather):
    return jax.ShapeDtypeStruct((N_DEV,) + tuple(x.shape if all_gather else x.shape[1:]), x.dtype)


def _exchange(x, all_gather, name):
    def body(x_ref, out_ref, *sems):
        _exchange_start(x_ref, out_ref, *sems, all_gather=all_gather)
        _exchange_wait(x_ref, out_ref, *sems, all_gather=all_gather)

    return pl.pallas_call(
        body, in_specs=[pl.BlockSpec(memory_space=pl.ANY)], out_specs=pl.BlockSpec(memory_space=pl.ANY),
        out_shape=_exchange_shape(x, all_gather), scratch_shapes=EXCHANGE_SCRATCH,
        compiler_params=pltpu.CompilerParams(has_side_effects=True), name=name,
    )(x)


ATT_TK = 256


def _att_tq(rn, fwd=False):
    return _pick(rn, ((1024,) if fwd else ()) + (512, 256, 128))


def _att_unroll(nkb):
    return _pick(nkb, (3, 2))


def _flash_fwd_call(q_t, k, v_t, xsend=None):
    g, dh, rn = q_t.shape
    nk = k.shape[1]
    tq, tk = _att_tq(rn, True), ATT_TK
    nkb = nk // tk
    unroll = _att_unroll(nkb)
    n_i = rn // tq

    def body(*refs):
        if xsend is None:
            qt_ref, k_ref, vt_ref, ot_ref, lse_ref = refs
        else:
            qt_ref, k_ref, vt_ref, x_ref, ot_ref, lse_ref, xout_ref, *sems = refs
            a, i = pl.program_id(0), pl.program_id(1)

            @pl.when((a == 0) & (i == 0))
            def _():
                _exchange_start(x_ref, xout_ref, *sems, all_gather=True)

        qst = qt_ref[...] * SCALE

        def trip(t, carry):
            m, l, acc = carry
            blocks = [t * unroll + u for u in range(unroll)]
            s_t = [_dot(k_ref[pl.ds(pl.multiple_of(j * tk, tk), tk), :], qst) for j in blocks]
            m_new = functools.reduce(jnp.maximum, [jnp.max(s, axis=0, keepdims=True) for s in s_t], m)
            p_t = [jnp.exp(s - m_new) for s in s_t]
            alpha = jnp.exp(m - m_new)
            l = alpha * l + sum(jnp.sum(p, axis=0, keepdims=True) for p in p_t)
            acc = alpha * acc + sum(_dot(vt_ref[j], p.astype(BF16)) for j, p in zip(blocks, p_t))
            return m_new, l, acc

        m, l, acc = lax.fori_loop(
            0, nkb // unroll, trip, (jnp.full((1, tq), NEG_BIG, F32), jnp.zeros((1, tq), F32), jnp.zeros((dh, tq), F32)))
        ot_ref[...] = acc / l
        lse_ref[...] = m + jnp.log(l)

        if xsend is not None:
            @pl.when((a == g - 1) & (i == n_i - 1))
            def _():
                _exchange_wait(x_ref, xout_ref, *sems, all_gather=True)

    col = pl.BlockSpec((None, dh, tq), lambda a, i: (a, 0, i))
    vec = pl.BlockSpec((None, 1, tq), lambda a, i: (a, 0, i))
    hbm = pl.BlockSpec(memory_space=pl.ANY)
    in_specs = [col, pl.BlockSpec((None, nk, dh), lambda a, i: (a, 0, 0)),
                pl.BlockSpec((None, nkb, dh, tk), lambda a, i: (a, 0, 0, 0))]
    out_specs = [col, vec]
    out_shape = [jax.ShapeDtypeStruct((g, dh, rn), F32), jax.ShapeDtypeStruct((g, 1, rn), F32)]
    args = (q_t, k, v_t)
    if xsend is not None:
        in_specs, out_specs, args = in_specs + [hbm], out_specs + [hbm], args + (xsend,)
        out_shape = out_shape + [_exchange_shape(xsend, True)]
    return pl.pallas_call(
        body, grid=(g, n_i), in_specs=in_specs, out_specs=out_specs, out_shape=out_shape,
        scratch_shapes=[] if xsend is None else EXCHANGE_SCRATCH,
        compiler_params=pltpu.CompilerParams(dimension_semantics=("arbitrary", "arbitrary"), vmem_limit_bytes=VMEM_BIG_LIMIT),
        name="flash_fwd" if xsend is None else "flash_fwd_gather",
    )(*args)


def _flash_bwd_call(q, q_t, k, k_t, v, do, do_t, o_t, lse_t, xsend=None):
    g, rn, dh = q.shape
    nk = k.shape[1]
    tq, tk = _att_tq(rn), ATT_TK
    nkb = nk // tk
    unroll = _att_unroll(nkb)
    n_i = rn // tq

    def body(*refs):
        q_ref, qt_ref, k_ref, kt_ref, v_ref, do_ref, dot_ref, ot_ref, lse_ref = refs[:9]
        if xsend is None:
            dqt_ref, dk_ref, dv_ref = refs[9:]
        else:
            x_ref, dqt_ref, dk_ref, dv_ref, xout_ref, *sems = refs[9:]
        a, i = pl.program_id(0), pl.program_id(1)

        if xsend is not None:
            @pl.when((a == 0) & (i == 0))
            def _():
                _exchange_start(x_ref, xout_ref, *sems, all_gather=False)

        @pl.when(i == 0)
        def _():
            dk_ref[...] = jnp.zeros_like(dk_ref)
            dv_ref[...] = jnp.zeros_like(dv_ref)

        qs = q_ref[...] * SCALE
        qst = qt_ref[...] * SCALE
        dov, dotv = do_ref[...], dot_ref[...]
        delta = jnp.sum(dotv.astype(F32) * ot_ref[...], axis=0, keepdims=True)
        lse = lse_ref[...]

        def step(j, dqt):
            off = pl.multiple_of(j * tk, tk)
            kj = k_ref[pl.ds(off, tk), :]
            vj = v_ref[pl.ds(off, tk), :]
            p_t = jnp.exp(_dot(kj, qst) - lse)
            ds_t = (p_t * (_dot(vj, dotv) - delta)).astype(BF16)
            dv_ref[pl.ds(off, tk), :] += _dot(p_t.astype(BF16), dov)
            dk_ref[pl.ds(off, tk), :] += _dot(ds_t, qs)
            return dqt + _dot(kt_ref[j], ds_t)

        def trip(t, dqt):
            for u in range(unroll):
                dqt = step(t * unroll + u, dqt)
            return dqt

        dqt_ref[...] = lax.fori_loop(0, nkb // unroll, trip, jnp.zeros((dh, tq), F32)) * SCALE

        if xsend is not None:
            @pl.when((a == g - 1) & (i == n_i - 1))
            def _():
                _exchange_wait(x_ref, xout_ref, *sems, all_gather=False)

    row = pl.BlockSpec((None, tq, dh), lambda a, i: (a, i, 0))
    col = pl.BlockSpec((None, dh, tq), lambda a, i: (a, 0, i))
    kv = pl.BlockSpec((None, nk, dh), lambda a, i: (a, 0, 0))
    hbm = pl.BlockSpec(memory_space=pl.ANY)
    in_specs = [row, col, kv, pl.BlockSpec((None, nkb, dh, tk), lambda a, i: (a, 0, 0, 0)), kv, row, col, col,
                pl.BlockSpec((None, 1, tq), lambda a, i: (a, 0, i))]
    out_specs = [col, kv, kv]
    out_shape = [jax.ShapeDtypeStruct((g, dh, rn), F32), jax.ShapeDtypeStruct((g, nk, dh), F32),
                 jax.ShapeDtypeStruct((g, nk, dh), F32)]
    args = (q, q_t, k, k_t, v, do, do_t, o_t, lse_t)
    if xsend is not None:
        in_specs, out_specs, args = in_specs + [hbm], out_specs + [hbm], args + (xsend,)
        out_shape = out_shape + [_exchange_shape(xsend, False)]
    return pl.pallas_call(
        body, grid=(g, n_i), in_specs=in_specs, out_specs=out_specs, out_shape=out_shape,
        scratch_shapes=[] if xsend is None else EXCHANGE_SCRATCH,
        compiler_params=pltpu.CompilerParams(dimension_semantics=("arbitrary", "arbitrary"), vmem_limit_bytes=VMEM_BIG_LIMIT),
        name="flash_bwd" if xsend is None else "flash_bwd_scatter",
    )(*args)


def _key_blocks_t(t):
    g, nk, dh = t.shape
    return t.reshape(g, nk // ATT_TK, ATT_TK, dh).transpose(0, 1, 3, 2)


@jax.custom_vjp
def attn(q, k, v):
    return _attn_fwd(q, k, v)[0]


def _attn_fwd(q, k, v):
    qb, kb, vb = q.astype(BF16), k.astype(BF16), v.astype(BF16)
    q_t = qb.transpose(0, 2, 1)
    o_t, lse_t = _flash_fwd_call(q_t, kb, _key_blocks_t(vb))
    return o_t.transpose(0, 2, 1), (qb, q_t, kb, vb, o_t, lse_t)


def _attn_bwd(res, do):
    qb, q_t, kb, vb, o_t, lse_t = res
    dob = do.astype(BF16)
    dq_t, dk, dv = _flash_bwd_call(qb, q_t, kb, _key_blocks_t(kb), vb, dob, dob.transpose(0, 2, 1), o_t, lse_t)
    return dq_t.transpose(0, 2, 1), dk, dv


attn.defvjp(_attn_fwd, _attn_bwd)


def _shard_rows(shards):
    return jnp.concatenate([shards[n].reshape(-1, D_MODEL) for n in BIG], axis=0).astype(BF16)


def _fulls_of(gathered, shards):
    out, off = {}, 0
    for n in BIG:
        k, nn = shards[n].shape
        r = k * nn // D_MODEL
        t = gathered[:, off:off + r].reshape(N_DEV, k, nn)
        out[n] = t.reshape(N_DEV * k, nn) if n in ROW_SHARDED else t.transpose(1, 0, 2).reshape(k, N_DEV * nn)
        off += r
    return out


def _slabs_of(dfulls):
    parts = []
    for n in BIG:
        k, nn = dfulls[n].shape
        t = (dfulls[n].reshape(N_DEV, k // N_DEV, nn) if n in ROW_SHARDED
             else dfulls[n].reshape(k, N_DEV, nn // N_DEV).transpose(1, 0, 2))
        parts.append(t.reshape(N_DEV, -1, D_MODEL))
    return jnp.concatenate(parts, axis=1)


def _dshards_of(parts, dfulls):
    rows, out, off = _sum8_call(parts), {}, 0
    for n in BIG:
        k, nn = dfulls[n].shape
        shape = (k // N_DEV, nn) if n in ROW_SHARDED else (k, nn // N_DEV)
        r = shape[0] * shape[1] // D_MODEL
        out[n] = rows[off:off + r].reshape(shape)
        off += r
    return out


@jax.custom_vjp
def gather_layer(shards):
    return _fulls_of(_exchange(_shard_rows(shards), True, "gather_weights"), shards)


def _gather_layer_fwd(shards):
    return gather_layer(shards), None


def _gather_layer_bwd(_, dfulls):
    return (_dshards_of(_exchange(_slabs_of(dfulls), False, "scatter_grads"), dfulls),)


gather_layer.defvjp(_gather_layer_fwd, _gather_layer_bwd)


@jax.custom_vjp
def attn_gather(q, k, v, shards):
    return _attn_gather_fwd(q, k, v, shards)[0]


def _attn_gather_fwd(q, k, v, shards):
    qb, kb, vb = q.astype(BF16), k.astype(BF16), v.astype(BF16)
    q_t = qb.transpose(0, 2, 1)
    o_t, lse_t, gathered = _flash_fwd_call(q_t, kb, _key_blocks_t(vb), _shard_rows(shards))
    return (o_t.transpose(0, 2, 1), _fulls_of(gathered, shards)), (qb, q_t, kb, vb, o_t, lse_t)


def _attn_gather_bwd(res, cts):
    qb, q_t, kb, vb, o_t, lse_t = res
    do, dfulls = cts
    dob = do.astype(BF16)
    dq_t, dk, dv, parts = _flash_bwd_call(qb, q_t, kb, _key_blocks_t(kb), vb, dob, dob.transpose(0, 2, 1), o_t, lse_t,
                                          _slabs_of(dfulls))
    return dq_t.transpose(0, 2, 1), dk, dv, _dshards_of(parts, dfulls)


attn_gather.defvjp(_attn_gather_fwd, _attn_gather_bwd)


def _na_key_row(i, rows):
    return jnp.clip(NA_QROWS * i - NA_WIN_H // 2, 0, rows - NA_KROWS)


def _na_type(i, nb):
    return jnp.where(i == 0, 0, jnp.where(i == nb - 1, 2, 1))


def _na_fwd_call(q, k, v, kc, vc, slab):
    h, s, dh = q.shape
    n_ctx = kc.shape[1]
    rows, nb = s // GRID_W, s // NA_QB
    hb = NA_FWD_HEADS

    def body(q_ref, k_ref, v_ref, kc_ref, vc_ref, slab_ref, o_ref, lse_ref):
        off = pl.multiple_of(_na_key_row(pl.program_id(1), rows) * GRID_W, NA_QB)
        for hh in range(hb):
            qs = q_ref[hh] * SCALE
            kw = k_ref[hh, pl.ds(off, NA_KW), :]
            vw = v_ref[hh, pl.ds(off, NA_KW), :]
            sw = _dot_nt(qs, kw) + slab_ref[hh]
            sc = _dot_nt(qs, kc_ref[hh])
            m = jnp.maximum(jnp.max(sw, axis=1, keepdims=True), jnp.max(sc, axis=1, keepdims=True))
            pw = jnp.exp(sw - m)
            pc = jnp.exp(sc - m)
            l = jnp.sum(pw, axis=1, keepdims=True) + jnp.sum(pc, axis=1, keepdims=True)
            o_ref[hh] = (_dot(pw.astype(BF16), vw) + _dot(pc.astype(BF16), vc_ref[hh])) / l
            lse_ref[hh] = m + jnp.log(l)

    qblk = pl.BlockSpec((hb, NA_QB, dh), lambda a, i: (a, i, 0))
    kv = pl.BlockSpec((hb, s, dh), lambda a, i: (a, 0, 0))
    ckv = pl.BlockSpec((hb, n_ctx, dh), lambda a, i: (a, 0, 0))
    return pl.pallas_call(
        body, grid=(h // hb, nb),
        in_specs=[qblk, kv, kv, ckv, ckv, pl.BlockSpec((None, hb, NA_QB, NA_KW), lambda a, i: (_na_type(i, nb), a, 0, 0))],
        out_specs=[qblk, pl.BlockSpec((hb, NA_QB, 1), lambda a, i: (a, i, 0))],
        out_shape=[jax.ShapeDtypeStruct((h, s, dh), F32), jax.ShapeDtypeStruct((h, s, 1), F32)],
        compiler_params=pltpu.CompilerParams(dimension_semantics=("parallel", "parallel"), vmem_limit_bytes=VMEM_BIG_LIMIT),
        name="na_fwd",
    )(q, k, v, kc, vc, slab)


def _na_bwd_call(q, k, v, kc, vc, slab, do, o, lse):
    h, s, dh = q.shape
    n_ctx = kc.shape[1]
    rows, nb = s // GRID_W, s // NA_QB

    def body(q_ref, k_ref, v_ref, kc_ref, vc_ref, slab_ref, do_ref, o_ref, lse_ref,
             dq_ref, dk_ref, dv_ref, dkc_ref, dvc_ref, dslab_ref):
        i = pl.program_id(1)

        @pl.when(i == 0)
        def _():
            dk_ref[...] = jnp.zeros_like(dk_ref)
            dv_ref[...] = jnp.zeros_like(dv_ref)
            dkc_ref[...] = jnp.zeros_like(dkc_ref)
            dvc_ref[...] = jnp.zeros_like(dvc_ref)

        @pl.when((i <= 1) | (i == nb - 1))
        def _():
            dslab_ref[...] = jnp.zeros_like(dslab_ref)

        off = pl.multiple_of(_na_key_row(i, rows) * GRID_W, NA_QB)
        qs = q_ref[...] * SCALE
        kw = k_ref[pl.ds(off, NA_KW), :]
        vw = v_ref[pl.ds(off, NA_KW), :]
        kcv, vcv = kc_ref[...], vc_ref[...]
        dov = do_ref[...]
        lse = lse_ref[...]
        delta = jnp.sum(dov.astype(F32) * o_ref[...], axis=1, keepdims=True)
        pw = jnp.exp(_dot_nt(qs, kw) + slab_ref[...] - lse)
        pc = jnp.exp(_dot_nt(qs, kcv) - lse)
        dsw = pw * (_dot_nt(dov, vw) - delta)
        dsc = pc * (_dot_nt(dov, vcv) - delta)
        dslab_ref[...] += dsw
        dsw, dsc = dsw.astype(BF16), dsc.astype(BF16)
        dq_ref[...] = (_dot(dsw, kw) + _dot(dsc, kcv)) * SCALE
        dk_ref[pl.ds(off, NA_KW), :] += _dot_tn(dsw, qs)
        dv_ref[pl.ds(off, NA_KW), :] += _dot_tn(pw.astype(BF16), dov)
        dkc_ref[...] += _dot_tn(dsc, qs)
        dvc_ref[...] += _dot_tn(pc.astype(BF16), dov)

    qblk = pl.BlockSpec((None, NA_QB, dh), lambda a, i: (a, i, 0))
    kv = pl.BlockSpec((None, s, dh), lambda a, i: (a, 0, 0))
    ckv = pl.BlockSpec((None, n_ctx, dh), lambda a, i: (a, 0, 0))
    slab_spec = pl.BlockSpec((None, None, NA_QB, NA_KW), lambda a, i: (_na_type(i, nb), a, 0, 0))
    return pl.pallas_call(
        body, grid=(h, nb),
        in_specs=[qblk, kv, kv, ckv, ckv, slab_spec, qblk, qblk, pl.BlockSpec((None, NA_QB, 1), lambda a, i: (a, i, 0))],
        out_specs=[qblk, kv, kv, ckv, ckv, slab_spec],
        out_shape=[jax.ShapeDtypeStruct((h, s, dh), F32), jax.ShapeDtypeStruct((h, s, dh), F32),
                   jax.ShapeDtypeStruct((h, s, dh), F32), jax.ShapeDtypeStruct((h, n_ctx, dh), F32),
                   jax.ShapeDtypeStruct((h, n_ctx, dh), F32), jax.ShapeDtypeStruct(slab.shape, F32)],
        compiler_params=pltpu.CompilerParams(dimension_semantics=("arbitrary", "arbitrary"), vmem_limit_bytes=VMEM_BIG_LIMIT),
        name="na_bwd",
    )(q, k, v, kc, vc, slab, do, o, lse)


@jax.custom_vjp
def na_attn(q, k, v, kc, vc, slab):
    return _na_fwd_call(q.astype(BF16), k.astype(BF16), v.astype(BF16), kc.astype(BF16), vc.astype(BF16), slab)[0]


def _na_attn_fwd(q, k, v, kc, vc, slab):
    qb, kb, vb, kcb, vcb = (t.astype(BF16) for t in (q, k, v, kc, vc))
    o, lse = _na_fwd_call(qb, kb, vb, kcb, vcb, slab)
    return o, (qb, kb, vb, kcb, vcb, slab, o, lse)


def _na_attn_bwd(res, do):
    qb, kb, vb, kcb, vcb, slab, o, lse = res
    return tuple(_na_bwd_call(qb, kb, vb, kcb, vcb, slab, do.astype(BF16), o, lse))


na_attn.defvjp(_na_attn_fwd, _na_attn_bwd)


def _na_tables(rows):
    sel = np.zeros((3, NA_QROWS, NA_KROWS, 2 * NA_WIN_H - 1), np.float32)
    row_ok = np.zeros((3, NA_QROWS, NA_KROWS), bool)
    for t, (r0, ks) in enumerate([(0, 0), (NA_QROWS, 0), (rows - NA_QROWS, rows - NA_KROWS)]):
        for a in range(NA_QROWS):
            r = r0 + a
            rs = min(max(r - NA_WIN_H // 2, 0), rows - NA_WIN_H)
            for b in range(NA_KROWS):
                kr = ks + b
                if rs <= kr < rs + NA_WIN_H:
                    sel[t, a, b, kr - r + NA_WIN_H - 1] = 1.0
                    row_ok[t, a, b] = True
    col = np.arange(GRID_W)
    c_start = np.clip(col - NA_WIN_W // 2, 0, GRID_W - NA_WIN_W)
    in_win = (col[None, :] >= c_start[:, None]) & (col[None, :] < c_start[:, None] + NA_WIN_W)
    dc_idx = np.clip(col[None, :] - col[:, None], -(NA_WIN_W - 1), NA_WIN_W - 1) + NA_WIN_W - 1
    onehot = (in_win[:, :, None] & (dc_idx[:, :, None] == np.arange(2 * NA_WIN_W - 1)[None, None, :])).astype(np.float32)
    ok = row_ok[:, :, None, :, None] & in_win[None, None, :, None, :]
    neg = np.where(ok, 0.0, NEG_BIG).astype(np.float32).reshape(3, 1, NA_QB, NA_KW)
    return jnp.asarray(sel), jnp.asarray(onehot), jnp.asarray(neg)


def _na_slab(rpb, tables):
    sel, onehot, neg = tables
    a = jnp.einsum("tabd,hdc->thabc", sel, rpb, precision=HIGHEST)
    val = jnp.einsum("thabc,wuc->thawbu", a, onehot, precision=HIGHEST)
    return val.reshape(3, NA_HEADS, NA_QB, NA_KW) + neg


def _loss_call(y, target):
    s, d = y.shape
    tt = _pick(s, (512, 256, 128))

    def body(y_ref, t_ref, loss_ref, diff_ref):
        @pl.when(pl.program_id(0) == 0)
        def _():
            loss_ref[...] = jnp.zeros_like(loss_ref)

        e = y_ref[...] - t_ref[...]
        diff_ref[...] = e * (1.0 / d)
        loss_ref[...] += 0.5 * jnp.sum(jnp.mean(e * e, axis=-1, keepdims=True), axis=0, keepdims=True)

    row = pl.BlockSpec((tt, d), lambda i: (i, 0))
    return pl.pallas_call(
        body, grid=(s // tt,), in_specs=[row, row], out_specs=[pl.BlockSpec((1, 1), lambda i: (0, 0)), row],
        out_shape=[jax.ShapeDtypeStruct((1, 1), F32), jax.ShapeDtypeStruct((s, d), F32)],
        compiler_params=pltpu.CompilerParams(dimension_semantics=("arbitrary",)), name="loss_head",
    )(y, target)


@jax.custom_vjp
def loss_head(y, target):
    return _loss_call(y, target)[0][0, 0]


def _loss_head_fwd(y, target):
    loss, diff = _loss_call(y, target)
    return loss[0, 0], diff


def _loss_head_bwd(diff, g):
    return diff * g, jnp.zeros_like(diff)


loss_head.defvjp(_loss_head_fwd, _loss_head_bwd)


def _sum8_call(parts):
    _, r, c = parts.shape
    tr = _pick(r, (328, 256, 128))

    def body(p_ref, g_ref):
        g = p_ref[0].astype(F32)
        for s in range(1, N_DEV):
            g = g + p_ref[s].astype(F32)
        g_ref[...] = g

    return pl.pallas_call(
        body, grid=(r // tr,), in_specs=[pl.BlockSpec((N_DEV, tr, c), lambda i: (0, i, 0))],
        out_specs=pl.BlockSpec((tr, c), lambda i: (i, 0)), out_shape=jax.ShapeDtypeStruct((r, c), F32),
        compiler_params=pltpu.CompilerParams(dimension_semantics=("parallel",)), name="sum8",
    )(parts)


def _adam_call(g, w, m, v):
    r, c = g.shape
    tr = _pick(r, (256, 128))

    def body(g_ref, w_ref, m_ref, v_ref, d_ref, nm_ref, nv_ref):
        gv = g_ref[...]
        mn = ADAM_B1 * m_ref[...] + (1.0 - ADAM_B1) * gv
        vn = ADAM_B2 * v_ref[...] + (1.0 - ADAM_B2) * (gv * gv)
        m_hat = mn / (1.0 - ADAM_B1 ** ADAM_STEP)
        v_hat = vn / (1.0 - ADAM_B2 ** ADAM_STEP)
        d_ref[...] = -ADAM_LR * (m_hat / (jnp.sqrt(v_hat) + ADAM_EPS) + ADAM_WD * w_ref[...])
        nm_ref[...] = mn
        nv_ref[...] = vn

    row = pl.BlockSpec((tr, c), lambda i: (i, 0))
    out = jax.ShapeDtypeStruct((r, c), F32)
    return pl.pallas_call(
        body, grid=(r // tr,), in_specs=[row, row, row, row], out_specs=[row, row, row], out_shape=[out, out, out],
        compiler_params=pltpu.CompilerParams(dimension_semantics=("parallel",)), name="adam",
    )(g, w, m, v)


def _pack_small(vals, extra=None):
    flat = [vals[n].reshape(-1) for n in SMALL] + ([] if extra is None else [extra.reshape(-1)])
    flat = jnp.concatenate(flat)
    rows = -(-flat.shape[0] // D_MODEL)
    rows = -(-rows // 8) * 8
    return jnp.pad(flat, (0, rows * D_MODEL - flat.shape[0])).reshape(rows, D_MODEL)


def _unpack_small(packed, like):
    flat, out, off = packed.reshape(-1), {}, 0
    for n in SMALL:
        size = like[n].size
        out[n] = flat[off:off + size].reshape(like[n].shape)
        off += size
    return out, flat[off]


def _rope_tables(s, n_ctx):
    t = jnp.arange(s)
    row = (t // GRID_W).astype(F32)
    col = (t % GRID_W).astype(F32)
    half = HEAD_DIM // 2
    inv = ROPE_THETA ** (-jnp.arange(0, half, 2, dtype=F32) / half)
    ang = jnp.concatenate([row[:, None] * inv, col[:, None] * inv], axis=-1)
    cos2 = jnp.repeat(jnp.cos(ang), 2, axis=-1)
    sin2 = jnp.repeat(jnp.sin(ang), 2, axis=-1)
    cos2 = jnp.concatenate([cos2, jnp.ones((n_ctx, HEAD_DIM), F32)], axis=0)
    sin2 = jnp.concatenate([sin2, jnp.zeros((n_ctx, HEAD_DIM), F32)], axis=0)
    return cos2, sin2


def _to_heads(t, n_heads):
    return t.reshape(t.shape[0], n_heads, HEAD_DIM).transpose(1, 0, 2)


def _from_heads(t):
    return t.transpose(1, 0, 2).reshape(t.shape[1], t.shape[0] * HEAD_DIM)


def _local_loss(shards, p, x, c, ctx, target, consts):
    s, n_ctx = x.shape[0], ctx.shape[0]
    cos2, sin2, rot, na_tables = consts
    is_ctx = (jnp.arange(s + n_ctx) >= s)[:, None]
    seg = lambda rows2: jnp.where(is_ctx, rows2[1:2], rows2[0:1])
    xa = jnp.concatenate([x, ctx], axis=0)
    cond = jnp.concatenate([jax.nn.silu(c), jax.nn.silu(p["c_ctx"])[None, :],
                            jnp.zeros((MOD_ROWS - 2, D_MODEL), F32)], axis=0).astype(BF16)
    layer_shards = lambda l: {n: shards[n][l] for n in BIG}
    w = gather_layer(layer_shards(0))

    for l in range(DEPTH):
        mod = (mm(cond, w["w_mod"]) + p["b_mod"][l])[:2]
        sh1, sc1, g1, sh2, sc2, g2 = jnp.split(mod, 6, axis=-1)

        h = norm_mod(xa, p["norm1"][l][None, :], sh1, sc1, s, BF16)
        na_q, na_k, na_v, gq, gk, gv, ga, gb = jnp.split(mm(h, w["w_in"]), IN_SPLITS, axis=-1)

        qa, ka, va = _to_heads(na_q, NA_HEADS), _to_heads(na_k, NA_HEADS), _to_heads(na_v, NA_HEADS)
        slab = _na_slab(p["na_rpb"][l], na_tables)
        ya_lat = na_attn(qa[:, :s], ka[:, :s], va[:, :s], ka[:, s:], va[:, s:], slab)
        ya_ctx = attn(qa[:, s:], ka[:, s:], va[:, s:])
        ya = _from_heads(jnp.concatenate([ya_lat, ya_ctx], axis=1))

        qk = _to_heads(jnp.concatenate([gq, gk], axis=-1), GQA_Q_HEADS + GQA_KV_HEADS)
        gains = jnp.stack([p["q_gain"][l], p["k_gain"][l]])
        qk = hn_rope(qk, gains, cos2, sin2, rot, GQA_Q_HEADS)
        qb, kb, vb = qk[:GQA_Q_HEADS], qk[GQA_Q_HEADS:], _to_heads(gv, GQA_KV_HEADS)
        q_lat = qb[:, :s].reshape(GQA_KV_HEADS, GQA_REP * s, HEAD_DIM)
        if l + 1 < DEPTH:
            ob_lat, w_next = attn_gather(q_lat, kb, vb, layer_shards(l + 1))
        else:
            ob_lat, w_next = attn(q_lat, kb, vb), None
        ob_ctx = attn(qb[:, s:].reshape(GQA_KV_HEADS, GQA_REP * n_ctx, HEAD_DIM), kb[:, s:], vb[:, s:])
        yb = _from_heads(jnp.concatenate([ob_lat.reshape(GQA_Q_HEADS, s, HEAD_DIM),
                                          ob_ctx.reshape(GQA_Q_HEADS, n_ctx, HEAD_DIM)], axis=1))

        merged = jax.nn.sigmoid(ga) * mm(ya.astype(BF16), w["w_pa"]) + jax.nn.sigmoid(gb) * mm(yb.astype(BF16), w["w_pb"])
        xa = xa + seg(g1) * mm(merged.astype(BF16), w["w_o"])

        h2 = norm_mod(xa, p["norm2"][l][None, :], sh2, sc2, s, BF16)
        a, u = jnp.split(mm(h2, w["w_ffn_in"]), 2, axis=-1)
        xa = xa + seg(g2) * mm((jax.nn.silu(a) * u).astype(BF16), w["w_ffn_out"])
        w = w_next

    zeros2 = jnp.zeros((2, D_MODEL), F32)
    y = norm_mod(xa, p["final_norm"][None, :], zeros2, zeros2, s, F32)[:s]
    return loss_head(y, target)


def kernel(x, c, ctx, c_ctx, w_mod, b_mod, norm1, w_in, na_rpb, q_gain, k_gain, w_pa, w_pb, w_o, norm2, w_ffn_in, w_ffn_out, final_norm, loss_target, m_c_ctx, m_w_mod, m_b_mod, m_norm1, m_w_in, m_na_rpb, m_q_gain, m_k_gain, m_w_pa, m_w_pb, m_w_o, m_norm2, m_w_ffn_in, m_w_ffn_out, m_final_norm, v_c_ctx, v_w_mod, v_b_mod, v_norm1, v_w_in, v_na_rpb, v_q_gain, v_k_gain, v_w_pa, v_w_pb, v_w_o, v_norm2, v_w_ffn_in, v_w_ffn_out, v_final_norm):
    w = dict(c_ctx=c_ctx, w_mod=w_mod, b_mod=b_mod, norm1=norm1, w_in=w_in, na_rpb=na_rpb, q_gain=q_gain, k_gain=k_gain,
             w_pa=w_pa, w_pb=w_pb, w_o=w_o, norm2=norm2, w_ffn_in=w_ffn_in, w_ffn_out=w_ffn_out, final_norm=final_norm)
    mom = dict(c_ctx=m_c_ctx, w_mod=m_w_mod, b_mod=m_b_mod, norm1=m_norm1, w_in=m_w_in, na_rpb=m_na_rpb, q_gain=m_q_gain,
               k_gain=m_k_gain, w_pa=m_w_pa, w_pb=m_w_pb, w_o=m_w_o, norm2=m_norm2, w_ffn_in=m_w_ffn_in,
               w_ffn_out=m_w_ffn_out, final_norm=m_final_norm)
    var = dict(c_ctx=v_c_ctx, w_mod=v_w_mod, b_mod=v_b_mod, norm1=v_norm1, w_in=v_w_in, na_rpb=v_na_rpb, q_gain=v_q_gain,
               k_gain=v_k_gain, w_pa=v_w_pa, w_pb=v_w_pb, w_o=v_w_o, norm2=v_norm2, w_ffn_in=v_w_ffn_in,
               w_ffn_out=v_w_ffn_out, final_norm=v_final_norm)
    s, n_ctx = x.shape[1], ctx.shape[1]
    depth = w_mod.shape[0]

    shards = {n: [w[n][l] for l in range(depth)] for n in BIG}
    small = {n: w[n] for n in SMALL}
    consts = (*_rope_tables(s, n_ctx), _rot_matrix(), _na_tables(s // GRID_W))
    loss, (g_shards, g_small, gx) = jax.value_and_grad(_local_loss, argnums=(0, 1, 2))(
        shards, small, x[0], c, ctx[0], loss_target[0], consts)

    parts_s = _exchange(_pack_small(g_small, loss), True, "gather_small_grads")
    g_packed = _sum8_call(parts_s)
    grads, loss = _unpack_small(g_packed, w)
    zero = jnp.zeros((1,), F32)
    upd_s = _adam_call(g_packed, _pack_small(w, zero), _pack_small(mom, zero), _pack_small(var, zero))
    outs = [grads] + [_unpack_small(u, w)[0] for u in upd_s]

    for n in BIG:
        flat = lambda t: t.reshape(-1, t.shape[-1])
        g = jnp.stack(g_shards[n])
        upd = _adam_call(flat(g), flat(w[n]), flat(mom[n]), flat(var[n]))
        for k, t in enumerate([g] + [u.reshape(w[n].shape) for u in upd]):
            outs[k][n] = t
    return (loss, gx[None], *[o[n] for o in outs for n in WEIGHTS])
```

```python
import functools

import numpy as np
import jax
import jax.numpy as jnp
from jax import lax
from jax.experimental import pallas as pl
from jax.experimental.pallas import tpu as pltpu

F32 = jnp.float32
BF16 = jnp.bfloat16
HIGHEST = lax.Precision.HIGHEST

D_MODEL = 1024
DEPTH = 4
GRID_W = 64
HEAD_DIM = 64
NA_HEADS = 8
NA_WIN_H = 8
NA_WIN_W = 16
GQA_Q_HEADS = 8
GQA_KV_HEADS = 2
GQA_REP = GQA_Q_HEADS // GQA_KV_HEADS
NA_WIDTH = NA_HEADS * HEAD_DIM
GQA_Q_WIDTH = GQA_Q_HEADS * HEAD_DIM
GQA_KV_WIDTH = GQA_KV_HEADS * HEAD_DIM
IN_SIZES = (NA_WIDTH, NA_WIDTH, NA_WIDTH, GQA_Q_WIDTH, GQA_KV_WIDTH, GQA_KV_WIDTH, D_MODEL, D_MODEL)
IN_SPLITS = tuple(int(v) for v in np.cumsum(IN_SIZES)[:-1])
ROPE_THETA = 10000.0
EPS = 1e-6
SCALE = HEAD_DIM ** -0.5
ADAM_LR = 0.001
ADAM_B1 = 0.9
ADAM_B2 = 0.999
ADAM_EPS = 1e-08
ADAM_WD = 0.01
ADAM_STEP = 10

N_DEV = 8
MESH = pl.DeviceIdType.MESH
NEG_BIG = -1e30

VMEM_BIG_LIMIT = 52 * 1024 * 1024
NA_QROWS = 4
NA_QB = NA_QROWS * GRID_W
NA_KROWS = 12
NA_KW = NA_KROWS * GRID_W
NA_FWD_HEADS = 4
MM_TILE_ELEMS = 512 * 2176
MM_OPERAND_ELEMS = 1408 * 2176
MOD_ROWS = 256

BIG = ("w_mod", "w_in", "w_pa", "w_pb", "w_o", "w_ffn_in", "w_ffn_out")
ROW_SHARDED = ("w_o", "w_ffn_out")
SMALL = ("c_ctx", "b_mod", "norm1", "na_rpb", "q_gain", "k_gain", "norm2", "final_norm")
WEIGHTS = ("c_ctx", "w_mod", "b_mod", "norm1", "w_in", "na_rpb", "q_gain", "k_gain", "w_pa", "w_pb", "w_o",
           "norm2", "w_ffn_in", "w_ffn_out", "final_norm")


def _pick(n, cands):
    for c in cands:
        if n % c == 0:
            return c
    return n


def _dot_nt(a, b):
    return lax.dot_general(a, b, (((1,), (1,)), ((), ())), preferred_element_type=F32)


def _dot_tn(a, b):
    return lax.dot_general(a, b, (((0,), (0,)), ((), ())), preferred_element_type=F32)


def _dot(a, b):
    return jnp.dot(a, b, preferred_element_type=F32)


def _mm_call(a, b, name, mode="nn", out_dtype=F32):
    (m, k) = a.shape if mode != "tn" else a.shape[::-1]
    n = b.shape[1] if mode != "nt" else b.shape[0]
    tn = _pick(n, (2176, 1408, 1024, 512, 256, 128))
    tm = _pick(m, tuple(t for t in (768, 512, 256, 128) if t * tn <= MM_TILE_ELEMS))
    tk = _pick(k, tuple(t for t in (2816, 2176, 1408, 1024, 768, 512, 256, 128) if t * max(tm, tn) <= MM_OPERAND_ELEMS))
    nk = k // tk
    dot = {"nn": _dot, "tn": _dot_tn, "nt": _dot_nt}[mode]
    a_blk = (tk, tm) if mode == "tn" else (tm, tk)
    b_blk = (tn, tk) if mode == "nt" else (tk, tn)
    a_idx = (lambda i, kk: (kk, i)) if mode == "tn" else (lambda i, kk: (i, kk))
    b_idx = (lambda j, kk: (j, kk)) if mode == "nt" else (lambda j, kk: (kk, j))

    def body(a_ref, b_ref, o_ref, acc_ref):
        kk = pl.program_id(2)
        part = dot(a_ref[...], b_ref[...])

        @pl.when(kk == 0)
        def _():
            acc_ref[...] = part

        @pl.when(kk > 0)
        def _():
            acc_ref[...] += part

        @pl.when(kk == nk - 1)
        def _():
            o_ref[...] = acc_ref[...].astype(o_ref.dtype)

    def body1(a_ref, b_ref, o_ref):
        o_ref[...] = dot(a_ref[...], b_ref[...]).astype(o_ref.dtype)

    footprint = 2 * (tm * tk * 2 + tk * tn * 2 + tm * tn * 4) + 2 * tm * tn * 4
    limit = int(footprint + (8 << 20))
    if nk == 1:
        return pl.pallas_call(
            body1, grid=(n // tn, m // tm),
            in_specs=[pl.BlockSpec(a_blk, lambda j, i: a_idx(i, 0)), pl.BlockSpec(b_blk, lambda j, i: b_idx(j, 0))],
            out_specs=pl.BlockSpec((tm, tn), lambda j, i: (i, j)),
            out_shape=jax.ShapeDtypeStruct((m, n), out_dtype),
            compiler_params=pltpu.CompilerParams(dimension_semantics=("parallel", "parallel"), vmem_limit_bytes=limit),
            name=name,
        )(a, b)
    return pl.pallas_call(
        body, grid=(m // tm, n // tn, nk),
        in_specs=[pl.BlockSpec(a_blk, lambda i, j, kk: a_idx(i, kk)), pl.BlockSpec(b_blk, lambda i, j, kk: b_idx(j, kk))],
        out_specs=pl.BlockSpec((tm, tn), lambda i, j, kk: (i, j)),
        out_shape=jax.ShapeDtypeStruct((m, n), out_dtype),
        scratch_shapes=[pltpu.VMEM((tm, tn), F32)],
        compiler_params=pltpu.CompilerParams(
            dimension_semantics=("parallel", "parallel", "arbitrary"), vmem_limit_bytes=limit),
        name=name,
    )(a, b)


@jax.custom_vjp
def mm(x, w):
    return _mm_call(x, w, "mm_fwd")


def _mm_fwd(x, w):
    return _mm_call(x, w, "mm_fwd"), (x, w)


def _mm_bwd(res, dy):
    x, w = res
    dyb = dy.astype(BF16)
    return _mm_call(dyb, w, "mm_dx", "nt", BF16), _mm_call(x, dyb, "mm_dw", "tn", BF16)


mm.defvjp(_mm_fwd, _mm_bwd)


def _ffn_tiles(t, f):
    return _pick(t, (384, 256, 128)), _pick(f, (1408, 1024, 512, 256, 128))


def _ffn_in_call(h, w_in):
    t, d = h.shape
    f = w_in.shape[1] // 2
    tm, tn = _ffn_tiles(t, f)
    nf = f // tn

    def body(h_ref, wa_ref, wu_ref, act_ref, au_ref):
        hv = h_ref[...]
        a, u = _dot(hv, wa_ref[...]), _dot(hv, wu_ref[...])
        au_ref[0] = a
        au_ref[1] = u
        act_ref[...] = (a * jax.nn.sigmoid(a) * u).astype(act_ref.dtype)

    return pl.pallas_call(
        body, grid=(nf, t // tm),
        in_specs=[pl.BlockSpec((tm, d), lambda j, i: (i, 0)), pl.BlockSpec((d, tn), lambda j, i: (0, j)),
                  pl.BlockSpec((d, tn), lambda j, i: (0, j + nf))],
        out_specs=[pl.BlockSpec((tm, tn), lambda j, i: (i, j)), pl.BlockSpec((2, tm, tn), lambda j, i: (0, i, j))],
        out_shape=[jax.ShapeDtypeStruct((t, f), BF16), jax.ShapeDtypeStruct((2, t, f), F32)],
        compiler_params=pltpu.CompilerParams(dimension_semantics=("parallel", "parallel"), vmem_limit_bytes=VMEM_BIG_LIMIT),
        name="ffn_in",
    )(h, w_in, w_in)


def _ffn_dact_call(dy, w_out, au):
    t, d = dy.shape
    f = w_out.shape[0]
    tm, tn = _ffn_tiles(t, f)

    def body(dy_ref, w_ref, au_ref, dau_ref):
        dact = _dot_nt(dy_ref[...], w_ref[...])
        a, u = au_ref[0], au_ref[1]
        sg = jax.nn.sigmoid(a)
        dau_ref[0] = (dact * u * (sg * (1.0 + a * (1.0 - sg)))).astype(dau_ref.dtype)
        dau_ref[1] = (dact * (a * sg)).astype(dau_ref.dtype)

    halves = pl.BlockSpec((2, tm, tn), lambda j, i: (0, i, j))
    return pl.pallas_call(
        body, grid=(f // tn, t // tm),
        in_specs=[pl.BlockSpec((tm, d), lambda j, i: (i, 0)), pl.BlockSpec((tn, d), lambda j, i: (j, 0)), halves],
        out_specs=halves, out_shape=jax.ShapeDtypeStruct((2, t, f), BF16),
        compiler_params=pltpu.CompilerParams(dimension_semantics=("parallel", "parallel"), vmem_limit_bytes=VMEM_BIG_LIMIT),
        name="ffn_dact",
    )(dy, w_out, au)


def _ffn_dh_call(dau, w_in):
    _, t, f = dau.shape
    d = w_in.shape[0]
    tm, tk = _ffn_tiles(t, f)
    nf = f // tk

    def body(a_ref, b_ref, o_ref, acc_ref):
        kk = pl.program_id(1)
        part = _dot_nt(a_ref[...], b_ref[...])

        @pl.when(kk == 0)
        def _():
            acc_ref[...] = part

        @pl.when(kk > 0)
        def _():
            acc_ref[...] += part

        @pl.when(kk == 2 * nf - 1)
        def _():
            o_ref[...] = acc_ref[...].astype(o_ref.dtype)

    return pl.pallas_call(
        body, grid=(t // tm, 2 * nf),
        in_specs=[pl.BlockSpec((None, tm, tk), lambda i, kk: (kk // nf, i, kk % nf)), pl.BlockSpec((d, tk), lambda i, kk: (0, kk))],
        out_specs=pl.BlockSpec((tm, d), lambda i, kk: (i, 0)), out_shape=jax.ShapeDtypeStruct((t, d), BF16),
        scratch_shapes=[pltpu.VMEM((tm, d), F32)],
        compiler_params=pltpu.CompilerParams(dimension_semantics=("parallel", "arbitrary"), vmem_limit_bytes=VMEM_BIG_LIMIT),
        name="ffn_dh",
    )(dau, w_in)


@jax.custom_vjp
def ffn(h, w_in, w_out):
    return _ffn_fwd(h, w_in, w_out)[0]


def _ffn_fwd(h, w_in, w_out):
    act, au = _ffn_in_call(h, w_in)
    return _mm_call(act, w_out, "mm_fwd"), (h, w_in, w_out, act, au)


def _ffn_bwd(res, dy):
    h, w_in, w_out, act, au = res
    dyb = dy.astype(BF16)
    dau = _ffn_dact_call(dyb, w_out, au)
    dw_in = jnp.concatenate([_mm_call(h, dau[0], "mm_dw", "tn", BF16), _mm_call(h, dau[1], "mm_dw", "tn", BF16)], axis=1)
    return _ffn_dh_call(dau, w_in), dw_in, _mm_call(act, dyb, "mm_dw", "tn", BF16)


ffn.defvjp(_ffn_fwd, _ffn_bwd)


def _seg_rows(ref, is_ctx):
    return jnp.where(is_ctx, ref[1:2, :], ref[0:1, :])


def _one_hot_row(second):
    return (lax.broadcasted_iota(jnp.int32, (2, 1), 0) == second.astype(jnp.int32)).astype(F32)


def _norm_mod_fwd_call(x, g, shift, scale, n_lat, out_dtype):
    t, d = x.shape
    tt = _pick(t, (256, 128))
    lat_tiles = n_lat // tt

    def body(x_ref, g_ref, sh_ref, sc_ref, y_ref):
        is_ctx = pl.program_id(0) >= lat_tiles
        xv = x_ref[...]
        r = lax.rsqrt(jnp.mean(xv * xv, axis=-1, keepdims=True) + EPS)
        yn = xv * r * g_ref[...]
        y_ref[...] = (yn * (1.0 + _seg_rows(sc_ref, is_ctx)) + _seg_rows(sh_ref, is_ctx)).astype(y_ref.dtype)

    row = pl.BlockSpec((tt, d), lambda i: (i, 0))
    full = lambda rws: pl.BlockSpec((rws, d), lambda i: (0, 0))
    return pl.pallas_call(
        body, grid=(t // tt,), in_specs=[row, full(1), full(2), full(2)], out_specs=row,
        out_shape=jax.ShapeDtypeStruct((t, d), out_dtype),
        compiler_params=pltpu.CompilerParams(dimension_semantics=("parallel",)), name="norm_mod_fwd",
    )(x, g, shift, scale)


def _norm_mod_bwd_call(x, g, scale, dy, n_lat):
    t, d = x.shape
    tt = _pick(t, (256, 128))
    lat_tiles = n_lat // tt

    def body(x_ref, g_ref, sc_ref, dy_ref, dx_ref, dg_ref, dsh_ref, dsc_ref):
        i = pl.program_id(0)
        is_ctx = i >= lat_tiles

        @pl.when(i == 0)
        def _():
            dg_ref[...] = jnp.zeros_like(dg_ref)
            dsh_ref[...] = jnp.zeros_like(dsh_ref)
            dsc_ref[...] = jnp.zeros_like(dsc_ref)

        xv, dyv, gv = x_ref[...], dy_ref[...].astype(F32), g_ref[...]
        r = lax.rsqrt(jnp.mean(xv * xv, axis=-1, keepdims=True) + EPS)
        nrm = xv * r
        yn = nrm * gv
        seg = _one_hot_row(is_ctx)
        dsh_ref[...] += seg * jnp.sum(dyv, axis=0, keepdims=True)
        dsc_ref[...] += seg * jnp.sum(dyv * yn, axis=0, keepdims=True)
        dyn = dyv * (1.0 + _seg_rows(sc_ref, is_ctx))
        dg_ref[...] += jnp.sum(dyn * nrm, axis=0, keepdims=True)
        dn = dyn * gv
        dx_ref[...] = r * (dn - nrm * jnp.mean(dn * nrm, axis=-1, keepdims=True))

    row = pl.BlockSpec((tt, d), lambda i: (i, 0))
    full = lambda rws: pl.BlockSpec((rws, d), lambda i: (0, 0))
    return pl.pallas_call(
        body, grid=(t // tt,), in_specs=[row, full(1), full(2), row], out_specs=[row, full(1), full(2), full(2)],
        out_shape=[jax.ShapeDtypeStruct((t, d), F32), jax.ShapeDtypeStruct((1, d), F32),
                   jax.ShapeDtypeStruct((2, d), F32), jax.ShapeDtypeStruct((2, d), F32)],
        compiler_params=pltpu.CompilerParams(dimension_semantics=("arbitrary",)), name="norm_mod_bwd",
    )(x, g, scale, dy)


@functools.partial(jax.custom_vjp, nondiff_argnums=(4, 5))
def norm_mod(x, g, shift, scale, n_lat, out_dtype):
    return _norm_mod_fwd_call(x, g, shift, scale, n_lat, out_dtype)


def _norm_mod_fwd(x, g, shift, scale, n_lat, out_dtype):
    return _norm_mod_fwd_call(x, g, shift, scale, n_lat, out_dtype), (x, g, scale)


def _norm_mod_bwd(n_lat, out_dtype, res, dy):
    x, g, scale = res
    dx, dg, dsh, dsc = _norm_mod_bwd_call(x, g, scale, dy, n_lat)
    return dx, dg, dsh, dsc


norm_mod.defvjp(_norm_mod_fwd, _norm_mod_bwd)


def _rot_matrix():
    p = np.zeros((HEAD_DIM, HEAD_DIM), np.float32)
    for i in range(HEAD_DIM // 2):
        p[2 * i + 1, 2 * i] = -1.0
        p[2 * i, 2 * i + 1] = 1.0
    return jnp.asarray(p)


def _hn_rope_fwd_call(x, gains, cos2, sin2, rot, n_q):
    nh, t, dh = x.shape
    tt = _pick(t, (2816, 1024, 768, 512, 256, 128))

    def body(x_ref, g_ref, cos_ref, sin_ref, rot_ref, y_ref):
        gv = jnp.where(pl.program_id(0) >= n_q, g_ref[1:2, :], g_ref[0:1, :])
        xv = x_ref[...]
        r = lax.rsqrt(jnp.mean(xv * xv, axis=-1, keepdims=True) + EPS)
        y = xv * r * gv
        yr = jnp.dot(y, rot_ref[...], precision=HIGHEST, preferred_element_type=F32)
        y_ref[...] = y * cos_ref[...] + yr * sin_ref[...]

    blk = pl.BlockSpec((None, tt, dh), lambda h, i: (h, i, 0))
    tab = pl.BlockSpec((tt, dh), lambda h, i: (i, 0))
    return pl.pallas_call(
        body, grid=(nh, t // tt),
        in_specs=[blk, pl.BlockSpec((2, dh), lambda h, i: (0, 0)), tab, tab, pl.BlockSpec((dh, dh), lambda h, i: (0, 0))],
        out_specs=blk, out_shape=jax.ShapeDtypeStruct((nh, t, dh), F32),
        compiler_params=pltpu.CompilerParams(dimension_semantics=("parallel", "parallel")), name="hn_rope_fwd",
    )(x, gains, cos2, sin2, rot)


def _hn_rope_bwd_call(x, gains, cos2, sin2, rot, dy, n_q):
    nh, t, dh = x.shape
    tt = _pick(t, (2816, 1024, 768, 512, 256, 128))

    def body(x_ref, g_ref, cos_ref, sin_ref, rot_ref, dy_ref, dx_ref, dg_ref):
        h, i = pl.program_id(0), pl.program_id(1)
        is_k = h >= n_q

        @pl.when((h == 0) & (i == 0))
        def _():
            dg_ref[...] = jnp.zeros_like(dg_ref)

        gv = jnp.where(is_k, g_ref[1:2, :], g_ref[0:1, :])
        xv, dyv = x_ref[...], dy_ref[...]
        r = lax.rsqrt(jnp.mean(xv * xv, axis=-1, keepdims=True) + EPS)
        nrm = xv * r
        dyn = dyv * cos_ref[...] - jnp.dot(dyv * sin_ref[...], rot_ref[...], precision=HIGHEST, preferred_element_type=F32)
        seg = _one_hot_row(is_k)
        dg_ref[...] += seg * jnp.sum(dyn * nrm, axis=0, keepdims=True)
        dn = dyn * gv
        dx_ref[...] = r * (dn - nrm * jnp.mean(dn * nrm, axis=-1, keepdims=True))

    blk = pl.BlockSpec((None, tt, dh), lambda h, i: (h, i, 0))
    tab = pl.BlockSpec((tt, dh), lambda h, i: (i, 0))
    g_spec = pl.BlockSpec((2, dh), lambda h, i: (0, 0))
    return pl.pallas_call(
        body, grid=(nh, t // tt),
        in_specs=[blk, g_spec, tab, tab, pl.BlockSpec((dh, dh), lambda h, i: (0, 0)), blk],
        out_specs=[blk, g_spec],
        out_shape=[jax.ShapeDtypeStruct((nh, t, dh), F32), jax.ShapeDtypeStruct((2, dh), F32)],
        compiler_params=pltpu.CompilerParams(dimension_semantics=("arbitrary", "arbitrary")), name="hn_rope_bwd",
    )(x, gains, cos2, sin2, rot, dy)


@functools.partial(jax.custom_vjp, nondiff_argnums=(5,))
def hn_rope(x, gains, cos2, sin2, rot, n_q):
    return _hn_rope_fwd_call(x, gains, cos2, sin2, rot, n_q)


def _hn_rope_fwd(x, gains, cos2, sin2, rot, n_q):
    return _hn_rope_fwd_call(x, gains, cos2, sin2, rot, n_q), (x, gains, cos2, sin2, rot)


def _hn_rope_bwd(n_q, res, dy):
    x, gains, cos2, sin2, rot = res
    dx, dg = _hn_rope_bwd_call(x, gains, cos2, sin2, rot, dy, n_q)
    return dx, dg, jnp.zeros_like(cos2), jnp.zeros_like(sin2), jnp.zeros_like(rot)


hn_rope.defvjp(_hn_rope_fwd, _hn_rope_bwd)


EXCHANGE_SCRATCH = [pltpu.SemaphoreType.DMA((N_DEV - 1,)), pltpu.SemaphoreType.DMA((N_DEV - 1,)), pltpu.SemaphoreType.DMA(())]


def _exchange_copies(x_ref, out_ref, send_sems, recv_sems, local_sem, all_gather):
    mx, my, mc = lax.axis_index("x"), lax.axis_index("y"), lax.axis_index("c")
    me = 4 * mx + 2 * my + mc
    src = (lambda p: x_ref) if all_gather else (lambda p: x_ref.at[p])
    local = pltpu.make_async_copy(src(me), out_ref.at[me], local_sem)
    remote = []
    for rel in range(1, N_DEV):
        px, py, pc = mx ^ (rel >> 2), my ^ ((rel >> 1) & 1), mc ^ (rel & 1)
        remote.append(pltpu.make_async_remote_copy(
            src_ref=src(4 * px + 2 * py + pc), dst_ref=out_ref.at[me],
            send_sem=send_sems.at[rel - 1], recv_sem=recv_sems.at[rel - 1],
            device_id=(px, py, pc), device_id_type=MESH))
    return local, remote


def _exchange_start(*refs, all_gather):
    local, remote = _exchange_copies(*refs, all_gather)
    local.start()
    for cp in remote:
        cp.start()


def _exchange_wait(*refs, all_gather):
    local, remote = _exchange_copies(*refs, all_gather)
    for cp in remote:
        cp.wait_send()
    for cp in remote:
        cp.wait_recv()
    local.wait()


def _exchange_shape(x, all_gather):
    return jax.ShapeDtypeStruct((N_DEV,) + tuple(x.shape if all_gather else x.shape[1:]), x.dtype)


def _exchange(x, all_gather, name):
    def body(x_ref, out_ref, *sems):
        _exchange_start(x_ref, out_ref, *sems, all_gather=all_gather)
        _exchange_wait(x_ref, out_ref, *sems, all_gather=all_gather)

    return pl.pallas_call(
        body, in_specs=[pl.BlockSpec(memory_space=pl.ANY)], out_specs=pl.BlockSpec(memory_space=pl.ANY),
        out_shape=_exchange_shape(x, all_gather), scratch_shapes=EXCHANGE_SCRATCH,
        compiler_params=pltpu.CompilerParams(has_side_effects=True), name=name,
    )(x)


ATT_TK = 256


def _att_tq(rn, fwd=False):
    return _pick(rn, ((1024,) if fwd else ()) + (512, 256, 128))


def _att_unroll(nkb):
    return _pick(nkb, (3, 2))


def _flash_fwd_call(q_t, k, v_t, xsend=None):
    g, dh, rn = q_t.shape
    nk = k.shape[1]
    tq, tk = _att_tq(rn, True), ATT_TK
    nkb = nk // tk
    unroll = _att_unroll(nkb)
    n_i = rn // tq

    def body(*refs):
        if xsend is None:
            qt_ref, k_ref, vt_ref, ot_ref, lse_ref = refs
        else:
            qt_ref, k_ref, vt_ref, x_ref, ot_ref, lse_ref, xout_ref, *sems = refs
            a, i = pl.program_id(0), pl.program_id(1)

            @pl.when((a == 0) & (i == 0))
            def _():
                _exchange_start(x_ref, xout_ref, *sems, all_gather=True)

        qst = qt_ref[...] * SCALE

        def trip(t, carry):
            m, l, acc = carry
            blocks = [t * unroll + u for u in range(unroll)]
            s_t = [_dot(k_ref[pl.ds(pl.multiple_of(j * tk, tk), tk), :], qst) for j in blocks]
            m_new = functools.reduce(jnp.maximum, [jnp.max(s, axis=0, keepdims=True) for s in s_t], m)
            p_t = [jnp.exp(s - m_new) for s in s_t]
            alpha = jnp.exp(m - m_new)
            l = alpha * l + sum(jnp.sum(p, axis=0, keepdims=True) for p in p_t)
            acc = alpha * acc + sum(_dot(vt_ref[j], p.astype(BF16)) for j, p in zip(blocks, p_t))
            return m_new, l, acc

        m, l, acc = lax.fori_loop(
            0, nkb // unroll, trip, (jnp.full((1, tq), NEG_BIG, F32), jnp.zeros((1, tq), F32), jnp.zeros((dh, tq), F32)))
        ot_ref[...] = acc / l
        lse_ref[...] = m + jnp.log(l)

        if xsend is not None:
            @pl.when((a == g - 1) & (i == n_i - 1))
            def _():
                _exchange_wait(x_ref, xout_ref, *sems, all_gather=True)

    col = pl.BlockSpec((None, dh, tq), lambda a, i: (a, 0, i))
    vec = pl.BlockSpec((None, 1, tq), lambda a, i: (a, 0, i))
    hbm = pl.BlockSpec(memory_space=pl.ANY)
    in_specs = [col, pl.BlockSpec((None, nk, dh), lambda a, i: (a, 0, 0)),
                pl.BlockSpec((None, nkb, dh, tk), lambda a, i: (a, 0, 0, 0))]
    out_specs = [col, vec]
    out_shape = [jax.ShapeDtypeStruct((g, dh, rn), F32), jax.ShapeDtypeStruct((g, 1, rn), F32)]
    args = (q_t, k, v_t)
    if xsend is not None:
        in_specs, out_specs, args = in_specs + [hbm], out_specs + [hbm], args + (xsend,)
        out_shape = out_shape + [_exchange_shape(xsend, True)]
    return pl.pallas_call(
        body, grid=(g, n_i), in_specs=in_specs, out_specs=out_specs, out_shape=out_shape,
        scratch_shapes=[] if xsend is None else EXCHANGE_SCRATCH,
        compiler_params=pltpu.CompilerParams(dimension_semantics=("arbitrary", "arbitrary"), vmem_limit_bytes=VMEM_BIG_LIMIT),
        name="flash_fwd" if xsend is None else "flash_fwd_gather",
    )(*args)


def _flash_bwd_call(q, q_t, k, k_t, v, do, do_t, o_t, lse_t, xsend=None):
    g, rn, dh = q.shape
    nk = k.shape[1]
    tq, tk = _att_tq(rn), ATT_TK
    nkb = nk // tk
    unroll = _att_unroll(nkb)
    n_i = rn // tq

    def body(*refs):
        q_ref, qt_ref, k_ref, kt_ref, v_ref, do_ref, dot_ref, ot_ref, lse_ref = refs[:9]
        if xsend is None:
            dqt_ref, dk_ref, dv_ref = refs[9:]
        else:
            x_ref, dqt_ref, dk_ref, dv_ref, xout_ref, *sems = refs[9:]
        a, i = pl.program_id(0), pl.program_id(1)

        if xsend is not None:
            @pl.when((a == 0) & (i == 0))
            def _():
                _exchange_start(x_ref, xout_ref, *sems, all_gather=False)

        @pl.when(i == 0)
        def _():
            dk_ref[...] = jnp.zeros_like(dk_ref)
            dv_ref[...] = jnp.zeros_like(dv_ref)

        qs = q_ref[...] * SCALE
        qst = qt_ref[...] * SCALE
        dov, dotv = do_ref[...], dot_ref[...]
        delta = jnp.sum(dotv.astype(F32) * ot_ref[...], axis=0, keepdims=True)
        lse = lse_ref[...]

        def step(j, dqt):
            off = pl.multiple_of(j * tk, tk)
            kj = k_ref[pl.ds(off, tk), :]
            vj = v_ref[pl.ds(off, tk), :]
            p_t = jnp.exp(_dot(kj, qst) - lse)
            ds_t = (p_t * (_dot(vj, dotv) - delta)).astype(BF16)
            dv_ref[pl.ds(off, tk), :] += _dot(p_t.astype(BF16), dov)
            dk_ref[pl.ds(off, tk), :] += _dot(ds_t, qs)
            return dqt + _dot(kt_ref[j], ds_t)

        def trip(t, dqt):
            for u in range(unroll):
                dqt = step(t * unroll + u, dqt)
            return dqt

        dqt_ref[...] = lax.fori_loop(0, nkb // unroll, trip, jnp.zeros((dh, tq), F32)) * SCALE

        if xsend is not None:
            @pl.when((a == g - 1) & (i == n_i - 1))
            def _():
                _exchange_wait(x_ref, xout_ref, *sems, all_gather=False)

    row = pl.BlockSpec((None, tq, dh), lambda a, i: (a, i, 0))
    col = pl.BlockSpec((None, dh, tq), lambda a, i: (a, 0, i))
    kv = pl.BlockSpec((None, nk, dh), lambda a, i: (a, 0, 0))
    hbm = pl.BlockSpec(memory_space=pl.ANY)
    in_specs = [row, col, kv, pl.BlockSpec((None, nkb, dh, tk), lambda a, i: (a, 0, 0, 0)), kv, row, col, col,
                pl.BlockSpec((None, 1, tq), lambda a, i: (a, 0, i))]
    out_specs = [col, kv, kv]
    out_shape = [jax.ShapeDtypeStruct((g, dh, rn), F32), jax.ShapeDtypeStruct((g, nk, dh), F32),
                 jax.ShapeDtypeStruct((g, nk, dh), F32)]
    args = (q, q_t, k, k_t, v, do, do_t, o_t, lse_t)
    if xsend is not None:
        in_specs, out_specs, args = in_specs + [hbm], out_specs + [hbm], args + (xsend,)
        out_shape = out_shape + [_exchange_shape(xsend, False)]
    return pl.pallas_call(
        body, grid=(g, n_i), in_specs=in_specs, out_specs=out_specs, out_shape=out_shape,
        scratch_shapes=[] if xsend is None else EXCHANGE_SCRATCH,
        compiler_params=pltpu.CompilerParams(dimension_semantics=("arbitrary", "arbitrary"), vmem_limit_bytes=VMEM_BIG_LIMIT),
        name="flash_bwd" if xsend is None else "flash_bwd_scatter",
    )(*args)


def _key_blocks_t(t):
    g, nk, dh = t.shape
    return t.reshape(g, nk // ATT_TK, ATT_TK, dh).transpose(0, 1, 3, 2)


@jax.custom_vjp
def attn(q, k, v):
    return _attn_fwd(q, k, v)[0]


def _attn_fwd(q, k, v):
    qb, kb, vb = q.astype(BF16), k.astype(BF16), v.astype(BF16)
    q_t = qb.transpose(0, 2, 1)
    o_t, lse_t = _flash_fwd_call(q_t, kb, _key_blocks_t(vb))
    return o_t.transpose(0, 2, 1), (qb, q_t, kb, vb, o_t, lse_t)


def _attn_bwd(res, do):
    qb, q_t, kb, vb, o_t, lse_t = res
    dob = do.astype(BF16)
    dq_t, dk, dv = _flash_bwd_call(qb, q_t, kb, _key_blocks_t(kb), vb, dob, dob.transpose(0, 2, 1), o_t, lse_t)
    return dq_t.transpose(0, 2, 1), dk, dv


attn.defvjp(_attn_fwd, _attn_bwd)


def _shard_rows(shards):
    return jnp.concatenate([shards[n].reshape(-1, D_MODEL) for n in BIG], axis=0).astype(BF16)


def _fulls_of(gathered, shards):
    out, off = {}, 0
    for n in BIG:
        k, nn = shards[n].shape
        r = k * nn // D_MODEL
        t = gathered[:, off:off + r].reshape(N_DEV, k, nn)
        out[n] = t.reshape(N_DEV * k, nn) if n in ROW_SHARDED else t.transpose(1, 0, 2).reshape(k, N_DEV * nn)
        off += r
    return out


def _slabs_of(dfulls):
    parts = []
    for n in BIG:
        k, nn = dfulls[n].shape
        t = (dfulls[n].reshape(N_DEV, k // N_DEV, nn) if n in ROW_SHARDED
             else dfulls[n].reshape(k, N_DEV, nn // N_DEV).transpose(1, 0, 2))
        parts.append(t.reshape(N_DEV, -1, D_MODEL))
    return jnp.concatenate(parts, axis=1)


def _dshards_of(parts, dfulls):
    rows, out, off = _sum8_call(parts), {}, 0
    for n in BIG:
        k, nn = dfulls[n].shape
        shape = (k // N_DEV, nn) if n in ROW_SHARDED else (k, nn // N_DEV)
        r = shape[0] * shape[1] // D_MODEL
        out[n] = rows[off:off + r].reshape(shape)
        off += r
    return out


@jax.custom_vjp
def gather_layer(shards):
    return _fulls_of(_exchange(_shard_rows(shards), True, "gather_weights"), shards)


def _gather_layer_fwd(shards):
    return gather_layer(shards), None


def _gather_layer_bwd(_, dfulls):
    return (_dshards_of(_exchange(_slabs_of(dfulls), False, "scatter_grads"), dfulls),)


gather_layer.defvjp(_gather_layer_fwd, _gather_layer_bwd)


@jax.custom_vjp
def attn_gather(q, k, v, shards):
    return _attn_gather_fwd(q, k, v, shards)[0]


def _attn_gather_fwd(q, k, v, shards):
    qb, kb, vb = q.astype(BF16), k.astype(BF16), v.astype(BF16)
    q_t = qb.transpose(0, 2, 1)
    o_t, lse_t, gathered = _flash_fwd_call(q_t, kb, _key_blocks_t(vb), _shard_rows(shards))
    return (o_t.transpose(0, 2, 1), _fulls_of(gathered, shards)), (qb, q_t, kb, vb, o_t, lse_t)


def _attn_gather_bwd(res, cts):
    qb, q_t, kb, vb, o_t, lse_t = res
    do, dfulls = cts
    dob = do.astype(BF16)
    dq_t, dk, dv, parts = _flash_bwd_call(qb, q_t, kb, _key_blocks_t(kb), vb, dob, dob.transpose(0, 2, 1), o_t, lse_t,
                                          _slabs_of(dfulls))
    return dq_t.transpose(0, 2, 1), dk, dv, _dshards_of(parts, dfulls)


attn_gather.defvjp(_attn_gather_fwd, _attn_gather_bwd)


def _na_key_row(i, rows):
    return jnp.clip(NA_QROWS * i - NA_WIN_H // 2, 0, rows - NA_KROWS)


def _na_type(i, nb):
    return jnp.where(i == 0, 0, jnp.where(i == nb - 1, 2, 1))


def _na_fwd_call(q, k, v, kc, vc, slab):
    h, s, dh = q.shape
    n_ctx = kc.shape[1]
    rows, nb = s // GRID_W, s // NA_QB
    hb = NA_FWD_HEADS

    def body(q_ref, k_ref, v_ref, kc_ref, vc_ref, slab_ref, o_ref, lse_ref):
        off = pl.multiple_of(_na_key_row(pl.program_id(1), rows) * GRID_W, NA_QB)
        for hh in range(hb):
            qs = q_ref[hh] * SCALE
            kw = k_ref[hh, pl.ds(off, NA_KW), :]
            vw = v_ref[hh, pl.ds(off, NA_KW), :]
            sw = _dot_nt(qs, kw) + slab_ref[hh]
            sc = _dot_nt(qs, kc_ref[hh])
            m = jnp.maximum(jnp.max(sw, axis=1, keepdims=True), jnp.max(sc, axis=1, keepdims=True))
            pw = jnp.exp(sw - m)
            pc = jnp.exp(sc - m)
            l = jnp.sum(pw, axis=1, keepdims=True) + jnp.sum(pc, axis=1, keepdims=True)
            o_ref[hh] = (_dot(pw.astype(BF16), vw) + _dot(pc.astype(BF16), vc_ref[hh])) / l
            lse_ref[hh] = m + jnp.log(l)

    qblk = pl.BlockSpec((hb, NA_QB, dh), lambda a, i: (a, i, 0))
    kv = pl.BlockSpec((hb, s, dh), lambda a, i: (a, 0, 0))
    ckv = pl.BlockSpec((hb, n_ctx, dh), lambda a, i: (a, 0, 0))
    return pl.pallas_call(
        body, grid=(h // hb, nb),
        in_specs=[qblk, kv, kv, ckv, ckv, pl.BlockSpec((None, hb, NA_QB, NA_KW), lambda a, i: (_na_type(i, nb), a, 0, 0))],
        out_specs=[qblk, pl.BlockSpec((hb, NA_QB, 1), lambda a, i: (a, i, 0))],
        out_shape=[jax.ShapeDtypeStruct((h, s, dh), F32), jax.ShapeDtypeStruct((h, s, 1), F32)],
        compiler_params=pltpu.CompilerParams(dimension_semantics=("parallel", "parallel"), vmem_limit_bytes=VMEM_BIG_LIMIT),
        name="na_fwd",
    )(q, k, v, kc, vc, slab)


def _na_bwd_call(q, k, v, kc, vc, slab, do, o, lse):
    h, s, dh = q.shape
    n_ctx = kc.shape[1]
    rows, nb = s // GRID_W, s // NA_QB

    def body(q_ref, k_ref, v_ref, kc_ref, vc_ref, slab_ref, do_ref, o_ref, lse_ref,
             dq_ref, dk_ref, dv_ref, dkc_ref, dvc_ref, dslab_ref):
        i = pl.program_id(1)

        @pl.when(i == 0)
        def _():
            dk_ref[...] = jnp.zeros_like(dk_ref)
            dv_ref[...] = jnp.zeros_like(dv_ref)
            dkc_ref[...] = jnp.zeros_like(dkc_ref)
            dvc_ref[...] = jnp.zeros_like(dvc_ref)

        @pl.when((i <= 1) | (i == nb - 1))
        def _():
            dslab_ref[...] = jnp.zeros_like(dslab_ref)

        off = pl.multiple_of(_na_key_row(i, rows) * GRID_W, NA_QB)
        qs = q_ref[...] * SCALE
        kw = k_ref[pl.ds(off, NA_KW), :]
        vw = v_ref[pl.ds(off, NA_KW), :]
        kcv, vcv = kc_ref[...], vc_ref[...]
        dov = do_ref[...]
        lse = lse_ref[...]
        delta = jnp.sum(dov.astype(F32) * o_ref[...], axis=1, keepdims=True)
        pw = jnp.exp(_dot_nt(qs, kw) + slab_ref[...] - lse)
        pc = jnp.exp(_dot_nt(qs, kcv) - lse)
        dsw = pw * (_dot_nt(dov, vw) - delta)
        dsc = pc * (_dot_nt(dov, vcv) - delta)
        dslab_ref[...] += dsw
        dsw, dsc = dsw.astype(BF16), dsc.astype(BF16)
        dq_ref[...] = (_dot(dsw, kw) + _dot(dsc, kcv)) * SCALE
        dk_ref[pl.ds(off, NA_KW), :] += _dot_tn(dsw, qs)
        dv_ref[pl.ds(off, NA_KW), :] += _dot_tn(pw.astype(BF16), dov)
        dkc_ref[...] += _dot_tn(dsc, qs)
        dvc_ref[...] += _dot_tn(pc.astype(BF16), dov)

    qblk = pl.BlockSpec((None, NA_QB, dh), lambda a, i: (a, i, 0))
    kv = pl.BlockSpec((None, s, dh), lambda a, i: (a, 0, 0))
    ckv = pl.BlockSpec((None, n_ctx, dh), lambda a, i: (a, 0, 0))
    slab_spec = pl.BlockSpec((None, None, NA_QB, NA_KW), lambda a, i: (_na_type(i, nb), a, 0, 0))
    return pl.pallas_call(
        body, grid=(h, nb),
        in_specs=[qblk, kv, kv, ckv, ckv, slab_spec, qblk, qblk, pl.BlockSpec((None, NA_QB, 1), lambda a, i: (a, i, 0))],
        out_specs=[qblk, kv, kv, ckv, ckv, slab_spec],
        out_shape=[jax.ShapeDtypeStruct((h, s, dh), F32), jax.ShapeDtypeStruct((h, s, dh), F32),
                   jax.ShapeDtypeStruct((h, s, dh), F32), jax.ShapeDtypeStruct((h, n_ctx, dh), F32),
                   jax.ShapeDtypeStruct((h, n_ctx, dh), F32), jax.ShapeDtypeStruct(slab.shape, F32)],
        compiler_params=pltpu.CompilerParams(dimension_semantics=("arbitrary", "arbitrary"), vmem_limit_bytes=VMEM_BIG_LIMIT),
        name="na_bwd",
    )(q, k, v, kc, vc, slab, do, o, lse)


@jax.custom_vjp
def na_attn(q, k, v, kc, vc, slab):
    return _na_fwd_call(q.astype(BF16), k.astype(BF16), v.astype(BF16), kc.astype(BF16), vc.astype(BF16), slab)[0]


def _na_attn_fwd(q, k, v, kc, vc, slab):
    qb, kb, vb, kcb, vcb = (t.astype(BF16) for t in (q, k, v, kc, vc))
    o, lse = _na_fwd_call(qb, kb, vb, kcb, vcb, slab)
    return o, (qb, kb, vb, kcb, vcb, slab, o, lse)


def _na_attn_bwd(res, do):
    qb, kb, vb, kcb, vcb, slab, o, lse = res
    return tuple(_na_bwd_call(qb, kb, vb, kcb, vcb, slab, do.astype(BF16), o, lse))


na_attn.defvjp(_na_attn_fwd, _na_attn_bwd)


def _na_tables(rows):
    col = np.arange(GRID_W)
    c_start = np.clip(col - NA_WIN_W // 2, 0, GRID_W - NA_WIN_W)
    in_win = (col[None, :] >= c_start[:, None]) & (col[None, :] < c_start[:, None] + NA_WIN_W)
    dc_idx = np.clip(col[None, :] - col[:, None], -(NA_WIN_W - 1), NA_WIN_W - 1) + NA_WIN_W - 1
    onehot = (in_win[:, :, None] & (dc_idx[:, :, None] == np.arange(2 * NA_WIN_W - 1)[None, None, :])).astype(np.float32)
    negcol = np.where(in_win, 0.0, NEG_BIG).astype(np.float32)
    plan = []
    for r0, ks in [(0, 0), (NA_QROWS, 0), (rows - NA_QROWS, rows - NA_KROWS)]:
        per_row = []
        for a in range(NA_QROWS):
            r = r0 + a
            rs = min(max(r - NA_WIN_H // 2, 0), rows - NA_WIN_H)
            valid = np.array([rs <= ks + b < rs + NA_WIN_H for b in range(NA_KROWS)])
            per_row.append((ks - r + NA_WIN_H - 1, valid))
        plan.append(per_row)
    return jnp.asarray(onehot), jnp.asarray(negcol), plan


def _na_slab(rpb, tables):
    onehot, negcol, plan = tables
    n_dr = 2 * NA_WIN_H - 1
    table = jnp.einsum("hdc,wuc->hwdu", rpb, onehot, precision=HIGHEST) + negcol[None, :, None, :]
    types = []
    for per_row in plan:
        slabs = []
        for first, valid in per_row:
            lo, hi = max(first, 0), min(first + NA_KROWS, n_dr)
            sel = jnp.pad(table[:, :, lo:hi, :], ((0, 0), (0, 0), (lo - first, first + NA_KROWS - hi), (0, 0)),
                          constant_values=NEG_BIG)
            sel = jnp.where(jnp.asarray(valid)[None, None, :, None], sel, NEG_BIG)
            slabs.append(sel.reshape(NA_HEADS, GRID_W, NA_KW))
        types.append(jnp.stack(slabs, axis=1).reshape(NA_HEADS, NA_QB, NA_KW))
    return jnp.stack(types)


def _loss_call(y, target):
    s, d = y.shape
    tt = _pick(s, (512, 256, 128))

    def body(y_ref, t_ref, loss_ref, diff_ref):
        @pl.when(pl.program_id(0) == 0)
        def _():
            loss_ref[...] = jnp.zeros_like(loss_ref)

        e = y_ref[...] - t_ref[...]
        diff_ref[...] = e * (1.0 / d)
        loss_ref[...] += 0.5 * jnp.sum(jnp.mean(e * e, axis=-1, keepdims=True), axis=0, keepdims=True)

    row = pl.BlockSpec((tt, d), lambda i: (i, 0))
    return pl.pallas_call(
        body, grid=(s // tt,), in_specs=[row, row], out_specs=[pl.BlockSpec((1, 1), lambda i: (0, 0)), row],
        out_shape=[jax.ShapeDtypeStruct((1, 1), F32), jax.ShapeDtypeStruct((s, d), F32)],
        compiler_params=pltpu.CompilerParams(dimension_semantics=("arbitrary",)), name="loss_head",
    )(y, target)


@jax.custom_vjp
def loss_head(y, target):
    return _loss_call(y, target)[0][0, 0]


def _loss_head_fwd(y, target):
    loss, diff = _loss_call(y, target)
    return loss[0, 0], diff


def _loss_head_bwd(diff, g):
    return diff * g, jnp.zeros_like(diff)


loss_head.defvjp(_loss_head_fwd, _loss_head_bwd)


def _sum8_call(parts):
    _, r, c = parts.shape
    tr = _pick(r, (328, 256, 128))

    def body(p_ref, g_ref):
        g = p_ref[0].astype(F32)
        for s in range(1, N_DEV):
            g = g + p_ref[s].astype(F32)
        g_ref[...] = g

    return pl.pallas_call(
        body, grid=(r // tr,), in_specs=[pl.BlockSpec((N_DEV, tr, c), lambda i: (0, i, 0))],
        out_specs=pl.BlockSpec((tr, c), lambda i: (i, 0)), out_shape=jax.ShapeDtypeStruct((r, c), F32),
        compiler_params=pltpu.CompilerParams(dimension_semantics=("parallel",)), name="sum8",
    )(parts)


def _adam_call(g, w, m, v):
    r, c = g.shape
    tr = _pick(r, (256, 128))

    def body(g_ref, w_ref, m_ref, v_ref, d_ref, nm_ref, nv_ref):
        gv = g_ref[...]
        mn = ADAM_B1 * m_ref[...] + (1.0 - ADAM_B1) * gv
        vn = ADAM_B2 * v_ref[...] + (1.0 - ADAM_B2) * (gv * gv)
        m_hat = mn / (1.0 - ADAM_B1 ** ADAM_STEP)
        v_hat = vn / (1.0 - ADAM_B2 ** ADAM_STEP)
        d_ref[...] = -ADAM_LR * (m_hat / (jnp.sqrt(v_hat) + ADAM_EPS) + ADAM_WD * w_ref[...])
        nm_ref[...] = mn
        nv_ref[...] = vn

    row = pl.BlockSpec((tr, c), lambda i: (i, 0))
    out = jax.ShapeDtypeStruct((r, c), F32)
    return pl.pallas_call(
        body, grid=(r // tr,), in_specs=[row, row, row, row], out_specs=[row, row, row], out_shape=[out, out, out],
        compiler_params=pltpu.CompilerParams(dimension_semantics=("parallel",)), name="adam",
    )(g, w, m, v)


def _pack_small(vals, extra=None):
    flat = [vals[n].reshape(-1) for n in SMALL] + ([] if extra is None else [extra.reshape(-1)])
    flat = jnp.concatenate(flat)
    rows = -(-flat.shape[0] // D_MODEL)
    rows = -(-rows // 8) * 8
    return jnp.pad(flat, (0, rows * D_MODEL - flat.shape[0])).reshape(rows, D_MODEL)


def _unpack_small(packed, like):
    flat, out, off = packed.reshape(-1), {}, 0
    for n in SMALL:
        size = like[n].size
        out[n] = flat[off:off + size].reshape(like[n].shape)
        off += size
    return out, flat[off]


def _rope_tables(s, n_ctx):
    t = jnp.arange(s)
    row = (t // GRID_W).astype(F32)
    col = (t % GRID_W).astype(F32)
    half = HEAD_DIM // 2
    inv = ROPE_THETA ** (-jnp.arange(0, half, 2, dtype=F32) / half)
    ang = jnp.concatenate([row[:, None] * inv, col[:, None] * inv], axis=-1)
    cos2 = jnp.repeat(jnp.cos(ang), 2, axis=-1)
    sin2 = jnp.repeat(jnp.sin(ang), 2, axis=-1)
    cos2 = jnp.concatenate([cos2, jnp.ones((n_ctx, HEAD_DIM), F32)], axis=0)
    sin2 = jnp.concatenate([sin2, jnp.zeros((n_ctx, HEAD_DIM), F32)], axis=0)
    return cos2, sin2


def _to_heads(t, n_heads):
    return t.reshape(t.shape[0], n_heads, HEAD_DIM).transpose(1, 0, 2)


def _from_heads(t):
    return t.transpose(1, 0, 2).reshape(t.shape[1], t.shape[0] * HEAD_DIM)


def _local_loss(shards, p, x, c, ctx, target, consts):
    s, n_ctx = x.shape[0], ctx.shape[0]
    cos2, sin2, rot, na_tables = consts
    is_ctx = (jnp.arange(s + n_ctx) >= s)[:, None]
    seg = lambda rows2: jnp.where(is_ctx, rows2[1:2], rows2[0:1])
    xa = jnp.concatenate([x, ctx], axis=0)
    cond = jnp.concatenate([jax.nn.silu(c), jax.nn.silu(p["c_ctx"])[None, :],
                            jnp.zeros((MOD_ROWS - 2, D_MODEL), F32)], axis=0).astype(BF16)
    layer_shards = lambda l: {n: shards[n][l] for n in BIG}
    w = gather_layer(layer_shards(0))

    for l in range(DEPTH):
        mod = (mm(cond, w["w_mod"]) + p["b_mod"][l])[:2]
        sh1, sc1, g1, sh2, sc2, g2 = jnp.split(mod, 6, axis=-1)

        h = norm_mod(xa, p["norm1"][l][None, :], sh1, sc1, s, BF16)
        na_q, na_k, na_v, gq, gk, gv, ga, gb = jnp.split(mm(h, w["w_in"]), IN_SPLITS, axis=-1)

        qa, ka, va = _to_heads(na_q, NA_HEADS), _to_heads(na_k, NA_HEADS), _to_heads(na_v, NA_HEADS)
        slab = _na_slab(p["na_rpb"][l], na_tables)
        ya_lat = na_attn(qa[:, :s], ka[:, :s], va[:, :s], ka[:, s:], va[:, s:], slab)
        ya_ctx = attn(qa[:, s:], ka[:, s:], va[:, s:])
        ya = _from_heads(jnp.concatenate([ya_lat, ya_ctx], axis=1))

        qk = _to_heads(jnp.concatenate([gq, gk], axis=-1), GQA_Q_HEADS + GQA_KV_HEADS)
        gains = jnp.stack([p["q_gain"][l], p["k_gain"][l]])
        qk = hn_rope(qk, gains, cos2, sin2, rot, GQA_Q_HEADS)
        qb, kb, vb = qk[:GQA_Q_HEADS], qk[GQA_Q_HEADS:], _to_heads(gv, GQA_KV_HEADS)
        q_lat = qb[:, :s].reshape(GQA_KV_HEADS, GQA_REP * s, HEAD_DIM)
        if l + 1 < DEPTH:
            ob_lat, w_next = attn_gather(q_lat, kb, vb, layer_shards(l + 1))
        else:
            ob_lat, w_next = attn(q_lat, kb, vb), None
        ob_ctx = attn(qb[:, s:].reshape(GQA_KV_HEADS, GQA_REP * n_ctx, HEAD_DIM), kb[:, s:], vb[:, s:])
        yb = _from_heads(jnp.concatenate([ob_lat.reshape(GQA_Q_HEADS, s, HEAD_DIM),
                                          ob_ctx.reshape(GQA_Q_HEADS, n_ctx, HEAD_DIM)], axis=1))

        merged = jax.nn.sigmoid(ga) * mm(ya.astype(BF16), w["w_pa"]) + jax.nn.sigmoid(gb) * mm(yb.astype(BF16), w["w_pb"])
        xa = xa + seg(g1) * mm(merged.astype(BF16), w["w_o"])

        h2 = norm_mod(xa, p["norm2"][l][None, :], sh2, sc2, s, BF16)
        xa = xa + seg(g2) * ffn(h2, w["w_ffn_in"], w["w_ffn_out"])
        w = w_next

    zeros2 = jnp.zeros((2, D_MODEL), F32)
    y = norm_mod(xa, p["final_norm"][None, :], zeros2, zeros2, s, F32)[:s]
    return loss_head(y, target)


def kernel(x, c, ctx, c_ctx, w_mod, b_mod, norm1, w_in, na_rpb, q_gain, k_gain, w_pa, w_pb, w_o, norm2, w_ffn_in, w_ffn_out, final_norm, loss_target, m_c_ctx, m_w_mod, m_b_mod, m_norm1, m_w_in, m_na_rpb, m_q_gain, m_k_gain, m_w_pa, m_w_pb, m_w_o, m_norm2, m_w_ffn_in, m_w_ffn_out, m_final_norm, v_c_ctx, v_w_mod, v_b_mod, v_norm1, v_w_in, v_na_rpb, v_q_gain, v_k_gain, v_w_pa, v_w_pb, v_w_o, v_norm2, v_w_ffn_in, v_w_ffn_out, v_final_norm):
    w = dict(c_ctx=c_ctx, w_mod=w_mod, b_mod=b_mod, norm1=norm1, w_in=w_in, na_rpb=na_rpb, q_gain=q_gain, k_gain=k_gain,
             w_pa=w_pa, w_pb=w_pb, w_o=w_o, norm2=norm2, w_ffn_in=w_ffn_in, w_ffn_out=w_ffn_out, final_norm=final_norm)
    mom = dict(c_ctx=m_c_ctx, w_mod=m_w_mod, b_mod=m_b_mod, norm1=m_norm1, w_in=m_w_in, na_rpb=m_na_rpb, q_gain=m_q_gain,
               k_gain=m_k_gain, w_pa=m_w_pa, w_pb=m_w_pb, w_o=m_w_o, norm2=m_norm2, w_ffn_in=m_w_ffn_in,
               w_ffn_out=m_w_ffn_out, final_norm=m_final_norm)
    var = dict(c_ctx=v_c_ctx, w_mod=v_w_mod, b_mod=v_b_mod, norm1=v_norm1, w_in=v_w_in, na_rpb=v_na_rpb, q_gain=v_q_gain,
               k_gain=v_k_gain, w_pa=v_w_pa, w_pb=v_w_pb, w_o=v_w_o, norm2=v_norm2, w_ffn_in=v_w_ffn_in,
               w_ffn_out=v_w_ffn_out, final_norm=v_final_norm)
    s, n_ctx = x.shape[1], ctx.shape[1]
    depth = w_mod.shape[0]

    shards = {n: [w[n][l] for l in range(depth)] for n in BIG}
    small = {n: w[n] for n in SMALL}
    consts = (*_rope_tables(s, n_ctx), _rot_matrix(), _na_tables(s // GRID_W))
    loss, (g_shards, g_small, gx) = jax.value_and_grad(_local_loss, argnums=(0, 1, 2))(
        shards, small, x[0], c, ctx[0], loss_target[0], consts)

    parts_s = _exchange(_pack_small(g_small, loss), True, "gather_small_grads")
    g_packed = _sum8_call(parts_s)
    grads, loss = _unpack_small(g_packed, w)
    zero = jnp.zeros((1,), F32)
    upd_s = _adam_call(g_packed, _pack_small(w, zero), _pack_small(mom, zero), _pack_small(var, zero))
    outs = [grads] + [_unpack_small(u, w)[0] for u in upd_s]

    for n in BIG:
        flat = lambda t: t.reshape(-1, t.shape[-1])
        g = jnp.stack(g_shards[n])
        upd = _adam_call(flat(g), flat(w[n]), flat(mom[n]), flat(var[n]))
        for k, t in enumerate([g] + [u.reshape(w[n].shape) for u in upd]):
            outs[k][n] = t
    return (loss, gx[None], *[o[n] for o in outs for n in WEIGHTS])
```

```python
import functools

import numpy as np
import jax
import jax.numpy as jnp
from jax import lax
from jax.experimental import pallas as pl
from jax.experimental.pallas import tpu as pltpu

F32 = jnp.float32
BF16 = jnp.bfloat16
HIGHEST = lax.Precision.HIGHEST

D_MODEL = 1024
DEPTH = 4
GRID_W = 64
HEAD_DIM = 64
NA_HEADS = 8
NA_WIN_H = 8
NA_WIN_W = 16
GQA_Q_HEADS = 8
GQA_KV_HEADS = 2
GQA_REP = GQA_Q_HEADS // GQA_KV_HEADS
NA_WIDTH = NA_HEADS * HEAD_DIM
GQA_Q_WIDTH = GQA_Q_HEADS * HEAD_DIM
GQA_KV_WIDTH = GQA_KV_HEADS * HEAD_DIM
IN_SIZES = (NA_WIDTH, NA_WIDTH, NA_WIDTH, GQA_Q_WIDTH, GQA_KV_WIDTH, GQA_KV_WIDTH, D_MODEL, D_MODEL)
IN_SPLITS = tuple(int(v) for v in np.cumsum(IN_SIZES)[:-1])
ROPE_THETA = 10000.0
EPS = 1e-6
SCALE = HEAD_DIM ** -0.5
ADAM_LR = 0.001
ADAM_B1 = 0.9
ADAM_B2 = 0.999
ADAM_EPS = 1e-08
ADAM_WD = 0.01
ADAM_STEP = 10

N_DEV = 8
MESH = pl.DeviceIdType.MESH
NEG_BIG = -1e30

VMEM_BIG_LIMIT = 52 * 1024 * 1024
NA_QROWS = 4
NA_QB = NA_QROWS * GRID_W
NA_KROWS = 12
NA_KW = NA_KROWS * GRID_W
NA_FWD_HEADS = 4
MM_TILE_ELEMS = 512 * 2176
MM_OPERAND_ELEMS = 1408 * 2176
MOD_ROWS = 256

BIG = ("w_mod", "w_in", "w_pa", "w_pb", "w_o", "w_ffn_in", "w_ffn_out")
ROW_SHARDED = ("w_o", "w_ffn_out")
SMALL = ("c_ctx", "b_mod", "norm1", "na_rpb", "q_gain", "k_gain", "norm2", "final_norm")
WEIGHTS = ("c_ctx", "w_mod", "b_mod", "norm1", "w_in", "na_rpb", "q_gain", "k_gain", "w_pa", "w_pb", "w_o",
           "norm2", "w_ffn_in", "w_ffn_out", "final_norm")


def _pick(n, cands):
    for c in cands:
        if n % c == 0:
            return c
    return n


def _dot_nt(a, b):
    return lax.dot_general(a, b, (((1,), (1,)), ((), ())), preferred_element_type=F32)


def _dot_tn(a, b):
    return lax.dot_general(a, b, (((0,), (0,)), ((), ())), preferred_element_type=F32)


def _dot(a, b):
    return jnp.dot(a, b, preferred_element_type=F32)


def _mm_call(a, b, name, mode="nn", out_dtype=F32, b_lead=None):
    (m, k) = a.shape if mode != "tn" else a.shape[::-1]
    n = b.shape[-1] if mode != "nt" else b.shape[-2]
    tn = _pick(n, (2176, 1408, 1024, 512, 256, 128))
    tm = _pick(m, tuple(t for t in (768, 512, 256, 128) if t * tn <= MM_TILE_ELEMS))
    tk = _pick(k, tuple(t for t in (2816, 2176, 1408, 1024, 768, 512, 256, 128) if t * max(tm, tn) <= MM_OPERAND_ELEMS))
    nk = k // tk
    dot = {"nn": _dot, "tn": _dot_tn, "nt": _dot_nt}[mode]
    a_blk = (tk, tm) if mode == "tn" else (tm, tk)
    b_blk = (tn, tk) if mode == "nt" else (tk, tn)
    a_idx = (lambda i, kk: (kk, i)) if mode == "tn" else (lambda i, kk: (i, kk))
    b_idx = (lambda j, kk: (j, kk)) if mode == "nt" else (lambda j, kk: (kk, j))
    if b_lead is not None:
        b_blk, b_idx2 = (None,) + b_blk, b_idx
        b_idx = lambda j, kk: (b_lead,) + b_idx2(j, kk)

    def body(a_ref, b_ref, o_ref, acc_ref):
        kk = pl.program_id(2)
        part = dot(a_ref[...], b_ref[...])

        @pl.when(kk == 0)
        def _():
            acc_ref[...] = part

        @pl.when(kk > 0)
        def _():
            acc_ref[...] += part

        @pl.when(kk == nk - 1)
        def _():
            o_ref[...] = acc_ref[...].astype(o_ref.dtype)

    def body1(a_ref, b_ref, o_ref):
        o_ref[...] = dot(a_ref[...], b_ref[...]).astype(o_ref.dtype)

    footprint = 2 * (tm * tk * 2 + tk * tn * 2 + tm * tn * 4) + 2 * tm * tn * 4
    limit = int(footprint + (8 << 20))
    if nk == 1:
        return pl.pallas_call(
            body1, grid=(n // tn, m // tm),
            in_specs=[pl.BlockSpec(a_blk, lambda j, i: a_idx(i, 0)), pl.BlockSpec(b_blk, lambda j, i: b_idx(j, 0))],
            out_specs=pl.BlockSpec((tm, tn), lambda j, i: (i, j)),
            out_shape=jax.ShapeDtypeStruct((m, n), out_dtype),
            compiler_params=pltpu.CompilerParams(dimension_semantics=("parallel", "parallel"), vmem_limit_bytes=limit),
            name=name,
        )(a, b)
    return pl.pallas_call(
        body, grid=(m // tm, n // tn, nk),
        in_specs=[pl.BlockSpec(a_blk, lambda i, j, kk: a_idx(i, kk)), pl.BlockSpec(b_blk, lambda i, j, kk: b_idx(j, kk))],
        out_specs=pl.BlockSpec((tm, tn), lambda i, j, kk: (i, j)),
        out_shape=jax.ShapeDtypeStruct((m, n), out_dtype),
        scratch_shapes=[pltpu.VMEM((tm, tn), F32)],
        compiler_params=pltpu.CompilerParams(
            dimension_semantics=("parallel", "parallel", "arbitrary"), vmem_limit_bytes=limit),
        name=name,
    )(a, b)


@jax.custom_vjp
def mm(x, w):
    return _mm_call(x, w, "mm_fwd")


def _mm_fwd(x, w):
    return _mm_call(x, w, "mm_fwd"), (x, w)


def _mm_bwd(res, dy):
    x, w = res
    dyb = dy.astype(BF16)
    return _mm_call(dyb, w, "mm_dx", "nt", BF16), _mm_call(x, dyb, "mm_dw", "tn", BF16)


mm.defvjp(_mm_fwd, _mm_bwd)


def _ffn_tiles(t, f):
    return _pick(t, (384, 256, 128)), _pick(f, (1408, 1024, 512, 256, 128))


def _ffn_in_call(h, w_in):
    t, d = h.shape
    f = w_in.shape[1] // 2
    tm, tn = _ffn_tiles(t, f)
    nf = f // tn

    def body(h_ref, wa_ref, wu_ref, act_ref, au_ref):
        hv = h_ref[...]
        a, u = _dot(hv, wa_ref[...]), _dot(hv, wu_ref[...])
        au_ref[0] = a
        au_ref[1] = u
        act_ref[...] = (a * jax.nn.sigmoid(a) * u).astype(act_ref.dtype)

    return pl.pallas_call(
        body, grid=(nf, t // tm),
        in_specs=[pl.BlockSpec((tm, d), lambda j, i: (i, 0)), pl.BlockSpec((d, tn), lambda j, i: (0, j)),
                  pl.BlockSpec((d, tn), lambda j, i: (0, j + nf))],
        out_specs=[pl.BlockSpec((tm, tn), lambda j, i: (i, j)), pl.BlockSpec((2, tm, tn), lambda j, i: (0, i, j))],
        out_shape=[jax.ShapeDtypeStruct((t, f), BF16), jax.ShapeDtypeStruct((2, t, f), F32)],
        compiler_params=pltpu.CompilerParams(dimension_semantics=("parallel", "parallel"), vmem_limit_bytes=VMEM_BIG_LIMIT),
        name="ffn_in",
    )(h, w_in, w_in)


def _ffn_dact_call(dy, w_out, au):
    t, d = dy.shape
    f = w_out.shape[0]
    tm, tn = _ffn_tiles(t, f)

    def body(dy_ref, w_ref, au_ref, dau_ref):
        dact = _dot_nt(dy_ref[...], w_ref[...])
        a, u = au_ref[0], au_ref[1]
        sg = jax.nn.sigmoid(a)
        dau_ref[0] = (dact * u * (sg * (1.0 + a * (1.0 - sg)))).astype(dau_ref.dtype)
        dau_ref[1] = (dact * (a * sg)).astype(dau_ref.dtype)

    halves = pl.BlockSpec((2, tm, tn), lambda j, i: (0, i, j))
    return pl.pallas_call(
        body, grid=(f // tn, t // tm),
        in_specs=[pl.BlockSpec((tm, d), lambda j, i: (i, 0)), pl.BlockSpec((tn, d), lambda j, i: (j, 0)), halves],
        out_specs=halves, out_shape=jax.ShapeDtypeStruct((2, t, f), BF16),
        compiler_params=pltpu.CompilerParams(dimension_semantics=("parallel", "parallel"), vmem_limit_bytes=VMEM_BIG_LIMIT),
        name="ffn_dact",
    )(dy, w_out, au)


def _ffn_dh_call(dau, w_in):
    _, t, f = dau.shape
    d = w_in.shape[0]
    tm, tk = _ffn_tiles(t, f)
    nf = f // tk

    def body(a_ref, b_ref, o_ref):
        acc = None
        for c in range(2 * nf):
            part = _dot_nt(a_ref[c // nf, :, (c % nf) * tk:(c % nf + 1) * tk], b_ref[:, c * tk:(c + 1) * tk])
            acc = part if acc is None else acc + part
        o_ref[...] = acc.astype(o_ref.dtype)

    return pl.pallas_call(
        body, grid=(t // tm,),
        in_specs=[pl.BlockSpec((2, tm, f), lambda i: (0, i, 0)), pl.BlockSpec((d, 2 * f), lambda i: (0, 0))],
        out_specs=pl.BlockSpec((tm, d), lambda i: (i, 0)), out_shape=jax.ShapeDtypeStruct((t, d), BF16),
        compiler_params=pltpu.CompilerParams(dimension_semantics=("parallel",), vmem_limit_bytes=VMEM_BIG_LIMIT),
        name="ffn_dh",
    )(dau, w_in)


@jax.custom_vjp
def ffn(h, w_in, w_out):
    return _ffn_fwd(h, w_in, w_out)[0]


def _ffn_fwd(h, w_in, w_out):
    act, au = _ffn_in_call(h, w_in)
    return _mm_call(act, w_out, "mm_fwd"), (h, w_in, w_out, act, au)


def _ffn_bwd(res, dy):
    h, w_in, w_out, act, au = res
    dyb = dy.astype(BF16)
    dau = _ffn_dact_call(dyb, w_out, au)
    dw_in = jnp.concatenate([_mm_call(h, dau, "mm_dw", "tn", BF16, b_lead=c) for c in range(2)], axis=1)
    return _ffn_dh_call(dau, w_in), dw_in, _mm_call(act, dyb, "mm_dw", "tn", BF16)


ffn.defvjp(_ffn_fwd, _ffn_bwd)


def _seg_rows(ref, is_ctx):
    return jnp.where(is_ctx, ref[1:2, :], ref[0:1, :])


def _one_hot_row(second):
    return (lax.broadcasted_iota(jnp.int32, (2, 1), 0) == second.astype(jnp.int32)).astype(F32)


def _norm_mod_fwd_call(x, g, shift, scale, n_lat, out_dtype):
    t, d = x.shape
    tt = _pick(t, (256, 128))
    lat_tiles = n_lat // tt

    def body(x_ref, g_ref, sh_ref, sc_ref, y_ref):
        is_ctx = pl.program_id(0) >= lat_tiles
        xv = x_ref[...]
        r = lax.rsqrt(jnp.mean(xv * xv, axis=-1, keepdims=True) + EPS)
        yn = xv * r * g_ref[...]
        y_ref[...] = (yn * (1.0 + _seg_rows(sc_ref, is_ctx)) + _seg_rows(sh_ref, is_ctx)).astype(y_ref.dtype)

    row = pl.BlockSpec((tt, d), lambda i: (i, 0))
    full = lambda rws: pl.BlockSpec((rws, d), lambda i: (0, 0))
    return pl.pallas_call(
        body, grid=(t // tt,), in_specs=[row, full(1), full(2), full(2)], out_specs=row,
        out_shape=jax.ShapeDtypeStruct((t, d), out_dtype),
        compiler_params=pltpu.CompilerParams(dimension_semantics=("parallel",)), name="norm_mod_fwd",
    )(x, g, shift, scale)


def _norm_mod_bwd_call(x, g, scale, dy, n_lat):
    t, d = x.shape
    tt = _pick(t, (256, 128))
    lat_tiles = n_lat // tt

    def body(x_ref, g_ref, sc_ref, dy_ref, dx_ref, dg_ref, dsh_ref, dsc_ref):
        i = pl.program_id(0)
        is_ctx = i >= lat_tiles

        @pl.when(i == 0)
        def _():
            dg_ref[...] = jnp.zeros_like(dg_ref)
            dsh_ref[...] = jnp.zeros_like(dsh_ref)
            dsc_ref[...] = jnp.zeros_like(dsc_ref)

        xv, dyv, gv = x_ref[...], dy_ref[...].astype(F32), g_ref[...]
        r = lax.rsqrt(jnp.mean(xv * xv, axis=-1, keepdims=True) + EPS)
        nrm = xv * r
        yn = nrm * gv
        seg = _one_hot_row(is_ctx)
        dsh_ref[...] += seg * jnp.sum(dyv, axis=0, keepdims=True)
        dsc_ref[...] += seg * jnp.sum(dyv * yn, axis=0, keepdims=True)
        dyn = dyv * (1.0 + _seg_rows(sc_ref, is_ctx))
        dg_ref[...] += jnp.sum(dyn * nrm, axis=0, keepdims=True)
        dn = dyn * gv
        dx_ref[...] = r * (dn - nrm * jnp.mean(dn * nrm, axis=-1, keepdims=True))

    row = pl.BlockSpec((tt, d), lambda i: (i, 0))
    full = lambda rws: pl.BlockSpec((rws, d), lambda i: (0, 0))
    return pl.pallas_call(
        body, grid=(t // tt,), in_specs=[row, full(1), full(2), row], out_specs=[row, full(1), full(2), full(2)],
        out_shape=[jax.ShapeDtypeStruct((t, d), F32), jax.ShapeDtypeStruct((1, d), F32),
                   jax.ShapeDtypeStruct((2, d), F32), jax.ShapeDtypeStruct((2, d), F32)],
        compiler_params=pltpu.CompilerParams(dimension_semantics=("arbitrary",)), name="norm_mod_bwd",
    )(x, g, scale, dy)


@functools.partial(jax.custom_vjp, nondiff_argnums=(4, 5))
def norm_mod(x, g, shift, scale, n_lat, out_dtype):
    return _norm_mod_fwd_call(x, g, shift, scale, n_lat, out_dtype)


def _norm_mod_fwd(x, g, shift, scale, n_lat, out_dtype):
    return _norm_mod_fwd_call(x, g, shift, scale, n_lat, out_dtype), (x, g, scale)


def _norm_mod_bwd(n_lat, out_dtype, res, dy):
    x, g, scale = res
    dx, dg, dsh, dsc = _norm_mod_bwd_call(x, g, scale, dy, n_lat)
    return dx, dg, dsh, dsc


norm_mod.defvjp(_norm_mod_fwd, _norm_mod_bwd)


def _rot_matrix():
    p = np.zeros((HEAD_DIM, HEAD_DIM), np.float32)
    for i in range(HEAD_DIM // 2):
        p[2 * i + 1, 2 * i] = -1.0
        p[2 * i, 2 * i + 1] = 1.0
    return jnp.asarray(p)


def _hn_rope_fwd_call(x, gains, cos2, sin2, rot, n_q):
    nh, t, dh = x.shape
    tt = _pick(t, (2816, 1024, 768, 512, 256, 128))

    def body(x_ref, g_ref, cos_ref, sin_ref, rot_ref, y_ref):
        gv = jnp.where(pl.program_id(0) >= n_q, g_ref[1:2, :], g_ref[0:1, :])
        xv = x_ref[...]
        r = lax.rsqrt(jnp.mean(xv * xv, axis=-1, keepdims=True) + EPS)
        y = xv * r * gv
        yr = jnp.dot(y, rot_ref[...], precision=HIGHEST, preferred_element_type=F32)
        y_ref[...] = y * cos_ref[...] + yr * sin_ref[...]

    blk = pl.BlockSpec((None, tt, dh), lambda h, i: (h, i, 0))
    tab = pl.BlockSpec((tt, dh), lambda h, i: (i, 0))
    return pl.pallas_call(
        body, grid=(nh, t // tt),
        in_specs=[blk, pl.BlockSpec((2, dh), lambda h, i: (0, 0)), tab, tab, pl.BlockSpec((dh, dh), lambda h, i: (0, 0))],
        out_specs=blk, out_shape=jax.ShapeDtypeStruct((nh, t, dh), F32),
        compiler_params=pltpu.CompilerParams(dimension_semantics=("parallel", "parallel")), name="hn_rope_fwd",
    )(x, gains, cos2, sin2, rot)


def _hn_rope_bwd_call(x, gains, cos2, sin2, rot, dy, n_q):
    nh, t, dh = x.shape
    tt = _pick(t, (2816, 1024, 768, 512, 256, 128))

    def body(x_ref, g_ref, cos_ref, sin_ref, rot_ref, dy_ref, dx_ref, dg_ref):
        h, i = pl.program_id(0), pl.program_id(1)
        is_k = h >= n_q

        @pl.when((h == 0) & (i == 0))
        def _():
            dg_ref[...] = jnp.zeros_like(dg_ref)

        gv = jnp.where(is_k, g_ref[1:2, :], g_ref[0:1, :])
        xv, dyv = x_ref[...], dy_ref[...]
        r = lax.rsqrt(jnp.mean(xv * xv, axis=-1, keepdims=True) + EPS)
        nrm = xv * r
        dyn = dyv * cos_ref[...] - jnp.dot(dyv * sin_ref[...], rot_ref[...], precision=HIGHEST, preferred_element_type=F32)
        seg = _one_hot_row(is_k)
        dg_ref[...] += seg * jnp.sum(dyn * nrm, axis=0, keepdims=True)
        dn = dyn * gv
        dx_ref[...] = r * (dn - nrm * jnp.mean(dn * nrm, axis=-1, keepdims=True))

    blk = pl.BlockSpec((None, tt, dh), lambda h, i: (h, i, 0))
    tab = pl.BlockSpec((tt, dh), lambda h, i: (i, 0))
    g_spec = pl.BlockSpec((2, dh), lambda h, i: (0, 0))
    return pl.pallas_call(
        body, grid=(nh, t // tt),
        in_specs=[blk, g_spec, tab, tab, pl.BlockSpec((dh, dh), lambda h, i: (0, 0)), blk],
        out_specs=[blk, g_spec],
        out_shape=[jax.ShapeDtypeStruct((nh, t, dh), F32), jax.ShapeDtypeStruct((2, dh), F32)],
        compiler_params=pltpu.CompilerParams(dimension_semantics=("arbitrary", "arbitrary")), name="hn_rope_bwd",
    )(x, gains, cos2, sin2, rot, dy)


@functools.partial(jax.custom_vjp, nondiff_argnums=(5,))
def hn_rope(x, gains, cos2, sin2, rot, n_q):
    return _hn_rope_fwd_call(x, gains, cos2, sin2, rot, n_q)


def _hn_rope_fwd(x, gains, cos2, sin2, rot, n_q):
    return _hn_rope_fwd_call(x, gains, cos2, sin2, rot, n_q), (x, gains, cos2, sin2, rot)


def _hn_rope_bwd(n_q, res, dy):
    x, gains, cos2, sin2, rot = res
    dx, dg = _hn_rope_bwd_call(x, gains, cos2, sin2, rot, dy, n_q)
    return dx, dg, jnp.zeros_like(cos2), jnp.zeros_like(sin2), jnp.zeros_like(rot)


hn_rope.defvjp(_hn_rope_fwd, _hn_rope_bwd)


EXCHANGE_SCRATCH = [pltpu.SemaphoreType.DMA((N_DEV - 1,)), pltpu.SemaphoreType.DMA((N_DEV - 1,)), pltpu.SemaphoreType.DMA(())]


def _exchange_copies(x_ref, out_ref, send_sems, recv_sems, local_sem, all_gather):
    mx, my, mc = lax.axis_index("x"), lax.axis_index("y"), lax.axis_index("c")
    me = 4 * mx + 2 * my + mc
    src = (lambda p: x_ref) if all_gather else (lambda p: x_ref.at[p])
    local = pltpu.make_async_copy(src(me), out_ref.at[me], local_sem)
    remote = []
    for rel in range(1, N_DEV):
        px, py, pc = mx ^ (rel >> 2), my ^ ((rel >> 1) & 1), mc ^ (rel & 1)
        remote.append(pltpu.make_async_remote_copy(
            src_ref=src(4 * px + 2 * py + pc), dst_ref=out_ref.at[me],
            send_sem=send_sems.at[rel - 1], recv_sem=recv_sems.at[rel - 1],
            device_id=(px, py, pc), device_id_type=MESH))
    return local, remote


def _exchange_start(*refs, all_gather):
    local, remote = _exchange_copies(*refs, all_gather)
    local.start()
    for cp in remote:
        cp.start()


def _exchange_wait(*refs, all_gather):
    local, remote = _exchange_copies(*refs, all_gather)
    for cp in remote:
        cp.wait_send()
    for cp in remote:
        cp.wait_recv()
    local.wait()


def _exchange_shape(x, all_gather):
    return jax.ShapeDtypeStruct((N_DEV,) + tuple(x.shape if all_gather else x.shape[1:]), x.dtype)


def _exchange(x, all_gather, name):
    def body(x_ref, out_ref, *sems):
        _exchange_start(x_ref, out_ref, *sems, all_gather=all_gather)
        _exchange_wait(x_ref, out_ref, *sems, all_gather=all_gather)

    return pl.pallas_call(
        body, in_specs=[pl.BlockSpec(memory_space=pl.ANY)], out_specs=pl.BlockSpec(memory_space=pl.ANY),
        out_shape=_exchange_shape(x, all_gather), scratch_shapes=EXCHANGE_SCRATCH,
        compiler_params=pltpu.CompilerParams(has_side_effects=True), name=name,
    )(x)


ATT_TK = 256


def _att_tq(rn, fwd=False):
    return _pick(rn, ((1024,) if fwd else ()) + (512, 256, 128))


def _att_unroll(nkb):
    return _pick(nkb, (3, 2))


def _flash_fwd_call(q_t, k, v_t, xsend=None):
    g, dh, rn = q_t.shape
    nk = k.shape[1]
    tq, tk = _att_tq(rn, True), ATT_TK
    nkb = nk // tk
    unroll = _att_unroll(nkb)
    n_i = rn // tq

    def body(*refs):
        if xsend is None:
            qt_ref, k_ref, vt_ref, ot_ref, lse_ref = refs
        else:
            qt_ref, k_ref, vt_ref, x_ref, ot_ref, lse_ref, xout_ref, *sems = refs
            a, i = pl.program_id(0), pl.program_id(1)

            @pl.when((a == 0) & (i == 0))
            def _():
                _exchange_start(x_ref, xout_ref, *sems, all_gather=True)

        qst = qt_ref[...] * SCALE

        def trip(t, carry):
            m, l, acc = carry
            blocks = [t * unroll + u for u in range(unroll)]
            s_t = [_dot(k_ref[pl.ds(pl.multiple_of(j * tk, tk), tk), :], qst) for j in blocks]
            m_new = functools.reduce(jnp.maximum, [jnp.max(s, axis=0, keepdims=True) for s in s_t], m)
            p_t = [jnp.exp(s - m_new) for s in s_t]
            alpha = jnp.exp(m - m_new)
            l = alpha * l + sum(jnp.sum(p, axis=0, keepdims=True) for p in p_t)
            acc = alpha * acc + sum(_dot(vt_ref[j], p.astype(BF16)) for j, p in zip(blocks, p_t))
            return m_new, l, acc

        m, l, acc = lax.fori_loop(
            0, nkb // unroll, trip, (jnp.full((1, tq), NEG_BIG, F32), jnp.zeros((1, tq), F32), jnp.zeros((dh, tq), F32)))
        ot_ref[...] = acc / l
        lse_ref[...] = m + jnp.log(l)

        if xsend is not None:
            @pl.when((a == g - 1) & (i == n_i - 1))
            def _():
                _exchange_wait(x_ref, xout_ref, *sems, all_gather=True)

    col = pl.BlockSpec((None, dh, tq), lambda a, i: (a, 0, i))
    vec = pl.BlockSpec((None, 1, tq), lambda a, i: (a, 0, i))
    hbm = pl.BlockSpec(memory_space=pl.ANY)
    in_specs = [col, pl.BlockSpec((None, nk, dh), lambda a, i: (a, 0, 0)),
                pl.BlockSpec((None, nkb, dh, tk), lambda a, i: (a, 0, 0, 0))]
    out_specs = [col, vec]
    out_shape = [jax.ShapeDtypeStruct((g, dh, rn), F32), jax.ShapeDtypeStruct((g, 1, rn), F32)]
    args = (q_t, k, v_t)
    if xsend is not None:
        in_specs, out_specs, args = in_specs + [hbm], out_specs + [hbm], args + (xsend,)
        out_shape = out_shape + [_exchange_shape(xsend, True)]
    return pl.pallas_call(
        body, grid=(g, n_i), in_specs=in_specs, out_specs=out_specs, out_shape=out_shape,
        scratch_shapes=[] if xsend is None else EXCHANGE_SCRATCH,
        compiler_params=pltpu.CompilerParams(dimension_semantics=("arbitrary", "arbitrary"), vmem_limit_bytes=VMEM_BIG_LIMIT),
        name="flash_fwd" if xsend is None else "flash_fwd_gather",
    )(*args)


def _flash_bwd_call(q, q_t, k, k_t, v, do, do_t, o_t, lse_t, xsend=None):
    g, rn, dh = q.shape
    nk = k.shape[1]
    tq, tk = _att_tq(rn), ATT_TK
    nkb = nk // tk
    unroll = _att_unroll(nkb)
    n_i = rn // tq

    def body(*refs):
        q_ref, qt_ref, k_ref, kt_ref, v_ref, do_ref, dot_ref, ot_ref, lse_ref = refs[:9]
        if xsend is None:
            dqt_ref, dk_ref, dv_ref = refs[9:]
        else:
            x_ref, dqt_ref, dk_ref, dv_ref, xout_ref, *sems = refs[9:]
        a, i = pl.program_id(0), pl.program_id(1)

        if xsend is not None:
            @pl.when((a == 0) & (i == 0))
            def _():
                _exchange_start(x_ref, xout_ref, *sems, all_gather=False)

        @pl.when(i == 0)
        def _():
            dk_ref[...] = jnp.zeros_like(dk_ref)
            dv_ref[...] = jnp.zeros_like(dv_ref)

        qs = q_ref[...] * SCALE
        qst = qt_ref[...] * SCALE
        dov, dotv = do_ref[...], dot_ref[...]
        delta = jnp.sum(dotv.astype(F32) * ot_ref[...], axis=0, keepdims=True)
        lse = lse_ref[...]

        def step(j, dqt):
            off = pl.multiple_of(j * tk, tk)
            kj = k_ref[pl.ds(off, tk), :]
            vj = v_ref[pl.ds(off, tk), :]
            p_t = jnp.exp(_dot(kj, qst) - lse)
            ds_t = (p_t * (_dot(vj, dotv) - delta)).astype(BF16)
            dv_ref[pl.ds(off, tk), :] += _dot(p_t.astype(BF16), dov)
            dk_ref[pl.ds(off, tk), :] += _dot(ds_t, qs)
            return dqt + _dot(kt_ref[j], ds_t)

        def trip(t, dqt):
            for u in range(unroll):
                dqt = step(t * unroll + u, dqt)
            return dqt

        dqt_ref[...] = lax.fori_loop(0, nkb // unroll, trip, jnp.zeros((dh, tq), F32)) * SCALE

        if xsend is not None:
            @pl.when((a == g - 1) & (i == n_i - 1))
            def _():
                _exchange_wait(x_ref, xout_ref, *sems, all_gather=False)

    row = pl.BlockSpec((None, tq, dh), lambda a, i: (a, i, 0))
    col = pl.BlockSpec((None, dh, tq), lambda a, i: (a, 0, i))
    kv = pl.BlockSpec((None, nk, dh), lambda a, i: (a, 0, 0))
    hbm = pl.BlockSpec(memory_space=pl.ANY)
    in_specs = [row, col, kv, pl.BlockSpec((None, nkb, dh, tk), lambda a, i: (a, 0, 0, 0)), kv, row, col, col,
                pl.BlockSpec((None, 1, tq), lambda a, i: (a, 0, i))]
    out_specs = [col, kv, kv]
    out_shape = [jax.ShapeDtypeStruct((g, dh, rn), F32), jax.ShapeDtypeStruct((g, nk, dh), F32),
                 jax.ShapeDtypeStruct((g, nk, dh), F32)]
    args = (q, q_t, k, k_t, v, do, do_t, o_t, lse_t)
    if xsend is not None:
        in_specs, out_specs, args = in_specs + [hbm], out_specs + [hbm], args + (xsend,)
        out_shape = out_shape + [_exchange_shape(xsend, False)]
    return pl.pallas_call(
        body, grid=(g, n_i), in_specs=in_specs, out_specs=out_specs, out_shape=out_shape,
        scratch_shapes=[] if xsend is None else EXCHANGE_SCRATCH,
        compiler_params=pltpu.CompilerParams(dimension_semantics=("arbitrary", "arbitrary"), vmem_limit_bytes=VMEM_BIG_LIMIT),
        name="flash_bwd" if xsend is None else "flash_bwd_scatter",
    )(*args)


def _key_blocks_t(t):
    g, nk, dh = t.shape
    return t.reshape(g, nk // ATT_TK, ATT_TK, dh).transpose(0, 1, 3, 2)


@jax.custom_vjp
def attn(q, k, v):
    return _attn_fwd(q, k, v)[0]


def _attn_fwd(q, k, v):
    qb, kb, vb = q.astype(BF16), k.astype(BF16), v.astype(BF16)
    q_t = qb.transpose(0, 2, 1)
    o_t, lse_t = _flash_fwd_call(q_t, kb, _key_blocks_t(vb))
    return o_t.transpose(0, 2, 1), (qb, q_t, kb, vb, o_t, lse_t)


def _attn_bwd(res, do):
    qb, q_t, kb, vb, o_t, lse_t = res
    dob = do.astype(BF16)
    dq_t, dk, dv = _flash_bwd_call(qb, q_t, kb, _key_blocks_t(kb), vb, dob, dob.transpose(0, 2, 1), o_t, lse_t)
    return dq_t.transpose(0, 2, 1), dk, dv


attn.defvjp(_attn_fwd, _attn_bwd)


def _shard_rows(shards):
    return jnp.concatenate([shards[n].reshape(-1, D_MODEL) for n in BIG], axis=0).astype(BF16)


def _fulls_of(gathered, shards):
    out, off = {}, 0
    for n in BIG:
        k, nn = shards[n].shape
        r = k * nn // D_MODEL
        t = gathered[:, off:off + r].reshape(N_DEV, k, nn)
        out[n] = t.reshape(N_DEV * k, nn) if n in ROW_SHARDED else t.transpose(1, 0, 2).reshape(k, N_DEV * nn)
        off += r
    return out


def _slabs_of(dfulls):
    parts = []
    for n in BIG:
        k, nn = dfulls[n].shape
        t = (dfulls[n].reshape(N_DEV, k // N_DEV, nn) if n in ROW_SHARDED
             else dfulls[n].reshape(k, N_DEV, nn // N_DEV).transpose(1, 0, 2))
        parts.append(t.reshape(N_DEV, -1, D_MODEL))
    return jnp.concatenate(parts, axis=1)


def _dshards_of(parts, dfulls):
    rows, out, off = _sum8_call(parts), {}, 0
    for n in BIG:
        k, nn = dfulls[n].shape
        shape = (k // N_DEV, nn) if n in ROW_SHARDED else (k, nn // N_DEV)
        r = shape[0] * shape[1] // D_MODEL
        out[n] = rows[off:off + r].reshape(shape)
        off += r
    return out


@jax.custom_vjp
def gather_layer(shards):
    return _fulls_of(_exchange(_shard_rows(shards), True, "gather_weights"), shards)


def _gather_layer_fwd(shards):
    return gather_layer(shards), None


def _gather_layer_bwd(_, dfulls):
    return (_dshards_of(_exchange(_slabs_of(dfulls), False, "scatter_grads"), dfulls),)


gather_layer.defvjp(_gather_layer_fwd, _gather_layer_bwd)


@jax.custom_vjp
def attn_gather(q, k, v, shards):
    return _attn_gather_fwd(q, k, v, shards)[0]


def _attn_gather_fwd(q, k, v, shards):
    qb, kb, vb = q.astype(BF16), k.astype(BF16), v.astype(BF16)
    q_t = qb.transpose(0, 2, 1)
    o_t, lse_t, gathered = _flash_fwd_call(q_t, kb, _key_blocks_t(vb), _shard_rows(shards))
    return (o_t.transpose(0, 2, 1), _fulls_of(gathered, shards)), (qb, q_t, kb, vb, o_t, lse_t)


def _attn_gather_bwd(res, cts):
    qb, q_t, kb, vb, o_t, lse_t = res
    do, dfulls = cts
    dob = do.astype(BF16)
    dq_t, dk, dv, parts = _flash_bwd_call(qb, q_t, kb, _key_blocks_t(kb), vb, dob, dob.transpose(0, 2, 1), o_t, lse_t,
                                          _slabs_of(dfulls))
    return dq_t.transpose(0, 2, 1), dk, dv, _dshards_of(parts, dfulls)


attn_gather.defvjp(_attn_gather_fwd, _attn_gather_bwd)


def _na_key_row(i, rows):
    return jnp.clip(NA_QROWS * i - NA_WIN_H // 2, 0, rows - NA_KROWS)


def _na_type(i, nb):
    return jnp.where(i == 0, 0, jnp.where(i == nb - 1, 2, 1))


def _na_fwd_call(q, k, v, kc, vc, slab):
    h, s, dh = q.shape
    n_ctx = kc.shape[1]
    rows, nb = s // GRID_W, s // NA_QB
    hb = NA_FWD_HEADS

    def body(q_ref, k_ref, v_ref, kc_ref, vc_ref, slab_ref, o_ref, lse_ref):
        off = pl.multiple_of(_na_key_row(pl.program_id(1), rows) * GRID_W, NA_QB)
        for hh in range(hb):
            qs = q_ref[hh] * SCALE
            kw = k_ref[hh, pl.ds(off, NA_KW), :]
            vw = v_ref[hh, pl.ds(off, NA_KW), :]
            sw = _dot_nt(qs, kw) + slab_ref[hh]
            sc = _dot_nt(qs, kc_ref[hh])
            m = jnp.maximum(jnp.max(sw, axis=1, keepdims=True), jnp.max(sc, axis=1, keepdims=True))
            pw = jnp.exp(sw - m)
            pc = jnp.exp(sc - m)
            l = jnp.sum(pw, axis=1, keepdims=True) + jnp.sum(pc, axis=1, keepdims=True)
            o_ref[hh] = (_dot(pw.astype(BF16), vw) + _dot(pc.astype(BF16), vc_ref[hh])) / l
            lse_ref[hh] = m + jnp.log(l)

    qblk = pl.BlockSpec((hb, NA_QB, dh), lambda a, i: (a, i, 0))
    kv = pl.BlockSpec((hb, s, dh), lambda a, i: (a, 0, 0))
    ckv = pl.BlockSpec((hb, n_ctx, dh), lambda a, i: (a, 0, 0))
    return pl.pallas_call(
        body, grid=(h // hb, nb),
        in_specs=[qblk, kv, kv, ckv, ckv, pl.BlockSpec((None, hb, NA_QB, NA_KW), lambda a, i: (_na_type(i, nb), a, 0, 0))],
        out_specs=[qblk, pl.BlockSpec((hb, NA_QB, 1), lambda a, i: (a, i, 0))],
        out_shape=[jax.ShapeDtypeStruct((h, s, dh), F32), jax.ShapeDtypeStruct((h, s, 1), F32)],
        compiler_params=pltpu.CompilerParams(dimension_semantics=("parallel", "parallel"), vmem_limit_bytes=VMEM_BIG_LIMIT),
        name="na_fwd",
    )(q, k, v, kc, vc, slab)


def _na_bwd_call(q, k, v, kc, vc, slab, do, o, lse):
    h, s, dh = q.shape
    n_ctx = kc.shape[1]
    rows, nb = s // GRID_W, s // NA_QB
    half = nb // 2

    def body(q_ref, k_ref, v_ref, kc_ref, vc_ref, slab0_ref, slab1_ref, do_ref, o_ref, lse_ref,
             dq_ref, dk_ref, dv_ref, dkc_ref, dvc_ref, dslab0_ref, dslab1_ref):
        j = pl.program_id(1)

        @pl.when(j == 0)
        def _():
            dk_ref[...] = jnp.zeros_like(dk_ref)
            dv_ref[...] = jnp.zeros_like(dv_ref)
            dkc_ref[...] = jnp.zeros_like(dkc_ref)
            dvc_ref[...] = jnp.zeros_like(dvc_ref)

        @pl.when(j <= 1)
        def _():
            dslab0_ref[...] = jnp.zeros_like(dslab0_ref)

        @pl.when((j == 0) | (j == half - 1))
        def _():
            dslab1_ref[...] = jnp.zeros_like(dslab1_ref)

        kcv, vcv = kc_ref[...], vc_ref[...]
        for sub, (slab_ref, dslab_ref) in enumerate(((slab0_ref, dslab0_ref), (slab1_ref, dslab1_ref))):
            off = pl.multiple_of(_na_key_row(2 * j + sub, rows) * GRID_W, NA_QB)
            blk = slice(sub * NA_QB, (sub + 1) * NA_QB)
            qs = q_ref[blk, :] * SCALE
            kw = k_ref[pl.ds(off, NA_KW), :]
            vw = v_ref[pl.ds(off, NA_KW), :]
            dov = do_ref[blk, :]
            lse = lse_ref[blk, :]
            delta = jnp.sum(dov.astype(F32) * o_ref[blk, :], axis=1, keepdims=True)
            pw = jnp.exp(_dot_nt(qs, kw) + slab_ref[...] - lse)
            pc = jnp.exp(_dot_nt(qs, kcv) - lse)
            dsw = pw * (_dot_nt(dov, vw) - delta)
            dsc = pc * (_dot_nt(dov, vcv) - delta)
            dslab_ref[...] += dsw
            dsw, dsc = dsw.astype(BF16), dsc.astype(BF16)
            dq_ref[blk, :] = (_dot(dsw, kw) + _dot(dsc, kcv)) * SCALE
            dk_ref[pl.ds(off, NA_KW), :] += _dot_tn(dsw, qs)
            dv_ref[pl.ds(off, NA_KW), :] += _dot_tn(pw.astype(BF16), dov)
            dkc_ref[...] += _dot_tn(dsc, qs)
            dvc_ref[...] += _dot_tn(pc.astype(BF16), dov)

    qblk = pl.BlockSpec((None, 2 * NA_QB, dh), lambda a, j: (a, j, 0))
    kv = pl.BlockSpec((None, s, dh), lambda a, j: (a, 0, 0))
    ckv = pl.BlockSpec((None, n_ctx, dh), lambda a, j: (a, 0, 0))
    slab_spec = lambda sub: pl.BlockSpec((None, None, NA_QB, NA_KW), lambda a, j: (_na_type(2 * j + sub, nb), a, 0, 0))
    dslab0, dslab1 = jax.ShapeDtypeStruct(slab.shape, F32), jax.ShapeDtypeStruct(slab.shape, F32)
    dq, dk, dv, dkc, dvc, d0, d1 = pl.pallas_call(
        body, grid=(h, half),
        in_specs=[qblk, kv, kv, ckv, ckv, slab_spec(0), slab_spec(1), qblk, qblk,
                  pl.BlockSpec((None, 2 * NA_QB, 1), lambda a, j: (a, j, 0))],
        out_specs=[qblk, kv, kv, ckv, ckv, slab_spec(0), slab_spec(1)],
        out_shape=[jax.ShapeDtypeStruct((h, s, dh), F32), jax.ShapeDtypeStruct((h, s, dh), F32),
                   jax.ShapeDtypeStruct((h, s, dh), F32), jax.ShapeDtypeStruct((h, n_ctx, dh), F32),
                   jax.ShapeDtypeStruct((h, n_ctx, dh), F32), dslab0, dslab1],
        compiler_params=pltpu.CompilerParams(dimension_semantics=("arbitrary", "arbitrary"), vmem_limit_bytes=VMEM_BIG_LIMIT),
        name="na_bwd",
    )(q, k, v, kc, vc, slab, slab, do, o, lse)
    return dq, dk, dv, dkc, dvc, jnp.stack([d0[0], d0[1] + d1[1], d1[2]])


@jax.custom_vjp
def na_attn(q, k, v, kc, vc, slab):
    return _na_fwd_call(q.astype(BF16), k.astype(BF16), v.astype(BF16), kc.astype(BF16), vc.astype(BF16), slab)[0]


def _na_attn_fwd(q, k, v, kc, vc, slab):
    qb, kb, vb, kcb, vcb = (t.astype(BF16) for t in (q, k, v, kc, vc))
    o, lse = _na_fwd_call(qb, kb, vb, kcb, vcb, slab)
    return o, (qb, kb, vb, kcb, vcb, slab, o, lse)


def _na_attn_bwd(res, do):
    qb, kb, vb, kcb, vcb, slab, o, lse = res
    return tuple(_na_bwd_call(qb, kb, vb, kcb, vcb, slab, do.astype(BF16), o, lse))


na_attn.defvjp(_na_attn_fwd, _na_attn_bwd)


def _na_tables(rows):
    col = np.arange(GRID_W)
    c_start = np.clip(col - NA_WIN_W // 2, 0, GRID_W - NA_WIN_W)
    in_win = (col[None, :] >= c_start[:, None]) & (col[None, :] < c_start[:, None] + NA_WIN_W)
    dc_idx = np.clip(col[None, :] - col[:, None], -(NA_WIN_W - 1), NA_WIN_W - 1) + NA_WIN_W - 1
    onehot = (in_win[:, :, None] & (dc_idx[:, :, None] == np.arange(2 * NA_WIN_W - 1)[None, None, :])).astype(np.float32)
    negcol = np.where(in_win, 0.0, NEG_BIG).astype(np.float32)
    plan = []
    for r0, ks in [(0, 0), (NA_QROWS, 0), (rows - NA_QROWS, rows - NA_KROWS)]:
        per_row = []
        for a in range(NA_QROWS):
            r = r0 + a
            rs = min(max(r - NA_WIN_H // 2, 0), rows - NA_WIN_H)
            valid = np.array([rs <= ks + b < rs + NA_WIN_H for b in range(NA_KROWS)])
            per_row.append((ks - r + NA_WIN_H - 1, valid))
        plan.append(per_row)
    return jnp.asarray(onehot), jnp.asarray(negcol), plan


def _na_slab(rpb, tables):
    onehot, negcol, plan = tables
    n_dr = 2 * NA_WIN_H - 1
    table = jnp.einsum("hdc,wuc->hwdu", rpb, onehot, precision=HIGHEST) + negcol[None, :, None, :]
    types = []
    for per_row in plan:
        slabs = []
        for first, valid in per_row:
            lo, hi = max(first, 0), min(first + NA_KROWS, n_dr)
            sel = jnp.pad(table[:, :, lo:hi, :], ((0, 0), (0, 0), (lo - first, first + NA_KROWS - hi), (0, 0)),
                          constant_values=NEG_BIG)
            sel = jnp.where(jnp.asarray(valid)[None, None, :, None], sel, NEG_BIG)
            slabs.append(sel.reshape(NA_HEADS, GRID_W, NA_KW))
        types.append(jnp.stack(slabs, axis=1).reshape(NA_HEADS, NA_QB, NA_KW))
    return jnp.stack(types)


def _loss_call(y, target):
    s, d = y.shape
    tt = _pick(s, (512, 256, 128))

    def body(y_ref, t_ref, loss_ref, diff_ref):
        @pl.when(pl.program_id(0) == 0)
        def _():
            loss_ref[...] = jnp.zeros_like(loss_ref)

        e = y_ref[...] - t_ref[...]
        diff_ref[...] = e * (1.0 / d)
        loss_ref[...] += 0.5 * jnp.sum(jnp.mean(e * e, axis=-1, keepdims=True), axis=0, keepdims=True)

    row = pl.BlockSpec((tt, d), lambda i: (i, 0))
    return pl.pallas_call(
        body, grid=(s // tt,), in_specs=[row, row], out_specs=[pl.BlockSpec((1, 1), lambda i: (0, 0)), row],
        out_shape=[jax.ShapeDtypeStruct((1, 1), F32), jax.ShapeDtypeStruct((s, d), F32)],
        compiler_params=pltpu.CompilerParams(dimension_semantics=("arbitrary",)), name="loss_head",
    )(y, target)


@jax.custom_vjp
def loss_head(y, target):
    return _loss_call(y, target)[0][0, 0]


def _loss_head_fwd(y, target):
    loss, diff = _loss_call(y, target)
    return loss[0, 0], diff


def _loss_head_bwd(diff, g):
    return diff * g, jnp.zeros_like(diff)


loss_head.defvjp(_loss_head_fwd, _loss_head_bwd)


def _sum8_call(parts):
    _, r, c = parts.shape
    tr = _pick(r, (328, 256, 128))

    def body(p_ref, g_ref):
        g = p_ref[0].astype(F32)
        for s in range(1, N_DEV):
            g = g + p_ref[s].astype(F32)
        g_ref[...] = g

    return pl.pallas_call(
        body, grid=(r // tr,), in_specs=[pl.BlockSpec((N_DEV, tr, c), lambda i: (0, i, 0))],
        out_specs=pl.BlockSpec((tr, c), lambda i: (i, 0)), out_shape=jax.ShapeDtypeStruct((r, c), F32),
        compiler_params=pltpu.CompilerParams(dimension_semantics=("parallel",)), name="sum8",
    )(parts)


def _adam_call(g, w, m, v):
    r, c = g.shape
    tr = _pick(r, (256, 128))

    def body(g_ref, w_ref, m_ref, v_ref, d_ref, nm_ref, nv_ref):
        gv = g_ref[...]
        mn = ADAM_B1 * m_ref[...] + (1.0 - ADAM_B1) * gv
        vn = ADAM_B2 * v_ref[...] + (1.0 - ADAM_B2) * (gv * gv)
        m_hat = mn / (1.0 - ADAM_B1 ** ADAM_STEP)
        v_hat = vn / (1.0 - ADAM_B2 ** ADAM_STEP)
        d_ref[...] = -ADAM_LR * (m_hat / (jnp.sqrt(v_hat) + ADAM_EPS) + ADAM_WD * w_ref[...])
        nm_ref[...] = mn
        nv_ref[...] = vn

    row = pl.BlockSpec((tr, c), lambda i: (i, 0))
    out = jax.ShapeDtypeStruct((r, c), F32)
    return pl.pallas_call(
        body, grid=(r // tr,), in_specs=[row, row, row, row], out_specs=[row, row, row], out_shape=[out, out, out],
        compiler_params=pltpu.CompilerParams(dimension_semantics=("parallel",)), name="adam",
    )(g, w, m, v)


def _pack_small(vals, extra=None):
    flat = [vals[n].reshape(-1) for n in SMALL] + ([] if extra is None else [extra.reshape(-1)])
    flat = jnp.concatenate(flat)
    rows = -(-flat.shape[0] // D_MODEL)
    rows = -(-rows // 8) * 8
    return jnp.pad(flat, (0, rows * D_MODEL - flat.shape[0])).reshape(rows, D_MODEL)


def _unpack_small(packed, like):
    flat, out, off = packed.reshape(-1), {}, 0
    for n in SMALL:
        size = like[n].size
        out[n] = flat[off:off + size].reshape(like[n].shape)
        off += size
    return out, flat[off]


def _rope_tables(s, n_ctx):
    t = jnp.arange(s)
    row = (t // GRID_W).astype(F32)
    col = (t % GRID_W).astype(F32)
    half = HEAD_DIM // 2
    inv = ROPE_THETA ** (-jnp.arange(0, half, 2, dtype=F32) / half)
    ang = jnp.concatenate([row[:, None] * inv, col[:, None] * inv], axis=-1)
    cos2 = jnp.repeat(jnp.cos(ang), 2, axis=-1)
    sin2 = jnp.repeat(jnp.sin(ang), 2, axis=-1)
    cos2 = jnp.concatenate([cos2, jnp.ones((n_ctx, HEAD_DIM), F32)], axis=0)
    sin2 = jnp.concatenate([sin2, jnp.zeros((n_ctx, HEAD_DIM), F32)], axis=0)
    return cos2, sin2


def _to_heads(t, n_heads):
    return t.reshape(t.shape[0], n_heads, HEAD_DIM).transpose(1, 0, 2)


def _from_heads(t):
    return t.transpose(1, 0, 2).reshape(t.shape[1], t.shape[0] * HEAD_DIM)


def _local_loss(shards, p, x, c, ctx, target, consts):
    s, n_ctx = x.shape[0], ctx.shape[0]
    cos2, sin2, rot, na_tables = consts
    is_ctx = (jnp.arange(s + n_ctx) >= s)[:, None]
    seg = lambda rows2: jnp.where(is_ctx, rows2[1:2], rows2[0:1])
    xa = jnp.concatenate([x, ctx], axis=0)
    cond = jnp.concatenate([jax.nn.silu(c), jax.nn.silu(p["c_ctx"])[None, :],
                            jnp.zeros((MOD_ROWS - 2, D_MODEL), F32)], axis=0).astype(BF16)
    layer_shards = lambda l: {n: shards[n][l] for n in BIG}
    w = gather_layer(layer_shards(0))

    for l in range(DEPTH):
        mod = (mm(cond, w["w_mod"]) + p["b_mod"][l])[:2]
        sh1, sc1, g1, sh2, sc2, g2 = jnp.split(mod, 6, axis=-1)

        h = norm_mod(xa, p["norm1"][l][None, :], sh1, sc1, s, BF16)
        na_q, na_k, na_v, gq, gk, gv, ga, gb = jnp.split(mm(h, w["w_in"]), IN_SPLITS, axis=-1)

        qa, ka, va = _to_heads(na_q, NA_HEADS), _to_heads(na_k, NA_HEADS), _to_heads(na_v, NA_HEADS)
        slab = _na_slab(p["na_rpb"][l], na_tables)
        ya_lat = na_attn(qa[:, :s], ka[:, :s], va[:, :s], ka[:, s:], va[:, s:], slab)
        ya_ctx = attn(qa[:, s:], ka[:, s:], va[:, s:])
        ya = _from_heads(jnp.concatenate([ya_lat, ya_ctx], axis=1))

        qk = _to_heads(jnp.concatenate([gq, gk], axis=-1), GQA_Q_HEADS + GQA_KV_HEADS)
        gains = jnp.stack([p["q_gain"][l], p["k_gain"][l]])
        qk = hn_rope(qk, gains, cos2, sin2, rot, GQA_Q_HEADS)
        qb, kb, vb = qk[:GQA_Q_HEADS], qk[GQA_Q_HEADS:], _to_heads(gv, GQA_KV_HEADS)
        q_lat = qb[:, :s].reshape(GQA_KV_HEADS, GQA_REP * s, HEAD_DIM)
        if l + 1 < DEPTH:
            ob_lat, w_next = attn_gather(q_lat, kb, vb, layer_shards(l + 1))
        else:
            ob_lat, w_next = attn(q_lat, kb, vb), None
        ob_ctx = attn(qb[:, s:].reshape(GQA_KV_HEADS, GQA_REP * n_ctx, HEAD_DIM), kb[:, s:], vb[:, s:])
        yb = _from_heads(jnp.concatenate([ob_lat.reshape(GQA_Q_HEADS, s, HEAD_DIM),
                                          ob_ctx.reshape(GQA_Q_HEADS, n_ctx, HEAD_DIM)], axis=1))

        merged = jax.nn.sigmoid(ga) * mm(ya.astype(BF16), w["w_pa"]) + jax.nn.sigmoid(gb) * mm(yb.astype(BF16), w["w_pb"])
        xa = xa + seg(g1) * mm(merged.astype(BF16), w["w_o"])

        h2 = norm_mod(xa, p["norm2"][l][None, :], sh2, sc2, s, BF16)
        xa = xa + seg(g2) * ffn(h2, w["w_ffn_in"], w["w_ffn_out"])
        w = w_next

    zeros2 = jnp.zeros((2, D_MODEL), F32)
    y = norm_mod(xa, p["final_norm"][None, :], zeros2, zeros2, s, F32)[:s]
    return loss_head(y, target)


def kernel(x, c, ctx, c_ctx, w_mod, b_mod, norm1, w_in, na_rpb, q_gain, k_gain, w_pa, w_pb, w_o, norm2, w_ffn_in, w_ffn_out, final_norm, loss_target, m_c_ctx, m_w_mod, m_b_mod, m_norm1, m_w_in, m_na_rpb, m_q_gain, m_k_gain, m_w_pa, m_w_pb, m_w_o, m_norm2, m_w_ffn_in, m_w_ffn_out, m_final_norm, v_c_ctx, v_w_mod, v_b_mod, v_norm1, v_w_in, v_na_rpb, v_q_gain, v_k_gain, v_w_pa, v_w_pb, v_w_o, v_norm2, v_w_ffn_in, v_w_ffn_out, v_final_norm):
    w = dict(c_ctx=c_ctx, w_mod=w_mod, b_mod=b_mod, norm1=norm1, w_in=w_in, na_rpb=na_rpb, q_gain=q_gain, k_gain=k_gain,
             w_pa=w_pa, w_pb=w_pb, w_o=w_o, norm2=norm2, w_ffn_in=w_ffn_in, w_ffn_out=w_ffn_out, final_norm=final_norm)
    mom = dict(c_ctx=m_c_ctx, w_mod=m_w_mod, b_mod=m_b_mod, norm1=m_norm1, w_in=m_w_in, na_rpb=m_na_rpb, q_gain=m_q_gain,
               k_gain=m_k_gain, w_pa=m_w_pa, w_pb=m_w_pb, w_o=m_w_o, norm2=m_norm2, w_ffn_in=m_w_ffn_in,
               w_ffn_out=m_w_ffn_out, final_norm=m_final_norm)
    var = dict(c_ctx=v_c_ctx, w_mod=v_w_mod, b_mod=v_b_mod, norm1=v_norm1, w_in=v_w_in, na_rpb=v_na_rpb, q_gain=v_q_gain,
               k_gain=v_k_gain, w_pa=v_w_pa, w_pb=v_w_pb, w_o=v_w_o, norm2=v_norm2, w_ffn_in=v_w_ffn_in,
               w_ffn_out=v_w_ffn_out, final_norm=v_final_norm)
    s, n_ctx = x.shape[1], ctx.shape[1]
    depth = w_mod.shape[0]

    shards = {n: [w[n][l] for l in range(depth)] for n in BIG}
    small = {n: w[n] for n in SMALL}
    consts = (*_rope_tables(s, n_ctx), _rot_matrix(), _na_tables(s // GRID_W))
    loss, (g_shards, g_small, gx) = jax.value_and_grad(_local_loss, argnums=(0, 1, 2))(
        shards, small, x[0], c, ctx[0], loss_target[0], consts)

    parts_s = _exchange(_pack_small(g_small, loss), True, "gather_small_grads")
    g_packed = _sum8_call(parts_s)
    grads, loss = _unpack_small(g_packed, w)
    zero = jnp.zeros((1,), F32)
    upd_s = _adam_call(g_packed, _pack_small(w, zero), _pack_small(mom, zero), _pack_small(var, zero))
    outs = [grads] + [_unpack_small(u, w)[0] for u in upd_s]

    for n in BIG:
        flat = lambda t: t.reshape(-1, t.shape[-1])
        g = jnp.stack(g_shards[n])
        upd = _adam_call(flat(g), flat(w[n]), flat(mom[n]), flat(var[n]))
        for k, t in enumerate([g] + [u.reshape(w[n].shape) for u in upd]):
            outs[k][n] = t
    return (loss, gx[None], *[o[n] for o in outs for n in WEIGHTS])
```

```python
import functools

import numpy as np
import jax
import jax.numpy as jnp
from jax import lax
from jax.experimental import pallas as pl
from jax.experimental.pallas import tpu as pltpu

F32 = jnp.float32
BF16 = jnp.bfloat16
HIGHEST = lax.Precision.HIGHEST

D_MODEL = 1024
DEPTH = 4
GRID_W = 64
HEAD_DIM = 64
NA_HEADS = 8
NA_WIN_H = 8
NA_WIN_W = 16
GQA_Q_HEADS = 8
GQA_KV_HEADS = 2
GQA_REP = GQA_Q_HEADS // GQA_KV_HEADS
NA_WIDTH = NA_HEADS * HEAD_DIM
GQA_Q_WIDTH = GQA_Q_HEADS * HEAD_DIM
GQA_KV_WIDTH = GQA_KV_HEADS * HEAD_DIM
IN_SIZES = (NA_WIDTH, NA_WIDTH, NA_WIDTH, GQA_Q_WIDTH, GQA_KV_WIDTH, GQA_KV_WIDTH, D_MODEL, D_MODEL)
IN_SPLITS = tuple(int(v) for v in np.cumsum(IN_SIZES)[:-1])
ROPE_THETA = 10000.0
EPS = 1e-6
SCALE = HEAD_DIM ** -0.5
ADAM_LR = 0.001
ADAM_B1 = 0.9
ADAM_B2 = 0.999
ADAM_EPS = 1e-08
ADAM_WD = 0.01
ADAM_STEP = 10

N_DEV = 8
MESH = pl.DeviceIdType.MESH
NEG_BIG = -1e30

VMEM_BIG_LIMIT = 52 * 1024 * 1024
NA_QROWS = 4
NA_QB = NA_QROWS * GRID_W
NA_KROWS = 12
NA_KW = NA_KROWS * GRID_W
NA_FWD_HEADS = 4
MM_TILE_ELEMS = 512 * 2176
MM_OPERAND_ELEMS = 1408 * 2176
MOD_ROWS = 256

BIG = ("w_mod", "w_in", "w_pa", "w_pb", "w_o", "w_ffn_in", "w_ffn_out")
ROW_SHARDED = ("w_o", "w_ffn_out")
SMALL = ("c_ctx", "b_mod", "norm1", "na_rpb", "q_gain", "k_gain", "norm2", "final_norm")
WEIGHTS = ("c_ctx", "w_mod", "b_mod", "norm1", "w_in", "na_rpb", "q_gain", "k_gain", "w_pa", "w_pb", "w_o",
           "norm2", "w_ffn_in", "w_ffn_out", "final_norm")


def _pick(n, cands):
    for c in cands:
        if n % c == 0:
            return c
    return n


def _dot_nt(a, b):
    return lax.dot_general(a, b, (((1,), (1,)), ((), ())), preferred_element_type=F32)


def _dot_tn(a, b):
    return lax.dot_general(a, b, (((0,), (0,)), ((), ())), preferred_element_type=F32)


def _dot(a, b):
    return jnp.dot(a, b, preferred_element_type=F32)


def _mm_call(a, b, name, mode="nn", out_dtype=F32, b_lead=None):
    (m, k) = a.shape if mode != "tn" else a.shape[::-1]
    n = b.shape[-1] if mode != "nt" else b.shape[-2]
    tn = _pick(n, (2176, 1408, 1024, 512, 256, 128))
    tm = _pick(m, tuple(t for t in (768, 512, 256, 128) if t * tn <= MM_TILE_ELEMS))
    tk = _pick(k, tuple(t for t in (2816, 2176, 1408, 1024, 768, 512, 256, 128) if t * max(tm, tn) <= MM_OPERAND_ELEMS))
    nk = k // tk
    dot = {"nn": _dot, "tn": _dot_tn, "nt": _dot_nt}[mode]
    a_blk = (tk, tm) if mode == "tn" else (tm, tk)
    b_blk = (tn, tk) if mode == "nt" else (tk, tn)
    a_idx = (lambda i, kk: (kk, i)) if mode == "tn" else (lambda i, kk: (i, kk))
    b_idx = (lambda j, kk: (j, kk)) if mode == "nt" else (lambda j, kk: (kk, j))
    if b_lead is not None:
        b_blk, b_idx2 = (None,) + b_blk, b_idx
        b_idx = lambda j, kk: (b_lead,) + b_idx2(j, kk)

    def body(a_ref, b_ref, o_ref, acc_ref):
        kk = pl.program_id(2)
        part = dot(a_ref[...], b_ref[...])

        @pl.when(kk == 0)
        def _():
            acc_ref[...] = part

        @pl.when(kk > 0)
        def _():
            acc_ref[...] += part

        @pl.when(kk == nk - 1)
        def _():
            o_ref[...] = acc_ref[...].astype(o_ref.dtype)

    def body1(a_ref, b_ref, o_ref):
        o_ref[...] = dot(a_ref[...], b_ref[...]).astype(o_ref.dtype)

    footprint = 2 * (tm * tk * 2 + tk * tn * 2 + tm * tn * 4) + 2 * tm * tn * 4
    limit = int(footprint + (8 << 20))
    if nk == 1:
        return pl.pallas_call(
            body1, grid=(n // tn, m // tm),
            in_specs=[pl.BlockSpec(a_blk, lambda j, i: a_idx(i, 0)), pl.BlockSpec(b_blk, lambda j, i: b_idx(j, 0))],
            out_specs=pl.BlockSpec((tm, tn), lambda j, i: (i, j)),
            out_shape=jax.ShapeDtypeStruct((m, n), out_dtype),
            compiler_params=pltpu.CompilerParams(dimension_semantics=("parallel", "parallel"), vmem_limit_bytes=limit),
            name=name,
        )(a, b)
    return pl.pallas_call(
        body, grid=(m // tm, n // tn, nk),
        in_specs=[pl.BlockSpec(a_blk, lambda i, j, kk: a_idx(i, kk)), pl.BlockSpec(b_blk, lambda i, j, kk: b_idx(j, kk))],
        out_specs=pl.BlockSpec((tm, tn), lambda i, j, kk: (i, j)),
        out_shape=jax.ShapeDtypeStruct((m, n), out_dtype),
        scratch_shapes=[pltpu.VMEM((tm, tn), F32)],
        compiler_params=pltpu.CompilerParams(
            dimension_semantics=("parallel", "parallel", "arbitrary"), vmem_limit_bytes=limit),
        name=name,
    )(a, b)


@jax.custom_vjp
def mm(x, w):
    return _mm_call(x, w, "mm_fwd")


def _mm_fwd(x, w):
    return _mm_call(x, w, "mm_fwd"), (x, w)


def _mm_bwd(res, dy):
    x, w = res
    dyb = dy.astype(BF16)
    return _mm_call(dyb, w, "mm_dx", "nt", BF16), _mm_call(x, dyb, "mm_dw", "tn", BF16)


mm.defvjp(_mm_fwd, _mm_bwd)


def _ffn_tiles(t, f):
    return _pick(t, (384, 256, 128)), _pick(f, (1408, 1024, 512, 256, 128))


def _ffn_in_call(h, w_in):
    t, d = h.shape
    f = w_in.shape[1] // 2
    tm, tn = _ffn_tiles(t, f)
    nf = f // tn

    def body(h_ref, wa_ref, wu_ref, act_ref, au_ref):
        hv = h_ref[...]
        a, u = _dot(hv, wa_ref[...]), _dot(hv, wu_ref[...])
        au_ref[0] = a
        au_ref[1] = u
        act_ref[...] = (a * jax.nn.sigmoid(a) * u).astype(act_ref.dtype)

    return pl.pallas_call(
        body, grid=(nf, t // tm),
        in_specs=[pl.BlockSpec((tm, d), lambda j, i: (i, 0)), pl.BlockSpec((d, tn), lambda j, i: (0, j)),
                  pl.BlockSpec((d, tn), lambda j, i: (0, j + nf))],
        out_specs=[pl.BlockSpec((tm, tn), lambda j, i: (i, j)), pl.BlockSpec((2, tm, tn), lambda j, i: (0, i, j))],
        out_shape=[jax.ShapeDtypeStruct((t, f), BF16), jax.ShapeDtypeStruct((2, t, f), F32)],
        compiler_params=pltpu.CompilerParams(dimension_semantics=("parallel", "parallel"), vmem_limit_bytes=VMEM_BIG_LIMIT),
        name="ffn_in",
    )(h, w_in, w_in)


def _ffn_dact_call(dy, w_out, au):
    t, d = dy.shape
    f = w_out.shape[0]
    tm, tn = _ffn_tiles(t, f)

    def body(dy_ref, w_ref, au_ref, dau_ref):
        dact = _dot_nt(dy_ref[...], w_ref[...])
        a, u = au_ref[0], au_ref[1]
        sg = jax.nn.sigmoid(a)
        dau_ref[0] = (dact * u * (sg * (1.0 + a * (1.0 - sg)))).astype(dau_ref.dtype)
        dau_ref[1] = (dact * (a * sg)).astype(dau_ref.dtype)

    halves = pl.BlockSpec((2, tm, tn), lambda j, i: (0, i, j))
    return pl.pallas_call(
        body, grid=(f // tn, t // tm),
        in_specs=[pl.BlockSpec((tm, d), lambda j, i: (i, 0)), pl.BlockSpec((tn, d), lambda j, i: (j, 0)), halves],
        out_specs=halves, out_shape=jax.ShapeDtypeStruct((2, t, f), BF16),
        compiler_params=pltpu.CompilerParams(dimension_semantics=("parallel", "parallel"), vmem_limit_bytes=VMEM_BIG_LIMIT),
        name="ffn_dact",
    )(dy, w_out, au)


def _ffn_dh_call(dau, w_in):
    _, t, f = dau.shape
    d = w_in.shape[0]
    tm, tk = _ffn_tiles(t, f)
    nf = f // tk

    def body(a_ref, b_ref, o_ref):
        acc = None
        for c in range(2 * nf):
            part = _dot_nt(a_ref[c // nf, :, (c % nf) * tk:(c % nf + 1) * tk], b_ref[:, c * tk:(c + 1) * tk])
            acc = part if acc is None else acc + part
        o_ref[...] = acc.astype(o_ref.dtype)

    return pl.pallas_call(
        body, grid=(t // tm,),
        in_specs=[pl.BlockSpec((2, tm, f), lambda i: (0, i, 0)), pl.BlockSpec((d, 2 * f), lambda i: (0, 0))],
        out_specs=pl.BlockSpec((tm, d), lambda i: (i, 0)), out_shape=jax.ShapeDtypeStruct((t, d), BF16),
        compiler_params=pltpu.CompilerParams(dimension_semantics=("parallel",), vmem_limit_bytes=VMEM_BIG_LIMIT),
        name="ffn_dh",
    )(dau, w_in)


@jax.custom_vjp
def ffn(h, w_in, w_out):
    return _ffn_fwd(h, w_in, w_out)[0]


def _ffn_fwd(h, w_in, w_out):
    act, au = _ffn_in_call(h, w_in)
    return _mm_call(act, w_out, "mm_fwd"), (h, w_in, w_out, act, au)


def _ffn_bwd(res, dy):
    h, w_in, w_out, act, au = res
    dyb = dy.astype(BF16)
    dau = _ffn_dact_call(dyb, w_out, au)
    dw_in = jnp.concatenate([_mm_call(h, dau, "mm_dw", "tn", BF16, b_lead=c) for c in range(2)], axis=1)
    return _ffn_dh_call(dau, w_in), dw_in, _mm_call(act, dyb, "mm_dw", "tn", BF16)


ffn.defvjp(_ffn_fwd, _ffn_bwd)


def _seg_rows(ref, is_ctx):
    return jnp.where(is_ctx, ref[1:2, :], ref[0:1, :])


def _one_hot_row(second):
    return (lax.broadcasted_iota(jnp.int32, (2, 1), 0) == second.astype(jnp.int32)).astype(F32)


def _norm_mod_fwd_call(x, g, shift, scale, n_lat, out_dtype):
    t, d = x.shape
    tt = _pick(t, (256, 128))
    lat_tiles = n_lat // tt

    def body(x_ref, g_ref, sh_ref, sc_ref, y_ref):
        is_ctx = pl.program_id(0) >= lat_tiles
        xv = x_ref[...]
        r = lax.rsqrt(jnp.mean(xv * xv, axis=-1, keepdims=True) + EPS)
        yn = xv * r * g_ref[...]
        y_ref[...] = (yn * (1.0 + _seg_rows(sc_ref, is_ctx)) + _seg_rows(sh_ref, is_ctx)).astype(y_ref.dtype)

    row = pl.BlockSpec((tt, d), lambda i: (i, 0))
    full = lambda rws: pl.BlockSpec((rws, d), lambda i: (0, 0))
    return pl.pallas_call(
        body, grid=(t // tt,), in_specs=[row, full(1), full(2), full(2)], out_specs=row,
        out_shape=jax.ShapeDtypeStruct((t, d), out_dtype),
        compiler_params=pltpu.CompilerParams(dimension_semantics=("parallel",)), name="norm_mod_fwd",
    )(x, g, shift, scale)


def _norm_mod_bwd_call(x, g, scale, dy, n_lat):
    t, d = x.shape
    tt = _pick(t, (256, 128))
    lat_tiles = n_lat // tt

    def body(x_ref, g_ref, sc_ref, dy_ref, dx_ref, dg_ref, dsh_ref, dsc_ref):
        i = pl.program_id(0)
        is_ctx = i >= lat_tiles

        @pl.when(i == 0)
        def _():
            dg_ref[...] = jnp.zeros_like(dg_ref)
            dsh_ref[...] = jnp.zeros_like(dsh_ref)
            dsc_ref[...] = jnp.zeros_like(dsc_ref)

        xv, dyv, gv = x_ref[...], dy_ref[...].astype(F32), g_ref[...]
        r = lax.rsqrt(jnp.mean(xv * xv, axis=-1, keepdims=True) + EPS)
        nrm = xv * r
        yn = nrm * gv
        seg = _one_hot_row(is_ctx)
        dsh_ref[...] += seg * jnp.sum(dyv, axis=0, keepdims=True)
        dsc_ref[...] += seg * jnp.sum(dyv * yn, axis=0, keepdims=True)
        dyn = dyv * (1.0 + _seg_rows(sc_ref, is_ctx))
        dg_ref[...] += jnp.sum(dyn * nrm, axis=0, keepdims=True)
        dn = dyn * gv
        dx_ref[...] = r * (dn - nrm * jnp.mean(dn * nrm, axis=-1, keepdims=True))

    row = pl.BlockSpec((tt, d), lambda i: (i, 0))
    full = lambda rws: pl.BlockSpec((rws, d), lambda i: (0, 0))
    return pl.pallas_call(
        body, grid=(t // tt,), in_specs=[row, full(1), full(2), row], out_specs=[row, full(1), full(2), full(2)],
        out_shape=[jax.ShapeDtypeStruct((t, d), F32), jax.ShapeDtypeStruct((1, d), F32),
                   jax.ShapeDtypeStruct((2, d), F32), jax.ShapeDtypeStruct((2, d), F32)],
        compiler_params=pltpu.CompilerParams(dimension_semantics=("arbitrary",)), name="norm_mod_bwd",
    )(x, g, scale, dy)


@functools.partial(jax.custom_vjp, nondiff_argnums=(4, 5))
def norm_mod(x, g, shift, scale, n_lat, out_dtype):
    return _norm_mod_fwd_call(x, g, shift, scale, n_lat, out_dtype)


def _norm_mod_fwd(x, g, shift, scale, n_lat, out_dtype):
    return _norm_mod_fwd_call(x, g, shift, scale, n_lat, out_dtype), (x, g, scale)


def _norm_mod_bwd(n_lat, out_dtype, res, dy):
    x, g, scale = res
    dx, dg, dsh, dsc = _norm_mod_bwd_call(x, g, scale, dy, n_lat)
    return dx, dg, dsh, dsc


norm_mod.defvjp(_norm_mod_fwd, _norm_mod_bwd)


def _rot_matrix():
    p = np.zeros((HEAD_DIM, HEAD_DIM), np.float32)
    for i in range(HEAD_DIM // 2):
        p[2 * i + 1, 2 * i] = -1.0
        p[2 * i, 2 * i + 1] = 1.0
    return jnp.asarray(p)


def _hn_rope_fwd_call(x, gains, cos2, sin2, rot, n_q):
    nh, t, dh = x.shape
    tt = _pick(t, (2816, 1024, 768, 512, 256, 128))

    def body(x_ref, g_ref, cos_ref, sin_ref, rot_ref, y_ref):
        gv = jnp.where(pl.program_id(0) >= n_q, g_ref[1:2, :], g_ref[0:1, :])
        xv = x_ref[...]
        r = lax.rsqrt(jnp.mean(xv * xv, axis=-1, keepdims=True) + EPS)
        y = xv * r * gv
        yr = jnp.dot(y, rot_ref[...], precision=HIGHEST, preferred_element_type=F32)
        y_ref[...] = y * cos_ref[...] + yr * sin_ref[...]

    blk = pl.BlockSpec((None, tt, dh), lambda h, i: (h, i, 0))
    tab = pl.BlockSpec((tt, dh), lambda h, i: (i, 0))
    return pl.pallas_call(
        body, grid=(nh, t // tt),
        in_specs=[blk, pl.BlockSpec((2, dh), lambda h, i: (0, 0)), tab, tab, pl.BlockSpec((dh, dh), lambda h, i: (0, 0))],
        out_specs=blk, out_shape=jax.ShapeDtypeStruct((nh, t, dh), F32),
        compiler_params=pltpu.CompilerParams(dimension_semantics=("parallel", "parallel")), name="hn_rope_fwd",
    )(x, gains, cos2, sin2, rot)


def _hn_rope_bwd_call(x, gains, cos2, sin2, rot, dy, n_q):
    nh, t, dh = x.shape
    tt = _pick(t, (2816, 1024, 768, 512, 256, 128))

    def body(x_ref, g_ref, cos_ref, sin_ref, rot_ref, dy_ref, dx_ref, dg_ref):
        h, i = pl.program_id(0), pl.program_id(1)
        is_k = h >= n_q

        @pl.when((h == 0) & (i == 0))
        def _():
            dg_ref[...] = jnp.zeros_like(dg_ref)

        gv = jnp.where(is_k, g_ref[1:2, :], g_ref[0:1, :])
        xv, dyv = x_ref[...], dy_ref[...]
        r = lax.rsqrt(jnp.mean(xv * xv, axis=-1, keepdims=True) + EPS)
        nrm = xv * r
        dyn = dyv * cos_ref[...] - jnp.dot(dyv * sin_ref[...], rot_ref[...], precision=HIGHEST, preferred_element_type=F32)
        seg = _one_hot_row(is_k)
        dg_ref[...] += seg * jnp.sum(dyn * nrm, axis=0, keepdims=True)
        dn = dyn * gv
        dx_ref[...] = r * (dn - nrm * jnp.mean(dn * nrm, axis=-1, keepdims=True))

    blk = pl.BlockSpec((None, tt, dh), lambda h, i: (h, i, 0))
    tab = pl.BlockSpec((tt, dh), lambda h, i: (i, 0))
    g_spec = pl.BlockSpec((2, dh), lambda h, i: (0, 0))
    return pl.pallas_call(
        body, grid=(nh, t // tt),
        in_specs=[blk, g_spec, tab, tab, pl.BlockSpec((dh, dh), lambda h, i: (0, 0)), blk],
        out_specs=[blk, g_spec],
        out_shape=[jax.ShapeDtypeStruct((nh, t, dh), F32), jax.ShapeDtypeStruct((2, dh), F32)],
        compiler_params=pltpu.CompilerParams(dimension_semantics=("arbitrary", "arbitrary")), name="hn_rope_bwd",
    )(x, gains, cos2, sin2, rot, dy)


@functools.partial(jax.custom_vjp, nondiff_argnums=(5,))
def hn_rope(x, gains, cos2, sin2, rot, n_q):
    return _hn_rope_fwd_call(x, gains, cos2, sin2, rot, n_q)


def _hn_rope_fwd(x, gains, cos2, sin2, rot, n_q):
    return _hn_rope_fwd_call(x, gains, cos2, sin2, rot, n_q), (x, gains, cos2, sin2, rot)


def _hn_rope_bwd(n_q, res, dy):
    x, gains, cos2, sin2, rot = res
    dx, dg = _hn_rope_bwd_call(x, gains, cos2, sin2, rot, dy, n_q)
    return dx, dg, jnp.zeros_like(cos2), jnp.zeros_like(sin2), jnp.zeros_like(rot)


hn_rope.defvjp(_hn_rope_fwd, _hn_rope_bwd)


EXCHANGE_SCRATCH = [pltpu.SemaphoreType.DMA((N_DEV - 1,)), pltpu.SemaphoreType.DMA((N_DEV - 1,)), pltpu.SemaphoreType.DMA(())]


def _exchange_copies(x_ref, out_ref, send_sems, recv_sems, local_sem, all_gather):
    mx, my, mc = lax.axis_index("x"), lax.axis_index("y"), lax.axis_index("c")
    me = 4 * mx + 2 * my + mc
    src = (lambda p: x_ref) if all_gather else (lambda p: x_ref.at[p])
    local = pltpu.make_async_copy(src(me), out_ref.at[me], local_sem)
    remote = []
    for rel in range(1, N_DEV):
        px, py, pc = mx ^ (rel >> 2), my ^ ((rel >> 1) & 1), mc ^ (rel & 1)
        remote.append(pltpu.make_async_remote_copy(
            src_ref=src(4 * px + 2 * py + pc), dst_ref=out_ref.at[me],
            send_sem=send_sems.at[rel - 1], recv_sem=recv_sems.at[rel - 1],
            device_id=(px, py, pc), device_id_type=MESH))
    return local, remote


def _exchange_start(*refs, all_gather):
    local, remote = _exchange_copies(*refs, all_gather)
    local.start()
    for cp in remote:
        cp.start()


def _exchange_wait(*refs, all_gather):
    local, remote = _exchange_copies(*refs, all_gather)
    for cp in remote:
        cp.wait_send()
    for cp in remote:
        cp.wait_recv()
    local.wait()


def _exchange_shape(x, all_gather):
    return jax.ShapeDtypeStruct((N_DEV,) + tuple(x.shape if all_gather else x.shape[1:]), x.dtype)


def _exchange(x, all_gather, name):
    def body(x_ref, out_ref, *sems):
        _exchange_start(x_ref, out_ref, *sems, all_gather=all_gather)
        _exchange_wait(x_ref, out_ref, *sems, all_gather=all_gather)

    return pl.pallas_call(
        body, in_specs=[pl.BlockSpec(memory_space=pl.ANY)], out_specs=pl.BlockSpec(memory_space=pl.ANY),
        out_shape=_exchange_shape(x, all_gather), scratch_shapes=EXCHANGE_SCRATCH,
        compiler_params=pltpu.CompilerParams(has_side_effects=True), name=name,
    )(x)


ATT_TK = 256


def _att_tq(rn, fwd=False):
    return _pick(rn, ((1024,) if fwd else ()) + (512, 256, 128))


def _att_unroll(nkb):
    return _pick(nkb, (3, 2))


def _flash_fwd_call(q_t, k, v_t, xsend=None):
    g, dh, rn = q_t.shape
    nk = k.shape[1]
    tq, tk = _att_tq(rn, True), ATT_TK
    nkb = nk // tk
    unroll = _att_unroll(nkb)
    n_i = rn // tq

    trips = nkb // unroll

    def body(*refs):
        if xsend is None:
            qt_ref, k_ref, vt_ref, ot_ref, lse_ref, s_scr = refs
        else:
            qt_ref, k_ref, vt_ref, x_ref, ot_ref, lse_ref, xout_ref, s_scr, *sems = refs
            a, i = pl.program_id(0), pl.program_id(1)

            @pl.when((a == 0) & (i == 0))
            def _():
                _exchange_start(x_ref, xout_ref, *sems, all_gather=True)

        qst = qt_ref[...] * SCALE

        def scores(t, slot):
            for u in range(unroll):
                off = pl.multiple_of((t * unroll + u) * tk, tk)
                s_scr[slot, u] = _dot(k_ref[pl.ds(off, tk), :], qst)

        def consume(t, slot, carry):
            m, l, acc = carry
            s_t = [s_scr[slot, u] for u in range(unroll)]
            m_new = functools.reduce(jnp.maximum, [jnp.max(s, axis=0, keepdims=True) for s in s_t], m)
            p_t = [jnp.exp(s - m_new) for s in s_t]
            alpha = jnp.exp(m - m_new)
            l = alpha * l + sum(jnp.sum(p, axis=0, keepdims=True) for p in p_t)
            acc = alpha * acc + sum(_dot(vt_ref[t * unroll + u], p.astype(BF16)) for u, p in enumerate(p_t))
            return m_new, l, acc

        def pair(u2, carry):
            t = 2 * u2
            scores(t + 1, 1)
            carry = consume(t, 0, carry)
            scores(t + 2, 0)
            return consume(t + 1, 1, carry)

        scores(0, 0)
        pairs = (trips - 1) // 2
        carry = lax.fori_loop(
            0, pairs, pair, (jnp.full((1, tq), NEG_BIG, F32), jnp.zeros((1, tq), F32), jnp.zeros((dh, tq), F32)))
        if trips - 2 * pairs == 2:
            scores(trips - 1, 1)
            carry = consume(trips - 2, 0, carry)
            carry = consume(trips - 1, 1, carry)
        else:
            carry = consume(trips - 1, 0, carry)
        m, l, acc = carry
        ot_ref[...] = acc / l
        lse_ref[...] = m + jnp.log(l)

        if xsend is not None:
            @pl.when((a == g - 1) & (i == n_i - 1))
            def _():
                _exchange_wait(x_ref, xout_ref, *sems, all_gather=True)

    col = pl.BlockSpec((None, dh, tq), lambda a, i: (a, 0, i))
    vec = pl.BlockSpec((None, 1, tq), lambda a, i: (a, 0, i))
    hbm = pl.BlockSpec(memory_space=pl.ANY)
    in_specs = [col, pl.BlockSpec((None, nk, dh), lambda a, i: (a, 0, 0)),
                pl.BlockSpec((None, nkb, dh, tk), lambda a, i: (a, 0, 0, 0))]
    out_specs = [col, vec]
    out_shape = [jax.ShapeDtypeStruct((g, dh, rn), F32), jax.ShapeDtypeStruct((g, 1, rn), F32)]
    args = (q_t, k, v_t)
    if xsend is not None:
        in_specs, out_specs, args = in_specs + [hbm], out_specs + [hbm], args + (xsend,)
        out_shape = out_shape + [_exchange_shape(xsend, True)]
    return pl.pallas_call(
        body, grid=(g, n_i), in_specs=in_specs, out_specs=out_specs, out_shape=out_shape,
        scratch_shapes=[pltpu.VMEM((2, unroll, tk, tq), F32)] + ([] if xsend is None else EXCHANGE_SCRATCH),
        compiler_params=pltpu.CompilerParams(dimension_semantics=("arbitrary", "arbitrary"), vmem_limit_bytes=VMEM_BIG_LIMIT),
        name="flash_fwd" if xsend is None else "flash_fwd_gather",
    )(*args)


def _flash_bwd_call(q, q_t, k, k_t, v, do, do_t, o_t, lse_t, xsend=None):
    g, rn, dh = q.shape
    nk = k.shape[1]
    tq, tk = _att_tq(rn), ATT_TK
    nkb = nk // tk
    unroll = _att_unroll(nkb)
    n_i = rn // tq

    def body(*refs):
        q_ref, qt_ref, k_ref, kt_ref, v_ref, do_ref, dot_ref, ot_ref, lse_ref = refs[:9]
        if xsend is None:
            dqt_ref, dk_ref, dv_ref = refs[9:]
        else:
            x_ref, dqt_ref, dk_ref, dv_ref, xout_ref, *sems = refs[9:]
        a, i = pl.program_id(0), pl.program_id(1)

        if xsend is not None:
            @pl.when((a == 0) & (i == 0))
            def _():
                _exchange_start(x_ref, xout_ref, *sems, all_gather=False)

        @pl.when(i == 0)
        def _():
            dk_ref[...] = jnp.zeros_like(dk_ref)
            dv_ref[...] = jnp.zeros_like(dv_ref)

        qs = q_ref[...] * SCALE
        qst = qt_ref[...] * SCALE
        dov, dotv = do_ref[...], dot_ref[...]
        delta = jnp.sum(dotv.astype(F32) * ot_ref[...], axis=0, keepdims=True)
        lse = lse_ref[...]

        def step(j, dqt):
            off = pl.multiple_of(j * tk, tk)
            kj = k_ref[pl.ds(off, tk), :]
            vj = v_ref[pl.ds(off, tk), :]
            p_t = jnp.exp(_dot(kj, qst) - lse)
            ds_t = (p_t * (_dot(vj, dotv) - delta)).astype(BF16)
            dv_ref[pl.ds(off, tk), :] += _dot(p_t.astype(BF16), dov)
            dk_ref[pl.ds(off, tk), :] += _dot(ds_t, qs)
            return dqt + _dot(kt_ref[j], ds_t)

        def trip(t, dqt):
            for u in range(unroll):
                dqt = step(t * unroll + u, dqt)
            return dqt

        dqt_ref[...] = lax.fori_loop(0, nkb // unroll, trip, jnp.zeros((dh, tq), F32)) * SCALE

        if xsend is not None:
            @pl.when((a == g - 1) & (i == n_i - 1))
            def _():
                _exchange_wait(x_ref, xout_ref, *sems, all_gather=False)

    row = pl.BlockSpec((None, tq, dh), lambda a, i: (a, i, 0))
    col = pl.BlockSpec((None, dh, tq), lambda a, i: (a, 0, i))
    kv = pl.BlockSpec((None, nk, dh), lambda a, i: (a, 0, 0))
    hbm = pl.BlockSpec(memory_space=pl.ANY)
    in_specs = [row, col, kv, pl.BlockSpec((None, nkb, dh, tk), lambda a, i: (a, 0, 0, 0)), kv, row, col, col,
                pl.BlockSpec((None, 1, tq), lambda a, i: (a, 0, i))]
    out_specs = [col, kv, kv]
    out_shape = [jax.ShapeDtypeStruct((g, dh, rn), F32), jax.ShapeDtypeStruct((g, nk, dh), F32),
                 jax.ShapeDtypeStruct((g, nk, dh), F32)]
    args = (q, q_t, k, k_t, v, do, do_t, o_t, lse_t)
    if xsend is not None:
        in_specs, out_specs, args = in_specs + [hbm], out_specs + [hbm], args + (xsend,)
        out_shape = out_shape + [_exchange_shape(xsend, False)]
    return pl.pallas_call(
        body, grid=(g, n_i), in_specs=in_specs, out_specs=out_specs, out_shape=out_shape,
        scratch_shapes=[] if xsend is None else EXCHANGE_SCRATCH,
        compiler_params=pltpu.CompilerParams(dimension_semantics=("arbitrary", "arbitrary"), vmem_limit_bytes=VMEM_BIG_LIMIT),
        name="flash_bwd" if xsend is None else "flash_bwd_scatter",
    )(*args)


def _key_blocks_t(t):
    g, nk, dh = t.shape
    return t.reshape(g, nk // ATT_TK, ATT_TK, dh).transpose(0, 1, 3, 2)


@jax.custom_vjp
def attn(q, k, v):
    return _attn_fwd(q, k, v)[0]


def _attn_fwd(q, k, v):
    qb, kb, vb = q.astype(BF16), k.astype(BF16), v.astype(BF16)
    q_t = qb.transpose(0, 2, 1)
    o_t, lse_t = _flash_fwd_call(q_t, kb, _key_blocks_t(vb))
    return o_t.transpose(0, 2, 1), (qb, q_t, kb, vb, o_t, lse_t)


def _attn_bwd(res, do):
    qb, q_t, kb, vb, o_t, lse_t = res
    dob = do.astype(BF16)
    dq_t, dk, dv = _flash_bwd_call(qb, q_t, kb, _key_blocks_t(kb), vb, dob, dob.transpose(0, 2, 1), o_t, lse_t)
    return dq_t.transpose(0, 2, 1), dk, dv


attn.defvjp(_attn_fwd, _attn_bwd)


def _shard_rows(shards):
    return jnp.concatenate([shards[n].reshape(-1, D_MODEL) for n in BIG], axis=0).astype(BF16)


def _fulls_of(gathered, shards):
    out, off = {}, 0
    for n in BIG:
        k, nn = shards[n].shape
        r = k * nn // D_MODEL
        t = gathered[:, off:off + r].reshape(N_DEV, k, nn)
        out[n] = t.reshape(N_DEV * k, nn) if n in ROW_SHARDED else t.transpose(1, 0, 2).reshape(k, N_DEV * nn)
        off += r
    return out


def _slabs_of(dfulls):
    parts = []
    for n in BIG:
        k, nn = dfulls[n].shape
        t = (dfulls[n].reshape(N_DEV, k // N_DEV, nn) if n in ROW_SHARDED
             else dfulls[n].reshape(k, N_DEV, nn // N_DEV).transpose(1, 0, 2))
        parts.append(t.reshape(N_DEV, -1, D_MODEL))
    return jnp.concatenate(parts, axis=1)


def _dshards_of(parts, dfulls):
    rows, out, off = _sum8_call(parts), {}, 0
    for n in BIG:
        k, nn = dfulls[n].shape
        shape = (k // N_DEV, nn) if n in ROW_SHARDED else (k, nn // N_DEV)
        r = shape[0] * shape[1] // D_MODEL
        out[n] = rows[off:off + r].reshape(shape)
        off += r
    return out


@jax.custom_vjp
def gather_layer(shards):
    return _fulls_of(_exchange(_shard_rows(shards), True, "gather_weights"), shards)


def _gather_layer_fwd(shards):
    return gather_layer(shards), None


def _gather_layer_bwd(_, dfulls):
    return (_dshards_of(_exchange(_slabs_of(dfulls), False, "scatter_grads"), dfulls),)


gather_layer.defvjp(_gather_layer_fwd, _gather_layer_bwd)


@jax.custom_vjp
def attn_gather(q, k, v, shards):
    return _attn_gather_fwd(q, k, v, shards)[0]


def _attn_gather_fwd(q, k, v, shards):
    qb, kb, vb = q.astype(BF16), k.astype(BF16), v.astype(BF16)
    q_t = qb.transpose(0, 2, 1)
    o_t, lse_t, gathered = _flash_fwd_call(q_t, kb, _key_blocks_t(vb), _shard_rows(shards))
    return (o_t.transpose(0, 2, 1), _fulls_of(gathered, shards)), (qb, q_t, kb, vb, o_t, lse_t)


def _attn_gather_bwd(res, cts):
    qb, q_t, kb, vb, o_t, lse_t = res
    do, dfulls = cts
    dob = do.astype(BF16)
    dq_t, dk, dv, parts = _flash_bwd_call(qb, q_t, kb, _key_blocks_t(kb), vb, dob, dob.transpose(0, 2, 1), o_t, lse_t,
                                          _slabs_of(dfulls))
    return dq_t.transpose(0, 2, 1), dk, dv, _dshards_of(parts, dfulls)


attn_gather.defvjp(_attn_gather_fwd, _attn_gather_bwd)


def _na_key_row(i, rows):
    return jnp.clip(NA_QROWS * i - NA_WIN_H // 2, 0, rows - NA_KROWS)


def _na_type(i, nb):
    return jnp.where(i == 0, 0, jnp.where(i == nb - 1, 2, 1))


def _na_fwd_call(q, k, v, kc, vc, slab):
    h, s, dh = q.shape
    n_ctx = kc.shape[1]
    rows, nb = s // GRID_W, s // NA_QB
    hb = NA_FWD_HEADS

    def body(q_ref, k_ref, v_ref, kc_ref, vc_ref, slab_ref, o_ref, lse_ref):
        off = pl.multiple_of(_na_key_row(pl.program_id(1), rows) * GRID_W, NA_QB)
        for hh in range(hb):
            qs = q_ref[hh] * SCALE
            kw = k_ref[hh, pl.ds(off, NA_KW), :]
            vw = v_ref[hh, pl.ds(off, NA_KW), :]
            sw = _dot_nt(qs, kw) + slab_ref[hh]
            sc = _dot_nt(qs, kc_ref[hh])
            m = jnp.maximum(jnp.max(sw, axis=1, keepdims=True), jnp.max(sc, axis=1, keepdims=True))
            pw = jnp.exp(sw - m)
            pc = jnp.exp(sc - m)
            l = jnp.sum(pw, axis=1, keepdims=True) + jnp.sum(pc, axis=1, keepdims=True)
            o_ref[hh] = (_dot(pw.astype(BF16), vw) + _dot(pc.astype(BF16), vc_ref[hh])) / l
            lse_ref[hh] = m + jnp.log(l)

    qblk = pl.BlockSpec((hb, NA_QB, dh), lambda a, i: (a, i, 0))
    kv = pl.BlockSpec((hb, s, dh), lambda a, i: (a, 0, 0))
    ckv = pl.BlockSpec((hb, n_ctx, dh), lambda a, i: (a, 0, 0))
    return pl.pallas_call(
        body, grid=(h // hb, nb),
        in_specs=[qblk, kv, kv, ckv, ckv, pl.BlockSpec((None, hb, NA_QB, NA_KW), lambda a, i: (_na_type(i, nb), a, 0, 0))],
        out_specs=[qblk, pl.BlockSpec((hb, NA_QB, 1), lambda a, i: (a, i, 0))],
        out_shape=[jax.ShapeDtypeStruct((h, s, dh), F32), jax.ShapeDtypeStruct((h, s, 1), F32)],
        compiler_params=pltpu.CompilerParams(dimension_semantics=("parallel", "parallel"), vmem_limit_bytes=VMEM_BIG_LIMIT),
        name="na_fwd",
    )(q, k, v, kc, vc, slab)


def _na_bwd_call(q, k, v, kc, vc, slab, do, o, lse):
    h, s, dh = q.shape
    n_ctx = kc.shape[1]
    rows, nb = s // GRID_W, s // NA_QB
    half = nb // 2

    def body(q_ref, k_ref, v_ref, kc_ref, vc_ref, slab0_ref, slab1_ref, do_ref, o_ref, lse_ref,
             dq_ref, dk_ref, dv_ref, dkc_ref, dvc_ref, dslab0_ref, dslab1_ref):
        j = pl.program_id(1)

        @pl.when(j == 0)
        def _():
            dk_ref[...] = jnp.zeros_like(dk_ref)
            dv_ref[...] = jnp.zeros_like(dv_ref)
            dkc_ref[...] = jnp.zeros_like(dkc_ref)
            dvc_ref[...] = jnp.zeros_like(dvc_ref)

        @pl.when(j <= 1)
        def _():
            dslab0_ref[...] = jnp.zeros_like(dslab0_ref)

        @pl.when((j == 0) | (j == half - 1))
        def _():
            dslab1_ref[...] = jnp.zeros_like(dslab1_ref)

        kcv, vcv = kc_ref[...], vc_ref[...]
        for sub, (slab_ref, dslab_ref) in enumerate(((slab0_ref, dslab0_ref), (slab1_ref, dslab1_ref))):
            off = pl.multiple_of(_na_key_row(2 * j + sub, rows) * GRID_W, NA_QB)
            blk = slice(sub * NA_QB, (sub + 1) * NA_QB)
            qs = q_ref[blk, :] * SCALE
            kw = k_ref[pl.ds(off, NA_KW), :]
            vw = v_ref[pl.ds(off, NA_KW), :]
            dov = do_ref[blk, :]
            lse = lse_ref[blk, :]
            delta = jnp.sum(dov.astype(F32) * o_ref[blk, :], axis=1, keepdims=True)
            pw = jnp.exp(_dot_nt(qs, kw) + slab_ref[...] - lse)
            pc = jnp.exp(_dot_nt(qs, kcv) - lse)
            dsw = pw * (_dot_nt(dov, vw) - delta)
            dsc = pc * (_dot_nt(dov, vcv) - delta)
            dslab_ref[...] += dsw
            dsw, dsc = dsw.astype(BF16), dsc.astype(BF16)
            dq_ref[blk, :] = (_dot(dsw, kw) + _dot(dsc, kcv)) * SCALE
            dk_ref[pl.ds(off, NA_KW), :] += _dot_tn(dsw, qs)
            dv_ref[pl.ds(off, NA_KW), :] += _dot_tn(pw.astype(BF16), dov)
            dkc_ref[...] += _dot_tn(dsc, qs)
            dvc_ref[...] += _dot_tn(pc.astype(BF16), dov)

    qblk = pl.BlockSpec((None, 2 * NA_QB, dh), lambda a, j: (a, j, 0))
    kv = pl.BlockSpec((None, s, dh), lambda a, j: (a, 0, 0))
    ckv = pl.BlockSpec((None, n_ctx, dh), lambda a, j: (a, 0, 0))
    slab_spec = lambda sub: pl.BlockSpec((None, None, NA_QB, NA_KW), lambda a, j: (_na_type(2 * j + sub, nb), a, 0, 0))
    dslab0, dslab1 = jax.ShapeDtypeStruct(slab.shape, F32), jax.ShapeDtypeStruct(slab.shape, F32)
    dq, dk, dv, dkc, dvc, d0, d1 = pl.pallas_call(
        body, grid=(h, half),
        in_specs=[qblk, kv, kv, ckv, ckv, slab_spec(0), slab_spec(1), qblk, qblk,
                  pl.BlockSpec((None, 2 * NA_QB, 1), lambda a, j: (a, j, 0))],
        out_specs=[qblk, kv, kv, ckv, ckv, slab_spec(0), slab_spec(1)],
        out_shape=[jax.ShapeDtypeStruct((h, s, dh), F32), jax.ShapeDtypeStruct((h, s, dh), F32),
                   jax.ShapeDtypeStruct((h, s, dh), F32), jax.ShapeDtypeStruct((h, n_ctx, dh), F32),
                   jax.ShapeDtypeStruct((h, n_ctx, dh), F32), dslab0, dslab1],
        compiler_params=pltpu.CompilerParams(dimension_semantics=("arbitrary", "arbitrary"), vmem_limit_bytes=VMEM_BIG_LIMIT),
        name="na_bwd",
    )(q, k, v, kc, vc, slab, slab, do, o, lse)
    return dq, dk, dv, dkc, dvc, jnp.stack([d0[0], d0[1] + d1[1], d1[2]])


@jax.custom_vjp
def na_attn(q, k, v, kc, vc, slab):
    return _na_fwd_call(q.astype(BF16), k.astype(BF16), v.astype(BF16), kc.astype(BF16), vc.astype(BF16), slab)[0]


def _na_attn_fwd(q, k, v, kc, vc, slab):
    qb, kb, vb, kcb, vcb = (t.astype(BF16) for t in (q, k, v, kc, vc))
    o, lse = _na_fwd_call(qb, kb, vb, kcb, vcb, slab)
    return o, (qb, kb, vb, kcb, vcb, slab, o, lse)


def _na_attn_bwd(res, do):
    qb, kb, vb, kcb, vcb, slab, o, lse = res
    return tuple(_na_bwd_call(qb, kb, vb, kcb, vcb, slab, do.astype(BF16), o, lse))


na_attn.defvjp(_na_attn_fwd, _na_attn_bwd)


def _na_tables(rows):
    col = np.arange(GRID_W)
    c_start = np.clip(col - NA_WIN_W // 2, 0, GRID_W - NA_WIN_W)
    in_win = (col[None, :] >= c_start[:, None]) & (col[None, :] < c_start[:, None] + NA_WIN_W)
    dc_idx = np.clip(col[None, :] - col[:, None], -(NA_WIN_W - 1), NA_WIN_W - 1) + NA_WIN_W - 1
    onehot = (in_win[:, :, None] & (dc_idx[:, :, None] == np.arange(2 * NA_WIN_W - 1)[None, None, :])).astype(np.float32)
    negcol = np.where(in_win, 0.0, NEG_BIG).astype(np.float32)
    plan = []
    for r0, ks in [(0, 0), (NA_QROWS, 0), (rows - NA_QROWS, rows - NA_KROWS)]:
        per_row = []
        for a in range(NA_QROWS):
            r = r0 + a
            rs = min(max(r - NA_WIN_H // 2, 0), rows - NA_WIN_H)
            valid = np.array([rs <= ks + b < rs + NA_WIN_H for b in range(NA_KROWS)])
            per_row.append((ks - r + NA_WIN_H - 1, valid))
        plan.append(per_row)
    return jnp.asarray(onehot), jnp.asarray(negcol), plan


def _na_slab(rpb, tables):
    onehot, negcol, plan = tables
    n_dr = 2 * NA_WIN_H - 1
    table = jnp.einsum("hdc,wuc->hwdu", rpb, onehot, precision=HIGHEST) + negcol[None, :, None, :]
    types = []
    for per_row in plan:
        slabs = []
        for first, valid in per_row:
            lo, hi = max(first, 0), min(first + NA_KROWS, n_dr)
            sel = jnp.pad(table[:, :, lo:hi, :], ((0, 0), (0, 0), (lo - first, first + NA_KROWS - hi), (0, 0)),
                          constant_values=NEG_BIG)
            sel = jnp.where(jnp.asarray(valid)[None, None, :, None], sel, NEG_BIG)
            slabs.append(sel.reshape(NA_HEADS, GRID_W, NA_KW))
        types.append(jnp.stack(slabs, axis=1).reshape(NA_HEADS, NA_QB, NA_KW))
    return jnp.stack(types)


def _loss_call(y, target):
    s, d = y.shape
    tt = _pick(s, (512, 256, 128))

    def body(y_ref, t_ref, loss_ref, diff_ref):
        @pl.when(pl.program_id(0) == 0)
        def _():
            loss_ref[...] = jnp.zeros_like(loss_ref)

        e = y_ref[...] - t_ref[...]
        diff_ref[...] = e * (1.0 / d)
        loss_ref[...] += 0.5 * jnp.sum(jnp.mean(e * e, axis=-1, keepdims=True), axis=0, keepdims=True)

    row = pl.BlockSpec((tt, d), lambda i: (i, 0))
    return pl.pallas_call(
        body, grid=(s // tt,), in_specs=[row, row], out_specs=[pl.BlockSpec((1, 1), lambda i: (0, 0)), row],
        out_shape=[jax.ShapeDtypeStruct((1, 1), F32), jax.ShapeDtypeStruct((s, d), F32)],
        compiler_params=pltpu.CompilerParams(dimension_semantics=("arbitrary",)), name="loss_head",
    )(y, target)


@jax.custom_vjp
def loss_head(y, target):
    return _loss_call(y, target)[0][0, 0]


def _loss_head_fwd(y, target):
    loss, diff = _loss_call(y, target)
    return loss[0, 0], diff


def _loss_head_bwd(diff, g):
    return diff * g, jnp.zeros_like(diff)


loss_head.defvjp(_loss_head_fwd, _loss_head_bwd)


def _sum8_call(parts):
    _, r, c = parts.shape
    tr = _pick(r, (328, 256, 128))

    def body(p_ref, g_ref):
        g = p_ref[0].astype(F32)
        for s in range(1, N_DEV):
            g = g + p_ref[s].astype(F32)
        g_ref[...] = g

    return pl.pallas_call(
        body, grid=(r // tr,), in_specs=[pl.BlockSpec((N_DEV, tr, c), lambda i: (0, i, 0))],
        out_specs=pl.BlockSpec((tr, c), lambda i: (i, 0)), out_shape=jax.ShapeDtypeStruct((r, c), F32),
        compiler_params=pltpu.CompilerParams(dimension_semantics=("parallel",)), name="sum8",
    )(parts)


def _adam_call(g, w, m, v):
    r, c = g.shape
    tr = _pick(r, (256, 128))

    def body(g_ref, w_ref, m_ref, v_ref, d_ref, nm_ref, nv_ref):
        gv = g_ref[...]
        mn = ADAM_B1 * m_ref[...] + (1.0 - ADAM_B1) * gv
        vn = ADAM_B2 * v_ref[...] + (1.0 - ADAM_B2) * (gv * gv)
        m_hat = mn / (1.0 - ADAM_B1 ** ADAM_STEP)
        v_hat = vn / (1.0 - ADAM_B2 ** ADAM_STEP)
        d_ref[...] = -ADAM_LR * (m_hat / (jnp.sqrt(v_hat) + ADAM_EPS) + ADAM_WD * w_ref[...])
        nm_ref[...] = mn
        nv_ref[...] = vn

    row = pl.BlockSpec((tr, c), lambda i: (i, 0))
    out = jax.ShapeDtypeStruct((r, c), F32)
    return pl.pallas_call(
        body, grid=(r // tr,), in_specs=[row, row, row, row], out_specs=[row, row, row], out_shape=[out, out, out],
        compiler_params=pltpu.CompilerParams(dimension_semantics=("parallel",)), name="adam",
    )(g, w, m, v)


def _pack_small(vals, extra=None):
    flat = [vals[n].reshape(-1) for n in SMALL] + ([] if extra is None else [extra.reshape(-1)])
    flat = jnp.concatenate(flat)
    rows = -(-flat.shape[0] // D_MODEL)
    rows = -(-rows // 8) * 8
    return jnp.pad(flat, (0, rows * D_MODEL - flat.shape[0])).reshape(rows, D_MODEL)


def _unpack_small(packed, like):
    flat, out, off = packed.reshape(-1), {}, 0
    for n in SMALL:
        size = like[n].size
        out[n] = flat[off:off + size].reshape(like[n].shape)
        off += size
    return out, flat[off]


def _rope_tables(s, n_ctx):
    t = jnp.arange(s)
    row = (t // GRID_W).astype(F32)
    col = (t % GRID_W).astype(F32)
    half = HEAD_DIM // 2
    inv = ROPE_THETA ** (-jnp.arange(0, half, 2, dtype=F32) / half)
    ang = jnp.concatenate([row[:, None] * inv, col[:, None] * inv], axis=-1)
    cos2 = jnp.repeat(jnp.cos(ang), 2, axis=-1)
    sin2 = jnp.repeat(jnp.sin(ang), 2, axis=-1)
    cos2 = jnp.concatenate([cos2, jnp.ones((n_ctx, HEAD_DIM), F32)], axis=0)
    sin2 = jnp.concatenate([sin2, jnp.zeros((n_ctx, HEAD_DIM), F32)], axis=0)
    return cos2, sin2


def _to_heads(t, n_heads):
    return t.reshape(t.shape[0], n_heads, HEAD_DIM).transpose(1, 0, 2)


def _from_heads(t):
    return t.transpose(1, 0, 2).reshape(t.shape[1], t.shape[0] * HEAD_DIM)


def _local_loss(shards, p, x, c, ctx, target, consts):
    s, n_ctx = x.shape[0], ctx.shape[0]
    cos2, sin2, rot, na_tables = consts
    is_ctx = (jnp.arange(s + n_ctx) >= s)[:, None]
    seg = lambda rows2: jnp.where(is_ctx, rows2[1:2], rows2[0:1])
    xa = jnp.concatenate([x, ctx], axis=0)
    cond = jnp.concatenate([jax.nn.silu(c), jax.nn.silu(p["c_ctx"])[None, :],
                            jnp.zeros((MOD_ROWS - 2, D_MODEL), F32)], axis=0).astype(BF16)
    layer_shards = lambda l: {n: shards[n][l] for n in BIG}
    w = gather_layer(layer_shards(0))

    for l in range(DEPTH):
        mod = (mm(cond, w["w_mod"]) + p["b_mod"][l])[:2]
        sh1, sc1, g1, sh2, sc2, g2 = jnp.split(mod, 6, axis=-1)

        h = norm_mod(xa, p["norm1"][l][None, :], sh1, sc1, s, BF16)
        na_q, na_k, na_v, gq, gk, gv, ga, gb = jnp.split(mm(h, w["w_in"]), IN_SPLITS, axis=-1)

        qa, ka, va = _to_heads(na_q, NA_HEADS), _to_heads(na_k, NA_HEADS), _to_heads(na_v, NA_HEADS)
        slab = _na_slab(p["na_rpb"][l], na_tables)
        ya_lat = na_attn(qa[:, :s], ka[:, :s], va[:, :s], ka[:, s:], va[:, s:], slab)
        ya_ctx = attn(qa[:, s:], ka[:, s:], va[:, s:])
        ya = _from_heads(jnp.concatenate([ya_lat, ya_ctx], axis=1))

        qk = _to_heads(jnp.concatenate([gq, gk], axis=-1), GQA_Q_HEADS + GQA_KV_HEADS)
        gains = jnp.stack([p["q_gain"][l], p["k_gain"][l]])
        qk = hn_rope(qk, gains, cos2, sin2, rot, GQA_Q_HEADS)
        qb, kb, vb = qk[:GQA_Q_HEADS], qk[GQA_Q_HEADS:], _to_heads(gv, GQA_KV_HEADS)
        q_lat = qb[:, :s].reshape(GQA_KV_HEADS, GQA_REP * s, HEAD_DIM)
        if l + 1 < DEPTH:
            ob_lat, w_next = attn_gather(q_lat, kb, vb, layer_shards(l + 1))
        else:
            ob_lat, w_next = attn(q_lat, kb, vb), None
        ob_ctx = attn(qb[:, s:].reshape(GQA_KV_HEADS, GQA_REP * n_ctx, HEAD_DIM), kb[:, s:], vb[:, s:])
        yb = _from_heads(jnp.concatenate([ob_lat.reshape(GQA_Q_HEADS, s, HEAD_DIM),
                                          ob_ctx.reshape(GQA_Q_HEADS, n_ctx, HEAD_DIM)], axis=1))

        merged = jax.nn.sigmoid(ga) * mm(ya.astype(BF16), w["w_pa"]) + jax.nn.sigmoid(gb) * mm(yb.astype(BF16), w["w_pb"])
        xa = xa + seg(g1) * mm(merged.astype(BF16), w["w_o"])

        h2 = norm_mod(xa, p["norm2"][l][None, :], sh2, sc2, s, BF16)
        xa = xa + seg(g2) * ffn(h2, w["w_ffn_in"], w["w_ffn_out"])
        w = w_next

    zeros2 = jnp.zeros((2, D_MODEL), F32)
    y = norm_mod(xa, p["final_norm"][None, :], zeros2, zeros2, s, F32)[:s]
    return loss_head(y, target)


def kernel(x, c, ctx, c_ctx, w_mod, b_mod, norm1, w_in, na_rpb, q_gain, k_gain, w_pa, w_pb, w_o, norm2, w_ffn_in, w_ffn_out, final_norm, loss_target, m_c_ctx, m_w_mod, m_b_mod, m_norm1, m_w_in, m_na_rpb, m_q_gain, m_k_gain, m_w_pa, m_w_pb, m_w_o, m_norm2, m_w_ffn_in, m_w_ffn_out, m_final_norm, v_c_ctx, v_w_mod, v_b_mod, v_norm1, v_w_in, v_na_rpb, v_q_gain, v_k_gain, v_w_pa, v_w_pb, v_w_o, v_norm2, v_w_ffn_in, v_w_ffn_out, v_final_norm):
    w = dict(c_ctx=c_ctx, w_mod=w_mod, b_mod=b_mod, norm1=norm1, w_in=w_in, na_rpb=na_rpb, q_gain=q_gain, k_gain=k_gain,
             w_pa=w_pa, w_pb=w_pb, w_o=w_o, norm2=norm2, w_ffn_in=w_ffn_in, w_ffn_out=w_ffn_out, final_norm=final_norm)
    mom = dict(c_ctx=m_c_ctx, w_mod=m_w_mod, b_mod=m_b_mod, norm1=m_norm1, w_in=m_w_in, na_rpb=m_na_rpb, q_gain=m_q_gain,
               k_gain=m_k_gain, w_pa=m_w_pa, w_pb=m_w_pb, w_o=m_w_o, norm2=m_norm2, w_ffn_in=m_w_ffn_in,
               w_ffn_out=m_w_ffn_out, final_norm=m_final_norm)
    var = dict(c_ctx=v_c_ctx, w_mod=v_w_mod, b_mod=v_b_mod, norm1=v_norm1, w_in=v_w_in, na_rpb=v_na_rpb, q_gain=v_q_gain,
               k_gain=v_k_gain, w_pa=v_w_pa, w_pb=v_w_pb, w_o=v_w_o, norm2=v_norm2, w_ffn_in=v_w_ffn_in,
               w_ffn_out=v_w_ffn_out, final_norm=v_final_norm)
    s, n_ctx = x.shape[1], ctx.shape[1]
    depth = w_mod.shape[0]

    shards = {n: [w[n][l] for l in range(depth)] for n in BIG}
    small = {n: w[n] for n in SMALL}
    consts = (*_rope_tables(s, n_ctx), _rot_matrix(), _na_tables(s // GRID_W))
    loss, (g_shards, g_small, gx) = jax.value_and_grad(_local_loss, argnums=(0, 1, 2))(
        shards, small, x[0], c, ctx[0], loss_target[0], consts)

    parts_s = _exchange(_pack_small(g_small, loss), True, "gather_small_grads")
    g_packed = _sum8_call(parts_s)
    grads, loss = _unpack_small(g_packed, w)
    zero = jnp.zeros((1,), F32)
    upd_s = _adam_call(g_packed, _pack_small(w, zero), _pack_small(mom, zero), _pack_small(var, zero))
    outs = [grads] + [_unpack_small(u, w)[0] for u in upd_s]

    for n in BIG:
        flat = lambda t: t.reshape(-1, t.shape[-1])
        g = jnp.stack(g_shards[n])
        upd = _adam_call(flat(g), flat(w[n]), flat(mom[n]), flat(var[n]))
        for k, t in enumerate([g] + [u.reshape(w[n].shape) for u in upd]):
            outs[k][n] = t
    return (loss, gx[None], *[o[n] for o in outs for n in WEIGHTS])
```

```python
import functools

import numpy as np
import jax
import jax.numpy as jnp
from jax import lax
from jax.experimental import pallas as pl
from jax.experimental.pallas import tpu as pltpu

F32 = jnp.float32
BF16 = jnp.bfloat16
HIGHEST = lax.Precision.HIGHEST
ROT_PRECISION = lax.Precision.HIGH

D_MODEL = 1024
DEPTH = 4
GRID_W = 64
HEAD_DIM = 64
NA_HEADS = 8
NA_WIN_H = 8
NA_WIN_W = 16
GQA_Q_HEADS = 8
GQA_KV_HEADS = 2
GQA_REP = GQA_Q_HEADS // GQA_KV_HEADS
NA_WIDTH = NA_HEADS * HEAD_DIM
GQA_Q_WIDTH = GQA_Q_HEADS * HEAD_DIM
GQA_KV_WIDTH = GQA_KV_HEADS * HEAD_DIM
IN_SIZES = (NA_WIDTH, NA_WIDTH, NA_WIDTH, GQA_Q_WIDTH, GQA_KV_WIDTH, GQA_KV_WIDTH, D_MODEL, D_MODEL)
IN_SPLITS = tuple(int(v) for v in np.cumsum(IN_SIZES)[:-1])
ROPE_THETA = 10000.0
EPS = 1e-6
SCALE = HEAD_DIM ** -0.5
ADAM_LR = 0.001
ADAM_B1 = 0.9
ADAM_B2 = 0.999
ADAM_EPS = 1e-08
ADAM_WD = 0.01
ADAM_STEP = 10

N_DEV = 8
MESH = pl.DeviceIdType.MESH
NEG_BIG = -1e30

VMEM_BIG_LIMIT = 52 * 1024 * 1024
NA_QROWS = 4
NA_QB = NA_QROWS * GRID_W
NA_KROWS = 12
NA_KW = NA_KROWS * GRID_W
NA_FWD_HEADS = 4
MM_TILE_ELEMS = 768 * 2176
MM_OPERAND_ELEMS = 1408 * 2176
MOD_ROWS = 256

BIG = ("w_mod", "w_in", "w_pa", "w_pb", "w_o", "w_ffn_in", "w_ffn_out")
ROW_SHARDED = ("w_o", "w_ffn_out")
SMALL = ("c_ctx", "b_mod", "norm1", "na_rpb", "q_gain", "k_gain", "norm2", "final_norm")
WEIGHTS = ("c_ctx", "w_mod", "b_mod", "norm1", "w_in", "na_rpb", "q_gain", "k_gain", "w_pa", "w_pb", "w_o",
           "norm2", "w_ffn_in", "w_ffn_out", "final_norm")


def _pick(n, cands):
    for c in cands:
        if n % c == 0:
            return c
    return n


def _dot_nt(a, b):
    return lax.dot_general(a, b, (((1,), (1,)), ((), ())), preferred_element_type=F32)


def _dot_tn(a, b):
    return lax.dot_general(a, b, (((0,), (0,)), ((), ())), preferred_element_type=F32)


def _dot(a, b):
    return jnp.dot(a, b, preferred_element_type=F32)


def _mm_call(a, b, name, mode="nn", out_dtype=F32, b_lead=None):
    (m, k) = a.shape if mode != "tn" else a.shape[::-1]
    n = b.shape[-1] if mode != "nt" else b.shape[-2]
    tn = _pick(n, (2176, 1408, 1024, 512, 256, 128))
    tm = _pick(m, tuple(t for t in (768, 512, 256, 128) if t * tn <= MM_TILE_ELEMS))
    tk = _pick(k, tuple(t for t in (2816, 2176, 1408, 1024, 768, 512, 256, 128) if t * max(tm, tn) <= MM_OPERAND_ELEMS))
    nk = k // tk
    dot = {"nn": _dot, "tn": _dot_tn, "nt": _dot_nt}[mode]
    a_blk = (tk, tm) if mode == "tn" else (tm, tk)
    b_blk = (tn, tk) if mode == "nt" else (tk, tn)
    a_idx = (lambda i, kk: (kk, i)) if mode == "tn" else (lambda i, kk: (i, kk))
    b_idx = (lambda j, kk: (j, kk)) if mode == "nt" else (lambda j, kk: (kk, j))
    if b_lead is not None:
        b_blk, b_idx2 = (None,) + b_blk, b_idx
        b_idx = lambda j, kk: (b_lead,) + b_idx2(j, kk)

    def body(a_ref, b_ref, o_ref, acc_ref):
        kk = pl.program_id(2)
        part = dot(a_ref[...], b_ref[...])

        @pl.when(kk == 0)
        def _():
            acc_ref[...] = part

        @pl.when(kk > 0)
        def _():
            acc_ref[...] += part

        @pl.when(kk == nk - 1)
        def _():
            o_ref[...] = acc_ref[...].astype(o_ref.dtype)

    def body1(a_ref, b_ref, o_ref):
        o_ref[...] = dot(a_ref[...], b_ref[...]).astype(o_ref.dtype)

    footprint = 2 * (tm * tk * 2 + tk * tn * 2 + tm * tn * 4) + 2 * tm * tn * 4
    limit = int(footprint + (8 << 20))
    if nk == 1:
        return pl.pallas_call(
            body1, grid=(n // tn, m // tm),
            in_specs=[pl.BlockSpec(a_blk, lambda j, i: a_idx(i, 0)), pl.BlockSpec(b_blk, lambda j, i: b_idx(j, 0))],
            out_specs=pl.BlockSpec((tm, tn), lambda j, i: (i, j)),
            out_shape=jax.ShapeDtypeStruct((m, n), out_dtype),
            compiler_params=pltpu.CompilerParams(dimension_semantics=("parallel", "parallel"), vmem_limit_bytes=limit),
            name=name,
        )(a, b)
    return pl.pallas_call(
        body, grid=(m // tm, n // tn, nk),
        in_specs=[pl.BlockSpec(a_blk, lambda i, j, kk: a_idx(i, kk)), pl.BlockSpec(b_blk, lambda i, j, kk: b_idx(j, kk))],
        out_specs=pl.BlockSpec((tm, tn), lambda i, j, kk: (i, j)),
        out_shape=jax.ShapeDtypeStruct((m, n), out_dtype),
        scratch_shapes=[pltpu.VMEM((tm, tn), F32)],
        compiler_params=pltpu.CompilerParams(
            dimension_semantics=("parallel", "parallel", "arbitrary"), vmem_limit_bytes=limit),
        name=name,
    )(a, b)


@jax.custom_vjp
def mm(x, w):
    return _mm_call(x, w, "mm_fwd")


def _mm_fwd(x, w):
    return _mm_call(x, w, "mm_fwd"), (x, w)


def _mm_bwd(res, dy):
    x, w = res
    dyb = dy.astype(BF16)
    return _mm_call(dyb, w, "mm_dx", "nt", BF16), _mm_call(x, dyb, "mm_dw", "tn", BF16)


mm.defvjp(_mm_fwd, _mm_bwd)


def _ffn_tiles(t, f):
    return _pick(t, (384, 256, 128)), _pick(f, (1408, 1024, 512, 256, 128))


def _ffn_in_call(h, w_in):
    t, d = h.shape
    f = w_in.shape[1] // 2
    tm, tn = _ffn_tiles(t, f)
    nf = f // tn

    def body(h_ref, wa_ref, wu_ref, act_ref, au_ref):
        hv = h_ref[...]
        a, u = _dot(hv, wa_ref[...]), _dot(hv, wu_ref[...])
        au_ref[0] = a
        au_ref[1] = u
        act_ref[...] = (a * jax.nn.sigmoid(a) * u).astype(act_ref.dtype)

    return pl.pallas_call(
        body, grid=(nf, t // tm),
        in_specs=[pl.BlockSpec((tm, d), lambda j, i: (i, 0)), pl.BlockSpec((d, tn), lambda j, i: (0, j)),
                  pl.BlockSpec((d, tn), lambda j, i: (0, j + nf))],
        out_specs=[pl.BlockSpec((tm, tn), lambda j, i: (i, j)), pl.BlockSpec((2, tm, tn), lambda j, i: (0, i, j))],
        out_shape=[jax.ShapeDtypeStruct((t, f), BF16), jax.ShapeDtypeStruct((2, t, f), F32)],
        compiler_params=pltpu.CompilerParams(dimension_semantics=("parallel", "parallel"), vmem_limit_bytes=VMEM_BIG_LIMIT),
        name="ffn_in",
    )(h, w_in, w_in)


def _ffn_dact_call(dy, w_out, au):
    t, d = dy.shape
    f = w_out.shape[0]
    tm, tn = _ffn_tiles(t, f)

    def body(dy_ref, w_ref, au_ref, dau_ref):
        dact = _dot_nt(dy_ref[...], w_ref[...])
        a, u = au_ref[0], au_ref[1]
        sg = jax.nn.sigmoid(a)
        dau_ref[0] = (dact * u * (sg * (1.0 + a * (1.0 - sg)))).astype(dau_ref.dtype)
        dau_ref[1] = (dact * (a * sg)).astype(dau_ref.dtype)

    halves = pl.BlockSpec((2, tm, tn), lambda j, i: (0, i, j))
    return pl.pallas_call(
        body, grid=(f // tn, t // tm),
        in_specs=[pl.BlockSpec((tm, d), lambda j, i: (i, 0)), pl.BlockSpec((tn, d), lambda j, i: (j, 0)), halves],
        out_specs=halves, out_shape=jax.ShapeDtypeStruct((2, t, f), BF16),
        compiler_params=pltpu.CompilerParams(dimension_semantics=("parallel", "parallel"), vmem_limit_bytes=VMEM_BIG_LIMIT),
        name="ffn_dact",
    )(dy, w_out, au)


def _ffn_dh_call(dau, w_in):
    _, t, f = dau.shape
    d = w_in.shape[0]
    tm, tk = _ffn_tiles(t, f)
    nf = f // tk

    def body(a_ref, b_ref, o_ref):
        acc = None
        for c in range(2 * nf):
            part = _dot_nt(a_ref[c // nf, :, (c % nf) * tk:(c % nf + 1) * tk], b_ref[:, c * tk:(c + 1) * tk])
            acc = part if acc is None else acc + part
        o_ref[...] = acc.astype(o_ref.dtype)

    return pl.pallas_call(
        body, grid=(t // tm,),
        in_specs=[pl.BlockSpec((2, tm, f), lambda i: (0, i, 0)), pl.BlockSpec((d, 2 * f), lambda i: (0, 0))],
        out_specs=pl.BlockSpec((tm, d), lambda i: (i, 0)), out_shape=jax.ShapeDtypeStruct((t, d), BF16),
        compiler_params=pltpu.CompilerParams(dimension_semantics=("parallel",), vmem_limit_bytes=VMEM_BIG_LIMIT),
        name="ffn_dh",
    )(dau, w_in)


@jax.custom_vjp
def ffn(h, w_in, w_out):
    return _ffn_fwd(h, w_in, w_out)[0]


def _ffn_fwd(h, w_in, w_out):
    act, au = _ffn_in_call(h, w_in)
    return _mm_call(act, w_out, "mm_fwd"), (h, w_in, w_out, act, au)


def _ffn_bwd(res, dy):
    h, w_in, w_out, act, au = res
    dyb = dy.astype(BF16)
    dau = _ffn_dact_call(dyb, w_out, au)
    dw_in = jnp.concatenate([_mm_call(h, dau, "mm_dw", "tn", BF16, b_lead=c) for c in range(2)], axis=1)
    return _ffn_dh_call(dau, w_in), dw_in, _mm_call(act, dyb, "mm_dw", "tn", BF16)


ffn.defvjp(_ffn_fwd, _ffn_bwd)


NORM_ROWS = (768, 512, 256, 128)


def _seg_rows(ref, is_ctx):
    return jnp.where(is_ctx, ref[1:2, :], ref[0:1, :])


def _ctx_rows(tt, n_lat):
    return pl.program_id(0) * tt + lax.broadcasted_iota(jnp.int32, (tt, 1), 0) >= n_lat


def _one_hot_row(second):
    return (lax.broadcasted_iota(jnp.int32, (2, 1), 0) == second.astype(jnp.int32)).astype(F32)


def _norm_mod_fwd_call(x, g, shift, scale, n_lat, out_dtype):
    t, d = x.shape
    tt = _pick(t, NORM_ROWS)

    def body(x_ref, g_ref, sh_ref, sc_ref, y_ref):
        is_ctx = _ctx_rows(tt, n_lat)
        xv = x_ref[...]
        r = lax.rsqrt(jnp.mean(xv * xv, axis=-1, keepdims=True) + EPS)
        yn = xv * r * g_ref[...]
        y_ref[...] = (yn * (1.0 + _seg_rows(sc_ref, is_ctx)) + _seg_rows(sh_ref, is_ctx)).astype(y_ref.dtype)

    row = pl.BlockSpec((tt, d), lambda i: (i, 0))
    full = lambda rws: pl.BlockSpec((rws, d), lambda i: (0, 0))
    return pl.pallas_call(
        body, grid=(t // tt,), in_specs=[row, full(1), full(2), full(2)], out_specs=row,
        out_shape=jax.ShapeDtypeStruct((t, d), out_dtype),
        compiler_params=pltpu.CompilerParams(dimension_semantics=("parallel",)), name="norm_mod_fwd",
    )(x, g, shift, scale)


def _norm_mod_bwd_call(x, g, scale, dy, n_lat):
    t, d = x.shape
    tt = _pick(t, NORM_ROWS)

    def body(x_ref, g_ref, sc_ref, dy_ref, dx_ref, dg_ref, dsh_ref, dsc_ref):
        i = pl.program_id(0)
        is_ctx = _ctx_rows(tt, n_lat)

        @pl.when(i == 0)
        def _():
            dg_ref[...] = jnp.zeros_like(dg_ref)
            dsh_ref[...] = jnp.zeros_like(dsh_ref)
            dsc_ref[...] = jnp.zeros_like(dsc_ref)

        xv, dyv, gv = x_ref[...], dy_ref[...].astype(F32), g_ref[...]
        r = lax.rsqrt(jnp.mean(xv * xv, axis=-1, keepdims=True) + EPS)
        nrm = xv * r
        yn = nrm * gv
        for acc_ref, term in ((dsh_ref, dyv), (dsc_ref, dyv * yn)):
            both = jnp.sum(term, axis=0, keepdims=True)
            ctx = jnp.sum(jnp.where(is_ctx, term, 0.0), axis=0, keepdims=True)
            acc_ref[0:1, :] += both - ctx
            acc_ref[1:2, :] += ctx
        dyn = dyv * (1.0 + _seg_rows(sc_ref, is_ctx))
        dg_ref[...] += jnp.sum(dyn * nrm, axis=0, keepdims=True)
        dn = dyn * gv
        dx_ref[...] = r * (dn - nrm * jnp.mean(dn * nrm, axis=-1, keepdims=True))

    row = pl.BlockSpec((tt, d), lambda i: (i, 0))
    full = lambda rws: pl.BlockSpec((rws, d), lambda i: (0, 0))
    return pl.pallas_call(
        body, grid=(t // tt,), in_specs=[row, full(1), full(2), row], out_specs=[row, full(1), full(2), full(2)],
        out_shape=[jax.ShapeDtypeStruct((t, d), F32), jax.ShapeDtypeStruct((1, d), F32),
                   jax.ShapeDtypeStruct((2, d), F32), jax.ShapeDtypeStruct((2, d), F32)],
        compiler_params=pltpu.CompilerParams(dimension_semantics=("arbitrary",)), name="norm_mod_bwd",
    )(x, g, scale, dy)


@functools.partial(jax.custom_vjp, nondiff_argnums=(4, 5))
def norm_mod(x, g, shift, scale, n_lat, out_dtype):
    return _norm_mod_fwd_call(x, g, shift, scale, n_lat, out_dtype)


def _norm_mod_fwd(x, g, shift, scale, n_lat, out_dtype):
    return _norm_mod_fwd_call(x, g, shift, scale, n_lat, out_dtype), (x, g, scale)


def _norm_mod_bwd(n_lat, out_dtype, res, dy):
    x, g, scale = res
    dx, dg, dsh, dsc = _norm_mod_bwd_call(x, g, scale, dy, n_lat)
    return dx, dg, dsh, dsc


norm_mod.defvjp(_norm_mod_fwd, _norm_mod_bwd)


def _rot_matrix():
    p = np.zeros((HEAD_DIM, HEAD_DIM), np.float32)
    for i in range(HEAD_DIM // 2):
        p[2 * i + 1, 2 * i] = -1.0
        p[2 * i, 2 * i + 1] = 1.0
    return jnp.asarray(p)


def _hn_rope_fwd_call(x, gains, cos2, sin2, rot, n_q):
    nh, t, dh = x.shape
    tt = _pick(t, (2816, 1024, 768, 512, 256, 128))

    def body(x_ref, g_ref, cos_ref, sin_ref, rot_ref, y_ref):
        gv = jnp.where(pl.program_id(0) >= n_q, g_ref[1:2, :], g_ref[0:1, :])
        xv = x_ref[...]
        r = lax.rsqrt(jnp.mean(xv * xv, axis=-1, keepdims=True) + EPS)
        y = xv * r * gv
        yr = jnp.dot(y, rot_ref[...], precision=ROT_PRECISION, preferred_element_type=F32)
        y_ref[...] = y * cos_ref[...] + yr * sin_ref[...]

    blk = pl.BlockSpec((None, tt, dh), lambda h, i: (h, i, 0))
    tab = pl.BlockSpec((tt, dh), lambda h, i: (i, 0))
    return pl.pallas_call(
        body, grid=(nh, t // tt),
        in_specs=[blk, pl.BlockSpec((2, dh), lambda h, i: (0, 0)), tab, tab, pl.BlockSpec((dh, dh), lambda h, i: (0, 0))],
        out_specs=blk, out_shape=jax.ShapeDtypeStruct((nh, t, dh), F32),
        compiler_params=pltpu.CompilerParams(dimension_semantics=("parallel", "parallel")), name="hn_rope_fwd",
    )(x, gains, cos2, sin2, rot)


def _hn_rope_bwd_call(x, gains, cos2, sin2, rot, dy, n_q):
    nh, t, dh = x.shape
    tt = _pick(t, (2816, 1024, 768, 512, 256, 128))

    def body(x_ref, g_ref, cos_ref, sin_ref, rot_ref, dy_ref, dx_ref, dg_ref):
        h, i = pl.program_id(0), pl.program_id(1)
        is_k = h >= n_q

        @pl.when((h == 0) & (i == 0))
        def _():
            dg_ref[...] = jnp.zeros_like(dg_ref)

        gv = jnp.where(is_k, g_ref[1:2, :], g_ref[0:1, :])
        xv, dyv = x_ref[...], dy_ref[...]
        r = lax.rsqrt(jnp.mean(xv * xv, axis=-1, keepdims=True) + EPS)
        nrm = xv * r
        dyn = dyv * cos_ref[...] - jnp.dot(dyv * sin_ref[...], rot_ref[...], precision=ROT_PRECISION, preferred_element_type=F32)
        seg = _one_hot_row(is_k)
        dg_ref[...] += seg * jnp.sum(dyn * nrm, axis=0, keepdims=True)
        dn = dyn * gv
        dx_ref[...] = r * (dn - nrm * jnp.mean(dn * nrm, axis=-1, keepdims=True))

    blk = pl.BlockSpec((None, tt, dh), lambda h, i: (h, i, 0))
    tab = pl.BlockSpec((tt, dh), lambda h, i: (i, 0))
    g_spec = pl.BlockSpec((2, dh), lambda h, i: (0, 0))
    return pl.pallas_call(
        body, grid=(nh, t // tt),
        in_specs=[blk, g_spec, tab, tab, pl.BlockSpec((dh, dh), lambda h, i: (0, 0)), blk],
        out_specs=[blk, g_spec],
        out_shape=[jax.ShapeDtypeStruct((nh, t, dh), F32), jax.ShapeDtypeStruct((2, dh), F32)],
        compiler_params=pltpu.CompilerParams(dimension_semantics=("arbitrary", "arbitrary")), name="hn_rope_bwd",
    )(x, gains, cos2, sin2, rot, dy)


@functools.partial(jax.custom_vjp, nondiff_argnums=(5,))
def hn_rope(x, gains, cos2, sin2, rot, n_q):
    return _hn_rope_fwd_call(x, gains, cos2, sin2, rot, n_q)


def _hn_rope_fwd(x, gains, cos2, sin2, rot, n_q):
    return _hn_rope_fwd_call(x, gains, cos2, sin2, rot, n_q), (x, gains, cos2, sin2, rot)


def _hn_rope_bwd(n_q, res, dy):
    x, gains, cos2, sin2, rot = res
    dx, dg = _hn_rope_bwd_call(x, gains, cos2, sin2, rot, dy, n_q)
    return dx, dg, jnp.zeros_like(cos2), jnp.zeros_like(sin2), jnp.zeros_like(rot)


hn_rope.defvjp(_hn_rope_fwd, _hn_rope_bwd)


EXCHANGE_SCRATCH = [pltpu.SemaphoreType.DMA((N_DEV - 1,)), pltpu.SemaphoreType.DMA((N_DEV - 1,)), pltpu.SemaphoreType.DMA(())]


def _exchange_copies(x_ref, out_ref, send_sems, recv_sems, local_sem, all_gather):
    mx, my, mc = lax.axis_index("x"), lax.axis_index("y"), lax.axis_index("c")
    me = 4 * mx + 2 * my + mc
    src = (lambda p: x_ref) if all_gather else (lambda p: x_ref.at[p])
    local = pltpu.make_async_copy(src(me), out_ref.at[me], local_sem)
    remote = []
    for rel in range(1, N_DEV):
        px, py, pc = mx ^ (rel >> 2), my ^ ((rel >> 1) & 1), mc ^ (rel & 1)
        remote.append(pltpu.make_async_remote_copy(
            src_ref=src(4 * px + 2 * py + pc), dst_ref=out_ref.at[me],
            send_sem=send_sems.at[rel - 1], recv_sem=recv_sems.at[rel - 1],
            device_id=(px, py, pc), device_id_type=MESH))
    return local, remote


def _exchange_start(*refs, all_gather):
    local, remote = _exchange_copies(*refs, all_gather)
    local.start()
    for cp in remote:
        cp.start()


def _exchange_wait(*refs, all_gather):
    local, remote = _exchange_copies(*refs, all_gather)
    for cp in remote:
        cp.wait_send()
    for cp in remote:
        cp.wait_recv()
    local.wait()


def _exchange_shape(x, all_gather):
    return jax.ShapeDtypeStruct((N_DEV,) + tuple(x.shape if all_gather else x.shape[1:]), x.dtype)


def _exchange(x, all_gather, name):
    def body(x_ref, out_ref, *sems):
        _exchange_start(x_ref, out_ref, *sems, all_gather=all_gather)
        _exchange_wait(x_ref, out_ref, *sems, all_gather=all_gather)

    return pl.pallas_call(
        body, in_specs=[pl.BlockSpec(memory_space=pl.ANY)], out_specs=pl.BlockSpec(memory_space=pl.ANY),
        out_shape=_exchange_shape(x, all_gather), scratch_shapes=EXCHANGE_SCRATCH,
        compiler_params=pltpu.CompilerParams(has_side_effects=True), name=name,
    )(x)


ATT_TK = 256


def _att_tq(rn, fwd=False):
    return _pick(rn, ((1024,) if fwd else ()) + (512, 256, 128))


def _att_unroll(nkb):
    return _pick(nkb, (3, 2))


def _flash_fwd_call(q_t, k, v_t, xsend=None):
    g, dh, rn = q_t.shape
    nk = k.shape[1]
    tq, tk = _att_tq(rn, True), ATT_TK
    nkb = nk // tk
    unroll = _att_unroll(nkb)
    n_i = rn // tq

    trips = nkb // unroll

    def body(*refs):
        if xsend is None:
            qt_ref, k_ref, vt_ref, ot_ref, lse_ref, s_scr = refs
        else:
            qt_ref, k_ref, vt_ref, x_ref, ot_ref, lse_ref, xout_ref, s_scr, *sems = refs
            a, i = pl.program_id(0), pl.program_id(1)

            @pl.when((a == 0) & (i == 0))
            def _():
                _exchange_start(x_ref, xout_ref, *sems, all_gather=True)

        qst = qt_ref[...] * SCALE

        def scores(t, slot):
            for u in range(unroll):
                off = pl.multiple_of((t * unroll + u) * tk, tk)
                s_scr[slot, u] = _dot(k_ref[pl.ds(off, tk), :], qst)

        def consume(t, slot, carry):
            m, l, acc = carry
            s_t = [s_scr[slot, u] for u in range(unroll)]
            m_new = functools.reduce(jnp.maximum, [jnp.max(s, axis=0, keepdims=True) for s in s_t], m)
            p_t = [jnp.exp(s - m_new) for s in s_t]
            alpha = jnp.exp(m - m_new)
            l = alpha * l + sum(jnp.sum(p, axis=0, keepdims=True) for p in p_t)
            acc = alpha * acc + sum(_dot(vt_ref[t * unroll + u], p.astype(BF16)) for u, p in enumerate(p_t))
            return m_new, l, acc

        def pair(u2, carry):
            t = 2 * u2
            scores(t + 1, 1)
            carry = consume(t, 0, carry)
            scores(t + 2, 0)
            return consume(t + 1, 1, carry)

        scores(0, 0)
        pairs = (trips - 1) // 2
        carry = lax.fori_loop(
            0, pairs, pair, (jnp.full((1, tq), NEG_BIG, F32), jnp.zeros((1, tq), F32), jnp.zeros((dh, tq), F32)))
        if trips - 2 * pairs == 2:
            scores(trips - 1, 1)
            carry = consume(trips - 2, 0, carry)
            carry = consume(trips - 1, 1, carry)
        else:
            carry = consume(trips - 1, 0, carry)
        m, l, acc = carry
        ot_ref[...] = acc / l
        lse_ref[...] = m + jnp.log(l)

        if xsend is not None:
            @pl.when((a == g - 1) & (i == n_i - 1))
            def _():
                _exchange_wait(x_ref, xout_ref, *sems, all_gather=True)

    col = pl.BlockSpec((None, dh, tq), lambda a, i: (a, 0, i))
    vec = pl.BlockSpec((None, 1, tq), lambda a, i: (a, 0, i))
    hbm = pl.BlockSpec(memory_space=pl.ANY)
    in_specs = [col, pl.BlockSpec((None, nk, dh), lambda a, i: (a, 0, 0)),
                pl.BlockSpec((None, nkb, dh, tk), lambda a, i: (a, 0, 0, 0))]
    out_specs = [col, vec]
    out_shape = [jax.ShapeDtypeStruct((g, dh, rn), F32), jax.ShapeDtypeStruct((g, 1, rn), F32)]
    args = (q_t, k, v_t)
    if xsend is not None:
        in_specs, out_specs, args = in_specs + [hbm], out_specs + [hbm], args + (xsend,)
        out_shape = out_shape + [_exchange_shape(xsend, True)]
    return pl.pallas_call(
        body, grid=(g, n_i), in_specs=in_specs, out_specs=out_specs, out_shape=out_shape,
        scratch_shapes=[pltpu.VMEM((2, unroll, tk, tq), F32)] + ([] if xsend is None else EXCHANGE_SCRATCH),
        compiler_params=pltpu.CompilerParams(dimension_semantics=("arbitrary", "arbitrary"), vmem_limit_bytes=VMEM_BIG_LIMIT),
        name="flash_fwd" if xsend is None else "flash_fwd_gather",
    )(*args)


def _flash_bwd_call(q, q_t, k, k_t, v, do, do_t, o_t, lse_t, xsend=None):
    g, rn, dh = q.shape
    nk = k.shape[1]
    tq, tk = _att_tq(rn), ATT_TK
    nkb = nk // tk
    unroll = _att_unroll(nkb)
    n_i = rn // tq

    def body(*refs):
        q_ref, qt_ref, k_ref, kt_ref, v_ref, do_ref, dot_ref, ot_ref, lse_ref = refs[:9]
        if xsend is None:
            dqt_ref, dk_ref, dv_ref = refs[9:]
        else:
            x_ref, dqt_ref, dk_ref, dv_ref, xout_ref, *sems = refs[9:]
        a, i = pl.program_id(0), pl.program_id(1)

        if xsend is not None:
            @pl.when((a == 0) & (i == 0))
            def _():
                _exchange_start(x_ref, xout_ref, *sems, all_gather=False)

        @pl.when(i == 0)
        def _():
            dk_ref[...] = jnp.zeros_like(dk_ref)
            dv_ref[...] = jnp.zeros_like(dv_ref)

        qs = q_ref[...] * SCALE
        qst = qt_ref[...] * SCALE
        dov, dotv = do_ref[...], dot_ref[...]
        delta = jnp.sum(dotv.astype(F32) * ot_ref[...], axis=0, keepdims=True)
        lse = lse_ref[...]

        def step(j, dqt):
            off = pl.multiple_of(j * tk, tk)
            kj = k_ref[pl.ds(off, tk), :]
            vj = v_ref[pl.ds(off, tk), :]
            p_t = jnp.exp(_dot(kj, qst) - lse)
            ds_t = (p_t * (_dot(vj, dotv) - delta)).astype(BF16)
            dv_ref[pl.ds(off, tk), :] += _dot(p_t.astype(BF16), dov)
            dk_ref[pl.ds(off, tk), :] += _dot(ds_t, qs)
            return dqt + _dot(kt_ref[j], ds_t)

        def trip(t, dqt):
            for u in range(unroll):
                dqt = step(t * unroll + u, dqt)
            return dqt

        dqt_ref[...] = lax.fori_loop(0, nkb // unroll, trip, jnp.zeros((dh, tq), F32)) * SCALE

        if xsend is not None:
            @pl.when((a == g - 1) & (i == n_i - 1))
            def _():
                _exchange_wait(x_ref, xout_ref, *sems, all_gather=False)

    row = pl.BlockSpec((None, tq, dh), lambda a, i: (a, i, 0))
    col = pl.BlockSpec((None, dh, tq), lambda a, i: (a, 0, i))
    kv = pl.BlockSpec((None, nk, dh), lambda a, i: (a, 0, 0))
    hbm = pl.BlockSpec(memory_space=pl.ANY)
    in_specs = [row, col, kv, pl.BlockSpec((None, nkb, dh, tk), lambda a, i: (a, 0, 0, 0)), kv, row, col, col,
                pl.BlockSpec((None, 1, tq), lambda a, i: (a, 0, i))]
    out_specs = [col, kv, kv]
    out_shape = [jax.ShapeDtypeStruct((g, dh, rn), F32), jax.ShapeDtypeStruct((g, nk, dh), F32),
                 jax.ShapeDtypeStruct((g, nk, dh), F32)]
    args = (q, q_t, k, k_t, v, do, do_t, o_t, lse_t)
    if xsend is not None:
        in_specs, out_specs, args = in_specs + [hbm], out_specs + [hbm], args + (xsend,)
        out_shape = out_shape + [_exchange_shape(xsend, False)]
    return pl.pallas_call(
        body, grid=(g, n_i), in_specs=in_specs, out_specs=out_specs, out_shape=out_shape,
        scratch_shapes=[] if xsend is None else EXCHANGE_SCRATCH,
        compiler_params=pltpu.CompilerParams(dimension_semantics=("arbitrary", "arbitrary"), vmem_limit_bytes=VMEM_BIG_LIMIT),
        name="flash_bwd" if xsend is None else "flash_bwd_scatter",
    )(*args)


def _key_blocks_t(t):
    g, nk, dh = t.shape
    return t.reshape(g, nk // ATT_TK, ATT_TK, dh).transpose(0, 1, 3, 2)


@jax.custom_vjp
def attn(q, k, v):
    return _attn_fwd(q, k, v)[0]


def _attn_fwd(q, k, v):
    qb, kb, vb = q.astype(BF16), k.astype(BF16), v.astype(BF16)
    q_t = qb.transpose(0, 2, 1)
    o_t, lse_t = _flash_fwd_call(q_t, kb, _key_blocks_t(vb))
    return o_t.transpose(0, 2, 1), (qb, q_t, kb, vb, o_t, lse_t)


def _attn_bwd(res, do):
    qb, q_t, kb, vb, o_t, lse_t = res
    dob = do.astype(BF16)
    dq_t, dk, dv = _flash_bwd_call(qb, q_t, kb, _key_blocks_t(kb), vb, dob, dob.transpose(0, 2, 1), o_t, lse_t)
    return dq_t.transpose(0, 2, 1), dk, dv


attn.defvjp(_attn_fwd, _attn_bwd)


def _shard_rows(shards):
    return jnp.concatenate([shards[n].reshape(-1, D_MODEL) for n in BIG], axis=0).astype(BF16)


def _fulls_of(gathered, shards):
    out, off = {}, 0
    for n in BIG:
        k, nn = shards[n].shape
        r = k * nn // D_MODEL
        t = gathered[:, off:off + r].reshape(N_DEV, k, nn)
        out[n] = t.reshape(N_DEV * k, nn) if n in ROW_SHARDED else t.transpose(1, 0, 2).reshape(k, N_DEV * nn)
        off += r
    return out


def _slabs_of(dfulls):
    parts = []
    for n in BIG:
        k, nn = dfulls[n].shape
        t = (dfulls[n].reshape(N_DEV, k // N_DEV, nn) if n in ROW_SHARDED
             else dfulls[n].reshape(k, N_DEV, nn // N_DEV).transpose(1, 0, 2))
        parts.append(t.reshape(N_DEV, -1, D_MODEL))
    return jnp.concatenate(parts, axis=1)


def _dshards_of(parts, dfulls):
    rows, out, off = _sum8_call(parts), {}, 0
    for n in BIG:
        k, nn = dfulls[n].shape
        shape = (k // N_DEV, nn) if n in ROW_SHARDED else (k, nn // N_DEV)
        r = shape[0] * shape[1] // D_MODEL
        out[n] = rows[off:off + r].reshape(shape)
        off += r
    return out


@jax.custom_vjp
def gather_layer(shards):
    return _fulls_of(_exchange(_shard_rows(shards), True, "gather_weights"), shards)


def _gather_layer_fwd(shards):
    return gather_layer(shards), None


def _gather_layer_bwd(_, dfulls):
    return (_dshards_of(_exchange(_slabs_of(dfulls), False, "scatter_grads"), dfulls),)


gather_layer.defvjp(_gather_layer_fwd, _gather_layer_bwd)


@jax.custom_vjp
def attn_gather(q, k, v, shards):
    return _attn_gather_fwd(q, k, v, shards)[0]


def _attn_gather_fwd(q, k, v, shards):
    qb, kb, vb = q.astype(BF16), k.astype(BF16), v.astype(BF16)
    q_t = qb.transpose(0, 2, 1)
    o_t, lse_t, gathered = _flash_fwd_call(q_t, kb, _key_blocks_t(vb), _shard_rows(shards))
    return (o_t.transpose(0, 2, 1), _fulls_of(gathered, shards)), (qb, q_t, kb, vb, o_t, lse_t)


def _attn_gather_bwd(res, cts):
    qb, q_t, kb, vb, o_t, lse_t = res
    do, dfulls = cts
    dob = do.astype(BF16)
    dq_t, dk, dv, parts = _flash_bwd_call(qb, q_t, kb, _key_blocks_t(kb), vb, dob, dob.transpose(0, 2, 1), o_t, lse_t,
                                          _slabs_of(dfulls))
    return dq_t.transpose(0, 2, 1), dk, dv, _dshards_of(parts, dfulls)


attn_gather.defvjp(_attn_gather_fwd, _attn_gather_bwd)


def _na_key_row(i, rows):
    return jnp.clip(NA_QROWS * i - NA_WIN_H // 2, 0, rows - NA_KROWS)


def _na_type(i, nb):
    return jnp.where(i == 0, 0, jnp.where(i == nb - 1, 2, 1))


def _na_fwd_call(q, k, v, kc, vc, slab):
    h, s, dh = q.shape
    n_ctx = kc.shape[1]
    rows, nb = s // GRID_W, s // NA_QB
    hb = NA_FWD_HEADS

    def body(q_ref, k_ref, v_ref, kc_ref, vc_ref, slab_ref, o_ref, lse_ref):
        off = pl.multiple_of(_na_key_row(pl.program_id(1), rows) * GRID_W, NA_QB)
        for hh in range(hb):
            qs = q_ref[hh] * SCALE
            kw = k_ref[hh, pl.ds(off, NA_KW), :]
            vw = v_ref[hh, pl.ds(off, NA_KW), :]
            sw = _dot_nt(qs, kw) + slab_ref[hh]
            sc = _dot_nt(qs, kc_ref[hh])
            m = jnp.maximum(jnp.max(sw, axis=1, keepdims=True), jnp.max(sc, axis=1, keepdims=True))
            pw = jnp.exp(sw - m)
            pc = jnp.exp(sc - m)
            l = jnp.sum(pw, axis=1, keepdims=True) + jnp.sum(pc, axis=1, keepdims=True)
            o_ref[hh] = (_dot(pw.astype(BF16), vw) + _dot(pc.astype(BF16), vc_ref[hh])) / l
            lse_ref[hh] = m + jnp.log(l)

    qblk = pl.BlockSpec((hb, NA_QB, dh), lambda a, i: (a, i, 0))
    kv = pl.BlockSpec((hb, s, dh), lambda a, i: (a, 0, 0))
    ckv = pl.BlockSpec((hb, n_ctx, dh), lambda a, i: (a, 0, 0))
    return pl.pallas_call(
        body, grid=(h // hb, nb),
        in_specs=[qblk, kv, kv, ckv, ckv, pl.BlockSpec((None, hb, NA_QB, NA_KW), lambda a, i: (_na_type(i, nb), a, 0, 0))],
        out_specs=[qblk, pl.BlockSpec((hb, NA_QB, 1), lambda a, i: (a, i, 0))],
        out_shape=[jax.ShapeDtypeStruct((h, s, dh), F32), jax.ShapeDtypeStruct((h, s, 1), F32)],
        compiler_params=pltpu.CompilerParams(dimension_semantics=("parallel", "parallel"), vmem_limit_bytes=VMEM_BIG_LIMIT),
        name="na_fwd",
    )(q, k, v, kc, vc, slab)


def _na_bwd_call(q, k, v, kc, vc, slab, do, o, lse):
    h, s, dh = q.shape
    n_ctx = kc.shape[1]
    rows, nb = s // GRID_W, s // NA_QB
    half = nb // 2

    def body(q_ref, k_ref, v_ref, kc_ref, vc_ref, slab0_ref, slab1_ref, do_ref, o_ref, lse_ref,
             dq_ref, dk_ref, dv_ref, dkc_ref, dvc_ref, dslab0_ref, dslab1_ref):
        j = pl.program_id(1)

        @pl.when(j == 0)
        def _():
            dk_ref[...] = jnp.zeros_like(dk_ref)
            dv_ref[...] = jnp.zeros_like(dv_ref)
            dkc_ref[...] = jnp.zeros_like(dkc_ref)
            dvc_ref[...] = jnp.zeros_like(dvc_ref)

        @pl.when(j <= 1)
        def _():
            dslab0_ref[...] = jnp.zeros_like(dslab0_ref)

        @pl.when((j == 0) | (j == half - 1))
        def _():
            dslab1_ref[...] = jnp.zeros_like(dslab1_ref)

        kcv, vcv = kc_ref[...], vc_ref[...]
        for sub, (slab_ref, dslab_ref) in enumerate(((slab0_ref, dslab0_ref), (slab1_ref, dslab1_ref))):
            off = pl.multiple_of(_na_key_row(2 * j + sub, rows) * GRID_W, NA_QB)
            blk = slice(sub * NA_QB, (sub + 1) * NA_QB)
            qs = q_ref[blk, :] * SCALE
            kw = k_ref[pl.ds(off, NA_KW), :]
            vw = v_ref[pl.ds(off, NA_KW), :]
            dov = do_ref[blk, :]
            lse = lse_ref[blk, :]
            delta = jnp.sum(dov.astype(F32) * o_ref[blk, :], axis=1, keepdims=True)
            pw = jnp.exp(_dot_nt(qs, kw) + slab_ref[...] - lse)
            pc = jnp.exp(_dot_nt(qs, kcv) - lse)
            dsw = pw * (_dot_nt(dov, vw) - delta)
            dsc = pc * (_dot_nt(dov, vcv) - delta)
            dslab_ref[...] += dsw
            dsw, dsc = dsw.astype(BF16), dsc.astype(BF16)
            dq_ref[blk, :] = (_dot(dsw, kw) + _dot(dsc, kcv)) * SCALE
            dk_ref[pl.ds(off, NA_KW), :] += _dot_tn(dsw, qs)
            dv_ref[pl.ds(off, NA_KW), :] += _dot_tn(pw.astype(BF16), dov)
            dkc_ref[...] += _dot_tn(dsc, qs)
            dvc_ref[...] += _dot_tn(pc.astype(BF16), dov)

    qblk = pl.BlockSpec((None, 2 * NA_QB, dh), lambda a, j: (a, j, 0))
    kv = pl.BlockSpec((None, s, dh), lambda a, j: (a, 0, 0))
    ckv = pl.BlockSpec((None, n_ctx, dh), lambda a, j: (a, 0, 0))
    slab_spec = lambda sub: pl.BlockSpec((None, None, NA_QB, NA_KW), lambda a, j: (_na_type(2 * j + sub, nb), a, 0, 0))
    dslab0, dslab1 = jax.ShapeDtypeStruct(slab.shape, F32), jax.ShapeDtypeStruct(slab.shape, F32)
    dq, dk, dv, dkc, dvc, d0, d1 = pl.pallas_call(
        body, grid=(h, half),
        in_specs=[qblk, kv, kv, ckv, ckv, slab_spec(0), slab_spec(1), qblk, qblk,
                  pl.BlockSpec((None, 2 * NA_QB, 1), lambda a, j: (a, j, 0))],
        out_specs=[qblk, kv, kv, ckv, ckv, slab_spec(0), slab_spec(1)],
        out_shape=[jax.ShapeDtypeStruct((h, s, dh), F32), jax.ShapeDtypeStruct((h, s, dh), F32),
                   jax.ShapeDtypeStruct((h, s, dh), F32), jax.ShapeDtypeStruct((h, n_ctx, dh), F32),
                   jax.ShapeDtypeStruct((h, n_ctx, dh), F32), dslab0, dslab1],
        compiler_params=pltpu.CompilerParams(dimension_semantics=("arbitrary", "arbitrary"), vmem_limit_bytes=VMEM_BIG_LIMIT),
        name="na_bwd",
    )(q, k, v, kc, vc, slab, slab, do, o, lse)
    return dq, dk, dv, dkc, dvc, jnp.stack([d0[0], d0[1] + d1[1], d1[2]])


@jax.custom_vjp
def na_attn(q, k, v, kc, vc, slab):
    return _na_fwd_call(q.astype(BF16), k.astype(BF16), v.astype(BF16), kc.astype(BF16), vc.astype(BF16), slab)[0]


def _na_attn_fwd(q, k, v, kc, vc, slab):
    qb, kb, vb, kcb, vcb = (t.astype(BF16) for t in (q, k, v, kc, vc))
    o, lse = _na_fwd_call(qb, kb, vb, kcb, vcb, slab)
    return o, (qb, kb, vb, kcb, vcb, slab, o, lse)


def _na_attn_bwd(res, do):
    qb, kb, vb, kcb, vcb, slab, o, lse = res
    return tuple(_na_bwd_call(qb, kb, vb, kcb, vcb, slab, do.astype(BF16), o, lse))


na_attn.defvjp(_na_attn_fwd, _na_attn_bwd)


def _na_tables(rows):
    col = np.arange(GRID_W)
    c_start = np.clip(col - NA_WIN_W // 2, 0, GRID_W - NA_WIN_W)
    in_win = (col[None, :] >= c_start[:, None]) & (col[None, :] < c_start[:, None] + NA_WIN_W)
    dc_idx = np.clip(col[None, :] - col[:, None], -(NA_WIN_W - 1), NA_WIN_W - 1) + NA_WIN_W - 1
    onehot = (in_win[:, :, None] & (dc_idx[:, :, None] == np.arange(2 * NA_WIN_W - 1)[None, None, :])).astype(np.float32)
    negcol = np.where(in_win, 0.0, NEG_BIG).astype(np.float32)
    plan = []
    for r0, ks in [(0, 0), (NA_QROWS, 0), (rows - NA_QROWS, rows - NA_KROWS)]:
        per_row = []
        for a in range(NA_QROWS):
            r = r0 + a
            rs = min(max(r - NA_WIN_H // 2, 0), rows - NA_WIN_H)
            valid = np.array([rs <= ks + b < rs + NA_WIN_H for b in range(NA_KROWS)])
            per_row.append((ks - r + NA_WIN_H - 1, valid))
        plan.append(per_row)
    return jnp.asarray(onehot), jnp.asarray(negcol), plan


def _na_slab(rpb, tables):
    onehot, negcol, plan = tables
    n_dr = 2 * NA_WIN_H - 1
    table = jnp.einsum("hdc,wuc->hwdu", rpb, onehot, precision=HIGHEST) + negcol[None, :, None, :]
    types = []
    for per_row in plan:
        slabs = []
        for first, valid in per_row:
            lo, hi = max(first, 0), min(first + NA_KROWS, n_dr)
            sel = jnp.pad(table[:, :, lo:hi, :], ((0, 0), (0, 0), (lo - first, first + NA_KROWS - hi), (0, 0)),
                          constant_values=NEG_BIG)
            sel = jnp.where(jnp.asarray(valid)[None, None, :, None], sel, NEG_BIG)
            slabs.append(sel.reshape(NA_HEADS, GRID_W, NA_KW))
        types.append(jnp.stack(slabs, axis=1).reshape(NA_HEADS, NA_QB, NA_KW))
    return jnp.stack(types)


def _loss_call(y, target):
    s, d = y.shape
    tt = _pick(s, (512, 256, 128))

    def body(y_ref, t_ref, loss_ref, diff_ref):
        @pl.when(pl.program_id(0) == 0)
        def _():
            loss_ref[...] = jnp.zeros_like(loss_ref)

        e = y_ref[...] - t_ref[...]
        diff_ref[...] = e * (1.0 / d)
        loss_ref[...] += 0.5 * jnp.sum(jnp.mean(e * e, axis=-1, keepdims=True), axis=0, keepdims=True)

    row = pl.BlockSpec((tt, d), lambda i: (i, 0))
    return pl.pallas_call(
        body, grid=(s // tt,), in_specs=[row, row], out_specs=[pl.BlockSpec((1, 1), lambda i: (0, 0)), row],
        out_shape=[jax.ShapeDtypeStruct((1, 1), F32), jax.ShapeDtypeStruct((s, d), F32)],
        compiler_params=pltpu.CompilerParams(dimension_semantics=("arbitrary",)), name="loss_head",
    )(y, target)


@jax.custom_vjp
def loss_head(y, target):
    return _loss_call(y, target)[0][0, 0]


def _loss_head_fwd(y, target):
    loss, diff = _loss_call(y, target)
    return loss[0, 0], diff


def _loss_head_bwd(diff, g):
    return diff * g, jnp.zeros_like(diff)


loss_head.defvjp(_loss_head_fwd, _loss_head_bwd)


def _sum8_call(parts):
    _, r, c = parts.shape
    tr = _pick(r, (328, 256, 128))

    def body(p_ref, g_ref):
        g = p_ref[0].astype(F32)
        for s in range(1, N_DEV):
            g = g + p_ref[s].astype(F32)
        g_ref[...] = g

    return pl.pallas_call(
        body, grid=(r // tr,), in_specs=[pl.BlockSpec((N_DEV, tr, c), lambda i: (0, i, 0))],
        out_specs=pl.BlockSpec((tr, c), lambda i: (i, 0)), out_shape=jax.ShapeDtypeStruct((r, c), F32),
        compiler_params=pltpu.CompilerParams(dimension_semantics=("parallel",)), name="sum8",
    )(parts)


def _adam_call(g, w, m, v):
    r, c = g.shape
    tr = _pick(r, (256, 128))

    def body(g_ref, w_ref, m_ref, v_ref, d_ref, nm_ref, nv_ref):
        gv = g_ref[...]
        mn = ADAM_B1 * m_ref[...] + (1.0 - ADAM_B1) * gv
        vn = ADAM_B2 * v_ref[...] + (1.0 - ADAM_B2) * (gv * gv)
        m_hat = mn / (1.0 - ADAM_B1 ** ADAM_STEP)
        v_hat = vn / (1.0 - ADAM_B2 ** ADAM_STEP)
        d_ref[...] = -ADAM_LR * (m_hat / (jnp.sqrt(v_hat) + ADAM_EPS) + ADAM_WD * w_ref[...])
        nm_ref[...] = mn
        nv_ref[...] = vn

    row = pl.BlockSpec((tr, c), lambda i: (i, 0))
    out = jax.ShapeDtypeStruct((r, c), F32)
    return pl.pallas_call(
        body, grid=(r // tr,), in_specs=[row, row, row, row], out_specs=[row, row, row], out_shape=[out, out, out],
        compiler_params=pltpu.CompilerParams(dimension_semantics=("parallel",)), name="adam",
    )(g, w, m, v)


def _pack_small(vals, extra=None):
    flat = [vals[n].reshape(-1) for n in SMALL] + ([] if extra is None else [extra.reshape(-1)])
    flat = jnp.concatenate(flat)
    rows = -(-flat.shape[0] // D_MODEL)
    rows = -(-rows // 8) * 8
    return jnp.pad(flat, (0, rows * D_MODEL - flat.shape[0])).reshape(rows, D_MODEL)


def _unpack_small(packed, like):
    flat, out, off = packed.reshape(-1), {}, 0
    for n in SMALL:
        size = like[n].size
        out[n] = flat[off:off + size].reshape(like[n].shape)
        off += size
    return out, flat[off]


def _rope_tables(s, n_ctx):
    t = jnp.arange(s)
    row = (t // GRID_W).astype(F32)
    col = (t % GRID_W).astype(F32)
    half = HEAD_DIM // 2
    inv = ROPE_THETA ** (-jnp.arange(0, half, 2, dtype=F32) / half)
    ang = jnp.concatenate([row[:, None] * inv, col[:, None] * inv], axis=-1)
    cos2 = jnp.repeat(jnp.cos(ang), 2, axis=-1)
    sin2 = jnp.repeat(jnp.sin(ang), 2, axis=-1)
    cos2 = jnp.concatenate([cos2, jnp.ones((n_ctx, HEAD_DIM), F32)], axis=0)
    sin2 = jnp.concatenate([sin2, jnp.zeros((n_ctx, HEAD_DIM), F32)], axis=0)
    return cos2, sin2


def _to_heads(t, n_heads):
    return t.reshape(t.shape[0], n_heads, HEAD_DIM).transpose(1, 0, 2)


def _from_heads(t):
    return t.transpose(1, 0, 2).reshape(t.shape[1], t.shape[0] * HEAD_DIM)


def _local_loss(shards, p, x, c, ctx, target, consts):
    s, n_ctx = x.shape[0], ctx.shape[0]
    cos2, sin2, rot, na_tables = consts
    is_ctx = (jnp.arange(s + n_ctx) >= s)[:, None]
    seg = lambda rows2: jnp.where(is_ctx, rows2[1:2], rows2[0:1])
    xa = jnp.concatenate([x, ctx], axis=0)
    cond = jnp.concatenate([jax.nn.silu(c), jax.nn.silu(p["c_ctx"])[None, :],
                            jnp.zeros((MOD_ROWS - 2, D_MODEL), F32)], axis=0).astype(BF16)
    layer_shards = lambda l: {n: shards[n][l] for n in BIG}
    w = gather_layer(layer_shards(0))

    for l in range(DEPTH):
        mod = (mm(cond, w["w_mod"]) + p["b_mod"][l])[:2]
        sh1, sc1, g1, sh2, sc2, g2 = jnp.split(mod, 6, axis=-1)

        h = norm_mod(xa, p["norm1"][l][None, :], sh1, sc1, s, BF16)
        na_q, na_k, na_v, gq, gk, gv, ga, gb = jnp.split(mm(h, w["w_in"]), IN_SPLITS, axis=-1)

        qa, ka, va = _to_heads(na_q, NA_HEADS), _to_heads(na_k, NA_HEADS), _to_heads(na_v, NA_HEADS)
        slab = _na_slab(p["na_rpb"][l], na_tables)
        ya_lat = na_attn(qa[:, :s], ka[:, :s], va[:, :s], ka[:, s:], va[:, s:], slab)
        ya_ctx = attn(qa[:, s:], ka[:, s:], va[:, s:])
        ya = _from_heads(jnp.concatenate([ya_lat, ya_ctx], axis=1))

        qk = _to_heads(jnp.concatenate([gq, gk], axis=-1), GQA_Q_HEADS + GQA_KV_HEADS)
        gains = jnp.stack([p["q_gain"][l], p["k_gain"][l]])
        qk = hn_rope(qk, gains, cos2, sin2, rot, GQA_Q_HEADS)
        qb, kb, vb = qk[:GQA_Q_HEADS], qk[GQA_Q_HEADS:], _to_heads(gv, GQA_KV_HEADS)
        q_lat = qb[:, :s].reshape(GQA_KV_HEADS, GQA_REP * s, HEAD_DIM)
        if l + 1 < DEPTH:
            ob_lat, w_next = attn_gather(q_lat, kb, vb, layer_shards(l + 1))
        else:
            ob_lat, w_next = attn(q_lat, kb, vb), None
        ob_ctx = attn(qb[:, s:].reshape(GQA_KV_HEADS, GQA_REP * n_ctx, HEAD_DIM), kb[:, s:], vb[:, s:])
        yb = _from_heads(jnp.concatenate([ob_lat.reshape(GQA_Q_HEADS, s, HEAD_DIM),
                                          ob_ctx.reshape(GQA_Q_HEADS, n_ctx, HEAD_DIM)], axis=1))

        merged = jax.nn.sigmoid(ga) * mm(ya.astype(BF16), w["w_pa"]) + jax.nn.sigmoid(gb) * mm(yb.astype(BF16), w["w_pb"])
        xa = xa + seg(g1) * mm(merged.astype(BF16), w["w_o"])

        h2 = norm_mod(xa, p["norm2"][l][None, :], sh2, sc2, s, BF16)
        xa = xa + seg(g2) * ffn(h2, w["w_ffn_in"], w["w_ffn_out"])
        w = w_next

    zeros2 = jnp.zeros((2, D_MODEL), F32)
    y = norm_mod(xa, p["final_norm"][None, :], zeros2, zeros2, s, F32)[:s]
    return loss_head(y, target)


def kernel(x, c, ctx, c_ctx, w_mod, b_mod, norm1, w_in, na_rpb, q_gain, k_gain, w_pa, w_pb, w_o, norm2, w_ffn_in, w_ffn_out, final_norm, loss_target, m_c_ctx, m_w_mod, m_b_mod, m_norm1, m_w_in, m_na_rpb, m_q_gain, m_k_gain, m_w_pa, m_w_pb, m_w_o, m_norm2, m_w_ffn_in, m_w_ffn_out, m_final_norm, v_c_ctx, v_w_mod, v_b_mod, v_norm1, v_w_in, v_na_rpb, v_q_gain, v_k_gain, v_w_pa, v_w_pb, v_w_o, v_norm2, v_w_ffn_in, v_w_ffn_out, v_final_norm):
    w = dict(c_ctx=c_ctx, w_mod=w_mod, b_mod=b_mod, norm1=norm1, w_in=w_in, na_rpb=na_rpb, q_gain=q_gain, k_gain=k_gain,
             w_pa=w_pa, w_pb=w_pb, w_o=w_o, norm2=norm2, w_ffn_in=w_ffn_in, w_ffn_out=w_ffn_out, final_norm=final_norm)
    mom = dict(c_ctx=m_c_ctx, w_mod=m_w_mod, b_mod=m_b_mod, norm1=m_norm1, w_in=m_w_in, na_rpb=m_na_rpb, q_gain=m_q_gain,
               k_gain=m_k_gain, w_pa=m_w_pa, w_pb=m_w_pb, w_o=m_w_o, norm2=m_norm2, w_ffn_in=m_w_ffn_in,
               w_ffn_out=m_w_ffn_out, final_norm=m_final_norm)
    var = dict(c_ctx=v_c_ctx, w_mod=v_w_mod, b_mod=v_b_mod, norm1=v_norm1, w_in=v_w_in, na_rpb=v_na_rpb, q_gain=v_q_gain,
               k_gain=v_k_gain, w_pa=v_w_pa, w_pb=v_w_pb, w_o=v_w_o, norm2=v_norm2, w_ffn_in=v_w_ffn_in,
               w_ffn_out=v_w_ffn_out, final_norm=v_final_norm)
    s, n_ctx = x.shape[1], ctx.shape[1]
    depth = w_mod.shape[0]

    shards = {n: [w[n][l] for l in range(depth)] for n in BIG}
    small = {n: w[n] for n in SMALL}
    consts = (*_rope_tables(s, n_ctx), _rot_matrix(), _na_tables(s // GRID_W))
    loss, (g_shards, g_small, gx) = jax.value_and_grad(_local_loss, argnums=(0, 1, 2))(
        shards, small, x[0], c, ctx[0], loss_target[0], consts)

    parts_s = _exchange(_pack_small(g_small, loss), True, "gather_small_grads")
    g_packed = _sum8_call(parts_s)
    grads, loss = _unpack_small(g_packed, w)
    zero = jnp.zeros((1,), F32)
    upd_s = _adam_call(g_packed, _pack_small(w, zero), _pack_small(mom, zero), _pack_small(var, zero))
    outs = [grads] + [_unpack_small(u, w)[0] for u in upd_s]

    for n in BIG:
        flat = lambda t: t.reshape(-1, t.shape[-1])
        g = jnp.stack(g_shards[n])
        upd = _adam_call(flat(g), flat(w[n]), flat(mom[n]), flat(var[n]))
        for k, t in enumerate([g] + [u.reshape(w[n].shape) for u in upd]):
            outs[k][n] = t
    return (loss, gx[None], *[o[n] for o in outs for n in WEIGHTS])
```

```python
import functools

import numpy as np
import jax
import jax.numpy as jnp
from jax import lax
from jax.experimental import pallas as pl
from jax.experimental.pallas import tpu as pltpu

F32 = jnp.float32
BF16 = jnp.bfloat16
HIGHEST = lax.Precision.HIGHEST
ROT_PRECISION = lax.Precision.HIGH

D_MODEL = 1024
DEPTH = 4
GRID_W = 64
HEAD_DIM = 64
NA_HEADS = 8
NA_WIN_H = 8
NA_WIN_W = 16
GQA_Q_HEADS = 8
GQA_KV_HEADS = 2
GQA_REP = GQA_Q_HEADS // GQA_KV_HEADS
NA_WIDTH = NA_HEADS * HEAD_DIM
GQA_Q_WIDTH = GQA_Q_HEADS * HEAD_DIM
GQA_KV_WIDTH = GQA_KV_HEADS * HEAD_DIM
IN_SIZES = (NA_WIDTH, NA_WIDTH, NA_WIDTH, GQA_Q_WIDTH, GQA_KV_WIDTH, GQA_KV_WIDTH, D_MODEL, D_MODEL)
IN_SPLITS = tuple(int(v) for v in np.cumsum(IN_SIZES)[:-1])
ROPE_THETA = 10000.0
EPS = 1e-6
SCALE = HEAD_DIM ** -0.5
ADAM_LR = 0.001
ADAM_B1 = 0.9
ADAM_B2 = 0.999
ADAM_EPS = 1e-08
ADAM_WD = 0.01
ADAM_STEP = 10

N_DEV = 8
MESH = pl.DeviceIdType.MESH
NEG_BIG = -1e30

VMEM_BIG_LIMIT = 52 * 1024 * 1024
NA_QROWS = 4
NA_QB = NA_QROWS * GRID_W
NA_KROWS = 12
NA_KW = NA_KROWS * GRID_W
NA_FWD_HEADS = 4
MM_TILE_ELEMS = 768 * 2176
MM_OPERAND_ELEMS = 1408 * 2176
MOD_ROWS = 256

BIG = ("w_mod", "w_in", "w_pa", "w_pb", "w_o", "w_ffn_in", "w_ffn_out")
ROW_SHARDED = ("w_o", "w_ffn_out")
EARLY = ("w_mod", "w_in")
LATE = ("w_pa", "w_pb", "w_o", "w_ffn_in", "w_ffn_out")
SMALL = ("c_ctx", "b_mod", "norm1", "na_rpb", "q_gain", "k_gain", "norm2", "final_norm")
WEIGHTS = ("c_ctx", "w_mod", "b_mod", "norm1", "w_in", "na_rpb", "q_gain", "k_gain", "w_pa", "w_pb", "w_o",
           "norm2", "w_ffn_in", "w_ffn_out", "final_norm")


def _pick(n, cands):
    for c in cands:
        if n % c == 0:
            return c
    return n


def _dot_nt(a, b):
    return lax.dot_general(a, b, (((1,), (1,)), ((), ())), preferred_element_type=F32)


def _dot_tn(a, b):
    return lax.dot_general(a, b, (((0,), (0,)), ((), ())), preferred_element_type=F32)


def _dot(a, b):
    return jnp.dot(a, b, preferred_element_type=F32)


def _mm_call(a, b, name, mode="nn", out_dtype=F32, b_lead=None):
    (m, k) = a.shape if mode != "tn" else a.shape[::-1]
    n = b.shape[-1] if mode != "nt" else b.shape[-2]
    tn = _pick(n, (2176, 1408, 1024, 512, 256, 128))
    tm = _pick(m, tuple(t for t in (768, 512, 256, 128) if t * tn <= MM_TILE_ELEMS))
    tk = _pick(k, tuple(t for t in (2816, 2176, 1408, 1024, 768, 512, 256, 128) if t * max(tm, tn) <= MM_OPERAND_ELEMS))
    nk = k // tk
    dot = {"nn": _dot, "tn": _dot_tn, "nt": _dot_nt}[mode]
    a_blk = (tk, tm) if mode == "tn" else (tm, tk)
    b_blk = (tn, tk) if mode == "nt" else (tk, tn)
    a_idx = (lambda i, kk: (kk, i)) if mode == "tn" else (lambda i, kk: (i, kk))
    b_idx = (lambda j, kk: (j, kk)) if mode == "nt" else (lambda j, kk: (kk, j))
    if b_lead is not None:
        b_blk, b_idx2 = (None,) + b_blk, b_idx
        b_idx = lambda j, kk: (b_lead,) + b_idx2(j, kk)

    def body(a_ref, b_ref, o_ref, acc_ref):
        kk = pl.program_id(2)
        part = dot(a_ref[...], b_ref[...])

        @pl.when(kk == 0)
        def _():
            acc_ref[...] = part

        @pl.when(kk > 0)
        def _():
            acc_ref[...] += part

        @pl.when(kk == nk - 1)
        def _():
            o_ref[...] = acc_ref[...].astype(o_ref.dtype)

    def body1(a_ref, b_ref, o_ref):
        o_ref[...] = dot(a_ref[...], b_ref[...]).astype(o_ref.dtype)

    footprint = 2 * (tm * tk * 2 + tk * tn * 2 + tm * tn * 4) + 2 * tm * tn * 4
    limit = int(footprint + (8 << 20))
    if nk == 1:
        return pl.pallas_call(
            body1, grid=(n // tn, m // tm),
            in_specs=[pl.BlockSpec(a_blk, lambda j, i: a_idx(i, 0)), pl.BlockSpec(b_blk, lambda j, i: b_idx(j, 0))],
            out_specs=pl.BlockSpec((tm, tn), lambda j, i: (i, j)),
            out_shape=jax.ShapeDtypeStruct((m, n), out_dtype),
            compiler_params=pltpu.CompilerParams(dimension_semantics=("parallel", "parallel"), vmem_limit_bytes=limit),
            name=name,
        )(a, b)
    return pl.pallas_call(
        body, grid=(m // tm, n // tn, nk),
        in_specs=[pl.BlockSpec(a_blk, lambda i, j, kk: a_idx(i, kk)), pl.BlockSpec(b_blk, lambda i, j, kk: b_idx(j, kk))],
        out_specs=pl.BlockSpec((tm, tn), lambda i, j, kk: (i, j)),
        out_shape=jax.ShapeDtypeStruct((m, n), out_dtype),
        scratch_shapes=[pltpu.VMEM((tm, tn), F32)],
        compiler_params=pltpu.CompilerParams(
            dimension_semantics=("parallel", "parallel", "arbitrary"), vmem_limit_bytes=limit),
        name=name,
    )(a, b)


@jax.custom_vjp
def mm(x, w):
    return _mm_call(x, w, "mm_fwd")


def _mm_fwd(x, w):
    return _mm_call(x, w, "mm_fwd"), (x, w)


def _mm_bwd(res, dy):
    x, w = res
    dyb = dy.astype(BF16)
    return _mm_call(dyb, w, "mm_dx", "nt", BF16), _mm_call(x, dyb, "mm_dw", "tn", BF16)


mm.defvjp(_mm_fwd, _mm_bwd)


def _ffn_tiles(t, f):
    return _pick(t, (384, 256, 128)), _pick(f, (1408, 1024, 512, 256, 128))


def _ffn_in_call(h, w_in):
    t, d = h.shape
    f = w_in.shape[1] // 2
    tm, tn = _ffn_tiles(t, f)
    nf = f // tn

    def body(h_ref, wa_ref, wu_ref, act_ref, au_ref):
        hv = h_ref[...]
        a, u = _dot(hv, wa_ref[...]), _dot(hv, wu_ref[...])
        au_ref[0] = a
        au_ref[1] = u
        act_ref[...] = (a * jax.nn.sigmoid(a) * u).astype(act_ref.dtype)

    return pl.pallas_call(
        body, grid=(nf, t // tm),
        in_specs=[pl.BlockSpec((tm, d), lambda j, i: (i, 0)), pl.BlockSpec((d, tn), lambda j, i: (0, j)),
                  pl.BlockSpec((d, tn), lambda j, i: (0, j + nf))],
        out_specs=[pl.BlockSpec((tm, tn), lambda j, i: (i, j)), pl.BlockSpec((2, tm, tn), lambda j, i: (0, i, j))],
        out_shape=[jax.ShapeDtypeStruct((t, f), BF16), jax.ShapeDtypeStruct((2, t, f), F32)],
        compiler_params=pltpu.CompilerParams(dimension_semantics=("parallel", "parallel"), vmem_limit_bytes=VMEM_BIG_LIMIT),
        name="ffn_in",
    )(h, w_in, w_in)


def _ffn_dact_call(dy, w_out, au):
    t, d = dy.shape
    f = w_out.shape[0]
    tm, tn = _ffn_tiles(t, f)

    def body(dy_ref, w_ref, au_ref, dau_ref):
        dact = _dot_nt(dy_ref[...], w_ref[...])
        a, u = au_ref[0], au_ref[1]
        sg = jax.nn.sigmoid(a)
        dau_ref[0] = (dact * u * (sg * (1.0 + a * (1.0 - sg)))).astype(dau_ref.dtype)
        dau_ref[1] = (dact * (a * sg)).astype(dau_ref.dtype)

    halves = pl.BlockSpec((2, tm, tn), lambda j, i: (0, i, j))
    return pl.pallas_call(
        body, grid=(f // tn, t // tm),
        in_specs=[pl.BlockSpec((tm, d), lambda j, i: (i, 0)), pl.BlockSpec((tn, d), lambda j, i: (j, 0)), halves],
        out_specs=halves, out_shape=jax.ShapeDtypeStruct((2, t, f), BF16),
        compiler_params=pltpu.CompilerParams(dimension_semantics=("parallel", "parallel"), vmem_limit_bytes=VMEM_BIG_LIMIT),
        name="ffn_dact",
    )(dy, w_out, au)


def _ffn_dh_call(dau, w_in):
    _, t, f = dau.shape
    d = w_in.shape[0]
    tm, tk = _ffn_tiles(t, f)
    nf = f // tk

    def body(a_ref, b_ref, o_ref):
        acc = None
        for c in range(2 * nf):
            part = _dot_nt(a_ref[c // nf, :, (c % nf) * tk:(c % nf + 1) * tk], b_ref[:, c * tk:(c + 1) * tk])
            acc = part if acc is None else acc + part
        o_ref[...] = acc.astype(o_ref.dtype)

    return pl.pallas_call(
        body, grid=(t // tm,),
        in_specs=[pl.BlockSpec((2, tm, f), lambda i: (0, i, 0)), pl.BlockSpec((d, 2 * f), lambda i: (0, 0))],
        out_specs=pl.BlockSpec((tm, d), lambda i: (i, 0)), out_shape=jax.ShapeDtypeStruct((t, d), BF16),
        compiler_params=pltpu.CompilerParams(dimension_semantics=("parallel",), vmem_limit_bytes=VMEM_BIG_LIMIT),
        name="ffn_dh",
    )(dau, w_in)


@jax.custom_vjp
def ffn(h, w_in, w_out):
    return _ffn_fwd(h, w_in, w_out)[0]


def _ffn_fwd(h, w_in, w_out):
    act, au = _ffn_in_call(h, w_in)
    return _mm_call(act, w_out, "mm_fwd"), (h, w_in, w_out, act, au)


def _ffn_bwd(res, dy):
    h, w_in, w_out, act, au = res
    dyb = dy.astype(BF16)
    dau = _ffn_dact_call(dyb, w_out, au)
    dw_in = jnp.concatenate([_mm_call(h, dau, "mm_dw", "tn", BF16, b_lead=c) for c in range(2)], axis=1)
    return _ffn_dh_call(dau, w_in), dw_in, _mm_call(act, dyb, "mm_dw", "tn", BF16)


ffn.defvjp(_ffn_fwd, _ffn_bwd)


NORM_ROWS = (768, 512, 256, 128)


def _seg_rows(ref, is_ctx):
    return jnp.where(is_ctx, ref[1:2, :], ref[0:1, :])


def _ctx_rows(tt, n_lat):
    return pl.program_id(0) * tt + lax.broadcasted_iota(jnp.int32, (tt, 1), 0) >= n_lat


def _one_hot_row(second):
    return (lax.broadcasted_iota(jnp.int32, (2, 1), 0) == second.astype(jnp.int32)).astype(F32)


def _norm_mod_fwd_call(x, g, shift, scale, n_lat, out_dtype):
    t, d = x.shape
    tt = _pick(t, NORM_ROWS)

    def body(x_ref, g_ref, sh_ref, sc_ref, y_ref):
        is_ctx = _ctx_rows(tt, n_lat)
        xv = x_ref[...]
        r = lax.rsqrt(jnp.mean(xv * xv, axis=-1, keepdims=True) + EPS)
        yn = xv * r * g_ref[...]
        y_ref[...] = (yn * (1.0 + _seg_rows(sc_ref, is_ctx)) + _seg_rows(sh_ref, is_ctx)).astype(y_ref.dtype)

    row = pl.BlockSpec((tt, d), lambda i: (i, 0))
    full = lambda rws: pl.BlockSpec((rws, d), lambda i: (0, 0))
    return pl.pallas_call(
        body, grid=(t // tt,), in_specs=[row, full(1), full(2), full(2)], out_specs=row,
        out_shape=jax.ShapeDtypeStruct((t, d), out_dtype),
        compiler_params=pltpu.CompilerParams(dimension_semantics=("parallel",)), name="norm_mod_fwd",
    )(x, g, shift, scale)


def _norm_mod_bwd_call(x, g, scale, dy, n_lat):
    t, d = x.shape
    tt = _pick(t, NORM_ROWS)

    def body(x_ref, g_ref, sc_ref, dy_ref, dx_ref, dg_ref, dsh_ref, dsc_ref):
        i = pl.program_id(0)
        is_ctx = _ctx_rows(tt, n_lat)

        @pl.when(i == 0)
        def _():
            dg_ref[...] = jnp.zeros_like(dg_ref)
            dsh_ref[...] = jnp.zeros_like(dsh_ref)
            dsc_ref[...] = jnp.zeros_like(dsc_ref)

        xv, dyv, gv = x_ref[...], dy_ref[...].astype(F32), g_ref[...]
        r = lax.rsqrt(jnp.mean(xv * xv, axis=-1, keepdims=True) + EPS)
        nrm = xv * r
        yn = nrm * gv
        for acc_ref, term in ((dsh_ref, dyv), (dsc_ref, dyv * yn)):
            both = jnp.sum(term, axis=0, keepdims=True)
            ctx = jnp.sum(jnp.where(is_ctx, term, 0.0), axis=0, keepdims=True)
            acc_ref[0:1, :] += both - ctx
            acc_ref[1:2, :] += ctx
        dyn = dyv * (1.0 + _seg_rows(sc_ref, is_ctx))
        dg_ref[...] += jnp.sum(dyn * nrm, axis=0, keepdims=True)
        dn = dyn * gv
        dx_ref[...] = r * (dn - nrm * jnp.mean(dn * nrm, axis=-1, keepdims=True))

    row = pl.BlockSpec((tt, d), lambda i: (i, 0))
    full = lambda rws: pl.BlockSpec((rws, d), lambda i: (0, 0))
    return pl.pallas_call(
        body, grid=(t // tt,), in_specs=[row, full(1), full(2), row], out_specs=[row, full(1), full(2), full(2)],
        out_shape=[jax.ShapeDtypeStruct((t, d), F32), jax.ShapeDtypeStruct((1, d), F32),
                   jax.ShapeDtypeStruct((2, d), F32), jax.ShapeDtypeStruct((2, d), F32)],
        compiler_params=pltpu.CompilerParams(dimension_semantics=("arbitrary",)), name="norm_mod_bwd",
    )(x, g, scale, dy)


@functools.partial(jax.custom_vjp, nondiff_argnums=(4, 5))
def norm_mod(x, g, shift, scale, n_lat, out_dtype):
    return _norm_mod_fwd_call(x, g, shift, scale, n_lat, out_dtype)


def _norm_mod_fwd(x, g, shift, scale, n_lat, out_dtype):
    return _norm_mod_fwd_call(x, g, shift, scale, n_lat, out_dtype), (x, g, scale)


def _norm_mod_bwd(n_lat, out_dtype, res, dy):
    x, g, scale = res
    dx, dg, dsh, dsc = _norm_mod_bwd_call(x, g, scale, dy, n_lat)
    return dx, dg, dsh, dsc


norm_mod.defvjp(_norm_mod_fwd, _norm_mod_bwd)


def _rot_matrix():
    p = np.zeros((HEAD_DIM, HEAD_DIM), np.float32)
    for i in range(HEAD_DIM // 2):
        p[2 * i + 1, 2 * i] = -1.0
        p[2 * i, 2 * i + 1] = 1.0
    return jnp.asarray(p)


def _hn_rope_fwd_call(x, gains, cos2, sin2, rot, n_q):
    nh, t, dh = x.shape
    tt = _pick(t, (2816, 1024, 768, 512, 256, 128))

    def body(x_ref, g_ref, cos_ref, sin_ref, rot_ref, y_ref):
        gv = jnp.where(pl.program_id(0) >= n_q, g_ref[1:2, :], g_ref[0:1, :])
        xv = x_ref[...]
        r = lax.rsqrt(jnp.mean(xv * xv, axis=-1, keepdims=True) + EPS)
        y = xv * r * gv
        yr = jnp.dot(y, rot_ref[...], precision=ROT_PRECISION, preferred_element_type=F32)
        y_ref[...] = y * cos_ref[...] + yr * sin_ref[...]

    blk = pl.BlockSpec((None, tt, dh), lambda h, i: (h, i, 0))
    tab = pl.BlockSpec((tt, dh), lambda h, i: (i, 0))
    return pl.pallas_call(
        body, grid=(nh, t // tt),
        in_specs=[blk, pl.BlockSpec((2, dh), lambda h, i: (0, 0)), tab, tab, pl.BlockSpec((dh, dh), lambda h, i: (0, 0))],
        out_specs=blk, out_shape=jax.ShapeDtypeStruct((nh, t, dh), F32),
        compiler_params=pltpu.CompilerParams(dimension_semantics=("parallel", "parallel")), name="hn_rope_fwd",
    )(x, gains, cos2, sin2, rot)


def _hn_rope_bwd_call(x, gains, cos2, sin2, rot, dy, n_q):
    nh, t, dh = x.shape
    tt = _pick(t, (2816, 1024, 768, 512, 256, 128))

    def body(x_ref, g_ref, cos_ref, sin_ref, rot_ref, dy_ref, dx_ref, dg_ref):
        h, i = pl.program_id(0), pl.program_id(1)
        is_k = h >= n_q

        @pl.when((h == 0) & (i == 0))
        def _():
            dg_ref[...] = jnp.zeros_like(dg_ref)

        gv = jnp.where(is_k, g_ref[1:2, :], g_ref[0:1, :])
        xv, dyv = x_ref[...], dy_ref[...]
        r = lax.rsqrt(jnp.mean(xv * xv, axis=-1, keepdims=True) + EPS)
        nrm = xv * r
        dyn = dyv * cos_ref[...] - jnp.dot(dyv * sin_ref[...], rot_ref[...], precision=ROT_PRECISION, preferred_element_type=F32)
        seg = _one_hot_row(is_k)
        dg_ref[...] += seg * jnp.sum(dyn * nrm, axis=0, keepdims=True)
        dn = dyn * gv
        dx_ref[...] = r * (dn - nrm * jnp.mean(dn * nrm, axis=-1, keepdims=True))

    blk = pl.BlockSpec((None, tt, dh), lambda h, i: (h, i, 0))
    tab = pl.BlockSpec((tt, dh), lambda h, i: (i, 0))
    g_spec = pl.BlockSpec((2, dh), lambda h, i: (0, 0))
    return pl.pallas_call(
        body, grid=(nh, t // tt),
        in_specs=[blk, g_spec, tab, tab, pl.BlockSpec((dh, dh), lambda h, i: (0, 0)), blk],
        out_specs=[blk, g_spec],
        out_shape=[jax.ShapeDtypeStruct((nh, t, dh), F32), jax.ShapeDtypeStruct((2, dh), F32)],
        compiler_params=pltpu.CompilerParams(dimension_semantics=("arbitrary", "arbitrary")), name="hn_rope_bwd",
    )(x, gains, cos2, sin2, rot, dy)


@functools.partial(jax.custom_vjp, nondiff_argnums=(5,))
def hn_rope(x, gains, cos2, sin2, rot, n_q):
    return _hn_rope_fwd_call(x, gains, cos2, sin2, rot, n_q)


def _hn_rope_fwd(x, gains, cos2, sin2, rot, n_q):
    return _hn_rope_fwd_call(x, gains, cos2, sin2, rot, n_q), (x, gains, cos2, sin2, rot)


def _hn_rope_bwd(n_q, res, dy):
    x, gains, cos2, sin2, rot = res
    dx, dg = _hn_rope_bwd_call(x, gains, cos2, sin2, rot, dy, n_q)
    return dx, dg, jnp.zeros_like(cos2), jnp.zeros_like(sin2), jnp.zeros_like(rot)


hn_rope.defvjp(_hn_rope_fwd, _hn_rope_bwd)


EXCHANGE_SCRATCH = [pltpu.SemaphoreType.DMA((N_DEV - 1,)), pltpu.SemaphoreType.DMA((N_DEV - 1,)), pltpu.SemaphoreType.DMA(())]


def _exchange_copies(x_ref, out_ref, send_sems, recv_sems, local_sem, all_gather):
    mx, my, mc = lax.axis_index("x"), lax.axis_index("y"), lax.axis_index("c")
    me = 4 * mx + 2 * my + mc
    src = (lambda p: x_ref) if all_gather else (lambda p: x_ref.at[p])
    local = pltpu.make_async_copy(src(me), out_ref.at[me], local_sem)
    remote = []
    for rel in range(1, N_DEV):
        px, py, pc = mx ^ (rel >> 2), my ^ ((rel >> 1) & 1), mc ^ (rel & 1)
        remote.append(pltpu.make_async_remote_copy(
            src_ref=src(4 * px + 2 * py + pc), dst_ref=out_ref.at[me],
            send_sem=send_sems.at[rel - 1], recv_sem=recv_sems.at[rel - 1],
            device_id=(px, py, pc), device_id_type=MESH))
    return local, remote


def _exchange_start(*refs, all_gather):
    local, remote = _exchange_copies(*refs, all_gather)
    local.start()
    for cp in remote:
        cp.start()


def _exchange_wait(*refs, all_gather):
    local, remote = _exchange_copies(*refs, all_gather)
    for cp in remote:
        cp.wait_send()
    for cp in remote:
        cp.wait_recv()
    local.wait()


def _exchange_shape(x, all_gather):
    return jax.ShapeDtypeStruct((N_DEV,) + tuple(x.shape if all_gather else x.shape[1:]), x.dtype)


def _exchange(x, all_gather, name):
    def body(x_ref, out_ref, *sems):
        _exchange_start(x_ref, out_ref, *sems, all_gather=all_gather)
        _exchange_wait(x_ref, out_ref, *sems, all_gather=all_gather)

    return pl.pallas_call(
        body, in_specs=[pl.BlockSpec(memory_space=pl.ANY)], out_specs=pl.BlockSpec(memory_space=pl.ANY),
        out_shape=_exchange_shape(x, all_gather), scratch_shapes=EXCHANGE_SCRATCH,
        compiler_params=pltpu.CompilerParams(has_side_effects=True), name=name,
    )(x)


ATT_TK = 256


def _att_tq(rn, fwd=False):
    return _pick(rn, ((1024,) if fwd else ()) + (512, 256, 128))


def _att_unroll(nkb):
    return _pick(nkb, (3, 2))


def _flash_fwd_call(q_t, k, v_t, xsend=None):
    g, dh, rn = q_t.shape
    nk = k.shape[1]
    tq, tk = _att_tq(rn, True), ATT_TK
    nkb = nk // tk
    unroll = _att_unroll(nkb)
    n_i = rn // tq

    trips = nkb // unroll

    def body(*refs):
        if xsend is None:
            qt_ref, k_ref, vt_ref, ot_ref, lse_ref, s_scr = refs
        else:
            qt_ref, k_ref, vt_ref, x_ref, ot_ref, lse_ref, xout_ref, s_scr, *sems = refs
            a, i = pl.program_id(0), pl.program_id(1)

            @pl.when((a == 0) & (i == 0))
            def _():
                _exchange_start(x_ref, xout_ref, *sems, all_gather=True)

        qst = qt_ref[...] * SCALE

        def scores(t, slot):
            for u in range(unroll):
                off = pl.multiple_of((t * unroll + u) * tk, tk)
                s_scr[slot, u] = _dot(k_ref[pl.ds(off, tk), :], qst)

        def consume(t, slot, carry):
            m, l, acc = carry
            s_t = [s_scr[slot, u] for u in range(unroll)]
            m_new = functools.reduce(jnp.maximum, [jnp.max(s, axis=0, keepdims=True) for s in s_t], m)
            p_t = [jnp.exp(s - m_new) for s in s_t]
            alpha = jnp.exp(m - m_new)
            l = alpha * l + sum(jnp.sum(p, axis=0, keepdims=True) for p in p_t)
            acc = alpha * acc + sum(_dot(vt_ref[t * unroll + u], p.astype(BF16)) for u, p in enumerate(p_t))
            return m_new, l, acc

        def pair(u2, carry):
            t = 2 * u2
            scores(t + 1, 1)
            carry = consume(t, 0, carry)
            scores(t + 2, 0)
            return consume(t + 1, 1, carry)

        scores(0, 0)
        pairs = (trips - 1) // 2
        carry = lax.fori_loop(
            0, pairs, pair, (jnp.full((1, tq), NEG_BIG, F32), jnp.zeros((1, tq), F32), jnp.zeros((dh, tq), F32)))
        if trips - 2 * pairs == 2:
            scores(trips - 1, 1)
            carry = consume(trips - 2, 0, carry)
            carry = consume(trips - 1, 1, carry)
        else:
            carry = consume(trips - 1, 0, carry)
        m, l, acc = carry
        ot_ref[...] = acc / l
        lse_ref[...] = m + jnp.log(l)

        if xsend is not None:
            @pl.when((a == g - 1) & (i == n_i - 1))
            def _():
                _exchange_wait(x_ref, xout_ref, *sems, all_gather=True)

    col = pl.BlockSpec((None, dh, tq), lambda a, i: (a, 0, i))
    vec = pl.BlockSpec((None, 1, tq), lambda a, i: (a, 0, i))
    hbm = pl.BlockSpec(memory_space=pl.ANY)
    in_specs = [col, pl.BlockSpec((None, nk, dh), lambda a, i: (a, 0, 0)),
                pl.BlockSpec((None, nkb, dh, tk), lambda a, i: (a, 0, 0, 0))]
    out_specs = [col, vec]
    out_shape = [jax.ShapeDtypeStruct((g, dh, rn), F32), jax.ShapeDtypeStruct((g, 1, rn), F32)]
    args = (q_t, k, v_t)
    if xsend is not None:
        in_specs, out_specs, args = in_specs + [hbm], out_specs + [hbm], args + (xsend,)
        out_shape = out_shape + [_exchange_shape(xsend, True)]
    return pl.pallas_call(
        body, grid=(g, n_i), in_specs=in_specs, out_specs=out_specs, out_shape=out_shape,
        scratch_shapes=[pltpu.VMEM((2, unroll, tk, tq), F32)] + ([] if xsend is None else EXCHANGE_SCRATCH),
        compiler_params=pltpu.CompilerParams(dimension_semantics=("arbitrary", "arbitrary"), vmem_limit_bytes=VMEM_BIG_LIMIT),
        name="flash_fwd" if xsend is None else "flash_fwd_gather",
    )(*args)


def _flash_bwd_call(q, q_t, k, k_t, v, do, do_t, o_t, lse_t, xsend=None):
    g, rn, dh = q.shape
    nk = k.shape[1]
    tq, tk = _att_tq(rn), ATT_TK
    nkb = nk // tk
    unroll = _att_unroll(nkb)
    n_i = rn // tq

    def body(*refs):
        q_ref, qt_ref, k_ref, kt_ref, v_ref, do_ref, dot_ref, ot_ref, lse_ref = refs[:9]
        if xsend is None:
            dqt_ref, dk_ref, dv_ref = refs[9:]
        else:
            x_ref, dqt_ref, dk_ref, dv_ref, xout_ref, *sems = refs[9:]
        a, i = pl.program_id(0), pl.program_id(1)

        if xsend is not None:
            @pl.when((a == 0) & (i == 0))
            def _():
                _exchange_start(x_ref, xout_ref, *sems, all_gather=False)

        @pl.when(i == 0)
        def _():
            dk_ref[...] = jnp.zeros_like(dk_ref)
            dv_ref[...] = jnp.zeros_like(dv_ref)

        qs = q_ref[...] * SCALE
        qst = qt_ref[...] * SCALE
        dov, dotv = do_ref[...], dot_ref[...]
        delta = jnp.sum(dotv.astype(F32) * ot_ref[...], axis=0, keepdims=True)
        lse = lse_ref[...]

        def step(j, dqt):
            off = pl.multiple_of(j * tk, tk)
            kj = k_ref[pl.ds(off, tk), :]
            vj = v_ref[pl.ds(off, tk), :]
            p_t = jnp.exp(_dot(kj, qst) - lse)
            ds_t = (p_t * (_dot(vj, dotv) - delta)).astype(BF16)
            dv_ref[pl.ds(off, tk), :] += _dot(p_t.astype(BF16), dov)
            dk_ref[pl.ds(off, tk), :] += _dot(ds_t, qs)
            return dqt + _dot(kt_ref[j], ds_t)

        def trip(t, dqt):
            for u in range(unroll):
                dqt = step(t * unroll + u, dqt)
            return dqt

        dqt_ref[...] = lax.fori_loop(0, nkb // unroll, trip, jnp.zeros((dh, tq), F32)) * SCALE

        if xsend is not None:
            @pl.when((a == g - 1) & (i == n_i - 1))
            def _():
                _exchange_wait(x_ref, xout_ref, *sems, all_gather=False)

    row = pl.BlockSpec((None, tq, dh), lambda a, i: (a, i, 0))
    col = pl.BlockSpec((None, dh, tq), lambda a, i: (a, 0, i))
    kv = pl.BlockSpec((None, nk, dh), lambda a, i: (a, 0, 0))
    hbm = pl.BlockSpec(memory_space=pl.ANY)
    in_specs = [row, col, kv, pl.BlockSpec((None, nkb, dh, tk), lambda a, i: (a, 0, 0, 0)), kv, row, col, col,
                pl.BlockSpec((None, 1, tq), lambda a, i: (a, 0, i))]
    out_specs = [col, kv, kv]
    out_shape = [jax.ShapeDtypeStruct((g, dh, rn), F32), jax.ShapeDtypeStruct((g, nk, dh), F32),
                 jax.ShapeDtypeStruct((g, nk, dh), F32)]
    args = (q, q_t, k, k_t, v, do, do_t, o_t, lse_t)
    if xsend is not None:
        in_specs, out_specs, args = in_specs + [hbm], out_specs + [hbm], args + (xsend,)
        out_shape = out_shape + [_exchange_shape(xsend, False)]
    return pl.pallas_call(
        body, grid=(g, n_i), in_specs=in_specs, out_specs=out_specs, out_shape=out_shape,
        scratch_shapes=[] if xsend is None else EXCHANGE_SCRATCH,
        compiler_params=pltpu.CompilerParams(dimension_semantics=("arbitrary", "arbitrary"), vmem_limit_bytes=VMEM_BIG_LIMIT),
        name="flash_bwd" if xsend is None else "flash_bwd_scatter",
    )(*args)


def _key_blocks_t(t):
    g, nk, dh = t.shape
    return t.reshape(g, nk // ATT_TK, ATT_TK, dh).transpose(0, 1, 3, 2)


@jax.custom_vjp
def attn(q, k, v):
    return _attn_fwd(q, k, v)[0]


def _attn_fwd(q, k, v):
    qb, kb, vb = q.astype(BF16), k.astype(BF16), v.astype(BF16)
    q_t = qb.transpose(0, 2, 1)
    o_t, lse_t = _flash_fwd_call(q_t, kb, _key_blocks_t(vb))
    return o_t.transpose(0, 2, 1), (qb, q_t, kb, vb, o_t, lse_t)


def _attn_bwd(res, do):
    qb, q_t, kb, vb, o_t, lse_t = res
    dob = do.astype(BF16)
    dq_t, dk, dv = _flash_bwd_call(qb, q_t, kb, _key_blocks_t(kb), vb, dob, dob.transpose(0, 2, 1), o_t, lse_t)
    return dq_t.transpose(0, 2, 1), dk, dv


attn.defvjp(_attn_fwd, _attn_bwd)


def _shard_rows(shards):
    return jnp.concatenate([t.reshape(-1, D_MODEL) for t in shards.values()], axis=0).astype(BF16)


def _fulls_of(gathered, shards):
    out, off = {}, 0
    for n in shards:
        k, nn = shards[n].shape
        r = k * nn // D_MODEL
        t = gathered[:, off:off + r].reshape(N_DEV, k, nn)
        out[n] = t.reshape(N_DEV * k, nn) if n in ROW_SHARDED else t.transpose(1, 0, 2).reshape(k, N_DEV * nn)
        off += r
    return out


def _slabs_of(dfulls):
    parts = []
    for n in dfulls:
        k, nn = dfulls[n].shape
        t = (dfulls[n].reshape(N_DEV, k // N_DEV, nn) if n in ROW_SHARDED
             else dfulls[n].reshape(k, N_DEV, nn // N_DEV).transpose(1, 0, 2))
        parts.append(t.reshape(N_DEV, -1, D_MODEL))
    return jnp.concatenate(parts, axis=1)


def _dshards_of(parts, dfulls):
    rows, out, off = _sum8_call(parts), {}, 0
    for n in dfulls:
        k, nn = dfulls[n].shape
        shape = (k // N_DEV, nn) if n in ROW_SHARDED else (k, nn // N_DEV)
        r = shape[0] * shape[1] // D_MODEL
        out[n] = rows[off:off + r].reshape(shape)
        off += r
    return out


@jax.custom_vjp
def gather_layer(shards):
    return _fulls_of(_exchange(_shard_rows(shards), True, "gather_weights"), shards)


def _gather_layer_fwd(shards):
    return gather_layer(shards), None


def _gather_layer_bwd(_, dfulls):
    return (_dshards_of(_exchange(_slabs_of(dfulls), False, "scatter_grads"), dfulls),)


gather_layer.defvjp(_gather_layer_fwd, _gather_layer_bwd)


@jax.custom_vjp
def attn_gather(q, k, v, shards):
    return _attn_gather_fwd(q, k, v, shards)[0]


def _attn_gather_fwd(q, k, v, shards):
    qb, kb, vb = q.astype(BF16), k.astype(BF16), v.astype(BF16)
    q_t = qb.transpose(0, 2, 1)
    o_t, lse_t, gathered = _flash_fwd_call(q_t, kb, _key_blocks_t(vb), _shard_rows(shards))
    return (o_t.transpose(0, 2, 1), _fulls_of(gathered, shards)), (qb, q_t, kb, vb, o_t, lse_t)


def _attn_gather_bwd(res, cts):
    qb, q_t, kb, vb, o_t, lse_t = res
    do, dfulls = cts
    dob = do.astype(BF16)
    dq_t, dk, dv, parts = _flash_bwd_call(qb, q_t, kb, _key_blocks_t(kb), vb, dob, dob.transpose(0, 2, 1), o_t, lse_t,
                                          _slabs_of(dfulls))
    return dq_t.transpose(0, 2, 1), dk, dv, _dshards_of(parts, dfulls)


attn_gather.defvjp(_attn_gather_fwd, _attn_gather_bwd)


def _na_key_row(i, rows):
    return jnp.clip(NA_QROWS * i - NA_WIN_H // 2, 0, rows - NA_KROWS)


def _na_type(i, nb):
    return jnp.where(i == 0, 0, jnp.where(i == nb - 1, 2, 1))


def _na_fwd_call(q, k, v, kc, vc, slab):
    h, s, dh = q.shape
    n_ctx = kc.shape[1]
    rows, nb = s // GRID_W, s // NA_QB
    hb = NA_FWD_HEADS

    def body(q_ref, k_ref, v_ref, kc_ref, vc_ref, slab_ref, o_ref, lse_ref):
        off = pl.multiple_of(_na_key_row(pl.program_id(1), rows) * GRID_W, NA_QB)
        for hh in range(hb):
            qs = q_ref[hh] * SCALE
            kw = k_ref[hh, pl.ds(off, NA_KW), :]
            vw = v_ref[hh, pl.ds(off, NA_KW), :]
            sw = _dot_nt(qs, kw) + slab_ref[hh]
            sc = _dot_nt(qs, kc_ref[hh])
            m = jnp.maximum(jnp.max(sw, axis=1, keepdims=True), jnp.max(sc, axis=1, keepdims=True))
            pw = jnp.exp(sw - m)
            pc = jnp.exp(sc - m)
            l = jnp.sum(pw, axis=1, keepdims=True) + jnp.sum(pc, axis=1, keepdims=True)
            o_ref[hh] = (_dot(pw.astype(BF16), vw) + _dot(pc.astype(BF16), vc_ref[hh])) / l
            lse_ref[hh] = m + jnp.log(l)

    qblk = pl.BlockSpec((hb, NA_QB, dh), lambda a, i: (a, i, 0))
    kv = pl.BlockSpec((hb, s, dh), lambda a, i: (a, 0, 0))
    ckv = pl.BlockSpec((hb, n_ctx, dh), lambda a, i: (a, 0, 0))
    return pl.pallas_call(
        body, grid=(h // hb, nb),
        in_specs=[qblk, kv, kv, ckv, ckv, pl.BlockSpec((None, hb, NA_QB, NA_KW), lambda a, i: (_na_type(i, nb), a, 0, 0))],
        out_specs=[qblk, pl.BlockSpec((hb, NA_QB, 1), lambda a, i: (a, i, 0))],
        out_shape=[jax.ShapeDtypeStruct((h, s, dh), F32), jax.ShapeDtypeStruct((h, s, 1), F32)],
        compiler_params=pltpu.CompilerParams(dimension_semantics=("parallel", "parallel"), vmem_limit_bytes=VMEM_BIG_LIMIT),
        name="na_fwd",
    )(q, k, v, kc, vc, slab)


def _na_bwd_call(q, k, v, kc, vc, slab, do, o, lse):
    h, s, dh = q.shape
    n_ctx = kc.shape[1]
    rows, nb = s // GRID_W, s // NA_QB
    half = nb // 2

    def body(q_ref, k_ref, v_ref, kc_ref, vc_ref, slab0_ref, slab1_ref, do_ref, o_ref, lse_ref,
             dq_ref, dk_ref, dv_ref, dkc_ref, dvc_ref, dslab0_ref, dslab1_ref):
        j = pl.program_id(1)

        @pl.when(j == 0)
        def _():
            dk_ref[...] = jnp.zeros_like(dk_ref)
            dv_ref[...] = jnp.zeros_like(dv_ref)
            dkc_ref[...] = jnp.zeros_like(dkc_ref)
            dvc_ref[...] = jnp.zeros_like(dvc_ref)

        @pl.when(j <= 1)
        def _():
            dslab0_ref[...] = jnp.zeros_like(dslab0_ref)

        @pl.when((j == 0) | (j == half - 1))
        def _():
            dslab1_ref[...] = jnp.zeros_like(dslab1_ref)

        kcv, vcv = kc_ref[...], vc_ref[...]
        for sub, (slab_ref, dslab_ref) in enumerate(((slab0_ref, dslab0_ref), (slab1_ref, dslab1_ref))):
            off = pl.multiple_of(_na_key_row(2 * j + sub, rows) * GRID_W, NA_QB)
            blk = slice(sub * NA_QB, (sub + 1) * NA_QB)
            qs = q_ref[blk, :] * SCALE
            kw = k_ref[pl.ds(off, NA_KW), :]
            vw = v_ref[pl.ds(off, NA_KW), :]
            dov = do_ref[blk, :]
            lse = lse_ref[blk, :]
            delta = jnp.sum(dov.astype(F32) * o_ref[blk, :], axis=1, keepdims=True)
            pw = jnp.exp(_dot_nt(qs, kw) + slab_ref[...] - lse)
            pc = jnp.exp(_dot_nt(qs, kcv) - lse)
            dsw = pw * (_dot_nt(dov, vw) - delta)
            dsc = pc * (_dot_nt(dov, vcv) - delta)
            dslab_ref[...] += dsw
            dsw, dsc = dsw.astype(BF16), dsc.astype(BF16)
            dq_ref[blk, :] = (_dot(dsw, kw) + _dot(dsc, kcv)) * SCALE
            dk_ref[pl.ds(off, NA_KW), :] += _dot_tn(dsw, qs)
            dv_ref[pl.ds(off, NA_KW), :] += _dot_tn(pw.astype(BF16), dov)
            dkc_ref[...] += _dot_tn(dsc, qs)
            dvc_ref[...] += _dot_tn(pc.astype(BF16), dov)

    qblk = pl.BlockSpec((None, 2 * NA_QB, dh), lambda a, j: (a, j, 0))
    kv = pl.BlockSpec((None, s, dh), lambda a, j: (a, 0, 0))
    ckv = pl.BlockSpec((None, n_ctx, dh), lambda a, j: (a, 0, 0))
    slab_spec = lambda sub: pl.BlockSpec((None, None, NA_QB, NA_KW), lambda a, j: (_na_type(2 * j + sub, nb), a, 0, 0))
    dslab0, dslab1 = jax.ShapeDtypeStruct(slab.shape, F32), jax.ShapeDtypeStruct(slab.shape, F32)
    dq, dk, dv, dkc, dvc, d0, d1 = pl.pallas_call(
        body, grid=(h, half),
        in_specs=[qblk, kv, kv, ckv, ckv, slab_spec(0), slab_spec(1), qblk, qblk,
                  pl.BlockSpec((None, 2 * NA_QB, 1), lambda a, j: (a, j, 0))],
        out_specs=[qblk, kv, kv, ckv, ckv, slab_spec(0), slab_spec(1)],
        out_shape=[jax.ShapeDtypeStruct((h, s, dh), F32), jax.ShapeDtypeStruct((h, s, dh), F32),
                   jax.ShapeDtypeStruct((h, s, dh), F32), jax.ShapeDtypeStruct((h, n_ctx, dh), F32),
                   jax.ShapeDtypeStruct((h, n_ctx, dh), F32), dslab0, dslab1],
        compiler_params=pltpu.CompilerParams(dimension_semantics=("arbitrary", "arbitrary"), vmem_limit_bytes=VMEM_BIG_LIMIT),
        name="na_bwd",
    )(q, k, v, kc, vc, slab, slab, do, o, lse)
    return dq, dk, dv, dkc, dvc, jnp.stack([d0[0], d0[1] + d1[1], d1[2]])


@jax.custom_vjp
def na_attn(q, k, v, kc, vc, slab):
    return _na_fwd_call(q.astype(BF16), k.astype(BF16), v.astype(BF16), kc.astype(BF16), vc.astype(BF16), slab)[0]


def _na_attn_fwd(q, k, v, kc, vc, slab):
    qb, kb, vb, kcb, vcb = (t.astype(BF16) for t in (q, k, v, kc, vc))
    o, lse = _na_fwd_call(qb, kb, vb, kcb, vcb, slab)
    return o, (qb, kb, vb, kcb, vcb, slab, o, lse)


def _na_attn_bwd(res, do):
    qb, kb, vb, kcb, vcb, slab, o, lse = res
    return tuple(_na_bwd_call(qb, kb, vb, kcb, vcb, slab, do.astype(BF16), o, lse))


na_attn.defvjp(_na_attn_fwd, _na_attn_bwd)


def _na_tables(rows):
    col = np.arange(GRID_W)
    c_start = np.clip(col - NA_WIN_W // 2, 0, GRID_W - NA_WIN_W)
    in_win = (col[None, :] >= c_start[:, None]) & (col[None, :] < c_start[:, None] + NA_WIN_W)
    dc_idx = np.clip(col[None, :] - col[:, None], -(NA_WIN_W - 1), NA_WIN_W - 1) + NA_WIN_W - 1
    onehot = (in_win[:, :, None] & (dc_idx[:, :, None] == np.arange(2 * NA_WIN_W - 1)[None, None, :])).astype(np.float32)
    negcol = np.where(in_win, 0.0, NEG_BIG).astype(np.float32)
    plan = []
    for r0, ks in [(0, 0), (NA_QROWS, 0), (rows - NA_QROWS, rows - NA_KROWS)]:
        per_row = []
        for a in range(NA_QROWS):
            r = r0 + a
            rs = min(max(r - NA_WIN_H // 2, 0), rows - NA_WIN_H)
            valid = np.array([rs <= ks + b < rs + NA_WIN_H for b in range(NA_KROWS)])
            per_row.append((ks - r + NA_WIN_H - 1, valid))
        plan.append(per_row)
    return jnp.asarray(onehot), jnp.asarray(negcol), plan


def _na_slab(rpb, tables):
    onehot, negcol, plan = tables
    n_dr = 2 * NA_WIN_H - 1
    table = jnp.einsum("hdc,wuc->hwdu", rpb, onehot, precision=HIGHEST) + negcol[None, :, None, :]
    types = []
    for per_row in plan:
        slabs = []
        for first, valid in per_row:
            lo, hi = max(first, 0), min(first + NA_KROWS, n_dr)
            sel = jnp.pad(table[:, :, lo:hi, :], ((0, 0), (0, 0), (lo - first, first + NA_KROWS - hi), (0, 0)),
                          constant_values=NEG_BIG)
            sel = jnp.where(jnp.asarray(valid)[None, None, :, None], sel, NEG_BIG)
            slabs.append(sel.reshape(NA_HEADS, GRID_W, NA_KW))
        types.append(jnp.stack(slabs, axis=1).reshape(NA_HEADS, NA_QB, NA_KW))
    return jnp.stack(types)


def _loss_call(y, target):
    s, d = y.shape
    tt = _pick(s, (512, 256, 128))

    def body(y_ref, t_ref, loss_ref, diff_ref):
        @pl.when(pl.program_id(0) == 0)
        def _():
            loss_ref[...] = jnp.zeros_like(loss_ref)

        e = y_ref[...] - t_ref[...]
        diff_ref[...] = e * (1.0 / d)
        loss_ref[...] += 0.5 * jnp.sum(jnp.mean(e * e, axis=-1, keepdims=True), axis=0, keepdims=True)

    row = pl.BlockSpec((tt, d), lambda i: (i, 0))
    return pl.pallas_call(
        body, grid=(s // tt,), in_specs=[row, row], out_specs=[pl.BlockSpec((1, 1), lambda i: (0, 0)), row],
        out_shape=[jax.ShapeDtypeStruct((1, 1), F32), jax.ShapeDtypeStruct((s, d), F32)],
        compiler_params=pltpu.CompilerParams(dimension_semantics=("arbitrary",)), name="loss_head",
    )(y, target)


@jax.custom_vjp
def loss_head(y, target):
    return _loss_call(y, target)[0][0, 0]


def _loss_head_fwd(y, target):
    loss, diff = _loss_call(y, target)
    return loss[0, 0], diff


def _loss_head_bwd(diff, g):
    return diff * g, jnp.zeros_like(diff)


loss_head.defvjp(_loss_head_fwd, _loss_head_bwd)


def _sum8_call(parts):
    _, r, c = parts.shape
    tr = _pick(r, (328, 256, 128))

    def body(p_ref, g_ref):
        g = p_ref[0].astype(F32)
        for s in range(1, N_DEV):
            g = g + p_ref[s].astype(F32)
        g_ref[...] = g

    return pl.pallas_call(
        body, grid=(r // tr,), in_specs=[pl.BlockSpec((N_DEV, tr, c), lambda i: (0, i, 0))],
        out_specs=pl.BlockSpec((tr, c), lambda i: (i, 0)), out_shape=jax.ShapeDtypeStruct((r, c), F32),
        compiler_params=pltpu.CompilerParams(dimension_semantics=("parallel",)), name="sum8",
    )(parts)


def _adam_call(g, w, m, v):
    r, c = g.shape
    tr = _pick(r, (256, 128))

    def body(g_ref, w_ref, m_ref, v_ref, d_ref, nm_ref, nv_ref):
        gv = g_ref[...]
        mn = ADAM_B1 * m_ref[...] + (1.0 - ADAM_B1) * gv
        vn = ADAM_B2 * v_ref[...] + (1.0 - ADAM_B2) * (gv * gv)
        m_hat = mn / (1.0 - ADAM_B1 ** ADAM_STEP)
        v_hat = vn / (1.0 - ADAM_B2 ** ADAM_STEP)
        d_ref[...] = -ADAM_LR * (m_hat / (jnp.sqrt(v_hat) + ADAM_EPS) + ADAM_WD * w_ref[...])
        nm_ref[...] = mn
        nv_ref[...] = vn

    row = pl.BlockSpec((tr, c), lambda i: (i, 0))
    out = jax.ShapeDtypeStruct((r, c), F32)
    return pl.pallas_call(
        body, grid=(r // tr,), in_specs=[row, row, row, row], out_specs=[row, row, row], out_shape=[out, out, out],
        compiler_params=pltpu.CompilerParams(dimension_semantics=("parallel",)), name="adam",
    )(g, w, m, v)


def _pack_small(vals, extra=None):
    flat = [vals[n].reshape(-1) for n in SMALL] + ([] if extra is None else [extra.reshape(-1)])
    flat = jnp.concatenate(flat)
    rows = -(-flat.shape[0] // D_MODEL)
    rows = -(-rows // 8) * 8
    return jnp.pad(flat, (0, rows * D_MODEL - flat.shape[0])).reshape(rows, D_MODEL)


def _unpack_small(packed, like):
    flat, out, off = packed.reshape(-1), {}, 0
    for n in SMALL:
        size = like[n].size
        out[n] = flat[off:off + size].reshape(like[n].shape)
        off += size
    return out, flat[off]


def _rope_tables(s, n_ctx):
    t = jnp.arange(s)
    row = (t // GRID_W).astype(F32)
    col = (t % GRID_W).astype(F32)
    half = HEAD_DIM // 2
    inv = ROPE_THETA ** (-jnp.arange(0, half, 2, dtype=F32) / half)
    ang = jnp.concatenate([row[:, None] * inv, col[:, None] * inv], axis=-1)
    cos2 = jnp.repeat(jnp.cos(ang), 2, axis=-1)
    sin2 = jnp.repeat(jnp.sin(ang), 2, axis=-1)
    cos2 = jnp.concatenate([cos2, jnp.ones((n_ctx, HEAD_DIM), F32)], axis=0)
    sin2 = jnp.concatenate([sin2, jnp.zeros((n_ctx, HEAD_DIM), F32)], axis=0)
    return cos2, sin2


def _to_heads(t, n_heads):
    return t.reshape(t.shape[0], n_heads, HEAD_DIM).transpose(1, 0, 2)


def _from_heads(t):
    return t.transpose(1, 0, 2).reshape(t.shape[1], t.shape[0] * HEAD_DIM)


def _local_loss(shards, p, x, c, ctx, target, consts):
    s, n_ctx = x.shape[0], ctx.shape[0]
    cos2, sin2, rot, na_tables = consts
    is_ctx = (jnp.arange(s + n_ctx) >= s)[:, None]
    seg = lambda rows2: jnp.where(is_ctx, rows2[1:2], rows2[0:1])
    xa = jnp.concatenate([x, ctx], axis=0)
    cond = jnp.concatenate([jax.nn.silu(c), jax.nn.silu(p["c_ctx"])[None, :],
                            jnp.zeros((MOD_ROWS - 2, D_MODEL), F32)], axis=0).astype(BF16)
    pick = lambda names, l: {n: shards[n][l] for n in names}
    w = gather_layer(pick(EARLY, 0))

    for l in range(DEPTH):
        mod = (mm(cond, w["w_mod"]) + p["b_mod"][l])[:2]
        sh1, sc1, g1, sh2, sc2, g2 = jnp.split(mod, 6, axis=-1)

        h = norm_mod(xa, p["norm1"][l][None, :], sh1, sc1, s, BF16)
        na_q, na_k, na_v, gq, gk, gv, ga, gb = jnp.split(mm(h, w["w_in"]), IN_SPLITS, axis=-1)

        qa, ka, va = _to_heads(na_q, NA_HEADS), _to_heads(na_k, NA_HEADS), _to_heads(na_v, NA_HEADS)
        slab = _na_slab(p["na_rpb"][l], na_tables)
        ya_lat = na_attn(qa[:, :s], ka[:, :s], va[:, :s], ka[:, s:], va[:, s:], slab)
        ya_ctx = attn(qa[:, s:], ka[:, s:], va[:, s:])
        ya = _from_heads(jnp.concatenate([ya_lat, ya_ctx], axis=1))

        qk = _to_heads(jnp.concatenate([gq, gk], axis=-1), GQA_Q_HEADS + GQA_KV_HEADS)
        gains = jnp.stack([p["q_gain"][l], p["k_gain"][l]])
        qk = hn_rope(qk, gains, cos2, sin2, rot, GQA_Q_HEADS)
        qb, kb, vb = qk[:GQA_Q_HEADS], qk[GQA_Q_HEADS:], _to_heads(gv, GQA_KV_HEADS)
        q_lat = qb[:, :s].reshape(GQA_KV_HEADS, GQA_REP * s, HEAD_DIM)
        ob_lat, got = attn_gather(q_lat, kb, vb, {**pick(LATE, l), **(pick(EARLY, l + 1) if l + 1 < DEPTH else {})})
        w = {n: got[n] for n in LATE}
        ob_ctx = attn(qb[:, s:].reshape(GQA_KV_HEADS, GQA_REP * n_ctx, HEAD_DIM), kb[:, s:], vb[:, s:])
        yb = _from_heads(jnp.concatenate([ob_lat.reshape(GQA_Q_HEADS, s, HEAD_DIM),
                                          ob_ctx.reshape(GQA_Q_HEADS, n_ctx, HEAD_DIM)], axis=1))

        merged = jax.nn.sigmoid(ga) * mm(ya.astype(BF16), w["w_pa"]) + jax.nn.sigmoid(gb) * mm(yb.astype(BF16), w["w_pb"])
        xa = xa + seg(g1) * mm(merged.astype(BF16), w["w_o"])

        h2 = norm_mod(xa, p["norm2"][l][None, :], sh2, sc2, s, BF16)
        xa = xa + seg(g2) * ffn(h2, w["w_ffn_in"], w["w_ffn_out"])
        w = {n: got[n] for n in EARLY if n in got}

    zeros2 = jnp.zeros((2, D_MODEL), F32)
    y = norm_mod(xa, p["final_norm"][None, :], zeros2, zeros2, s, F32)[:s]
    return loss_head(y, target)


def kernel(x, c, ctx, c_ctx, w_mod, b_mod, norm1, w_in, na_rpb, q_gain, k_gain, w_pa, w_pb, w_o, norm2, w_ffn_in, w_ffn_out, final_norm, loss_target, m_c_ctx, m_w_mod, m_b_mod, m_norm1, m_w_in, m_na_rpb, m_q_gain, m_k_gain, m_w_pa, m_w_pb, m_w_o, m_norm2, m_w_ffn_in, m_w_ffn_out, m_final_norm, v_c_ctx, v_w_mod, v_b_mod, v_norm1, v_w_in, v_na_rpb, v_q_gain, v_k_gain, v_w_pa, v_w_pb, v_w_o, v_norm2, v_w_ffn_in, v_w_ffn_out, v_final_norm):
    w = dict(c_ctx=c_ctx, w_mod=w_mod, b_mod=b_mod, norm1=norm1, w_in=w_in, na_rpb=na_rpb, q_gain=q_gain, k_gain=k_gain,
             w_pa=w_pa, w_pb=w_pb, w_o=w_o, norm2=norm2, w_ffn_in=w_ffn_in, w_ffn_out=w_ffn_out, final_norm=final_norm)
    mom = dict(c_ctx=m_c_ctx, w_mod=m_w_mod, b_mod=m_b_mod, norm1=m_norm1, w_in=m_w_in, na_rpb=m_na_rpb, q_gain=m_q_gain,
               k_gain=m_k_gain, w_pa=m_w_pa, w_pb=m_w_pb, w_o=m_w_o, norm2=m_norm2, w_ffn_in=m_w_ffn_in,
               w_ffn_out=m_w_ffn_out, final_norm=m_final_norm)
    var = dict(c_ctx=v_c_ctx, w_mod=v_w_mod, b_mod=v_b_mod, norm1=v_norm1, w_in=v_w_in, na_rpb=v_na_rpb, q_gain=v_q_gain,
               k_gain=v_k_gain, w_pa=v_w_pa, w_pb=v_w_pb, w_o=v_w_o, norm2=v_norm2, w_ffn_in=v_w_ffn_in,
               w_ffn_out=v_w_ffn_out, final_norm=v_final_norm)
    s, n_ctx = x.shape[1], ctx.shape[1]
    depth = w_mod.shape[0]

    shards = {n: [w[n][l] for l in range(depth)] for n in BIG}
    small = {n: w[n] for n in SMALL}
    consts = (*_rope_tables(s, n_ctx), _rot_matrix(), _na_tables(s // GRID_W))
    loss, (g_shards, g_small, gx) = jax.value_and_grad(_local_loss, argnums=(0, 1, 2))(
        shards, small, x[0], c, ctx[0], loss_target[0], consts)

    parts_s = _exchange(_pack_small(g_small, loss), True, "gather_small_grads")
    g_packed = _sum8_call(parts_s)
    grads, loss = _unpack_small(g_packed, w)
    zero = jnp.zeros((1,), F32)
    upd_s = _adam_call(g_packed, _pack_small(w, zero), _pack_small(mom, zero), _pack_small(var, zero))
    outs = [grads] + [_unpack_small(u, w)[0] for u in upd_s]

    for n in BIG:
        flat = lambda t: t.reshape(-1, t.shape[-1])
        g = jnp.stack(g_shards[n])
        upd = _adam_call(flat(g), flat(w[n]), flat(mom[n]), flat(var[n]))
        for k, t in enumerate([g] + [u.reshape(w[n].shape) for u in upd]):
            outs[k][n] = t
    return (loss, gx[None], *[o[n] for o in outs for n in WEIGHTS])
```

```python
import functools

import numpy as np
import jax
import jax.numpy as jnp
from jax import lax
from jax.experimental import pallas as pl
from jax.experimental.pallas import tpu as pltpu

F32 = jnp.float32
BF16 = jnp.bfloat16
HIGHEST = lax.Precision.HIGHEST
ROT_PRECISION = lax.Precision.HIGH

D_MODEL = 1024
DEPTH = 4
GRID_W = 64
HEAD_DIM = 64
NA_HEADS = 8
NA_WIN_H = 8
NA_WIN_W = 16
GQA_Q_HEADS = 8
GQA_KV_HEADS = 2
GQA_REP = GQA_Q_HEADS // GQA_KV_HEADS
NA_WIDTH = NA_HEADS * HEAD_DIM
GQA_Q_WIDTH = GQA_Q_HEADS * HEAD_DIM
GQA_KV_WIDTH = GQA_KV_HEADS * HEAD_DIM
IN_SIZES = (NA_WIDTH, NA_WIDTH, NA_WIDTH, GQA_Q_WIDTH, GQA_KV_WIDTH, GQA_KV_WIDTH, D_MODEL, D_MODEL)
IN_SPLITS = tuple(int(v) for v in np.cumsum(IN_SIZES)[:-1])
ROPE_THETA = 10000.0
EPS = 1e-6
SCALE = HEAD_DIM ** -0.5
ADAM_LR = 0.001
ADAM_B1 = 0.9
ADAM_B2 = 0.999
ADAM_EPS = 1e-08
ADAM_WD = 0.01
ADAM_STEP = 10

N_DEV = 8
MESH = pl.DeviceIdType.MESH
NEG_BIG = -1e30

VMEM_BIG_LIMIT = 52 * 1024 * 1024
NA_QROWS = 4
NA_QB = NA_QROWS * GRID_W
NA_KROWS = 12
NA_KW = NA_KROWS * GRID_W
NA_FWD_HEADS = 4
MM_TILE_ELEMS = 768 * 2176
MM_OPERAND_ELEMS = 1408 * 2176
MOD_ROWS = 256

BIG = ("w_mod", "w_in", "w_pa", "w_pb", "w_o", "w_ffn_in", "w_ffn_out")
ROW_SHARDED = ("w_o", "w_ffn_out")
EARLY = ("w_mod", "w_in")
LATE = ("w_pa", "w_pb", "w_o", "w_ffn_in", "w_ffn_out")
SMALL = ("c_ctx", "b_mod", "norm1", "na_rpb", "q_gain", "k_gain", "norm2", "final_norm")
WEIGHTS = ("c_ctx", "w_mod", "b_mod", "norm1", "w_in", "na_rpb", "q_gain", "k_gain", "w_pa", "w_pb", "w_o",
           "norm2", "w_ffn_in", "w_ffn_out", "final_norm")


def _pick(n, cands):
    for c in cands:
        if n % c == 0:
            return c
    return n


def _dot_nt(a, b):
    return lax.dot_general(a, b, (((1,), (1,)), ((), ())), preferred_element_type=F32)


def _dot_tn(a, b):
    return lax.dot_general(a, b, (((0,), (0,)), ((), ())), preferred_element_type=F32)


def _dot(a, b):
    return jnp.dot(a, b, preferred_element_type=F32)


def _mm_call(a, b, name, mode="nn", out_dtype=F32, b_lead=None):
    (m, k) = a.shape if mode != "tn" else a.shape[::-1]
    n = b.shape[-1] if mode != "nt" else b.shape[-2]
    tn = _pick(n, (2176, 1408, 1024, 512, 256, 128))
    tm = _pick(m, tuple(t for t in (768, 512, 256, 128) if t * tn <= MM_TILE_ELEMS))
    tk = _pick(k, tuple(t for t in (2816, 2176, 1408, 1024, 768, 512, 256, 128) if t * max(tm, tn) <= MM_OPERAND_ELEMS))
    nk = k // tk
    dot = {"nn": _dot, "tn": _dot_tn, "nt": _dot_nt}[mode]
    a_blk = (tk, tm) if mode == "tn" else (tm, tk)
    b_blk = (tn, tk) if mode == "nt" else (tk, tn)
    a_idx = (lambda i, kk: (kk, i)) if mode == "tn" else (lambda i, kk: (i, kk))
    b_idx = (lambda j, kk: (j, kk)) if mode == "nt" else (lambda j, kk: (kk, j))
    if b_lead is not None:
        b_blk, b_idx2 = (None,) + b_blk, b_idx
        b_idx = lambda j, kk: (b_lead,) + b_idx2(j, kk)

    def body(a_ref, b_ref, o_ref, acc_ref):
        kk = pl.program_id(2)
        part = dot(a_ref[...], b_ref[...])

        @pl.when(kk == 0)
        def _():
            acc_ref[...] = part

        @pl.when(kk > 0)
        def _():
            acc_ref[...] += part

        @pl.when(kk == nk - 1)
        def _():
            o_ref[...] = acc_ref[...].astype(o_ref.dtype)

    def body1(a_ref, b_ref, o_ref):
        o_ref[...] = dot(a_ref[...], b_ref[...]).astype(o_ref.dtype)

    footprint = 2 * (tm * tk * 2 + tk * tn * 2 + tm * tn * 4) + 2 * tm * tn * 4
    limit = int(footprint + (8 << 20))
    if nk == 1:
        return pl.pallas_call(
            body1, grid=(n // tn, m // tm),
            in_specs=[pl.BlockSpec(a_blk, lambda j, i: a_idx(i, 0)), pl.BlockSpec(b_blk, lambda j, i: b_idx(j, 0))],
            out_specs=pl.BlockSpec((tm, tn), lambda j, i: (i, j)),
            out_shape=jax.ShapeDtypeStruct((m, n), out_dtype),
            compiler_params=pltpu.CompilerParams(dimension_semantics=("parallel", "parallel"), vmem_limit_bytes=limit),
            name=name,
        )(a, b)
    return pl.pallas_call(
        body, grid=(m // tm, n // tn, nk),
        in_specs=[pl.BlockSpec(a_blk, lambda i, j, kk: a_idx(i, kk)), pl.BlockSpec(b_blk, lambda i, j, kk: b_idx(j, kk))],
        out_specs=pl.BlockSpec((tm, tn), lambda i, j, kk: (i, j)),
        out_shape=jax.ShapeDtypeStruct((m, n), out_dtype),
        scratch_shapes=[pltpu.VMEM((tm, tn), F32)],
        compiler_params=pltpu.CompilerParams(
            dimension_semantics=("parallel", "parallel", "arbitrary"), vmem_limit_bytes=limit),
        name=name,
    )(a, b)


@jax.custom_vjp
def mm(x, w):
    return _mm_call(x, w, "mm_fwd")


def _mm_fwd(x, w):
    return _mm_call(x, w, "mm_fwd"), (x, w)


def _mm_bwd(res, dy):
    x, w = res
    dyb = dy.astype(BF16)
    return _mm_call(dyb, w, "mm_dx", "nt", BF16), _mm_call(x, dyb, "mm_dw", "tn", BF16)


mm.defvjp(_mm_fwd, _mm_bwd)


def _ffn_tiles(t, f):
    return _pick(t, (384, 256, 128)), _pick(f, (1408, 1024, 512, 256, 128))


def _ffn_in_call(h, w_in):
    t, d = h.shape
    f = w_in.shape[1] // 2
    tm, tn = _ffn_tiles(t, f)
    nf = f // tn

    def body(h_ref, wa_ref, wu_ref, act_ref, au_ref):
        hv = h_ref[...]
        a, u = _dot(hv, wa_ref[...]), _dot(hv, wu_ref[...])
        au_ref[0] = a
        au_ref[1] = u
        act_ref[...] = (a * jax.nn.sigmoid(a) * u).astype(act_ref.dtype)

    return pl.pallas_call(
        body, grid=(nf, t // tm),
        in_specs=[pl.BlockSpec((tm, d), lambda j, i: (i, 0)), pl.BlockSpec((d, tn), lambda j, i: (0, j)),
                  pl.BlockSpec((d, tn), lambda j, i: (0, j + nf))],
        out_specs=[pl.BlockSpec((tm, tn), lambda j, i: (i, j)), pl.BlockSpec((2, tm, tn), lambda j, i: (0, i, j))],
        out_shape=[jax.ShapeDtypeStruct((t, f), BF16), jax.ShapeDtypeStruct((2, t, f), F32)],
        compiler_params=pltpu.CompilerParams(dimension_semantics=("parallel", "parallel"), vmem_limit_bytes=VMEM_BIG_LIMIT),
        name="ffn_in",
    )(h, w_in, w_in)


def _ffn_dact_call(dy, w_out, au):
    t, d = dy.shape
    f = w_out.shape[0]
    tm, tn = _ffn_tiles(t, f)

    def body(dy_ref, w_ref, au_ref, dau_ref):
        dact = _dot_nt(dy_ref[...], w_ref[...])
        a, u = au_ref[0], au_ref[1]
        sg = jax.nn.sigmoid(a)
        dau_ref[0] = (dact * u * (sg * (1.0 + a * (1.0 - sg)))).astype(dau_ref.dtype)
        dau_ref[1] = (dact * (a * sg)).astype(dau_ref.dtype)

    halves = pl.BlockSpec((2, tm, tn), lambda j, i: (0, i, j))
    return pl.pallas_call(
        body, grid=(f // tn, t // tm),
        in_specs=[pl.BlockSpec((tm, d), lambda j, i: (i, 0)), pl.BlockSpec((tn, d), lambda j, i: (j, 0)), halves],
        out_specs=halves, out_shape=jax.ShapeDtypeStruct((2, t, f), BF16),
        compiler_params=pltpu.CompilerParams(dimension_semantics=("parallel", "parallel"), vmem_limit_bytes=VMEM_BIG_LIMIT),
        name="ffn_dact",
    )(dy, w_out, au)


def _ffn_dh_call(dau, w_in):
    _, t, f = dau.shape
    d = w_in.shape[0]
    tm, tk = _ffn_tiles(t, f)
    nf = f // tk

    def body(a_ref, b_ref, o_ref):
        acc = None
        for c in range(2 * nf):
            part = _dot_nt(a_ref[c // nf, :, (c % nf) * tk:(c % nf + 1) * tk], b_ref[:, c * tk:(c + 1) * tk])
            acc = part if acc is None else acc + part
        o_ref[...] = acc.astype(o_ref.dtype)

    return pl.pallas_call(
        body, grid=(t // tm,),
        in_specs=[pl.BlockSpec((2, tm, f), lambda i: (0, i, 0)), pl.BlockSpec((d, 2 * f), lambda i: (0, 0))],
        out_specs=pl.BlockSpec((tm, d), lambda i: (i, 0)), out_shape=jax.ShapeDtypeStruct((t, d), BF16),
        compiler_params=pltpu.CompilerParams(dimension_semantics=("parallel",), vmem_limit_bytes=VMEM_BIG_LIMIT),
        name="ffn_dh",
    )(dau, w_in)


@jax.custom_vjp
def ffn(h, w_in, w_out):
    return _ffn_fwd(h, w_in, w_out)[0]


def _ffn_fwd(h, w_in, w_out):
    act, au = _ffn_in_call(h, w_in)
    return _mm_call(act, w_out, "mm_fwd"), (h, w_in, w_out, act, au)


def _ffn_bwd(res, dy):
    h, w_in, w_out, act, au = res
    dyb = dy.astype(BF16)
    dau = _ffn_dact_call(dyb, w_out, au)
    dw_in = jnp.concatenate([_mm_call(h, dau, "mm_dw", "tn", BF16, b_lead=c) for c in range(2)], axis=1)
    return _ffn_dh_call(dau, w_in), dw_in, _mm_call(act, dyb, "mm_dw", "tn", BF16)


ffn.defvjp(_ffn_fwd, _ffn_bwd)


NORM_ROWS = (768, 512, 256, 128)


def _seg_rows(ref, is_ctx):
    return jnp.where(is_ctx, ref[1:2, :], ref[0:1, :])


def _ctx_rows(tt, n_lat):
    return pl.program_id(0) * tt + lax.broadcasted_iota(jnp.int32, (tt, 1), 0) >= n_lat


def _one_hot_row(second):
    return (lax.broadcasted_iota(jnp.int32, (2, 1), 0) == second.astype(jnp.int32)).astype(F32)


def _norm_mod_fwd_call(x, g, shift, scale, n_lat, out_dtype):
    t, d = x.shape
    tt = _pick(t, NORM_ROWS)

    def body(x_ref, g_ref, sh_ref, sc_ref, y_ref):
        is_ctx = _ctx_rows(tt, n_lat)
        xv = x_ref[...]
        r = lax.rsqrt(jnp.mean(xv * xv, axis=-1, keepdims=True) + EPS)
        yn = xv * r * g_ref[...]
        y_ref[...] = (yn * (1.0 + _seg_rows(sc_ref, is_ctx)) + _seg_rows(sh_ref, is_ctx)).astype(y_ref.dtype)

    row = pl.BlockSpec((tt, d), lambda i: (i, 0))
    full = lambda rws: pl.BlockSpec((rws, d), lambda i: (0, 0))
    return pl.pallas_call(
        body, grid=(t // tt,), in_specs=[row, full(1), full(2), full(2)], out_specs=row,
        out_shape=jax.ShapeDtypeStruct((t, d), out_dtype),
        compiler_params=pltpu.CompilerParams(dimension_semantics=("parallel",)), name="norm_mod_fwd",
    )(x, g, shift, scale)


def _norm_mod_bwd_call(x, g, scale, dy, n_lat):
    t, d = x.shape
    tt = _pick(t, NORM_ROWS)

    def body(x_ref, g_ref, sc_ref, dy_ref, dx_ref, dg_ref, dsh_ref, dsc_ref):
        i = pl.program_id(0)
        is_ctx = _ctx_rows(tt, n_lat)

        @pl.when(i == 0)
        def _():
            dg_ref[...] = jnp.zeros_like(dg_ref)
            dsh_ref[...] = jnp.zeros_like(dsh_ref)
            dsc_ref[...] = jnp.zeros_like(dsc_ref)

        xv, dyv, gv = x_ref[...], dy_ref[...].astype(F32), g_ref[...]
        r = lax.rsqrt(jnp.mean(xv * xv, axis=-1, keepdims=True) + EPS)
        nrm = xv * r
        yn = nrm * gv
        for acc_ref, term in ((dsh_ref, dyv), (dsc_ref, dyv * yn)):
            both = jnp.sum(term, axis=0, keepdims=True)
            ctx = jnp.sum(jnp.where(is_ctx, term, 0.0), axis=0, keepdims=True)
            acc_ref[0:1, :] += both - ctx
            acc_ref[1:2, :] += ctx
        dyn = dyv * (1.0 + _seg_rows(sc_ref, is_ctx))
        dg_ref[...] += jnp.sum(dyn * nrm, axis=0, keepdims=True)
        dn = dyn * gv
        dx_ref[...] = r * (dn - nrm * jnp.mean(dn * nrm, axis=-1, keepdims=True))

    row = pl.BlockSpec((tt, d), lambda i: (i, 0))
    full = lambda rws: pl.BlockSpec((rws, d), lambda i: (0, 0))
    return pl.pallas_call(
        body, grid=(t // tt,), in_specs=[row, full(1), full(2), row], out_specs=[row, full(1), full(2), full(2)],
        out_shape=[jax.ShapeDtypeStruct((t, d), F32), jax.ShapeDtypeStruct((1, d), F32),
                   jax.ShapeDtypeStruct((2, d), F32), jax.ShapeDtypeStruct((2, d), F32)],
        compiler_params=pltpu.CompilerParams(dimension_semantics=("arbitrary",)), name="norm_mod_bwd",
    )(x, g, scale, dy)


@functools.partial(jax.custom_vjp, nondiff_argnums=(4, 5))
def norm_mod(x, g, shift, scale, n_lat, out_dtype):
    return _norm_mod_fwd_call(x, g, shift, scale, n_lat, out_dtype)


def _norm_mod_fwd(x, g, shift, scale, n_lat, out_dtype):
    return _norm_mod_fwd_call(x, g, shift, scale, n_lat, out_dtype), (x, g, scale)


def _norm_mod_bwd(n_lat, out_dtype, res, dy):
    x, g, scale = res
    dx, dg, dsh, dsc = _norm_mod_bwd_call(x, g, scale, dy, n_lat)
    return dx, dg, dsh, dsc


norm_mod.defvjp(_norm_mod_fwd, _norm_mod_bwd)


def _rot_matrix():
    p = np.zeros((HEAD_DIM, HEAD_DIM), np.float32)
    for i in range(HEAD_DIM // 2):
        p[2 * i + 1, 2 * i] = -1.0
        p[2 * i, 2 * i + 1] = 1.0
    return jnp.asarray(p)


def _hn_rope_fwd_call(x, gains, cos2, sin2, rot, n_q):
    nh, t, dh = x.shape
    tt = _pick(t, (2816, 1024, 768, 512, 256, 128))

    def body(x_ref, g_ref, cos_ref, sin_ref, rot_ref, y_ref):
        gv = jnp.where(pl.program_id(0) >= n_q, g_ref[1:2, :], g_ref[0:1, :])
        xv = x_ref[...]
        r = lax.rsqrt(jnp.mean(xv * xv, axis=-1, keepdims=True) + EPS)
        y = xv * r * gv
        yr = jnp.dot(y, rot_ref[...], precision=ROT_PRECISION, preferred_element_type=F32)
        y_ref[...] = y * cos_ref[...] + yr * sin_ref[...]

    blk = pl.BlockSpec((None, tt, dh), lambda h, i: (h, i, 0))
    tab = pl.BlockSpec((tt, dh), lambda h, i: (i, 0))
    return pl.pallas_call(
        body, grid=(nh, t // tt),
        in_specs=[blk, pl.BlockSpec((2, dh), lambda h, i: (0, 0)), tab, tab, pl.BlockSpec((dh, dh), lambda h, i: (0, 0))],
        out_specs=blk, out_shape=jax.ShapeDtypeStruct((nh, t, dh), F32),
        compiler_params=pltpu.CompilerParams(dimension_semantics=("parallel", "parallel")), name="hn_rope_fwd",
    )(x, gains, cos2, sin2, rot)


def _hn_rope_bwd_call(x, gains, cos2, sin2, rot, dy, n_q):
    nh, t, dh = x.shape
    tt = _pick(t, (2816, 1024, 768, 512, 256, 128))

    def body(x_ref, g_ref, cos_ref, sin_ref, rot_ref, dy_ref, dx_ref, dg_ref):
        h, i = pl.program_id(0), pl.program_id(1)
        is_k = h >= n_q

        @pl.when((h == 0) & (i == 0))
        def _():
            dg_ref[...] = jnp.zeros_like(dg_ref)

        gv = jnp.where(is_k, g_ref[1:2, :], g_ref[0:1, :])
        xv, dyv = x_ref[...], dy_ref[...]
        r = lax.rsqrt(jnp.mean(xv * xv, axis=-1, keepdims=True) + EPS)
        nrm = xv * r
        dyn = dyv * cos_ref[...] - jnp.dot(dyv * sin_ref[...], rot_ref[...], precision=ROT_PRECISION, preferred_element_type=F32)
        seg = _one_hot_row(is_k)
        dg_ref[...] += seg * jnp.sum(dyn * nrm, axis=0, keepdims=True)
        dn = dyn * gv
        dx_ref[...] = r * (dn - nrm * jnp.mean(dn * nrm, axis=-1, keepdims=True))

    blk = pl.BlockSpec((None, tt, dh), lambda h, i: (h, i, 0))
    tab = pl.BlockSpec((tt, dh), lambda h, i: (i, 0))
    g_spec = pl.BlockSpec((2, dh), lambda h, i: (0, 0))
    return pl.pallas_call(
        body, grid=(nh, t // tt),
        in_specs=[blk, g_spec, tab, tab, pl.BlockSpec((dh, dh), lambda h, i: (0, 0)), blk],
        out_specs=[blk, g_spec],
        out_shape=[jax.ShapeDtypeStruct((nh, t, dh), F32), jax.ShapeDtypeStruct((2, dh), F32)],
        compiler_params=pltpu.CompilerParams(dimension_semantics=("arbitrary", "arbitrary")), name="hn_rope_bwd",
    )(x, gains, cos2, sin2, rot, dy)


@functools.partial(jax.custom_vjp, nondiff_argnums=(5,))
def hn_rope(x, gains, cos2, sin2, rot, n_q):
    return _hn_rope_fwd_call(x, gains, cos2, sin2, rot, n_q)


def _hn_rope_fwd(x, gains, cos2, sin2, rot, n_q):
    return _hn_rope_fwd_call(x, gains, cos2, sin2, rot, n_q), (x, gains, cos2, sin2, rot)


def _hn_rope_bwd(n_q, res, dy):
    x, gains, cos2, sin2, rot = res
    dx, dg = _hn_rope_bwd_call(x, gains, cos2, sin2, rot, dy, n_q)
    return dx, dg, jnp.zeros_like(cos2), jnp.zeros_like(sin2), jnp.zeros_like(rot)


hn_rope.defvjp(_hn_rope_fwd, _hn_rope_bwd)


EXCHANGE_SCRATCH = [pltpu.SemaphoreType.DMA((N_DEV - 1,)), pltpu.SemaphoreType.DMA((N_DEV - 1,)), pltpu.SemaphoreType.DMA(())]


def _exchange_copies(x_ref, out_ref, send_sems, recv_sems, local_sem, all_gather):
    mx, my, mc = lax.axis_index("x"), lax.axis_index("y"), lax.axis_index("c")
    me = 4 * mx + 2 * my + mc
    src = (lambda p: x_ref) if all_gather else (lambda p: x_ref.at[p])
    local = pltpu.make_async_copy(src(me), out_ref.at[me], local_sem)
    remote = []
    for rel in range(1, N_DEV):
        px, py, pc = mx ^ (rel >> 2), my ^ ((rel >> 1) & 1), mc ^ (rel & 1)
        remote.append(pltpu.make_async_remote_copy(
            src_ref=src(4 * px + 2 * py + pc), dst_ref=out_ref.at[me],
            send_sem=send_sems.at[rel - 1], recv_sem=recv_sems.at[rel - 1],
            device_id=(px, py, pc), device_id_type=MESH))
    return local, remote


def _exchange_start(*refs, all_gather):
    local, remote = _exchange_copies(*refs, all_gather)
    local.start()
    for cp in remote:
        cp.start()


def _exchange_wait(*refs, all_gather):
    local, remote = _exchange_copies(*refs, all_gather)
    for cp in remote:
        cp.wait_send()
    for cp in remote:
        cp.wait_recv()
    local.wait()


def _exchange_shape(x, all_gather):
    return jax.ShapeDtypeStruct((N_DEV,) + tuple(x.shape if all_gather else x.shape[1:]), x.dtype)


def _exchange(x, all_gather, name):
    def body(x_ref, out_ref, *sems):
        _exchange_start(x_ref, out_ref, *sems, all_gather=all_gather)
        _exchange_wait(x_ref, out_ref, *sems, all_gather=all_gather)

    return pl.pallas_call(
        body, in_specs=[pl.BlockSpec(memory_space=pl.ANY)], out_specs=pl.BlockSpec(memory_space=pl.ANY),
        out_shape=_exchange_shape(x, all_gather), scratch_shapes=EXCHANGE_SCRATCH,
        compiler_params=pltpu.CompilerParams(has_side_effects=True), name=name,
    )(x)


ATT_TK = 256
ONES_ROWS = 16


def _att_tq(rn, fwd=False):
    return _pick(rn, ((1024,) if fwd else ()) + (512, 256, 128))


def _att_unroll(nkb):
    return _pick(nkb, (3, 2))


def _flash_fwd_call(q_t, k, v_t, xsend=None):
    g, dh, rn = q_t.shape
    nk = k.shape[1]
    tq, tk = _att_tq(rn, True), ATT_TK
    nkb = nk // tk
    unroll = _att_unroll(nkb)
    n_i = rn // tq

    trips = nkb // unroll

    def body(*refs):
        if xsend is None:
            qt_ref, k_ref, vt_ref, ot_ref, lse_ref, s_scr = refs
        else:
            qt_ref, k_ref, vt_ref, x_ref, ot_ref, lse_ref, xout_ref, s_scr, *sems = refs
            a, i = pl.program_id(0), pl.program_id(1)

            @pl.when((a == 0) & (i == 0))
            def _():
                _exchange_start(x_ref, xout_ref, *sems, all_gather=True)

        qst = qt_ref[...] * SCALE

        def scores(t, slot):
            for u in range(unroll):
                off = pl.multiple_of((t * unroll + u) * tk, tk)
                s_scr[slot, u] = _dot(k_ref[pl.ds(off, tk), :], qst)

        def consume(t, slot, carry):
            m, acc = carry
            s_t = [s_scr[slot, u] for u in range(unroll)]
            m_new = functools.reduce(jnp.maximum, [jnp.max(s, axis=0, keepdims=True) for s in s_t], m)
            p_t = [jnp.exp(s - m_new) for s in s_t]
            acc = jnp.exp(m - m_new) * acc + sum(_dot(vt_ref[t * unroll + u], p.astype(BF16)) for u, p in enumerate(p_t))
            return m_new, acc

        def pair(u2, carry):
            t = 2 * u2
            scores(t + 1, 1)
            carry = consume(t, 0, carry)
            scores(t + 2, 0)
            return consume(t + 1, 1, carry)

        scores(0, 0)
        pairs = (trips - 1) // 2
        carry = lax.fori_loop(0, pairs, pair, (jnp.full((1, tq), NEG_BIG, F32), jnp.zeros((dh + ONES_ROWS, tq), F32)))
        if trips - 2 * pairs == 2:
            scores(trips - 1, 1)
            carry = consume(trips - 2, 0, carry)
            carry = consume(trips - 1, 1, carry)
        else:
            carry = consume(trips - 1, 0, carry)
        m, acc = carry
        l = acc[dh:dh + 1]
        ot_ref[...] = acc[:dh] / l
        lse_ref[...] = m + jnp.log(l)

        if xsend is not None:
            @pl.when((a == g - 1) & (i == n_i - 1))
            def _():
                _exchange_wait(x_ref, xout_ref, *sems, all_gather=True)

    col = pl.BlockSpec((None, dh, tq), lambda a, i: (a, 0, i))
    vec = pl.BlockSpec((None, 1, tq), lambda a, i: (a, 0, i))
    hbm = pl.BlockSpec(memory_space=pl.ANY)
    in_specs = [col, pl.BlockSpec((None, nk, dh), lambda a, i: (a, 0, 0)),
                pl.BlockSpec((None, nkb, dh + ONES_ROWS, tk), lambda a, i: (a, 0, 0, 0))]
    out_specs = [col, vec]
    out_shape = [jax.ShapeDtypeStruct((g, dh, rn), F32), jax.ShapeDtypeStruct((g, 1, rn), F32)]
    args = (q_t, k, v_t)
    if xsend is not None:
        in_specs, out_specs, args = in_specs + [hbm], out_specs + [hbm], args + (xsend,)
        out_shape = out_shape + [_exchange_shape(xsend, True)]
    return pl.pallas_call(
        body, grid=(g, n_i), in_specs=in_specs, out_specs=out_specs, out_shape=out_shape,
        scratch_shapes=[pltpu.VMEM((2, unroll, tk, tq), F32)] + ([] if xsend is None else EXCHANGE_SCRATCH),
        compiler_params=pltpu.CompilerParams(dimension_semantics=("arbitrary", "arbitrary"), vmem_limit_bytes=VMEM_BIG_LIMIT),
        name="flash_fwd" if xsend is None else "flash_fwd_gather",
    )(*args)


def _flash_bwd_call(q, q_t, k, k_t, v, do, do_t, o_t, lse_t, xsend=None):
    g, rn, dh = q.shape
    nk = k.shape[1]
    tq, tk = _att_tq(rn), ATT_TK
    nkb = nk // tk
    unroll = _att_unroll(nkb)
    n_i = rn // tq

    def body(*refs):
        q_ref, qt_ref, k_ref, kt_ref, v_ref, do_ref, dot_ref, ot_ref, lse_ref = refs[:9]
        if xsend is None:
            dqt_ref, dk_ref, dv_ref = refs[9:]
        else:
            x_ref, dqt_ref, dk_ref, dv_ref, xout_ref, *sems = refs[9:]
        a, i = pl.program_id(0), pl.program_id(1)

        if xsend is not None:
            @pl.when((a == 0) & (i == 0))
            def _():
                _exchange_start(x_ref, xout_ref, *sems, all_gather=False)

        @pl.when(i == 0)
        def _():
            dk_ref[...] = jnp.zeros_like(dk_ref)
            dv_ref[...] = jnp.zeros_like(dv_ref)

        qs = q_ref[...] * SCALE
        qst = qt_ref[...] * SCALE
        dov, dotv = do_ref[...], dot_ref[...]
        delta = jnp.sum(dotv.astype(F32) * ot_ref[...], axis=0, keepdims=True)
        lse = lse_ref[...]

        def step(j, dqt):
            off = pl.multiple_of(j * tk, tk)
            kj = k_ref[pl.ds(off, tk), :]
            vj = v_ref[pl.ds(off, tk), :]
            p_t = jnp.exp(_dot(kj, qst) - lse)
            ds_t = (p_t * (_dot(vj, dotv) - delta)).astype(BF16)
            dv_ref[pl.ds(off, tk), :] += _dot(p_t.astype(BF16), dov)
            dk_ref[pl.ds(off, tk), :] += _dot(ds_t, qs)
            return dqt + _dot(kt_ref[j], ds_t)

        def trip(t, dqt):
            for u in range(unroll):
                dqt = step(t * unroll + u, dqt)
            return dqt

        dqt_ref[...] = lax.fori_loop(0, nkb // unroll, trip, jnp.zeros((dh, tq), F32)) * SCALE

        if xsend is not None:
            @pl.when((a == g - 1) & (i == n_i - 1))
            def _():
                _exchange_wait(x_ref, xout_ref, *sems, all_gather=False)

    row = pl.BlockSpec((None, tq, dh), lambda a, i: (a, i, 0))
    col = pl.BlockSpec((None, dh, tq), lambda a, i: (a, 0, i))
    kv = pl.BlockSpec((None, nk, dh), lambda a, i: (a, 0, 0))
    hbm = pl.BlockSpec(memory_space=pl.ANY)
    in_specs = [row, col, kv, pl.BlockSpec((None, nkb, dh, tk), lambda a, i: (a, 0, 0, 0)), kv, row, col, col,
                pl.BlockSpec((None, 1, tq), lambda a, i: (a, 0, i))]
    out_specs = [col, kv, kv]
    out_shape = [jax.ShapeDtypeStruct((g, dh, rn), F32), jax.ShapeDtypeStruct((g, nk, dh), F32),
                 jax.ShapeDtypeStruct((g, nk, dh), F32)]
    args = (q, q_t, k, k_t, v, do, do_t, o_t, lse_t)
    if xsend is not None:
        in_specs, out_specs, args = in_specs + [hbm], out_specs + [hbm], args + (xsend,)
        out_shape = out_shape + [_exchange_shape(xsend, False)]
    return pl.pallas_call(
        body, grid=(g, n_i), in_specs=in_specs, out_specs=out_specs, out_shape=out_shape,
        scratch_shapes=[] if xsend is None else EXCHANGE_SCRATCH,
        compiler_params=pltpu.CompilerParams(dimension_semantics=("arbitrary", "arbitrary"), vmem_limit_bytes=VMEM_BIG_LIMIT),
        name="flash_bwd" if xsend is None else "flash_bwd_scatter",
    )(*args)


def _key_blocks_t(t):
    g, nk, dh = t.shape
    return t.reshape(g, nk // ATT_TK, ATT_TK, dh).transpose(0, 1, 3, 2)


def _value_blocks_t(v):
    v_t = _key_blocks_t(v)
    extra = jnp.zeros(v_t.shape[:2] + (ONES_ROWS, ATT_TK), v.dtype).at[:, :, 0, :].set(1.0)
    return jnp.concatenate([v_t, extra], axis=2)


@jax.custom_vjp
def attn(q, k, v):
    return _attn_fwd(q, k, v)[0]


def _attn_fwd(q, k, v):
    qb, kb, vb = q.astype(BF16), k.astype(BF16), v.astype(BF16)
    q_t = qb.transpose(0, 2, 1)
    o_t, lse_t = _flash_fwd_call(q_t, kb, _value_blocks_t(vb))
    return o_t.transpose(0, 2, 1), (qb, q_t, kb, vb, o_t, lse_t)


def _attn_bwd(res, do):
    qb, q_t, kb, vb, o_t, lse_t = res
    dob = do.astype(BF16)
    dq_t, dk, dv = _flash_bwd_call(qb, q_t, kb, _key_blocks_t(kb), vb, dob, dob.transpose(0, 2, 1), o_t, lse_t)
    return dq_t.transpose(0, 2, 1), dk, dv


attn.defvjp(_attn_fwd, _attn_bwd)


def _shard_rows(shards):
    return jnp.concatenate([t.reshape(-1, D_MODEL) for t in shards.values()], axis=0).astype(BF16)


def _fulls_of(gathered, shards):
    out, off = {}, 0
    for n in shards:
        k, nn = shards[n].shape
        r = k * nn // D_MODEL
        t = gathered[:, off:off + r].reshape(N_DEV, k, nn)
        out[n] = t.reshape(N_DEV * k, nn) if n in ROW_SHARDED else t.transpose(1, 0, 2).reshape(k, N_DEV * nn)
        off += r
    return out


def _slabs_of(dfulls):
    parts = []
    for n in dfulls:
        k, nn = dfulls[n].shape
        t = (dfulls[n].reshape(N_DEV, k // N_DEV, nn) if n in ROW_SHARDED
             else dfulls[n].reshape(k, N_DEV, nn // N_DEV).transpose(1, 0, 2))
        parts.append(t.reshape(N_DEV, -1, D_MODEL))
    return jnp.concatenate(parts, axis=1)


def _dshards_of(parts, dfulls):
    rows, out, off = _sum8_call(parts), {}, 0
    for n in dfulls:
        k, nn = dfulls[n].shape
        shape = (k // N_DEV, nn) if n in ROW_SHARDED else (k, nn // N_DEV)
        r = shape[0] * shape[1] // D_MODEL
        out[n] = rows[off:off + r].reshape(shape)
        off += r
    return out


@jax.custom_vjp
def gather_layer(shards):
    return _fulls_of(_exchange(_shard_rows(shards), True, "gather_weights"), shards)


def _gather_layer_fwd(shards):
    return gather_layer(shards), None


def _gather_layer_bwd(_, dfulls):
    return (_dshards_of(_exchange(_slabs_of(dfulls), False, "scatter_grads"), dfulls),)


gather_layer.defvjp(_gather_layer_fwd, _gather_layer_bwd)


@jax.custom_vjp
def attn_gather(q, k, v, shards):
    return _attn_gather_fwd(q, k, v, shards)[0]


def _attn_gather_fwd(q, k, v, shards):
    qb, kb, vb = q.astype(BF16), k.astype(BF16), v.astype(BF16)
    q_t = qb.transpose(0, 2, 1)
    o_t, lse_t, gathered = _flash_fwd_call(q_t, kb, _value_blocks_t(vb), _shard_rows(shards))
    return (o_t.transpose(0, 2, 1), _fulls_of(gathered, shards)), (qb, q_t, kb, vb, o_t, lse_t)


def _attn_gather_bwd(res, cts):
    qb, q_t, kb, vb, o_t, lse_t = res
    do, dfulls = cts
    dob = do.astype(BF16)
    dq_t, dk, dv, parts = _flash_bwd_call(qb, q_t, kb, _key_blocks_t(kb), vb, dob, dob.transpose(0, 2, 1), o_t, lse_t,
                                          _slabs_of(dfulls))
    return dq_t.transpose(0, 2, 1), dk, dv, _dshards_of(parts, dfulls)


attn_gather.defvjp(_attn_gather_fwd, _attn_gather_bwd)


def _na_key_row(i, rows):
    return jnp.clip(NA_QROWS * i - NA_WIN_H // 2, 0, rows - NA_KROWS)


def _na_type(i, nb):
    return jnp.where(i == 0, 0, jnp.where(i == nb - 1, 2, 1))


def _na_fwd_call(q, k, v, kc, vc, slab):
    h, s, dh = q.shape
    n_ctx = kc.shape[1]
    rows, nb = s // GRID_W, s // NA_QB
    hb = NA_FWD_HEADS

    def body(q_ref, k_ref, v_ref, kc_ref, vc_ref, slab_ref, o_ref, lse_ref):
        off = pl.multiple_of(_na_key_row(pl.program_id(1), rows) * GRID_W, NA_QB)
        for hh in range(hb):
            qs = q_ref[hh] * SCALE
            kw = k_ref[hh, pl.ds(off, NA_KW), :]
            vw = v_ref[hh, pl.ds(off, NA_KW), :]
            sw = _dot_nt(qs, kw) + slab_ref[hh]
            sc = _dot_nt(qs, kc_ref[hh])
            m = jnp.maximum(jnp.max(sw, axis=1, keepdims=True), jnp.max(sc, axis=1, keepdims=True))
            pw = jnp.exp(sw - m)
            pc = jnp.exp(sc - m)
            l = jnp.sum(pw, axis=1, keepdims=True) + jnp.sum(pc, axis=1, keepdims=True)
            o_ref[hh] = (_dot(pw.astype(BF16), vw) + _dot(pc.astype(BF16), vc_ref[hh])) / l
            lse_ref[hh] = m + jnp.log(l)

    qblk = pl.BlockSpec((hb, NA_QB, dh), lambda a, i: (a, i, 0))
    kv = pl.BlockSpec((hb, s, dh), lambda a, i: (a, 0, 0))
    ckv = pl.BlockSpec((hb, n_ctx, dh), lambda a, i: (a, 0, 0))
    return pl.pallas_call(
        body, grid=(h // hb, nb),
        in_specs=[qblk, kv, kv, ckv, ckv, pl.BlockSpec((None, hb, NA_QB, NA_KW), lambda a, i: (_na_type(i, nb), a, 0, 0))],
        out_specs=[qblk, pl.BlockSpec((hb, NA_QB, 1), lambda a, i: (a, i, 0))],
        out_shape=[jax.ShapeDtypeStruct((h, s, dh), F32), jax.ShapeDtypeStruct((h, s, 1), F32)],
        compiler_params=pltpu.CompilerParams(dimension_semantics=("parallel", "parallel"), vmem_limit_bytes=VMEM_BIG_LIMIT),
        name="na_fwd",
    )(q, k, v, kc, vc, slab)


def _na_bwd_call(q, k, v, kc, vc, slab, do, o, lse):
    h, s, dh = q.shape
    n_ctx = kc.shape[1]
    rows, nb = s // GRID_W, s // NA_QB
    half = nb // 2

    def body(q_ref, k_ref, v_ref, kc_ref, vc_ref, slab0_ref, slab1_ref, do_ref, o_ref, lse_ref,
             dq_ref, dk_ref, dv_ref, dkc_ref, dvc_ref, dslab0_ref, dslab1_ref):
        j = pl.program_id(1)

        @pl.when(j == 0)
        def _():
            dk_ref[...] = jnp.zeros_like(dk_ref)
            dv_ref[...] = jnp.zeros_like(dv_ref)
            dkc_ref[...] = jnp.zeros_like(dkc_ref)
            dvc_ref[...] = jnp.zeros_like(dvc_ref)

        @pl.when(j <= 1)
        def _():
            dslab0_ref[...] = jnp.zeros_like(dslab0_ref)

        @pl.when((j == 0) | (j == half - 1))
        def _():
            dslab1_ref[...] = jnp.zeros_like(dslab1_ref)

        kcv, vcv = kc_ref[...], vc_ref[...]
        for sub, (slab_ref, dslab_ref) in enumerate(((slab0_ref, dslab0_ref), (slab1_ref, dslab1_ref))):
            off = pl.multiple_of(_na_key_row(2 * j + sub, rows) * GRID_W, NA_QB)
            blk = slice(sub * NA_QB, (sub + 1) * NA_QB)
            qs = q_ref[blk, :] * SCALE
            kw = k_ref[pl.ds(off, NA_KW), :]
            vw = v_ref[pl.ds(off, NA_KW), :]
            dov = do_ref[blk, :]
            lse = lse_ref[blk, :]
            delta = jnp.sum(dov.astype(F32) * o_ref[blk, :], axis=1, keepdims=True)
            pw = jnp.exp(_dot_nt(qs, kw) + slab_ref[...] - lse)
            pc = jnp.exp(_dot_nt(qs, kcv) - lse)
            dsw = pw * (_dot_nt(dov, vw) - delta)
            dsc = pc * (_dot_nt(dov, vcv) - delta)
            dslab_ref[...] += dsw
            dsw, dsc = dsw.astype(BF16), dsc.astype(BF16)
            dq_ref[blk, :] = (_dot(dsw, kw) + _dot(dsc, kcv)) * SCALE
            dk_ref[pl.ds(off, NA_KW), :] += _dot_tn(dsw, qs)
            dv_ref[pl.ds(off, NA_KW), :] += _dot_tn(pw.astype(BF16), dov)
            dkc_ref[...] += _dot_tn(dsc, qs)
            dvc_ref[...] += _dot_tn(pc.astype(BF16), dov)

    qblk = pl.BlockSpec((None, 2 * NA_QB, dh), lambda a, j: (a, j, 0))
    kv = pl.BlockSpec((None, s, dh), lambda a, j: (a, 0, 0))
    ckv = pl.BlockSpec((None, n_ctx, dh), lambda a, j: (a, 0, 0))
    slab_spec = lambda sub: pl.BlockSpec((None, None, NA_QB, NA_KW), lambda a, j: (_na_type(2 * j + sub, nb), a, 0, 0))
    dslab0, dslab1 = jax.ShapeDtypeStruct(slab.shape, F32), jax.ShapeDtypeStruct(slab.shape, F32)
    dq, dk, dv, dkc, dvc, d0, d1 = pl.pallas_call(
        body, grid=(h, half),
        in_specs=[qblk, kv, kv, ckv, ckv, slab_spec(0), slab_spec(1), qblk, qblk,
                  pl.BlockSpec((None, 2 * NA_QB, 1), lambda a, j: (a, j, 0))],
        out_specs=[qblk, kv, kv, ckv, ckv, slab_spec(0), slab_spec(1)],
        out_shape=[jax.ShapeDtypeStruct((h, s, dh), F32), jax.ShapeDtypeStruct((h, s, dh), F32),
                   jax.ShapeDtypeStruct((h, s, dh), F32), jax.ShapeDtypeStruct((h, n_ctx, dh), F32),
                   jax.ShapeDtypeStruct((h, n_ctx, dh), F32), dslab0, dslab1],
        compiler_params=pltpu.CompilerParams(dimension_semantics=("arbitrary", "arbitrary"), vmem_limit_bytes=VMEM_BIG_LIMIT),
        name="na_bwd",
    )(q, k, v, kc, vc, slab, slab, do, o, lse)
    return dq, dk, dv, dkc, dvc, jnp.stack([d0[0], d0[1] + d1[1], d1[2]])


@jax.custom_vjp
def na_attn(q, k, v, kc, vc, slab):
    return _na_fwd_call(q.astype(BF16), k.astype(BF16), v.astype(BF16), kc.astype(BF16), vc.astype(BF16), slab)[0]


def _na_attn_fwd(q, k, v, kc, vc, slab):
    qb, kb, vb, kcb, vcb = (t.astype(BF16) for t in (q, k, v, kc, vc))
    o, lse = _na_fwd_call(qb, kb, vb, kcb, vcb, slab)
    return o, (qb, kb, vb, kcb, vcb, slab, o, lse)


def _na_attn_bwd(res, do):
    qb, kb, vb, kcb, vcb, slab, o, lse = res
    return tuple(_na_bwd_call(qb, kb, vb, kcb, vcb, slab, do.astype(BF16), o, lse))


na_attn.defvjp(_na_attn_fwd, _na_attn_bwd)


def _na_tables(rows):
    col = np.arange(GRID_W)
    c_start = np.clip(col - NA_WIN_W // 2, 0, GRID_W - NA_WIN_W)
    in_win = (col[None, :] >= c_start[:, None]) & (col[None, :] < c_start[:, None] + NA_WIN_W)
    dc_idx = np.clip(col[None, :] - col[:, None], -(NA_WIN_W - 1), NA_WIN_W - 1) + NA_WIN_W - 1
    onehot = (in_win[:, :, None] & (dc_idx[:, :, None] == np.arange(2 * NA_WIN_W - 1)[None, None, :])).astype(np.float32)
    negcol = np.where(in_win, 0.0, NEG_BIG).astype(np.float32)
    plan = []
    for r0, ks in [(0, 0), (NA_QROWS, 0), (rows - NA_QROWS, rows - NA_KROWS)]:
        per_row = []
        for a in range(NA_QROWS):
            r = r0 + a
            rs = min(max(r - NA_WIN_H // 2, 0), rows - NA_WIN_H)
            valid = np.array([rs <= ks + b < rs + NA_WIN_H for b in range(NA_KROWS)])
            per_row.append((ks - r + NA_WIN_H - 1, valid))
        plan.append(per_row)
    return jnp.asarray(onehot), jnp.asarray(negcol), plan


def _na_slab(rpb, tables):
    onehot, negcol, plan = tables
    n_dr = 2 * NA_WIN_H - 1
    table = jnp.einsum("hdc,wuc->hwdu", rpb, onehot, precision=HIGHEST) + negcol[None, :, None, :]
    types = []
    for per_row in plan:
        slabs = []
        for first, valid in per_row:
            lo, hi = max(first, 0), min(first + NA_KROWS, n_dr)
            sel = jnp.pad(table[:, :, lo:hi, :], ((0, 0), (0, 0), (lo - first, first + NA_KROWS - hi), (0, 0)),
                          constant_values=NEG_BIG)
            sel = jnp.where(jnp.asarray(valid)[None, None, :, None], sel, NEG_BIG)
            slabs.append(sel.reshape(NA_HEADS, GRID_W, NA_KW))
        types.append(jnp.stack(slabs, axis=1).reshape(NA_HEADS, NA_QB, NA_KW))
    return jnp.stack(types)


def _loss_call(y, target):
    s, d = y.shape
    tt = _pick(s, (512, 256, 128))

    def body(y_ref, t_ref, loss_ref, diff_ref):
        @pl.when(pl.program_id(0) == 0)
        def _():
            loss_ref[...] = jnp.zeros_like(loss_ref)

        e = y_ref[...] - t_ref[...]
        diff_ref[...] = e * (1.0 / d)
        loss_ref[...] += 0.5 * jnp.sum(jnp.mean(e * e, axis=-1, keepdims=True), axis=0, keepdims=True)

    row = pl.BlockSpec((tt, d), lambda i: (i, 0))
    return pl.pallas_call(
        body, grid=(s // tt,), in_specs=[row, row], out_specs=[pl.BlockSpec((1, 1), lambda i: (0, 0)), row],
        out_shape=[jax.ShapeDtypeStruct((1, 1), F32), jax.ShapeDtypeStruct((s, d), F32)],
        compiler_params=pltpu.CompilerParams(dimension_semantics=("arbitrary",)), name="loss_head",
    )(y, target)


@jax.custom_vjp
def loss_head(y, target):
    return _loss_call(y, target)[0][0, 0]


def _loss_head_fwd(y, target):
    loss, diff = _loss_call(y, target)
    return loss[0, 0], diff


def _loss_head_bwd(diff, g):
    return diff * g, jnp.zeros_like(diff)


loss_head.defvjp(_loss_head_fwd, _loss_head_bwd)


def _sum8_call(parts):
    _, r, c = parts.shape
    tr = _pick(r, (328, 256, 128))

    def body(p_ref, g_ref):
        g = p_ref[0].astype(F32)
        for s in range(1, N_DEV):
            g = g + p_ref[s].astype(F32)
        g_ref[...] = g

    return pl.pallas_call(
        body, grid=(r // tr,), in_specs=[pl.BlockSpec((N_DEV, tr, c), lambda i: (0, i, 0))],
        out_specs=pl.BlockSpec((tr, c), lambda i: (i, 0)), out_shape=jax.ShapeDtypeStruct((r, c), F32),
        compiler_params=pltpu.CompilerParams(dimension_semantics=("parallel",)), name="sum8",
    )(parts)


def _adam_call(g, w, m, v):
    r, c = g.shape
    tr = _pick(r, (256, 128))

    def body(g_ref, w_ref, m_ref, v_ref, d_ref, nm_ref, nv_ref):
        gv = g_ref[...]
        mn = ADAM_B1 * m_ref[...] + (1.0 - ADAM_B1) * gv
        vn = ADAM_B2 * v_ref[...] + (1.0 - ADAM_B2) * (gv * gv)
        m_hat = mn / (1.0 - ADAM_B1 ** ADAM_STEP)
        v_hat = vn / (1.0 - ADAM_B2 ** ADAM_STEP)
        d_ref[...] = -ADAM_LR * (m_hat / (jnp.sqrt(v_hat) + ADAM_EPS) + ADAM_WD * w_ref[...])
        nm_ref[...] = mn
        nv_ref[...] = vn

    row = pl.BlockSpec((tr, c), lambda i: (i, 0))
    out = jax.ShapeDtypeStruct((r, c), F32)
    return pl.pallas_call(
        body, grid=(r // tr,), in_specs=[row, row, row, row], out_specs=[row, row, row], out_shape=[out, out, out],
        compiler_params=pltpu.CompilerParams(dimension_semantics=("parallel",)), name="adam",
    )(g, w, m, v)


def _pack_small(vals, extra=None):
    flat = [vals[n].reshape(-1) for n in SMALL] + ([] if extra is None else [extra.reshape(-1)])
    flat = jnp.concatenate(flat)
    rows = -(-flat.shape[0] // D_MODEL)
    rows = -(-rows // 8) * 8
    return jnp.pad(flat, (0, rows * D_MODEL - flat.shape[0])).reshape(rows, D_MODEL)


def _unpack_small(packed, like):
    flat, out, off = packed.reshape(-1), {}, 0
    for n in SMALL:
        size = like[n].size
        out[n] = flat[off:off + size].reshape(like[n].shape)
        off += size
    return out, flat[off]


def _rope_tables(s, n_ctx):
    t = jnp.arange(s)
    row = (t // GRID_W).astype(F32)
    col = (t % GRID_W).astype(F32)
    half = HEAD_DIM // 2
    inv = ROPE_THETA ** (-jnp.arange(0, half, 2, dtype=F32) / half)
    ang = jnp.concatenate([row[:, None] * inv, col[:, None] * inv], axis=-1)
    cos2 = jnp.repeat(jnp.cos(ang), 2, axis=-1)
    sin2 = jnp.repeat(jnp.sin(ang), 2, axis=-1)
    cos2 = jnp.concatenate([cos2, jnp.ones((n_ctx, HEAD_DIM), F32)], axis=0)
    sin2 = jnp.concatenate([sin2, jnp.zeros((n_ctx, HEAD_DIM), F32)], axis=0)
    return cos2, sin2


def _to_heads(t, n_heads):
    return t.reshape(t.shape[0], n_heads, HEAD_DIM).transpose(1, 0, 2)


def _from_heads(t):
    return t.transpose(1, 0, 2).reshape(t.shape[1], t.shape[0] * HEAD_DIM)


def _local_loss(shards, p, x, c, ctx, target, consts):
    s, n_ctx = x.shape[0], ctx.shape[0]
    cos2, sin2, rot, na_tables = consts
    is_ctx = (jnp.arange(s + n_ctx) >= s)[:, None]
    seg = lambda rows2: jnp.where(is_ctx, rows2[1:2], rows2[0:1])
    xa = jnp.concatenate([x, ctx], axis=0)
    cond = jnp.concatenate([jax.nn.silu(c), jax.nn.silu(p["c_ctx"])[None, :],
                            jnp.zeros((MOD_ROWS - 2, D_MODEL), F32)], axis=0).astype(BF16)
    pick = lambda names, l: {n: shards[n][l] for n in names}
    w = gather_layer(pick(EARLY, 0))

    for l in range(DEPTH):
        mod = (mm(cond, w["w_mod"]) + p["b_mod"][l])[:2]
        sh1, sc1, g1, sh2, sc2, g2 = jnp.split(mod, 6, axis=-1)

        h = norm_mod(xa, p["norm1"][l][None, :], sh1, sc1, s, BF16)
        na_q, na_k, na_v, gq, gk, gv, ga, gb = jnp.split(mm(h, w["w_in"]), IN_SPLITS, axis=-1)

        qa, ka, va = _to_heads(na_q, NA_HEADS), _to_heads(na_k, NA_HEADS), _to_heads(na_v, NA_HEADS)
        slab = _na_slab(p["na_rpb"][l], na_tables)
        ya_lat = na_attn(qa[:, :s], ka[:, :s], va[:, :s], ka[:, s:], va[:, s:], slab)
        ya_ctx = attn(qa[:, s:], ka[:, s:], va[:, s:])
        ya = _from_heads(jnp.concatenate([ya_lat, ya_ctx], axis=1))

        qk = _to_heads(jnp.concatenate([gq, gk], axis=-1), GQA_Q_HEADS + GQA_KV_HEADS)
        gains = jnp.stack([p["q_gain"][l], p["k_gain"][l]])
        qk = hn_rope(qk, gains, cos2, sin2, rot, GQA_Q_HEADS)
        qb, kb, vb = qk[:GQA_Q_HEADS], qk[GQA_Q_HEADS:], _to_heads(gv, GQA_KV_HEADS)
        q_lat = qb[:, :s].reshape(GQA_KV_HEADS, GQA_REP * s, HEAD_DIM)
        ob_lat, got = attn_gather(q_lat, kb, vb, {**pick(LATE, l), **(pick(EARLY, l + 1) if l + 1 < DEPTH else {})})
        w = {n: got[n] for n in LATE}
        ob_ctx = attn(qb[:, s:].reshape(GQA_KV_HEADS, GQA_REP * n_ctx, HEAD_DIM), kb[:, s:], vb[:, s:])
        yb = _from_heads(jnp.concatenate([ob_lat.reshape(GQA_Q_HEADS, s, HEAD_DIM),
                                          ob_ctx.reshape(GQA_Q_HEADS, n_ctx, HEAD_DIM)], axis=1))

        merged = jax.nn.sigmoid(ga) * mm(ya.astype(BF16), w["w_pa"]) + jax.nn.sigmoid(gb) * mm(yb.astype(BF16), w["w_pb"])
        xa = xa + seg(g1) * mm(merged.astype(BF16), w["w_o"])

        h2 = norm_mod(xa, p["norm2"][l][None, :], sh2, sc2, s, BF16)
        xa = xa + seg(g2) * ffn(h2, w["w_ffn_in"], w["w_ffn_out"])
        w = {n: got[n] for n in EARLY if n in got}

    zeros2 = jnp.zeros((2, D_MODEL), F32)
    y = norm_mod(xa, p["final_norm"][None, :], zeros2, zeros2, s, F32)[:s]
    return loss_head(y, target)


def kernel(x, c, ctx, c_ctx, w_mod, b_mod, norm1, w_in, na_rpb, q_gain, k_gain, w_pa, w_pb, w_o, norm2, w_ffn_in, w_ffn_out, final_norm, loss_target, m_c_ctx, m_w_mod, m_b_mod, m_norm1, m_w_in, m_na_rpb, m_q_gain, m_k_gain, m_w_pa, m_w_pb, m_w_o, m_norm2, m_w_ffn_in, m_w_ffn_out, m_final_norm, v_c_ctx, v_w_mod, v_b_mod, v_norm1, v_w_in, v_na_rpb, v_q_gain, v_k_gain, v_w_pa, v_w_pb, v_w_o, v_norm2, v_w_ffn_in, v_w_ffn_out, v_final_norm):
    w = dict(c_ctx=c_ctx, w_mod=w_mod, b_mod=b_mod, norm1=norm1, w_in=w_in, na_rpb=na_rpb, q_gain=q_gain, k_gain=k_gain,
             w_pa=w_pa, w_pb=w_pb, w_o=w_o, norm2=norm2, w_ffn_in=w_ffn_in, w_ffn_out=w_ffn_out, final_norm=final_norm)
    mom = dict(c_ctx=m_c_ctx, w_mod=m_w_mod, b_mod=m_b_mod, norm1=m_norm1, w_in=m_w_in, na_rpb=m_na_rpb, q_gain=m_q_gain,
               k_gain=m_k_gain, w_pa=m_w_pa, w_pb=m_w_pb, w_o=m_w_o, norm2=m_norm2, w_ffn_in=m_w_ffn_in,
               w_ffn_out=m_w_ffn_out, final_norm=m_final_norm)
    var = dict(c_ctx=v_c_ctx, w_mod=v_w_mod, b_mod=v_b_mod, norm1=v_norm1, w_in=v_w_in, na_rpb=v_na_rpb, q_gain=v_q_gain,
               k_gain=v_k_gain, w_pa=v_w_pa, w_pb=v_w_pb, w_o=v_w_o, norm2=v_norm2, w_ffn_in=v_w_ffn_in,
               w_ffn_out=v_w_ffn_out, final_norm=v_final_norm)
    s, n_ctx = x.shape[1], ctx.shape[1]
    depth = w_mod.shape[0]

    shards = {n: [w[n][l] for l in range(depth)] for n in BIG}
    small = {n: w[n] for n in SMALL}
    consts = (*_rope_tables(s, n_ctx), _rot_matrix(), _na_tables(s // GRID_W))
    loss, (g_shards, g_small, gx) = jax.value_and_grad(_local_loss, argnums=(0, 1, 2))(
        shards, small, x[0], c, ctx[0], loss_target[0], consts)

    parts_s = _exchange(_pack_small(g_small, loss), True, "gather_small_grads")
    g_packed = _sum8_call(parts_s)
    grads, loss = _unpack_small(g_packed, w)
    zero = jnp.zeros((1,), F32)
    upd_s = _adam_call(g_packed, _pack_small(w, zero), _pack_small(mom, zero), _pack_small(var, zero))
    outs = [grads] + [_unpack_small(u, w)[0] for u in upd_s]

    for n in BIG:
        flat = lambda t: t.reshape(-1, t.shape[-1])
        g = jnp.stack(g_shards[n])
        upd = _adam_call(flat(g), flat(w[n]), flat(mom[n]), flat(var[n]))
        for k, t in enumerate([g] + [u.reshape(w[n].shape) for u in upd]):
            outs[k][n] = t
    return (loss, gx[None], *[o[n] for o in outs for n in WEIGHTS])
```

```python
import functools

import numpy as np
import jax
import jax.numpy as jnp
from jax import lax
from jax.experimental import pallas as pl
from jax.experimental.pallas import tpu as pltpu

F32 = jnp.float32
BF16 = jnp.bfloat16
HIGHEST = lax.Precision.HIGHEST
ROT_PRECISION = lax.Precision.HIGH

D_MODEL = 1024
DEPTH = 4
GRID_W = 64
HEAD_DIM = 64
NA_HEADS = 8
NA_WIN_H = 8
NA_WIN_W = 16
GQA_Q_HEADS = 8
GQA_KV_HEADS = 2
GQA_REP = GQA_Q_HEADS // GQA_KV_HEADS
NA_WIDTH = NA_HEADS * HEAD_DIM
GQA_Q_WIDTH = GQA_Q_HEADS * HEAD_DIM
GQA_KV_WIDTH = GQA_KV_HEADS * HEAD_DIM
IN_SIZES = (NA_WIDTH, NA_WIDTH, NA_WIDTH, GQA_Q_WIDTH, GQA_KV_WIDTH, GQA_KV_WIDTH, D_MODEL, D_MODEL)
IN_SPLITS = tuple(int(v) for v in np.cumsum(IN_SIZES)[:-1])
ROPE_THETA = 10000.0
EPS = 1e-6
SCALE = HEAD_DIM ** -0.5
ADAM_LR = 0.001
ADAM_B1 = 0.9
ADAM_B2 = 0.999
ADAM_EPS = 1e-08
ADAM_WD = 0.01
ADAM_STEP = 10

N_DEV = 8
MESH = pl.DeviceIdType.MESH
NEG_BIG = -1e30

VMEM_BIG_LIMIT = 52 * 1024 * 1024
NA_QROWS = 4
NA_QB = NA_QROWS * GRID_W
NA_KROWS = 12
NA_KW = NA_KROWS * GRID_W
NA_FWD_HEADS = 4
MM_TILE_ELEMS = 768 * 2176
MM_OPERAND_ELEMS = 1408 * 2176
MOD_ROWS = 256

BIG = ("w_mod", "w_in", "w_pa", "w_pb", "w_o", "w_ffn_in", "w_ffn_out")
ROW_SHARDED = ("w_o", "w_ffn_out")
EARLY = ("w_mod", "w_in")
LATE = ("w_pa", "w_pb", "w_o", "w_ffn_in", "w_ffn_out")
SMALL = ("c_ctx", "b_mod", "norm1", "na_rpb", "q_gain", "k_gain", "norm2", "final_norm")
WEIGHTS = ("c_ctx", "w_mod", "b_mod", "norm1", "w_in", "na_rpb", "q_gain", "k_gain", "w_pa", "w_pb", "w_o",
           "norm2", "w_ffn_in", "w_ffn_out", "final_norm")


def _pick(n, cands):
    for c in cands:
        if n % c == 0:
            return c
    return n


def _dot_nt(a, b):
    return lax.dot_general(a, b, (((1,), (1,)), ((), ())), preferred_element_type=F32)


def _dot_tn(a, b):
    return lax.dot_general(a, b, (((0,), (0,)), ((), ())), preferred_element_type=F32)


def _dot(a, b):
    return jnp.dot(a, b, preferred_element_type=F32)


def _mm_call(a, b, name, mode="nn", out_dtype=F32, b_lead=None):
    (m, k) = a.shape if mode != "tn" else a.shape[::-1]
    n = b.shape[-1] if mode != "nt" else b.shape[-2]
    tn = _pick(n, (2176, 1408, 1024, 512, 256, 128))
    tm = _pick(m, tuple(t for t in (768, 512, 256, 128) if t * tn <= MM_TILE_ELEMS))
    tk = _pick(k, tuple(t for t in (2816, 2176, 1408, 1024, 768, 512, 256, 128) if t * max(tm, tn) <= MM_OPERAND_ELEMS))
    nk = k // tk
    dot = {"nn": _dot, "tn": _dot_tn, "nt": _dot_nt}[mode]
    a_blk = (tk, tm) if mode == "tn" else (tm, tk)
    b_blk = (tn, tk) if mode == "nt" else (tk, tn)
    a_idx = (lambda i, kk: (kk, i)) if mode == "tn" else (lambda i, kk: (i, kk))
    b_idx = (lambda j, kk: (j, kk)) if mode == "nt" else (lambda j, kk: (kk, j))
    if b_lead is not None:
        b_blk, b_idx2 = (None,) + b_blk, b_idx
        b_idx = lambda j, kk: (b_lead,) + b_idx2(j, kk)

    def body(a_ref, b_ref, o_ref, acc_ref):
        kk = pl.program_id(2)
        part = dot(a_ref[...], b_ref[...])

        @pl.when(kk == 0)
        def _():
            acc_ref[...] = part

        @pl.when(kk > 0)
        def _():
            acc_ref[...] += part

        @pl.when(kk == nk - 1)
        def _():
            o_ref[...] = acc_ref[...].astype(o_ref.dtype)

    def body1(a_ref, b_ref, o_ref):
        o_ref[...] = dot(a_ref[...], b_ref[...]).astype(o_ref.dtype)

    footprint = 2 * (tm * tk * 2 + tk * tn * 2 + tm * tn * 4) + 2 * tm * tn * 4
    limit = int(footprint + (8 << 20))
    if nk == 1:
        return pl.pallas_call(
            body1, grid=(n // tn, m // tm),
            in_specs=[pl.BlockSpec(a_blk, lambda j, i: a_idx(i, 0)), pl.BlockSpec(b_blk, lambda j, i: b_idx(j, 0))],
            out_specs=pl.BlockSpec((tm, tn), lambda j, i: (i, j)),
            out_shape=jax.ShapeDtypeStruct((m, n), out_dtype),
            compiler_params=pltpu.CompilerParams(dimension_semantics=("parallel", "parallel"), vmem_limit_bytes=limit),
            name=name,
        )(a, b)
    return pl.pallas_call(
        body, grid=(m // tm, n // tn, nk),
        in_specs=[pl.BlockSpec(a_blk, lambda i, j, kk: a_idx(i, kk)), pl.BlockSpec(b_blk, lambda i, j, kk: b_idx(j, kk))],
        out_specs=pl.BlockSpec((tm, tn), lambda i, j, kk: (i, j)),
        out_shape=jax.ShapeDtypeStruct((m, n), out_dtype),
        scratch_shapes=[pltpu.VMEM((tm, tn), F32)],
        compiler_params=pltpu.CompilerParams(
            dimension_semantics=("parallel", "parallel", "arbitrary"), vmem_limit_bytes=limit),
        name=name,
    )(a, b)


@jax.custom_vjp
def mm(x, w):
    return _mm_call(x, w, "mm_fwd")


def _mm_fwd(x, w):
    return _mm_call(x, w, "mm_fwd"), (x, w)


def _mm_bwd(res, dy):
    x, w = res
    dyb = dy.astype(BF16)
    return _mm_call(dyb, w, "mm_dx", "nt", BF16), _mm_call(x, dyb, "mm_dw", "tn", BF16)


mm.defvjp(_mm_fwd, _mm_bwd)


def _ffn_tiles(t, f):
    return _pick(t, (384, 256, 128)), _pick(f, (1408, 1024, 512, 256, 128))


def _ffn_in_call(h, w_in):
    t, d = h.shape
    f = w_in.shape[1] // 2
    tm, tn = _ffn_tiles(t, f)
    nf = f // tn

    def body(h_ref, wa_ref, wu_ref, act_ref, au_ref):
        hv = h_ref[...]
        a, u = _dot(hv, wa_ref[...]), _dot(hv, wu_ref[...])
        au_ref[0] = a
        au_ref[1] = u
        act_ref[...] = (a * jax.nn.sigmoid(a) * u).astype(act_ref.dtype)

    return pl.pallas_call(
        body, grid=(nf, t // tm),
        in_specs=[pl.BlockSpec((tm, d), lambda j, i: (i, 0)), pl.BlockSpec((d, tn), lambda j, i: (0, j)),
                  pl.BlockSpec((d, tn), lambda j, i: (0, j + nf))],
        out_specs=[pl.BlockSpec((tm, tn), lambda j, i: (i, j)), pl.BlockSpec((2, tm, tn), lambda j, i: (0, i, j))],
        out_shape=[jax.ShapeDtypeStruct((t, f), BF16), jax.ShapeDtypeStruct((2, t, f), F32)],
        compiler_params=pltpu.CompilerParams(dimension_semantics=("parallel", "parallel"), vmem_limit_bytes=VMEM_BIG_LIMIT),
        name="ffn_in",
    )(h, w_in, w_in)


def _ffn_dact_call(dy, w_out, au):
    t, d = dy.shape
    f = w_out.shape[0]
    tm, tn = _ffn_tiles(t, f)

    def body(dy_ref, w_ref, au_ref, dau_ref):
        dact = _dot_nt(dy_ref[...], w_ref[...])
        a, u = au_ref[0], au_ref[1]
        sg = jax.nn.sigmoid(a)
        dau_ref[0] = (dact * u * (sg * (1.0 + a * (1.0 - sg)))).astype(dau_ref.dtype)
        dau_ref[1] = (dact * (a * sg)).astype(dau_ref.dtype)

    halves = pl.BlockSpec((2, tm, tn), lambda j, i: (0, i, j))
    return pl.pallas_call(
        body, grid=(f // tn, t // tm),
        in_specs=[pl.BlockSpec((tm, d), lambda j, i: (i, 0)), pl.BlockSpec((tn, d), lambda j, i: (j, 0)), halves],
        out_specs=halves, out_shape=jax.ShapeDtypeStruct((2, t, f), BF16),
        compiler_params=pltpu.CompilerParams(dimension_semantics=("parallel", "parallel"), vmem_limit_bytes=VMEM_BIG_LIMIT),
        name="ffn_dact",
    )(dy, w_out, au)


def _ffn_dh_call(dau, w_in):
    _, t, f = dau.shape
    d = w_in.shape[0]
    tm, tk = _ffn_tiles(t, f)
    nf = f // tk

    def body(a_ref, b_ref, o_ref):
        acc = None
        for c in range(2 * nf):
            part = _dot_nt(a_ref[c // nf, :, (c % nf) * tk:(c % nf + 1) * tk], b_ref[:, c * tk:(c + 1) * tk])
            acc = part if acc is None else acc + part
        o_ref[...] = acc.astype(o_ref.dtype)

    return pl.pallas_call(
        body, grid=(t // tm,),
        in_specs=[pl.BlockSpec((2, tm, f), lambda i: (0, i, 0)), pl.BlockSpec((d, 2 * f), lambda i: (0, 0))],
        out_specs=pl.BlockSpec((tm, d), lambda i: (i, 0)), out_shape=jax.ShapeDtypeStruct((t, d), BF16),
        compiler_params=pltpu.CompilerParams(dimension_semantics=("parallel",), vmem_limit_bytes=VMEM_BIG_LIMIT),
        name="ffn_dh",
    )(dau, w_in)


@jax.custom_vjp
def ffn(h, w_in, w_out):
    return _ffn_fwd(h, w_in, w_out)[0]


def _ffn_fwd(h, w_in, w_out):
    act, au = _ffn_in_call(h, w_in)
    return _mm_call(act, w_out, "mm_fwd"), (h, w_in, w_out, act, au)


def _ffn_bwd(res, dy):
    h, w_in, w_out, act, au = res
    dyb = dy.astype(BF16)
    dau = _ffn_dact_call(dyb, w_out, au)
    dw_in = jnp.concatenate([_mm_call(h, dau, "mm_dw", "tn", BF16, b_lead=c) for c in range(2)], axis=1)
    return _ffn_dh_call(dau, w_in), dw_in, _mm_call(act, dyb, "mm_dw", "tn", BF16)


ffn.defvjp(_ffn_fwd, _ffn_bwd)


NORM_ROWS = (768, 512, 256, 128)


def _seg_rows(ref, is_ctx):
    return jnp.where(is_ctx, ref[1:2, :], ref[0:1, :])


def _ctx_rows(tt, n_lat):
    return pl.program_id(0) * tt + lax.broadcasted_iota(jnp.int32, (tt, 1), 0) >= n_lat


def _one_hot_row(second):
    return (lax.broadcasted_iota(jnp.int32, (2, 1), 0) == second.astype(jnp.int32)).astype(F32)


def _norm_mod_fwd_call(x, g, shift, scale, n_lat, out_dtype):
    t, d = x.shape
    tt = _pick(t, NORM_ROWS)

    def body(x_ref, g_ref, sh_ref, sc_ref, y_ref):
        is_ctx = _ctx_rows(tt, n_lat)
        xv = x_ref[...]
        r = lax.rsqrt(jnp.mean(xv * xv, axis=-1, keepdims=True) + EPS)
        yn = xv * r * g_ref[...]
        y_ref[...] = (yn * (1.0 + _seg_rows(sc_ref, is_ctx)) + _seg_rows(sh_ref, is_ctx)).astype(y_ref.dtype)

    row = pl.BlockSpec((tt, d), lambda i: (i, 0))
    full = lambda rws: pl.BlockSpec((rws, d), lambda i: (0, 0))
    return pl.pallas_call(
        body, grid=(t // tt,), in_specs=[row, full(1), full(2), full(2)], out_specs=row,
        out_shape=jax.ShapeDtypeStruct((t, d), out_dtype),
        compiler_params=pltpu.CompilerParams(dimension_semantics=("parallel",)), name="norm_mod_fwd",
    )(x, g, shift, scale)


def _norm_mod_bwd_call(x, g, scale, dy, n_lat):
    t, d = x.shape
    tt = _pick(t, NORM_ROWS)

    def body(x_ref, g_ref, sc_ref, dy_ref, dx_ref, dg_ref, dsh_ref, dsc_ref):
        i = pl.program_id(0)
        is_ctx = _ctx_rows(tt, n_lat)

        @pl.when(i == 0)
        def _():
            dg_ref[...] = jnp.zeros_like(dg_ref)
            dsh_ref[...] = jnp.zeros_like(dsh_ref)
            dsc_ref[...] = jnp.zeros_like(dsc_ref)

        xv, dyv, gv = x_ref[...], dy_ref[...].astype(F32), g_ref[...]
        r = lax.rsqrt(jnp.mean(xv * xv, axis=-1, keepdims=True) + EPS)
        nrm = xv * r
        yn = nrm * gv
        for acc_ref, term in ((dsh_ref, dyv), (dsc_ref, dyv * yn)):
            both = jnp.sum(term, axis=0, keepdims=True)
            ctx = jnp.sum(jnp.where(is_ctx, term, 0.0), axis=0, keepdims=True)
            acc_ref[0:1, :] += both - ctx
            acc_ref[1:2, :] += ctx
        dyn = dyv * (1.0 + _seg_rows(sc_ref, is_ctx))
        dg_ref[...] += jnp.sum(dyn * nrm, axis=0, keepdims=True)
        dn = dyn * gv
        dx_ref[...] = r * (dn - nrm * jnp.mean(dn * nrm, axis=-1, keepdims=True))

    row = pl.BlockSpec((tt, d), lambda i: (i, 0))
    full = lambda rws: pl.BlockSpec((rws, d), lambda i: (0, 0))
    return pl.pallas_call(
        body, grid=(t // tt,), in_specs=[row, full(1), full(2), row], out_specs=[row, full(1), full(2), full(2)],
        out_shape=[jax.ShapeDtypeStruct((t, d), F32), jax.ShapeDtypeStruct((1, d), F32),
                   jax.ShapeDtypeStruct((2, d), F32), jax.ShapeDtypeStruct((2, d), F32)],
        compiler_params=pltpu.CompilerParams(dimension_semantics=("arbitrary",)), name="norm_mod_bwd",
    )(x, g, scale, dy)


@functools.partial(jax.custom_vjp, nondiff_argnums=(4, 5))
def norm_mod(x, g, shift, scale, n_lat, out_dtype):
    return _norm_mod_fwd_call(x, g, shift, scale, n_lat, out_dtype)


def _norm_mod_fwd(x, g, shift, scale, n_lat, out_dtype):
    return _norm_mod_fwd_call(x, g, shift, scale, n_lat, out_dtype), (x, g, scale)


def _norm_mod_bwd(n_lat, out_dtype, res, dy):
    x, g, scale = res
    dx, dg, dsh, dsc = _norm_mod_bwd_call(x, g, scale, dy, n_lat)
    return dx, dg, dsh, dsc


norm_mod.defvjp(_norm_mod_fwd, _norm_mod_bwd)


def _rot_matrix():
    p = np.zeros((HEAD_DIM, HEAD_DIM), np.float32)
    for i in range(HEAD_DIM // 2):
        p[2 * i + 1, 2 * i] = -1.0
        p[2 * i, 2 * i + 1] = 1.0
    return jnp.asarray(p)


def _hn_rope_fwd_call(x, gains, cos2, sin2, rot, n_q):
    nh, t, dh = x.shape
    tt = _pick(t, (2816, 1024, 768, 512, 256, 128))

    def body(x_ref, g_ref, cos_ref, sin_ref, rot_ref, y_ref):
        gv = jnp.where(pl.program_id(0) >= n_q, g_ref[1:2, :], g_ref[0:1, :])
        xv = x_ref[...]
        r = lax.rsqrt(jnp.mean(xv * xv, axis=-1, keepdims=True) + EPS)
        y = xv * r * gv
        yr = jnp.dot(y, rot_ref[...], precision=ROT_PRECISION, preferred_element_type=F32)
        y_ref[...] = y * cos_ref[...] + yr * sin_ref[...]

    blk = pl.BlockSpec((None, tt, dh), lambda h, i: (h, i, 0))
    tab = pl.BlockSpec((tt, dh), lambda h, i: (i, 0))
    return pl.pallas_call(
        body, grid=(nh, t // tt),
        in_specs=[blk, pl.BlockSpec((2, dh), lambda h, i: (0, 0)), tab, tab, pl.BlockSpec((dh, dh), lambda h, i: (0, 0))],
        out_specs=blk, out_shape=jax.ShapeDtypeStruct((nh, t, dh), F32),
        compiler_params=pltpu.CompilerParams(dimension_semantics=("parallel", "parallel")), name="hn_rope_fwd",
    )(x, gains, cos2, sin2, rot)


def _hn_rope_bwd_call(x, gains, cos2, sin2, rot, dy, n_q):
    nh, t, dh = x.shape
    tt = _pick(t, (2816, 1024, 768, 512, 256, 128))

    def body(x_ref, g_ref, cos_ref, sin_ref, rot_ref, dy_ref, dx_ref, dg_ref):
        h, i = pl.program_id(0), pl.program_id(1)
        is_k = h >= n_q

        @pl.when((h == 0) & (i == 0))
        def _():
            dg_ref[...] = jnp.zeros_like(dg_ref)

        gv = jnp.where(is_k, g_ref[1:2, :], g_ref[0:1, :])
        xv, dyv = x_ref[...], dy_ref[...]
        r = lax.rsqrt(jnp.mean(xv * xv, axis=-1, keepdims=True) + EPS)
        nrm = xv * r
        dyn = dyv * cos_ref[...] - jnp.dot(dyv * sin_ref[...], rot_ref[...], precision=ROT_PRECISION, preferred_element_type=F32)
        seg = _one_hot_row(is_k)
        dg_ref[...] += seg * jnp.sum(dyn * nrm, axis=0, keepdims=True)
        dn = dyn * gv
        dx_ref[...] = r * (dn - nrm * jnp.mean(dn * nrm, axis=-1, keepdims=True))

    blk = pl.BlockSpec((None, tt, dh), lambda h, i: (h, i, 0))
    tab = pl.BlockSpec((tt, dh), lambda h, i: (i, 0))
    g_spec = pl.BlockSpec((2, dh), lambda h, i: (0, 0))
    return pl.pallas_call(
        body, grid=(nh, t // tt),
        in_specs=[blk, g_spec, tab, tab, pl.BlockSpec((dh, dh), lambda h, i: (0, 0)), blk],
        out_specs=[blk, g_spec],
        out_shape=[jax.ShapeDtypeStruct((nh, t, dh), F32), jax.ShapeDtypeStruct((2, dh), F32)],
        compiler_params=pltpu.CompilerParams(dimension_semantics=("arbitrary", "arbitrary")), name="hn_rope_bwd",
    )(x, gains, cos2, sin2, rot, dy)


@functools.partial(jax.custom_vjp, nondiff_argnums=(5,))
def hn_rope(x, gains, cos2, sin2, rot, n_q):
    return _hn_rope_fwd_call(x, gains, cos2, sin2, rot, n_q)


def _hn_rope_fwd(x, gains, cos2, sin2, rot, n_q):
    return _hn_rope_fwd_call(x, gains, cos2, sin2, rot, n_q), (x, gains, cos2, sin2, rot)


def _hn_rope_bwd(n_q, res, dy):
    x, gains, cos2, sin2, rot = res
    dx, dg = _hn_rope_bwd_call(x, gains, cos2, sin2, rot, dy, n_q)
    return dx, dg, jnp.zeros_like(cos2), jnp.zeros_like(sin2), jnp.zeros_like(rot)


hn_rope.defvjp(_hn_rope_fwd, _hn_rope_bwd)


EXCHANGE_SCRATCH = [pltpu.SemaphoreType.DMA((N_DEV - 1,)), pltpu.SemaphoreType.DMA((N_DEV - 1,)), pltpu.SemaphoreType.DMA(())]


def _exchange_copies(x_ref, out_ref, send_sems, recv_sems, local_sem, all_gather):
    mx, my, mc = lax.axis_index("x"), lax.axis_index("y"), lax.axis_index("c")
    me = 4 * mx + 2 * my + mc
    src = (lambda p: x_ref) if all_gather else (lambda p: x_ref.at[p])
    local = pltpu.make_async_copy(src(me), out_ref.at[me], local_sem)
    remote = []
    for rel in range(1, N_DEV):
        px, py, pc = mx ^ (rel >> 2), my ^ ((rel >> 1) & 1), mc ^ (rel & 1)
        remote.append(pltpu.make_async_remote_copy(
            src_ref=src(4 * px + 2 * py + pc), dst_ref=out_ref.at[me],
            send_sem=send_sems.at[rel - 1], recv_sem=recv_sems.at[rel - 1],
            device_id=(px, py, pc), device_id_type=MESH))
    return local, remote


def _exchange_start(*refs, all_gather):
    local, remote = _exchange_copies(*refs, all_gather)
    local.start()
    for cp in remote:
        cp.start()


def _exchange_wait(*refs, all_gather):
    local, remote = _exchange_copies(*refs, all_gather)
    for cp in remote:
        cp.wait_send()
    for cp in remote:
        cp.wait_recv()
    local.wait()


def _exchange_shape(x, all_gather):
    return jax.ShapeDtypeStruct((N_DEV,) + tuple(x.shape if all_gather else x.shape[1:]), x.dtype)


def _exchange(x, all_gather, name):
    def body(x_ref, out_ref, *sems):
        _exchange_start(x_ref, out_ref, *sems, all_gather=all_gather)
        _exchange_wait(x_ref, out_ref, *sems, all_gather=all_gather)

    return pl.pallas_call(
        body, in_specs=[pl.BlockSpec(memory_space=pl.ANY)], out_specs=pl.BlockSpec(memory_space=pl.ANY),
        out_shape=_exchange_shape(x, all_gather), scratch_shapes=EXCHANGE_SCRATCH,
        compiler_params=pltpu.CompilerParams(has_side_effects=True), name=name,
    )(x)


ATT_TK = 256
ONES_ROWS = 16


def _att_tq(rn, fwd=False):
    return _pick(rn, ((1024,) if fwd else ()) + (512, 256, 128))


def _att_unroll(nkb, bwd=False):
    return _pick(nkb, ((11,) if bwd else ()) + (3, 2))


def _flash_fwd_call(q_t, k, v_t, xsend=None):
    g, dh, rn = q_t.shape
    nk = k.shape[1]
    tq, tk = _att_tq(rn, True), ATT_TK
    nkb = nk // tk
    unroll = _att_unroll(nkb)
    n_i = rn // tq

    trips = nkb // unroll

    def body(*refs):
        if xsend is None:
            qt_ref, k_ref, vt_ref, ot_ref, lse_ref, s_scr = refs
        else:
            qt_ref, k_ref, vt_ref, x_ref, ot_ref, lse_ref, xout_ref, s_scr, *sems = refs
            a, i = pl.program_id(0), pl.program_id(1)

            @pl.when((a == 0) & (i == 0))
            def _():
                _exchange_start(x_ref, xout_ref, *sems, all_gather=True)

        qst = qt_ref[...] * SCALE

        def scores(t, slot):
            for u in range(unroll):
                off = pl.multiple_of((t * unroll + u) * tk, tk)
                s_scr[slot, u] = _dot(k_ref[pl.ds(off, tk), :], qst)

        def consume(t, slot, carry):
            m, acc = carry
            s_t = [s_scr[slot, u] for u in range(unroll)]
            m_new = functools.reduce(jnp.maximum, [jnp.max(s, axis=0, keepdims=True) for s in s_t], m)
            p_t = [jnp.exp(s - m_new) for s in s_t]
            acc = jnp.exp(m - m_new) * acc + sum(_dot(vt_ref[t * unroll + u], p.astype(BF16)) for u, p in enumerate(p_t))
            return m_new, acc

        def pair(u2, carry):
            t = 2 * u2
            scores(t + 1, 1)
            carry = consume(t, 0, carry)
            scores(t + 2, 0)
            return consume(t + 1, 1, carry)

        scores(0, 0)
        pairs = (trips - 1) // 2
        carry = lax.fori_loop(0, pairs, pair, (jnp.full((1, tq), NEG_BIG, F32), jnp.zeros((dh + ONES_ROWS, tq), F32)))
        if trips - 2 * pairs == 2:
            scores(trips - 1, 1)
            carry = consume(trips - 2, 0, carry)
            carry = consume(trips - 1, 1, carry)
        else:
            carry = consume(trips - 1, 0, carry)
        m, acc = carry
        l = acc[dh:dh + 1]
        ot_ref[...] = acc[:dh] / l
        lse_ref[...] = m + jnp.log(l)

        if xsend is not None:
            @pl.when((a == g - 1) & (i == n_i - 1))
            def _():
                _exchange_wait(x_ref, xout_ref, *sems, all_gather=True)

    col = pl.BlockSpec((None, dh, tq), lambda a, i: (a, 0, i))
    vec = pl.BlockSpec((None, 1, tq), lambda a, i: (a, 0, i))
    hbm = pl.BlockSpec(memory_space=pl.ANY)
    in_specs = [col, pl.BlockSpec((None, nk, dh), lambda a, i: (a, 0, 0)),
                pl.BlockSpec((None, nkb, dh + ONES_ROWS, tk), lambda a, i: (a, 0, 0, 0))]
    out_specs = [col, vec]
    out_shape = [jax.ShapeDtypeStruct((g, dh, rn), F32), jax.ShapeDtypeStruct((g, 1, rn), F32)]
    args = (q_t, k, v_t)
    if xsend is not None:
        in_specs, out_specs, args = in_specs + [hbm], out_specs + [hbm], args + (xsend,)
        out_shape = out_shape + [_exchange_shape(xsend, True)]
    return pl.pallas_call(
        body, grid=(g, n_i), in_specs=in_specs, out_specs=out_specs, out_shape=out_shape,
        scratch_shapes=[pltpu.VMEM((2, unroll, tk, tq), F32)] + ([] if xsend is None else EXCHANGE_SCRATCH),
        compiler_params=pltpu.CompilerParams(dimension_semantics=("arbitrary", "arbitrary"), vmem_limit_bytes=VMEM_BIG_LIMIT),
        name="flash_fwd" if xsend is None else "flash_fwd_gather",
    )(*args)


def _flash_bwd_call(q, q_t, k, k_t, v, do, do_t, o_t, lse_t, xsend=None):
    g, rn, dh = q.shape
    nk = k.shape[1]
    tq, tk = _att_tq(rn), ATT_TK
    nkb = nk // tk
    unroll = _att_unroll(nkb, True)
    n_i = rn // tq

    def body(*refs):
        q_ref, qt_ref, k_ref, kt_ref, v_ref, do_ref, dot_ref, ot_ref, lse_ref = refs[:9]
        if xsend is None:
            dqt_ref, dk_ref, dv_ref = refs[9:]
        else:
            x_ref, dqt_ref, dk_ref, dv_ref, xout_ref, *sems = refs[9:]
        a, i = pl.program_id(0), pl.program_id(1)

        if xsend is not None:
            @pl.when((a == 0) & (i == 0))
            def _():
                _exchange_start(x_ref, xout_ref, *sems, all_gather=False)

        @pl.when(i == 0)
        def _():
            dk_ref[...] = jnp.zeros_like(dk_ref)
            dv_ref[...] = jnp.zeros_like(dv_ref)

        qs = q_ref[...] * SCALE
        qst = qt_ref[...] * SCALE
        dov, dotv = do_ref[...], dot_ref[...]
        delta = jnp.sum(dotv.astype(F32) * ot_ref[...], axis=0, keepdims=True)
        lse = lse_ref[...]

        def step(j, dqt):
            off = pl.multiple_of(j * tk, tk)
            kj = k_ref[pl.ds(off, tk), :]
            vj = v_ref[pl.ds(off, tk), :]
            p_t = jnp.exp(_dot(kj, qst) - lse)
            ds_t = (p_t * (_dot(vj, dotv) - delta)).astype(BF16)
            dv_ref[pl.ds(off, tk), :] += _dot(p_t.astype(BF16), dov)
            dk_ref[pl.ds(off, tk), :] += _dot(ds_t, qs)
            return dqt + _dot(kt_ref[j], ds_t)

        def trip(t, dqt):
            for u in range(unroll):
                dqt = step(t * unroll + u, dqt)
            return dqt

        dqt_ref[...] = lax.fori_loop(0, nkb // unroll, trip, jnp.zeros((dh, tq), F32)) * SCALE

        if xsend is not None:
            @pl.when((a == g - 1) & (i == n_i - 1))
            def _():
                _exchange_wait(x_ref, xout_ref, *sems, all_gather=False)

    row = pl.BlockSpec((None, tq, dh), lambda a, i: (a, i, 0))
    col = pl.BlockSpec((None, dh, tq), lambda a, i: (a, 0, i))
    kv = pl.BlockSpec((None, nk, dh), lambda a, i: (a, 0, 0))
    hbm = pl.BlockSpec(memory_space=pl.ANY)
    in_specs = [row, col, kv, pl.BlockSpec((None, nkb, dh, tk), lambda a, i: (a, 0, 0, 0)), kv, row, col, col,
                pl.BlockSpec((None, 1, tq), lambda a, i: (a, 0, i))]
    out_specs = [col, kv, kv]
    out_shape = [jax.ShapeDtypeStruct((g, dh, rn), F32), jax.ShapeDtypeStruct((g, nk, dh), F32),
                 jax.ShapeDtypeStruct((g, nk, dh), F32)]
    args = (q, q_t, k, k_t, v, do, do_t, o_t, lse_t)
    if xsend is not None:
        in_specs, out_specs, args = in_specs + [hbm], out_specs + [hbm], args + (xsend,)
        out_shape = out_shape + [_exchange_shape(xsend, False)]
    return pl.pallas_call(
        body, grid=(g, n_i), in_specs=in_specs, out_specs=out_specs, out_shape=out_shape,
        scratch_shapes=[] if xsend is None else EXCHANGE_SCRATCH,
        compiler_params=pltpu.CompilerParams(dimension_semantics=("arbitrary", "arbitrary"), vmem_limit_bytes=VMEM_BIG_LIMIT),
        name="flash_bwd" if xsend is None else "flash_bwd_scatter",
    )(*args)


def _key_blocks_t(t):
    g, nk, dh = t.shape
    return t.reshape(g, nk // ATT_TK, ATT_TK, dh).transpose(0, 1, 3, 2)


def _value_blocks_t(v):
    v_t = _key_blocks_t(v)
    extra = jnp.zeros(v_t.shape[:2] + (ONES_ROWS, ATT_TK), v.dtype).at[:, :, 0, :].set(1.0)
    return jnp.concatenate([v_t, extra], axis=2)


@jax.custom_vjp
def attn(q, k, v):
    return _attn_fwd(q, k, v)[0]


def _attn_fwd(q, k, v):
    qb, kb, vb = q.astype(BF16), k.astype(BF16), v.astype(BF16)
    q_t = qb.transpose(0, 2, 1)
    o_t, lse_t = _flash_fwd_call(q_t, kb, _value_blocks_t(vb))
    return o_t.transpose(0, 2, 1), (qb, q_t, kb, vb, o_t, lse_t)


def _attn_bwd(res, do):
    qb, q_t, kb, vb, o_t, lse_t = res
    dob = do.astype(BF16)
    dq_t, dk, dv = _flash_bwd_call(qb, q_t, kb, _key_blocks_t(kb), vb, dob, dob.transpose(0, 2, 1), o_t, lse_t)
    return dq_t.transpose(0, 2, 1), dk, dv


attn.defvjp(_attn_fwd, _attn_bwd)


def _shard_rows(shards):
    return jnp.concatenate([t.reshape(-1, D_MODEL) for t in shards.values()], axis=0).astype(BF16)


def _fulls_of(gathered, shards):
    out, off = {}, 0
    for n in shards:
        k, nn = shards[n].shape
        r = k * nn // D_MODEL
        t = gathered[:, off:off + r].reshape(N_DEV, k, nn)
        out[n] = t.reshape(N_DEV * k, nn) if n in ROW_SHARDED else t.transpose(1, 0, 2).reshape(k, N_DEV * nn)
        off += r
    return out


def _slabs_of(dfulls):
    parts = []
    for n in dfulls:
        k, nn = dfulls[n].shape
        t = (dfulls[n].reshape(N_DEV, k // N_DEV, nn) if n in ROW_SHARDED
             else dfulls[n].reshape(k, N_DEV, nn // N_DEV).transpose(1, 0, 2))
        parts.append(t.reshape(N_DEV, -1, D_MODEL))
    return jnp.concatenate(parts, axis=1)


def _dshards_of(parts, dfulls):
    rows, out, off = _sum8_call(parts), {}, 0
    for n in dfulls:
        k, nn = dfulls[n].shape
        shape = (k // N_DEV, nn) if n in ROW_SHARDED else (k, nn // N_DEV)
        r = shape[0] * shape[1] // D_MODEL
        out[n] = rows[off:off + r].reshape(shape)
        off += r
    return out


@jax.custom_vjp
def gather_layer(shards):
    return _fulls_of(_exchange(_shard_rows(shards), True, "gather_weights"), shards)


def _gather_layer_fwd(shards):
    return gather_layer(shards), None


def _gather_layer_bwd(_, dfulls):
    return (_dshards_of(_exchange(_slabs_of(dfulls), False, "scatter_grads"), dfulls),)


gather_layer.defvjp(_gather_layer_fwd, _gather_layer_bwd)


@jax.custom_vjp
def attn_gather(q, k, v, shards):
    return _attn_gather_fwd(q, k, v, shards)[0]


def _attn_gather_fwd(q, k, v, shards):
    qb, kb, vb = q.astype(BF16), k.astype(BF16), v.astype(BF16)
    q_t = qb.transpose(0, 2, 1)
    o_t, lse_t, gathered = _flash_fwd_call(q_t, kb, _value_blocks_t(vb), _shard_rows(shards))
    return (o_t.transpose(0, 2, 1), _fulls_of(gathered, shards)), (qb, q_t, kb, vb, o_t, lse_t)


def _attn_gather_bwd(res, cts):
    qb, q_t, kb, vb, o_t, lse_t = res
    do, dfulls = cts
    dob = do.astype(BF16)
    dq_t, dk, dv, parts = _flash_bwd_call(qb, q_t, kb, _key_blocks_t(kb), vb, dob, dob.transpose(0, 2, 1), o_t, lse_t,
                                          _slabs_of(dfulls))
    return dq_t.transpose(0, 2, 1), dk, dv, _dshards_of(parts, dfulls)


attn_gather.defvjp(_attn_gather_fwd, _attn_gather_bwd)


def _na_key_row(i, rows):
    return jnp.clip(NA_QROWS * i - NA_WIN_H // 2, 0, rows - NA_KROWS)


def _na_type(i, nb):
    return jnp.where(i == 0, 0, jnp.where(i == nb - 1, 2, 1))


def _na_fwd_call(q, k, v, kc, vc, slab):
    h, s, dh = q.shape
    n_ctx = kc.shape[1]
    rows, nb = s // GRID_W, s // NA_QB
    hb = NA_FWD_HEADS

    def body(q_ref, k_ref, v_ref, kc_ref, vc_ref, slab_ref, o_ref, lse_ref):
        off = pl.multiple_of(_na_key_row(pl.program_id(1), rows) * GRID_W, NA_QB)
        for hh in range(hb):
            qs = q_ref[hh] * SCALE
            kw = k_ref[hh, pl.ds(off, NA_KW), :]
            vw = v_ref[hh, pl.ds(off, NA_KW), :]
            sw = _dot_nt(qs, kw) + slab_ref[hh]
            sc = _dot_nt(qs, kc_ref[hh])
            m = jnp.maximum(jnp.max(sw, axis=1, keepdims=True), jnp.max(sc, axis=1, keepdims=True))
            pw = jnp.exp(sw - m)
            pc = jnp.exp(sc - m)
            l = jnp.sum(pw, axis=1, keepdims=True) + jnp.sum(pc, axis=1, keepdims=True)
            o_ref[hh] = (_dot(pw.astype(BF16), vw) + _dot(pc.astype(BF16), vc_ref[hh])) / l
            lse_ref[hh] = m + jnp.log(l)

    qblk = pl.BlockSpec((hb, NA_QB, dh), lambda a, i: (a, i, 0))
    kv = pl.BlockSpec((hb, s, dh), lambda a, i: (a, 0, 0))
    ckv = pl.BlockSpec((hb, n_ctx, dh), lambda a, i: (a, 0, 0))
    return pl.pallas_call(
        body, grid=(h // hb, nb),
        in_specs=[qblk, kv, kv, ckv, ckv, pl.BlockSpec((None, hb, NA_QB, NA_KW), lambda a, i: (_na_type(i, nb), a, 0, 0))],
        out_specs=[qblk, pl.BlockSpec((hb, NA_QB, 1), lambda a, i: (a, i, 0))],
        out_shape=[jax.ShapeDtypeStruct((h, s, dh), F32), jax.ShapeDtypeStruct((h, s, 1), F32)],
        compiler_params=pltpu.CompilerParams(dimension_semantics=("parallel", "parallel"), vmem_limit_bytes=VMEM_BIG_LIMIT),
        name="na_fwd",
    )(q, k, v, kc, vc, slab)


def _na_bwd_call(q, k, v, kc, vc, slab, do, o, lse):
    h, s, dh = q.shape
    n_ctx = kc.shape[1]
    rows, nb = s // GRID_W, s // NA_QB
    half = nb // 2

    def body(q_ref, k_ref, v_ref, kc_ref, vc_ref, slab0_ref, slab1_ref, do_ref, o_ref, lse_ref,
             dq_ref, dk_ref, dv_ref, dkc_ref, dvc_ref, dslab0_ref, dslab1_ref):
        j = pl.program_id(1)

        @pl.when(j == 0)
        def _():
            dk_ref[...] = jnp.zeros_like(dk_ref)
            dv_ref[...] = jnp.zeros_like(dv_ref)
            dkc_ref[...] = jnp.zeros_like(dkc_ref)
            dvc_ref[...] = jnp.zeros_like(dvc_ref)

        @pl.when(j <= 1)
        def _():
            dslab0_ref[...] = jnp.zeros_like(dslab0_ref)

        @pl.when((j == 0) | (j == half - 1))
        def _():
            dslab1_ref[...] = jnp.zeros_like(dslab1_ref)

        kcv, vcv = kc_ref[...], vc_ref[...]
        for sub, (slab_ref, dslab_ref) in enumerate(((slab0_ref, dslab0_ref), (slab1_ref, dslab1_ref))):
            off = pl.multiple_of(_na_key_row(2 * j + sub, rows) * GRID_W, NA_QB)
            blk = slice(sub * NA_QB, (sub + 1) * NA_QB)
            qs = q_ref[blk, :] * SCALE
            kw = k_ref[pl.ds(off, NA_KW), :]
            vw = v_ref[pl.ds(off, NA_KW), :]
            dov = do_ref[blk, :]
            lse = lse_ref[blk, :]
            delta = jnp.sum(dov.astype(F32) * o_ref[blk, :], axis=1, keepdims=True)
            pw = jnp.exp(_dot_nt(qs, kw) + slab_ref[...] - lse)
            pc = jnp.exp(_dot_nt(qs, kcv) - lse)
            dsw = pw * (_dot_nt(dov, vw) - delta)
            dsc = pc * (_dot_nt(dov, vcv) - delta)
            dslab_ref[...] += dsw
            dsw, dsc = dsw.astype(BF16), dsc.astype(BF16)
            dq_ref[blk, :] = (_dot(dsw, kw) + _dot(dsc, kcv)) * SCALE
            dk_ref[pl.ds(off, NA_KW), :] += _dot_tn(dsw, qs)
            dv_ref[pl.ds(off, NA_KW), :] += _dot_tn(pw.astype(BF16), dov)
            dkc_ref[...] += _dot_tn(dsc, qs)
            dvc_ref[...] += _dot_tn(pc.astype(BF16), dov)

    qblk = pl.BlockSpec((None, 2 * NA_QB, dh), lambda a, j: (a, j, 0))
    kv = pl.BlockSpec((None, s, dh), lambda a, j: (a, 0, 0))
    ckv = pl.BlockSpec((None, n_ctx, dh), lambda a, j: (a, 0, 0))
    slab_spec = lambda sub: pl.BlockSpec((None, None, NA_QB, NA_KW), lambda a, j: (_na_type(2 * j + sub, nb), a, 0, 0))
    dslab0, dslab1 = jax.ShapeDtypeStruct(slab.shape, F32), jax.ShapeDtypeStruct(slab.shape, F32)
    dq, dk, dv, dkc, dvc, d0, d1 = pl.pallas_call(
        body, grid=(h, half),
        in_specs=[qblk, kv, kv, ckv, ckv, slab_spec(0), slab_spec(1), qblk, qblk,
                  pl.BlockSpec((None, 2 * NA_QB, 1), lambda a, j: (a, j, 0))],
        out_specs=[qblk, kv, kv, ckv, ckv, slab_spec(0), slab_spec(1)],
        out_shape=[jax.ShapeDtypeStruct((h, s, dh), F32), jax.ShapeDtypeStruct((h, s, dh), F32),
                   jax.ShapeDtypeStruct((h, s, dh), F32), jax.ShapeDtypeStruct((h, n_ctx, dh), F32),
                   jax.ShapeDtypeStruct((h, n_ctx, dh), F32), dslab0, dslab1],
        compiler_params=pltpu.CompilerParams(dimension_semantics=("arbitrary", "arbitrary"), vmem_limit_bytes=VMEM_BIG_LIMIT),
        name="na_bwd",
    )(q, k, v, kc, vc, slab, slab, do, o, lse)
    return dq, dk, dv, dkc, dvc, jnp.stack([d0[0], d0[1] + d1[1], d1[2]])


@jax.custom_vjp
def na_attn(q, k, v, kc, vc, slab):
    return _na_fwd_call(q.astype(BF16), k.astype(BF16), v.astype(BF16), kc.astype(BF16), vc.astype(BF16), slab)[0]


def _na_attn_fwd(q, k, v, kc, vc, slab):
    qb, kb, vb, kcb, vcb = (t.astype(BF16) for t in (q, k, v, kc, vc))
    o, lse = _na_fwd_call(qb, kb, vb, kcb, vcb, slab)
    return o, (qb, kb, vb, kcb, vcb, slab, o, lse)


def _na_attn_bwd(res, do):
    qb, kb, vb, kcb, vcb, slab, o, lse = res
    return tuple(_na_bwd_call(qb, kb, vb, kcb, vcb, slab, do.astype(BF16), o, lse))


na_attn.defvjp(_na_attn_fwd, _na_attn_bwd)


def _na_tables(rows):
    col = np.arange(GRID_W)
    c_start = np.clip(col - NA_WIN_W // 2, 0, GRID_W - NA_WIN_W)
    in_win = (col[None, :] >= c_start[:, None]) & (col[None, :] < c_start[:, None] + NA_WIN_W)
    dc_idx = np.clip(col[None, :] - col[:, None], -(NA_WIN_W - 1), NA_WIN_W - 1) + NA_WIN_W - 1
    onehot = (in_win[:, :, None] & (dc_idx[:, :, None] == np.arange(2 * NA_WIN_W - 1)[None, None, :])).astype(np.float32)
    negcol = np.where(in_win, 0.0, NEG_BIG).astype(np.float32)
    plan = []
    for r0, ks in [(0, 0), (NA_QROWS, 0), (rows - NA_QROWS, rows - NA_KROWS)]:
        per_row = []
        for a in range(NA_QROWS):
            r = r0 + a
            rs = min(max(r - NA_WIN_H // 2, 0), rows - NA_WIN_H)
            valid = np.array([rs <= ks + b < rs + NA_WIN_H for b in range(NA_KROWS)])
            per_row.append((ks - r + NA_WIN_H - 1, valid))
        plan.append(per_row)
    return jnp.asarray(onehot), jnp.asarray(negcol), plan


def _na_slab(rpb, tables):
    onehot, negcol, plan = tables
    n_dr = 2 * NA_WIN_H - 1
    table = jnp.einsum("hdc,wuc->hwdu", rpb, onehot, precision=HIGHEST) + negcol[None, :, None, :]
    types = []
    for per_row in plan:
        slabs = []
        for first, valid in per_row:
            lo, hi = max(first, 0), min(first + NA_KROWS, n_dr)
            sel = jnp.pad(table[:, :, lo:hi, :], ((0, 0), (0, 0), (lo - first, first + NA_KROWS - hi), (0, 0)),
                          constant_values=NEG_BIG)
            sel = jnp.where(jnp.asarray(valid)[None, None, :, None], sel, NEG_BIG)
            slabs.append(sel.reshape(NA_HEADS, GRID_W, NA_KW))
        types.append(jnp.stack(slabs, axis=1).reshape(NA_HEADS, NA_QB, NA_KW))
    return jnp.stack(types)


def _loss_call(y, target):
    s, d = y.shape
    tt = _pick(s, (512, 256, 128))

    def body(y_ref, t_ref, loss_ref, diff_ref):
        @pl.when(pl.program_id(0) == 0)
        def _():
            loss_ref[...] = jnp.zeros_like(loss_ref)

        e = y_ref[...] - t_ref[...]
        diff_ref[...] = e * (1.0 / d)
        loss_ref[...] += 0.5 * jnp.sum(jnp.mean(e * e, axis=-1, keepdims=True), axis=0, keepdims=True)

    row = pl.BlockSpec((tt, d), lambda i: (i, 0))
    return pl.pallas_call(
        body, grid=(s // tt,), in_specs=[row, row], out_specs=[pl.BlockSpec((1, 1), lambda i: (0, 0)), row],
        out_shape=[jax.ShapeDtypeStruct((1, 1), F32), jax.ShapeDtypeStruct((s, d), F32)],
        compiler_params=pltpu.CompilerParams(dimension_semantics=("arbitrary",)), name="loss_head",
    )(y, target)


@jax.custom_vjp
def loss_head(y, target):
    return _loss_call(y, target)[0][0, 0]


def _loss_head_fwd(y, target):
    loss, diff = _loss_call(y, target)
    return loss[0, 0], diff


def _loss_head_bwd(diff, g):
    return diff * g, jnp.zeros_like(diff)


loss_head.defvjp(_loss_head_fwd, _loss_head_bwd)


def _sum8_call(parts):
    _, r, c = parts.shape
    tr = _pick(r, (328, 256, 128))

    def body(p_ref, g_ref):
        g = p_ref[0].astype(F32)
        for s in range(1, N_DEV):
            g = g + p_ref[s].astype(F32)
        g_ref[...] = g

    return pl.pallas_call(
        body, grid=(r // tr,), in_specs=[pl.BlockSpec((N_DEV, tr, c), lambda i: (0, i, 0))],
        out_specs=pl.BlockSpec((tr, c), lambda i: (i, 0)), out_shape=jax.ShapeDtypeStruct((r, c), F32),
        compiler_params=pltpu.CompilerParams(dimension_semantics=("parallel",)), name="sum8",
    )(parts)


def _adam_call(g, w, m, v):
    r, c = g.shape
    tr = _pick(r, (256, 128))

    def body(g_ref, w_ref, m_ref, v_ref, d_ref, nm_ref, nv_ref):
        gv = g_ref[...]
        mn = ADAM_B1 * m_ref[...] + (1.0 - ADAM_B1) * gv
        vn = ADAM_B2 * v_ref[...] + (1.0 - ADAM_B2) * (gv * gv)
        m_hat = mn / (1.0 - ADAM_B1 ** ADAM_STEP)
        v_hat = vn / (1.0 - ADAM_B2 ** ADAM_STEP)
        d_ref[...] = -ADAM_LR * (m_hat / (jnp.sqrt(v_hat) + ADAM_EPS) + ADAM_WD * w_ref[...])
        nm_ref[...] = mn
        nv_ref[...] = vn

    row = pl.BlockSpec((tr, c), lambda i: (i, 0))
    out = jax.ShapeDtypeStruct((r, c), F32)
    return pl.pallas_call(
        body, grid=(r // tr,), in_specs=[row, row, row, row], out_specs=[row, row, row], out_shape=[out, out, out],
        compiler_params=pltpu.CompilerParams(dimension_semantics=("parallel",)), name="adam",
    )(g, w, m, v)


def _pack_small(vals, extra=None):
    flat = [vals[n].reshape(-1) for n in SMALL] + ([] if extra is None else [extra.reshape(-1)])
    flat = jnp.concatenate(flat)
    rows = -(-flat.shape[0] // D_MODEL)
    rows = -(-rows // 8) * 8
    return jnp.pad(flat, (0, rows * D_MODEL - flat.shape[0])).reshape(rows, D_MODEL)


def _unpack_small(packed, like):
    flat, out, off = packed.reshape(-1), {}, 0
    for n in SMALL:
        size = like[n].size
        out[n] = flat[off:off + size].reshape(like[n].shape)
        off += size
    return out, flat[off]


def _rope_tables(s, n_ctx):
    t = jnp.arange(s)
    row = (t // GRID_W).astype(F32)
    col = (t % GRID_W).astype(F32)
    half = HEAD_DIM // 2
    inv = ROPE_THETA ** (-jnp.arange(0, half, 2, dtype=F32) / half)
    ang = jnp.concatenate([row[:, None] * inv, col[:, None] * inv], axis=-1)
    cos2 = jnp.repeat(jnp.cos(ang), 2, axis=-1)
    sin2 = jnp.repeat(jnp.sin(ang), 2, axis=-1)
    cos2 = jnp.concatenate([cos2, jnp.ones((n_ctx, HEAD_DIM), F32)], axis=0)
    sin2 = jnp.concatenate([sin2, jnp.zeros((n_ctx, HEAD_DIM), F32)], axis=0)
    return cos2, sin2


def _to_heads(t, n_heads):
    return t.reshape(t.shape[0], n_heads, HEAD_DIM).transpose(1, 0, 2)


def _from_heads(t):
    return t.transpose(1, 0, 2).reshape(t.shape[1], t.shape[0] * HEAD_DIM)


def _local_loss(shards, p, x, c, ctx, target, consts):
    s, n_ctx = x.shape[0], ctx.shape[0]
    cos2, sin2, rot, na_tables = consts
    is_ctx = (jnp.arange(s + n_ctx) >= s)[:, None]
    seg = lambda rows2: jnp.where(is_ctx, rows2[1:2], rows2[0:1])
    xa = jnp.concatenate([x, ctx], axis=0)
    cond = jnp.concatenate([jax.nn.silu(c), jax.nn.silu(p["c_ctx"])[None, :],
                            jnp.zeros((MOD_ROWS - 2, D_MODEL), F32)], axis=0).astype(BF16)
    pick = lambda names, l: {n: shards[n][l] for n in names}
    w = gather_layer(pick(EARLY, 0))

    for l in range(DEPTH):
        mod = (mm(cond, w["w_mod"]) + p["b_mod"][l])[:2]
        sh1, sc1, g1, sh2, sc2, g2 = jnp.split(mod, 6, axis=-1)

        h = norm_mod(xa, p["norm1"][l][None, :], sh1, sc1, s, BF16)
        na_q, na_k, na_v, gq, gk, gv, ga, gb = jnp.split(mm(h, w["w_in"]), IN_SPLITS, axis=-1)

        qa, ka, va = _to_heads(na_q, NA_HEADS), _to_heads(na_k, NA_HEADS), _to_heads(na_v, NA_HEADS)
        slab = _na_slab(p["na_rpb"][l], na_tables)
        ya_lat = na_attn(qa[:, :s], ka[:, :s], va[:, :s], ka[:, s:], va[:, s:], slab)
        ya_ctx = attn(qa[:, s:], ka[:, s:], va[:, s:])
        ya = _from_heads(jnp.concatenate([ya_lat, ya_ctx], axis=1))

        qk = _to_heads(jnp.concatenate([gq, gk], axis=-1), GQA_Q_HEADS + GQA_KV_HEADS)
        gains = jnp.stack([p["q_gain"][l], p["k_gain"][l]])
        qk = hn_rope(qk, gains, cos2, sin2, rot, GQA_Q_HEADS)
        qb, kb, vb = qk[:GQA_Q_HEADS], qk[GQA_Q_HEADS:], _to_heads(gv, GQA_KV_HEADS)
        q_lat = qb[:, :s].reshape(GQA_KV_HEADS, GQA_REP * s, HEAD_DIM)
        ob_lat, got = attn_gather(q_lat, kb, vb, {**pick(LATE, l), **(pick(EARLY, l + 1) if l + 1 < DEPTH else {})})
        w = {n: got[n] for n in LATE}
        ob_ctx = attn(qb[:, s:].reshape(GQA_KV_HEADS, GQA_REP * n_ctx, HEAD_DIM), kb[:, s:], vb[:, s:])
        yb = _from_heads(jnp.concatenate([ob_lat.reshape(GQA_Q_HEADS, s, HEAD_DIM),
                                          ob_ctx.reshape(GQA_Q_HEADS, n_ctx, HEAD_DIM)], axis=1))

        merged = jax.nn.sigmoid(ga) * mm(ya.astype(BF16), w["w_pa"]) + jax.nn.sigmoid(gb) * mm(yb.astype(BF16), w["w_pb"])
        xa = xa + seg(g1) * mm(merged.astype(BF16), w["w_o"])

        h2 = norm_mod(xa, p["norm2"][l][None, :], sh2, sc2, s, BF16)
        xa = xa + seg(g2) * ffn(h2, w["w_ffn_in"], w["w_ffn_out"])
        w = {n: got[n] for n in EARLY if n in got}

    zeros2 = jnp.zeros((2, D_MODEL), F32)
    y = norm_mod(xa, p["final_norm"][None, :], zeros2, zeros2, s, F32)[:s]
    return loss_head(y, target)


def kernel(x, c, ctx, c_ctx, w_mod, b_mod, norm1, w_in, na_rpb, q_gain, k_gain, w_pa, w_pb, w_o, norm2, w_ffn_in, w_ffn_out, final_norm, loss_target, m_c_ctx, m_w_mod, m_b_mod, m_norm1, m_w_in, m_na_rpb, m_q_gain, m_k_gain, m_w_pa, m_w_pb, m_w_o, m_norm2, m_w_ffn_in, m_w_ffn_out, m_final_norm, v_c_ctx, v_w_mod, v_b_mod, v_norm1, v_w_in, v_na_rpb, v_q_gain, v_k_gain, v_w_pa, v_w_pb, v_w_o, v_norm2, v_w_ffn_in, v_w_ffn_out, v_final_norm):
    w = dict(c_ctx=c_ctx, w_mod=w_mod, b_mod=b_mod, norm1=norm1, w_in=w_in, na_rpb=na_rpb, q_gain=q_gain, k_gain=k_gain,
             w_pa=w_pa, w_pb=w_pb, w_o=w_o, norm2=norm2, w_ffn_in=w_ffn_in, w_ffn_out=w_ffn_out, final_norm=final_norm)
    mom = dict(c_ctx=m_c_ctx, w_mod=m_w_mod, b_mod=m_b_mod, norm1=m_norm1, w_in=m_w_in, na_rpb=m_na_rpb, q_gain=m_q_gain,
               k_gain=m_k_gain, w_pa=m_w_pa, w_pb=m_w_pb, w_o=m_w_o, norm2=m_norm2, w_ffn_in=m_w_ffn_in,
               w_ffn_out=m_w_ffn_out, final_norm=m_final_norm)
    var = dict(c_ctx=v_c_ctx, w_mod=v_w_mod, b_mod=v_b_mod, norm1=v_norm1, w_in=v_w_in, na_rpb=v_na_rpb, q_gain=v_q_gain,
               k_gain=v_k_gain, w_pa=v_w_pa, w_pb=v_w_pb, w_o=v_w_o, norm2=v_norm2, w_ffn_in=v_w_ffn_in,
               w_ffn_out=v_w_ffn_out, final_norm=v_final_norm)
    s, n_ctx = x.shape[1], ctx.shape[1]
    depth = w_mod.shape[0]

    shards = {n: [w[n][l] for l in range(depth)] for n in BIG}
    small = {n: w[n] for n in SMALL}
    consts = (*_rope_tables(s, n_ctx), _rot_matrix(), _na_tables(s // GRID_W))
    loss, (g_shards, g_small, gx) = jax.value_and_grad(_local_loss, argnums=(0, 1, 2))(
        shards, small, x[0], c, ctx[0], loss_target[0], consts)

    parts_s = _exchange(_pack_small(g_small, loss), True, "gather_small_grads")
    g_packed = _sum8_call(parts_s)
    grads, loss = _unpack_small(g_packed, w)
    zero = jnp.zeros((1,), F32)
    upd_s = _adam_call(g_packed, _pack_small(w, zero), _pack_small(mom, zero), _pack_small(var, zero))
    outs = [grads] + [_unpack_small(u, w)[0] for u in upd_s]

    for n in BIG:
        flat = lambda t: t.reshape(-1, t.shape[-1])
        g = jnp.stack(g_shards[n])
        upd = _adam_call(flat(g), flat(w[n]), flat(mom[n]), flat(var[n]))
        for k, t in enumerate([g] + [u.reshape(w[n].shape) for u in upd]):
            outs[k][n] = t
    return (loss, gx[None], *[o[n] for o in outs for n in WEIGHTS])
```
